```python
import math
import jax, jax.numpy as jnp
from jax import lax
import numpy as np

D_MODEL = 1024
BATCH = 8
SEQ = 2048
DEPTH = 1
DEC_BATCH = 128
DEC_SEQ = 4
PAST_LEN = 16384
PAGE_SIZE = 128

D_MIX = 2 * D_MODEL
S5_WIDTH = D_MIX // 4
S5_GROUP = 16
S5_GROUPS = S5_WIDTH // S5_GROUP
S5_STATE = 64
SSD_WIDTH = D_MIX - S5_WIDTH
SSD_HEAD_DIM = 64
SSD_HEADS = SSD_WIDTH // SSD_HEAD_DIM
SSD_GROUPS = 4
SSD_STATE = 128
SSD_CONV = 4
SSD_CHUNK = 128
SSD_XBC = SSD_WIDTH + 2 * SSD_GROUPS * SSD_STATE
IN_WIDTH = S5_WIDTH + SSD_WIDTH + SSD_XBC + SSD_HEADS
D_FF = ((8 * D_MODEL // 3 + 255) // 256) * 256
EPS = 1e-6

kernel_name = "hymba_s5_ssd_macaron_step"


def rmsnorm(x, w):
    xf = x.astype(jnp.float32)
    xf = xf * lax.rsqrt(jnp.mean(xf * xf, axis=-1, keepdims=True) + EPS)
    return (xf * w.astype(jnp.float32)).astype(x.dtype)


def swiglu(h, w_gate, w_up, w_down):
    return (jax.nn.silu(h @ w_gate) * (h @ w_up)) @ w_down


def _cplx_combine(e1, e2):
    a1r, a1i, b1r, b1i = e1
    a2r, a2i, b2r, b2i = e2
    return (a2r * a1r - a2i * a1i, a2r * a1i + a2i * a1r,
            a2r * b1r - a2i * b1i + b2r, a2r * b1i + a2i * b1r + b2i)


def s5_scan(u, s_re0, s_im0, lam_re, lam_im, log_step, b_re, b_im, c_re, c_im, d_skip):
    f32 = jnp.float32
    lam_re, lam_im = lam_re.astype(f32), lam_im.astype(f32)
    step = jnp.exp(log_step.astype(f32))[:, None]
    mag = jnp.exp(lam_re * step)
    ang = lam_im * step
    abar_re, abar_im = mag * jnp.cos(ang), mag * jnp.sin(ang)
    den = lam_re * lam_re + lam_im * lam_im
    nre, nim = abar_re - 1.0, abar_im
    coef_re = (nre * lam_re + nim * lam_im) / den
    coef_im = (nim * lam_re - nre * lam_im) / den
    b_re, b_im = b_re.astype(f32), b_im.astype(f32)
    bbar_re = coef_re[..., None] * b_re - coef_im[..., None] * b_im
    bbar_im = coef_re[..., None] * b_im + coef_im[..., None] * b_re
    bu_re = jnp.einsum('blgh,gph->lbgp', u, bbar_re)
    bu_im = jnp.einsum('blgh,gph->lbgp', u, bbar_im)
    s_re0, s_im0 = s_re0.astype(f32), s_im0.astype(f32)
    bu_re = bu_re.at[0].add(abar_re * s_re0 - abar_im * s_im0)
    bu_im = bu_im.at[0].add(abar_re * s_im0 + abar_im * s_re0)
    l = u.shape[1]
    a_re = jnp.broadcast_to(abar_re[None, None], (l, 1) + abar_re.shape)
    a_im = jnp.broadcast_to(abar_im[None, None], (l, 1) + abar_im.shape)
    _, _, x_re, x_im = lax.associative_scan(_cplx_combine, (a_re, a_im, bu_re, bu_im), axis=0)
    y = (jnp.einsum('lbgp,ghp->blgh', x_re, c_re.astype(f32))
         - jnp.einsum('lbgp,ghp->blgh', x_im, c_im.astype(f32))
         + d_skip.astype(f32) * u)
    return y, x_re[-1], x_im[-1]


def ssd_chunked(x, dt, a, bmat, cmat, h0):
    b, l, nh, p = x.shape
    g, n = bmat.shape[2], bmat.shape[3]
    r = nh // g
    cs = math.gcd(l, SSD_CHUNK)
    nc = l // cs
    xd = (x * dt[..., None]).reshape(b, nc, cs, g, r, p)
    la = (dt * a).reshape(b, nc, cs, g, r)
    bm = bmat.reshape(b, nc, cs, g, n)
    cm = cmat.reshape(b, nc, cs, g, n)
    acum = jnp.cumsum(la, axis=2)
    diff = acum[:, :, :, None] - acum[:, :, None, :]
    mask = jnp.tril(jnp.ones((cs, cs), dtype=bool))[None, None, :, :, None, None]
    decay = jnp.exp(jnp.where(mask, diff, -jnp.inf))
    cb = jnp.einsum('bclgn,bcsgn->bclsg', cm, bm)
    y_diag = jnp.einsum('bclsg,bclsgr,bcsgrp->bclgrp', cb, decay, xd)
    dstate = jnp.exp(acum[:, :, -1:] - acum)
    states = jnp.einsum('bclgn,bclgr,bclgrp->bcgrpn', bm, dstate, xd)
    chunk_decay = jnp.exp(acum[:, :, -1])

    def step(hc, inp):
        st, dec = inp
        return hc * dec[..., None, None] + st, hc

    h_init = h0.astype(jnp.float32).reshape(b, g, r, p, n)
    h_fin, h_enter = lax.scan(step, h_init, (jnp.moveaxis(states, 1, 0), jnp.moveaxis(chunk_decay, 1, 0)))
    h_enter = jnp.moveaxis(h_enter, 0, 1)
    y_off = jnp.einsum('bclgn,bcgrpn,bclgr->bclgrp', cm, h_enter, jnp.exp(acum))
    y = (y_diag + y_off).reshape(b, l, nh, p)
    return y, h_fin.reshape(b, nh, p, n)


def mixer(h, s5_re0, s5_im0, ssd0, conv0, w_in, s5_lambda_re, s5_lambda_im, s5_log_step,
          s5_b_re, s5_b_im, s5_c_re, s5_c_im, s5_d, s5_w_glu, s5_b_glu,
          ssd_conv_w, ssd_conv_b, ssd_dt_bias, ssd_a_log, ssd_d, ssd_norm, w_out):
    f32 = jnp.float32
    b, l, _ = h.shape
    proj = (h @ w_in).astype(f32)
    u, z, xbc, dt_raw = jnp.split(
        proj, [S5_WIDTH, S5_WIDTH + SSD_WIDTH, S5_WIDTH + SSD_WIDTH + SSD_XBC], axis=-1)
    y5, s5_re1, s5_im1 = s5_scan(u.reshape(b, l, S5_GROUPS, S5_GROUP), s5_re0, s5_im0,
                                 s5_lambda_re, s5_lambda_im, s5_log_step,
                                 s5_b_re, s5_b_im, s5_c_re, s5_c_im, s5_d)
    v = jax.nn.gelu(y5.reshape(b, l, S5_WIDTH))
    o5 = v * jax.nn.sigmoid(v @ s5_w_glu.astype(f32) + s5_b_glu.astype(f32))
    xbc_full = jnp.concatenate([conv0.astype(f32), xbc], axis=1)
    conv_w = ssd_conv_w.astype(f32)
    conv = ssd_conv_b.astype(f32) + sum(conv_w[k] * xbc_full[:, k:k + l] for k in range(SSD_CONV))
    conv1 = xbc_full[:, l:]
    xbc_c = jax.nn.silu(conv)
    xs, bm, cm = jnp.split(xbc_c, [SSD_WIDTH, SSD_WIDTH + SSD_GROUPS * SSD_STATE], axis=-1)
    dt = jax.nn.softplus(dt_raw + ssd_dt_bias.astype(f32))
    a = -jnp.exp(ssd_a_log.astype(f32))
    xh = xs.reshape(b, l, SSD_HEADS, SSD_HEAD_DIM)
    ys, ssd1 = ssd_chunked(xh, dt, a, bm.reshape(b, l, SSD_GROUPS, SSD_STATE),
                           cm.reshape(b, l, SSD_GROUPS, SSD_STATE), ssd0)
    ys = (ys + ssd_d.astype(f32)[:, None] * xh).reshape(b, l, SSD_WIDTH) * jax.nn.silu(z)
    yg = ys.reshape(b, l, SSD_GROUPS, SSD_WIDTH // SSD_GROUPS)
    yg = yg * lax.rsqrt(jnp.mean(yg * yg, axis=-1, keepdims=True) + EPS)
    o_ssd = yg.reshape(b, l, SSD_WIDTH) * ssd_norm.astype(f32)
    out = jnp.concatenate([o5, o_ssd], axis=-1).astype(h.dtype) @ w_out
    return out, s5_re1, s5_im1, ssd1, conv1


def setup_inputs(seed: int = 0) -> dict:
    key = jax.random.key(seed)
    ks = jax.random.split(key, 40)
    f32 = jnp.float32
    nrm = lambda k, shape, s: (jax.random.normal(k, shape, f32) * s)
    D, L = D_MODEL, DEPTH
    G, P, H = S5_GROUPS, S5_STATE, S5_GROUP
    dt0 = jnp.exp(jax.random.uniform(ks[27], (L, SSD_HEADS), f32, math.log(1e-3), math.log(1e-1)))
    return {
        "x_prompt": nrm(ks[0], (BATCH, SEQ, D), 1.0),
        "x_sample": nrm(ks[1], (DEC_BATCH, DEC_SEQ, D), 1.0),
        "state_s5_re": nrm(ks[2], (L, DEC_BATCH, G, P), 0.5),
        "state_s5_im": nrm(ks[3], (L, DEC_BATCH, G, P), 0.5),
        "state_ssd": nrm(ks[4], (L, DEC_BATCH, SSD_HEADS, SSD_HEAD_DIM, SSD_STATE), 0.1),
        "state_conv": nrm(ks[5], (L, DEC_BATCH, SSD_CONV - 1, SSD_XBC), 1.0),
        "ffn1_norm": 1.0 + nrm(ks[6], (L, D), 0.01),
        "ffn1_w_gate": nrm(ks[7], (L, D, D_FF), D ** -0.5),
        "ffn1_w_up": nrm(ks[8], (L, D, D_FF), D ** -0.5),
        "ffn1_w_down": nrm(ks[9], (L, D_FF, D), D_FF ** -0.5),
        "mix_norm": 1.0 + nrm(ks[10], (L, D), 0.01),
        "w_in": nrm(ks[11], (L, D, IN_WIDTH), D ** -0.5),
        "s5_lambda_re": -0.5 + nrm(ks[12], (L, G, P), 0.01),
        "s5_lambda_im": jnp.pi * jnp.arange(P, dtype=f32) + nrm(ks[13], (L, G, P), 0.01),
        "s5_log_step": jax.random.uniform(ks[14], (L, G), f32, math.log(1e-3), math.log(1e-1)),
        "s5_b_re": nrm(ks[15], (L, G, P, H), (2 * H) ** -0.5),
        "s5_b_im": nrm(ks[16], (L, G, P, H), (2 * H) ** -0.5),
        "s5_c_re": nrm(ks[17], (L, G, H, P), (2 * P) ** -0.5),
        "s5_c_im": nrm(ks[18], (L, G, H, P), (2 * P) ** -0.5),
        "s5_d": nrm(ks[19], (L, G, H), 1.0),
        "s5_w_glu": nrm(ks[20], (L, S5_WIDTH, S5_WIDTH), S5_WIDTH ** -0.5),
        "s5_b_glu": nrm(ks[21], (L, S5_WIDTH), 0.01),
        "ssd_conv_w": nrm(ks[22], (L, SSD_CONV, SSD_XBC), SSD_CONV ** -0.5),
        "ssd_conv_b": nrm(ks[23], (L, SSD_XBC), 0.01),
        "ssd_dt_bias": dt0 + jnp.log(-jnp.expm1(-dt0)),
        "ssd_a_log": jnp.log(jax.random.uniform(ks[24], (L, SSD_HEADS), f32, 1.0, 16.0)),
        "ssd_d": 1.0 + nrm(ks[25], (L, SSD_HEADS), 0.1),
        "ssd_norm": 1.0 + nrm(ks[26], (L, SSD_WIDTH), 0.01),
        "w_out": nrm(ks[28], (L, D_MIX, D), D_MIX ** -0.5),
        "ffn2_norm": 1.0 + nrm(ks[29], (L, D), 0.01),
        "ffn2_w_gate": nrm(ks[30], (L, D, D_FF), D ** -0.5),
        "ffn2_w_up": nrm(ks[31], (L, D, D_FF), D ** -0.5),
        "ffn2_w_down": nrm(ks[32], (L, D_FF, D), D_FF ** -0.5),
        "final_norm": 1.0 + nrm(ks[33], (D,), 0.01),
    }


def reference(x_prompt, x_sample, state_s5_re, state_s5_im, state_ssd, state_conv,
              ffn1_norm, ffn1_w_gate, ffn1_w_up, ffn1_w_down, mix_norm, w_in,
              s5_lambda_re, s5_lambda_im, s5_log_step, s5_b_re, s5_b_im, s5_c_re, s5_c_im,
              s5_d, s5_w_glu, s5_b_glu, ssd_conv_w, ssd_conv_b, ssd_dt_bias, ssd_a_log,
              ssd_d, ssd_norm, w_out, ffn2_norm, ffn2_w_gate, ffn2_w_up, ffn2_w_down, final_norm):
    f32 = jnp.float32
    xp, xs = x_prompt, x_sample
    p_states = (jnp.zeros((BATCH, S5_GROUPS, S5_STATE), f32),
                jnp.zeros((BATCH, S5_GROUPS, S5_STATE), f32),
                jnp.zeros((BATCH, SSD_HEADS, SSD_HEAD_DIM, SSD_STATE), f32),
                jnp.zeros((BATCH, SSD_CONV - 1, SSD_XBC), f32))
    new_p = ([], [], [], [])
    new_s = ([], [], [], [])
    for i in range(DEPTH):
        mix_params = (w_in[i], s5_lambda_re[i], s5_lambda_im[i], s5_log_step[i],
                      s5_b_re[i], s5_b_im[i], s5_c_re[i], s5_c_im[i], s5_d[i],
                      s5_w_glu[i], s5_b_glu[i], ssd_conv_w[i], ssd_conv_b[i],
                      ssd_dt_bias[i], ssd_a_log[i], ssd_d[i], ssd_norm[i], w_out[i])
        s_states = (state_s5_re[i], state_s5_im[i], state_ssd[i], state_conv[i])
        outs = []
        for x, st, acc in ((xp, p_states, new_p), (xs, s_states, new_s)):
            x = x + 0.5 * swiglu(rmsnorm(x, ffn1_norm[i]), ffn1_w_gate[i], ffn1_w_up[i], ffn1_w_down[i])
            m, s5r, s5i, ssd1, conv1 = mixer(rmsnorm(x, mix_norm[i]), *st, *mix_params)
            x = x + m.astype(x.dtype)
            x = x + 0.5 * swiglu(rmsnorm(x, ffn2_norm[i]), ffn2_w_gate[i], ffn2_w_up[i], ffn2_w_down[i])
            for lst, val in zip(acc, (s5r, s5i, ssd1, conv1)):
                lst.append(val)
            outs.append(x)
        xp, xs = outs
    y_prompt = rmsnorm(xp, final_norm)
    y_sample = rmsnorm(xs, final_norm)
    return (y_prompt, y_sample,
            jnp.stack(new_p[0]), jnp.stack(new_p[1]), jnp.stack(new_p[2]), jnp.stack(new_p[3]),
            jnp.stack(new_s[0]), jnp.stack(new_s[1]), jnp.stack(new_s[2]), jnp.stack(new_s[3]))
```

```python
import functools
import math

import jax
import jax.numpy as jnp
from jax import lax
from jax.experimental import pallas as pl
from jax.experimental.pallas import tpu as pltpu

F32 = jnp.float32
BF16 = jnp.bfloat16
EPS = 1e-6

LANES = 128
SUBLANES = 8
VMEM_LIMIT = 56 * 1024 * 1024

S5_GROUP = 16
S5_STATE = 64
SSD_HEAD_DIM = 64
SSD_GROUPS = 4
SSD_STATE = 128
SSD_CONV = 4
SSD_CHUNK = 128
CONV_PAD = SUBLANES


def _sigmoid(x):
    return 1.0 / (1.0 + jnp.exp(-x))


def _silu(x):
    return x * _sigmoid(x)


def _gelu_tanh(x):
    c = math.sqrt(2.0 / math.pi)
    return 0.5 * x * (1.0 + jnp.tanh(c * (x + 0.044715 * (x * x * x))))


def _softplus(x):
    return jnp.maximum(x, 0.0) + jnp.log1p(jnp.exp(-jnp.abs(x)))


def _rms(x, w):
    return x * lax.rsqrt(jnp.mean(x * x, axis=-1, keepdims=True) + EPS) * w


def _dot(a, b):
    return jnp.dot(a, b, preferred_element_type=F32)


def _dot_nt(a, b):
    return lax.dot_general(a, b, (((1,), (1,)), ((), ())), preferred_element_type=F32)


def _const_spec(shape):
    nd = len(shape)
    return pl.BlockSpec(shape, lambda *_: (0,) * nd, pipeline_mode=pl.Buffered(1))


def _params(sem):
    return pltpu.CompilerParams(dimension_semantics=sem, vmem_limit_bytes=VMEM_LIMIT)


def _swiglu_half(x, nw, wg_ref, wu_ref, wd_ref):
    hn = _rms(x, nw).astype(BF16)
    g = _dot(hn, wg_ref[...])
    u = _dot(hn, wu_ref[...])
    a = (_silu(g) * u).astype(BF16)
    return x + 0.5 * _dot(a, wd_ref[...])


def _ffn1_kernel(x_ref, nw_ref, wg_ref, wu_ref, wd_ref, mixw_ref, x1_ref, h_ref):
    x1 = _swiglu_half(x_ref[...], nw_ref[...], wg_ref, wu_ref, wd_ref)
    x1_ref[...] = x1
    h_ref[...] = _rms(x1, mixw_ref[...]).astype(BF16)


def _ffn1(x, nw, wg, wu, wd, mixw, tm):
    m, d = x.shape
    dff = wg.shape[1]
    row = lambda w: pl.BlockSpec((tm, w), lambda i: (i, 0))
    return pl.pallas_call(
        _ffn1_kernel,
        grid=(m // tm,),
        in_specs=[row(d), _const_spec((1, d)), _const_spec((d, dff)), _const_spec((d, dff)),
                  _const_spec((dff, d)), _const_spec((1, d))],
        out_specs=[row(d), row(d)],
        out_shape=[jax.ShapeDtypeStruct((m, d), F32), jax.ShapeDtypeStruct((m, d), BF16)],
        compiler_params=_params(("arbitrary",)),
        name="ffn1",
    )(x, nw, wg, wu, wd, mixw)


def _out_ffn2_kernel(x1_ref, o5_ref, os_ref, wo5_ref, wos_ref, nw_ref, wg_ref, wu_ref, wd_ref,
                     fin_ref, y_ref):
    x2 = x1_ref[...] + (_dot(o5_ref[...], wo5_ref[...]) + _dot(os_ref[...], wos_ref[...]))
    x3 = _swiglu_half(x2, nw_ref[...], wg_ref, wu_ref, wd_ref)
    y_ref[...] = _rms(x3, fin_ref[...])


def _out_ffn2(x1, o5, o_ssd, wo5, wos, nw, wg, wu, wd, fin, tm):
    m, d = x1.shape
    dff = wg.shape[1]
    row = lambda w: pl.BlockSpec((tm, w), lambda i: (i, 0))
    return pl.pallas_call(
        _out_ffn2_kernel,
        grid=(m // tm,),
        in_specs=[row(d), row(o5.shape[1]), row(o_ssd.shape[1]),
                  _const_spec(wo5.shape), _const_spec(wos.shape), _const_spec((1, d)),
                  _const_spec((d, dff)), _const_spec((d, dff)), _const_spec((dff, d)),
                  _const_spec((1, d))],
        out_specs=row(d),
        out_shape=jax.ShapeDtypeStruct((m, d), F32),
        compiler_params=_params(("arbitrary",)),
        name="out_ffn2",
    )(x1, o5, o_ssd, wo5, wos, nw, wg, wu, wd, fin)


def _s5_param_kernel(lre_ref, lim_ref, step_ref, bre_ref, bim_ref,
                     are_ref, aim_ref, bbre_ref, bbim_ref):
    lre, lim, step = lre_ref[...], lim_ref[...], step_ref[...]
    mag = jnp.exp(lre * step)
    ang = lim * step
    are = mag * jnp.cos(ang)
    aim = mag * jnp.sin(ang)
    den = lre * lre + lim * lim
    nre, nim = are - 1.0, aim
    cre = (nre * lre + nim * lim) / den
    cim = (nim * lre - nre * lim) / den
    are_ref[...] = are
    aim_ref[...] = aim
    bre, bim = bre_ref[...], bim_ref[...]
    bbre_ref[...] = cre * bre - cim * bim
    bbim_ref[...] = cre * bim + cim * bre


def _s5_params(lam_re, lam_im, log_step, b_re, b_im):
    g, p = lam_re.shape
    hh = b_re.shape[-1]
    gp = g * p
    step = jnp.exp(log_step.astype(F32))
    step_col = jnp.broadcast_to(step[:, None], (g, p)).reshape(gp, 1)
    full = lambda shape: pl.BlockSpec(shape, lambda: (0,) * len(shape))
    are, aim, bbre, bbim = pl.pallas_call(
        _s5_param_kernel,
        in_specs=[full((gp, 1)), full((gp, 1)), full((gp, 1)), full((gp, hh)), full((gp, hh))],
        out_specs=[full((gp, 1)), full((gp, 1)), full((gp, hh)), full((gp, hh))],
        out_shape=[jax.ShapeDtypeStruct((gp, 1), F32), jax.ShapeDtypeStruct((gp, 1), F32),
                   jax.ShapeDtypeStruct((gp, hh), F32), jax.ShapeDtypeStruct((gp, hh), F32)],
        name="s5_params",
    )(lam_re.astype(F32).reshape(gp, 1), lam_im.astype(F32).reshape(gp, 1), step_col,
      b_re.astype(F32).reshape(gp, hh), b_im.astype(F32).reshape(gp, hh))
    return (are.reshape(1, gp), aim.reshape(1, gp),
            bbre.reshape(g, p, hh), bbim.reshape(g, p, hh))


def _block_diag(w, nblk):
    g, a, b = w.shape
    gl = g // nblk
    w = w.reshape(nblk, gl, a, b)
    eye = jnp.eye(gl, dtype=w.dtype)
    out = w[:, :, :, None, :] * eye[None, :, None, :, None]
    return out.reshape(nblk, gl * a, gl * b)


def _s5_kernel(h_ref, wu_ref, wb_ref, wcre_ref, wcim_ref, are_ref, aim_ref, d_ref, wglu_ref,
               bglu_ref, s0re_ref, s0im_ref, o5_ref, sre_ref, sim_ref, u_s, xre_s, xim_s, y_s,
               *, nb, tl, lane_chunk):
    width = u_s.shape[1]
    nstate = xre_s.shape[1]
    nblk = wb_ref.shape[0]
    cin = width // nblk
    cst = nstate // nblk

    @pl.when(pl.program_id(0) == 0)
    def _():
        sre_ref[...] = s0re_ref[...]
        sim_ref[...] = s0im_ref[...]

    u = _dot(h_ref[...], wu_ref[...])
    u_s[...] = u
    ub = u.astype(BF16)
    for k in range(nblk):
        r = _dot(ub[:, k * cin:(k + 1) * cin], wb_ref[k])
        xre_s[:, k * cst:(k + 1) * cst] = r[:, :cst]
        xim_s[:, k * cst:(k + 1) * cst] = r[:, cst:]

    for c in range(nstate // lane_chunk):
        lanes = slice(c * lane_chunk, (c + 1) * lane_chunk)
        a_re = jnp.broadcast_to(are_ref[:, lanes], (SUBLANES, lane_chunk))
        a_im = jnp.broadcast_to(aim_ref[:, lanes], (SUBLANES, lane_chunk))

        def group_body(sg, carry, lanes=lanes, a_re=a_re, a_im=a_im):
            srow = pl.ds(pl.multiple_of(sg * SUBLANES, SUBLANES), SUBLANES)

            def step(t, st):
                s_re, s_im = st
                rows = pl.ds(pl.multiple_of(t * nb + sg * SUBLANES, SUBLANES), SUBLANES)
                n_re = a_re * s_re - a_im * s_im + xre_s[rows, lanes]
                n_im = a_re * s_im + a_im * s_re + xim_s[rows, lanes]
                xre_s[rows, lanes] = n_re
                xim_s[rows, lanes] = n_im
                return n_re, n_im

            s_re, s_im = lax.fori_loop(0, tl, step, (sre_ref[srow, lanes], sim_ref[srow, lanes]),
                                       unroll=min(tl, 8))
            sre_ref[srow, lanes] = s_re
            sim_ref[srow, lanes] = s_im
            return carry

        lax.fori_loop(0, nb // SUBLANES, group_body, 0)

    for k in range(nblk):
        st = slice(k * cst, (k + 1) * cst)
        y_s[:, k * cin:(k + 1) * cin] = (_dot(xre_s[:, st].astype(BF16), wcre_ref[k])
                                         - _dot(xim_s[:, st].astype(BF16), wcim_ref[k]))
    y = y_s[...] + d_ref[...] * u_s[...]
    v = _gelu_tanh(y)
    gate = _dot(v.astype(BF16), wglu_ref[...]) + bglu_ref[...]
    o5_ref[...] = (v * _sigmoid(gate)).astype(BF16)


def _s5(h_tm, wu, wb, wcre, wcim, are, aim, dsk, wglu, bglu, s0re, s0im, nb, tl):
    m, d = h_tm.shape
    width = wu.shape[1]
    nstate = are.shape[1]
    r = nb * tl
    row = lambda w: pl.BlockSpec((r, w), lambda i: (i, 0))
    kern = functools.partial(_s5_kernel, nb=nb, tl=tl, lane_chunk=4 * LANES)
    return pl.pallas_call(
        kern,
        grid=(m // r,),
        in_specs=[row(d), _const_spec(wu.shape), _const_spec(wb.shape), _const_spec(wcre.shape),
                  _const_spec(wcim.shape), _const_spec(are.shape), _const_spec(aim.shape),
                  _const_spec(dsk.shape), _const_spec(wglu.shape), _const_spec(bglu.shape),
                  _const_spec(s0re.shape), _const_spec(s0im.shape)],
        out_specs=[row(width), pl.BlockSpec((nb, nstate), lambda i: (0, 0)),
                   pl.BlockSpec((nb, nstate), lambda i: (0, 0))],
        out_shape=[jax.ShapeDtypeStruct((m, width), BF16),
                   jax.ShapeDtypeStruct((nb, nstate), F32),
                   jax.ShapeDtypeStruct((nb, nstate), F32)],
        scratch_shapes=[pltpu.VMEM((r, width), F32), pltpu.VMEM((r, nstate), F32),
                        pltpu.VMEM((r, nstate), F32), pltpu.VMEM((r, width), F32)],
        compiler_params=_params(("arbitrary",)),
        name="s5",
    )(h_tm, wu, wb, wcre, wcim, are, aim, dsk, wglu, bglu, s0re, s0im)


def _ssd_kernel(h_ref, wz_ref, wx_ref, wdt_ref, cw_ref, cb_ref, dtb_ref, alog_ref, dsk_ref,
                nw_ref, ssd0_ref, conv0_ref, o_ref, st_ref, cv_ref, xbc_s, xc_s, z_s, dt_s,
                *, nb, nch, t, t_real):
    r = nch * t
    nheads = st_ref.shape[1]
    hd_dim = st_ref.shape[2]
    nst = st_ref.shape[3]
    width = nheads * hd_dim
    hpg = nheads // SSD_GROUPS
    gw = width // SSD_GROUPS
    last = (nch - 1) * t + t_real

    z_s[...] = _dot(h_ref[...], wz_ref[...])
    dt_s[...] = _dot(h_ref[...], wdt_ref[...])

    @pl.when(pl.program_id(1) == 0)
    def _():
        st_ref[...] = ssd0_ref[...]
        for b in range(nb):
            xbc_s[b, CONV_PAD - (SSD_CONV - 1):CONV_PAD, :] = conv0_ref[b]

    xbc = _dot(h_ref[...], wx_ref[...])
    for b in range(nb):
        xbc_s[b, CONV_PAD:CONV_PAD + r, :] = xbc[b * r:(b + 1) * r, :]

    piece = min(r, SSD_CHUNK)
    cw = cw_ref[...]
    cbias = cb_ref[...]
    for b in range(nb):
        for i in range(r // piece):
            acc = cbias
            for k in range(SSD_CONV):
                off = CONV_PAD - (SSD_CONV - 1) + k + i * piece
                acc = acc + cw[k:k + 1, :] * xbc_s[b, off:off + piece, :]
            xc_s[b * r + i * piece:b * r + (i + 1) * piece, :] = _silu(acc)

    for b in range(nb):
        tail = xbc_s[b, CONV_PAD + last - (SSD_CONV - 1):CONV_PAD + last, :]
        cv_ref[b] = tail
        xbc_s[b, CONV_PAD - (SSD_CONV - 1):CONV_PAD, :] = tail

    rid = lax.broadcasted_iota(jnp.int32, (t, t), 0)
    cid = lax.broadcasted_iota(jnp.int32, (t, t), 1)
    causal = rid >= cid
    tril = jnp.where(causal, 1.0, 0.0).astype(BF16)
    a_neg = -jnp.exp(alog_ref[...])
    dtb = dtb_ref[...]
    dsk = dsk_ref[...]
    nw = nw_ref[...]

    for b in range(nb):
        def chunk_body(c, carry, b=b):
            rows = pl.ds(pl.multiple_of(b * r + c * t, SUBLANES), t)
            xc = xc_s[rows, :]
            xs = xc[:, :width]
            bmat = xc[:, width:width + SSD_GROUPS * nst]
            cmat = xc[:, width + SSD_GROUPS * nst:]
            dt = _softplus(dt_s[rows, :] + dtb)
            if t_real < t:
                dt = jnp.where(lax.broadcasted_iota(jnp.int32, dt.shape, 0) < t_real, dt, 0.0)
            la = dt * a_neg
            hi = la.astype(BF16)
            r1 = la - hi.astype(F32)
            mid = r1.astype(BF16)
            lo = (r1 - mid.astype(F32)).astype(BF16)
            acum = _dot(tril, hi) + _dot(tril, mid) + _dot(tril, lo)
            acum_t = acum.T
            tot = acum[t - 1:t, :]
            dstate = jnp.exp(tot - acum)
            eacum = jnp.exp(acum)
            cdec = jnp.exp(tot)
            ys = []
            for g in range(SSD_GROUPS):
                cg = cmat[:, g * nst:(g + 1) * nst].astype(BF16)
                bg = bmat[:, g * nst:(g + 1) * nst].astype(BF16)
                cb = _dot_nt(cg, bg)
                for jh in range(hpg):
                    hd = g * hpg + jh
                    diff = acum[:, hd:hd + 1] - acum_t[hd:hd + 1, :]
                    decay = jnp.exp(jnp.where(causal, diff, -jnp.inf))
                    xh = xs[:, hd * hd_dim:(hd + 1) * hd_dim]
                    xd = xh * dt[:, hd:hd + 1]
                    y = _dot((cb * decay).astype(BF16), xd.astype(BF16))
                    hst = st_ref[b, hd]
                    y = y + _dot_nt(cg, hst.astype(BF16)) * eacum[:, hd:hd + 1]
                    xdd = xd * dstate[:, hd:hd + 1]
                    st_ref[b, hd] = hst * cdec[:, hd:hd + 1] + _dot(xdd.T.astype(BF16), bg)
                    ys.append(y + dsk[:, hd * hd_dim:(hd + 1) * hd_dim] * xh)
            yall = jnp.concatenate(ys, axis=1) * _silu(z_s[rows, :])
            outs = []
            for g in range(SSD_GROUPS):
                seg = yall[:, g * gw:(g + 1) * gw]
                outs.append(seg * lax.rsqrt(jnp.mean(seg * seg, axis=-1, keepdims=True) + EPS))
            o_ref[rows, :] = (jnp.concatenate(outs, axis=1) * nw).astype(BF16)
            return carry
        lax.fori_loop(0, nch, chunk_body, 0)


def _ssd(h, wz, wx, wdt, cw, cb, dtb, alog, dsk, nw, ssd0, conv0, nb, nch, t, t_real):
    m, d = h.shape
    nseq, nheads, hd_dim, nst = ssd0.shape
    width = nheads * hd_dim
    xbc_w = wx.shape[1]
    r = nch * t
    nblk = m // (nseq * r)
    assert nb == 1 or nblk == 1
    row = lambda w: pl.BlockSpec((nb * r, w), lambda i, j: (i * nblk + j, 0))
    seq4 = pl.BlockSpec((nb, nheads, hd_dim, nst), lambda i, j: (i, 0, 0, 0))
    seq3 = pl.BlockSpec((nb, SSD_CONV - 1, xbc_w), lambda i, j: (i, 0, 0))
    kern = functools.partial(_ssd_kernel, nb=nb, nch=nch, t=t, t_real=t_real)
    return pl.pallas_call(
        kern,
        grid=(nseq // nb, nblk),
        in_specs=[row(d), _const_spec(wz.shape), _const_spec(wx.shape), _const_spec(wdt.shape),
                  _const_spec(cw.shape), _const_spec(cb.shape), _const_spec(dtb.shape),
                  _const_spec(alog.shape), _const_spec(dsk.shape), _const_spec(nw.shape),
                  seq4, seq3],
        out_specs=[row(width), seq4, seq3],
        out_shape=[jax.ShapeDtypeStruct((m, width), BF16),
                   jax.ShapeDtypeStruct(ssd0.shape, F32),
                   jax.ShapeDtypeStruct(conv0.shape, F32)],
        scratch_shapes=[pltpu.VMEM((nb, CONV_PAD + r, xbc_w), F32),
                        pltpu.VMEM((nb * r, xbc_w), F32),
                        pltpu.VMEM((nb * r, width), F32),
                        pltpu.VMEM((nb * r, LANES), F32)],
        compiler_params=_params(("arbitrary", "arbitrary")),
        name="ssd",
    )(h, wz, wx, wdt, cw, cb, dtb, alog, dsk, nw, ssd0, conv0)


def _pad_lanes(v, n=LANES):
    return jnp.pad(v, [(0, 0)] * (v.ndim - 1) + [(0, n - v.shape[-1])])


def _layer(x, s5re0, s5im0, ssd0, conv0, w, *, tm, s5_tl, ssd_nb, ssd_nch, ssd_t, ssd_t_real):
    bsz, seq, d = x.shape
    m = bsz * seq
    x1, h = _ffn1(x.reshape(m, d), w["ffn1_norm"], w["ffn1_wg"], w["ffn1_wu"], w["ffn1_wd"],
                  w["mix_norm"], tm)

    h_tm = h.reshape(bsz, seq, d).transpose(1, 0, 2).reshape(m, d)
    g, p = s5re0.shape[1:]
    o5_tm, s5re1, s5im1 = _s5(h_tm, w["w_u"], w["s5_wb"], w["s5_wcre"], w["s5_wcim"],
                              w["s5_are"], w["s5_aim"], w["s5_d"], w["s5_wglu"], w["s5_bglu"],
                              s5re0.reshape(bsz, g * p), s5im0.reshape(bsz, g * p), bsz, s5_tl)
    o5 = o5_tm.reshape(seq, bsz, -1).transpose(1, 0, 2).reshape(m, -1)

    pad_t = ssd_t - ssd_t_real if seq < ssd_t else 0
    hs = h
    if pad_t:
        hs = jnp.pad(h.reshape(bsz, seq, d), ((0, 0), (0, pad_t), (0, 0))).reshape(-1, d)
    o_ssd, ssd1, conv1 = _ssd(hs, w["w_z"], w["w_x"], w["w_dt"], w["conv_w"], w["conv_b"],
                              w["dt_bias"], w["a_log"], w["ssd_d"], w["ssd_norm"], ssd0, conv0,
                              ssd_nb, ssd_nch, ssd_t, ssd_t_real)
    if pad_t:
        o_ssd = o_ssd.reshape(bsz, seq + pad_t, -1)[:, :seq].reshape(m, -1)

    y = _out_ffn2(x1, o5, o_ssd, w["w_out5"], w["w_outs"], w["ffn2_norm"], w["ffn2_wg"],
                  w["ffn2_wu"], w["ffn2_wd"], w["final_norm"], tm)
    return (y.reshape(bsz, seq, d), s5re1.reshape(bsz, g, p), s5im1.reshape(bsz, g, p),
            ssd1, conv1)


def kernel(x_prompt, x_sample, state_s5_re, state_s5_im, state_ssd, state_conv, ffn1_norm, ffn1_w_gate, ffn1_w_up, ffn1_w_down, mix_norm, w_in, s5_lambda_re, s5_lambda_im, s5_log_step, s5_b_re, s5_b_im, s5_c_re, s5_c_im, s5_d, s5_w_glu, s5_b_glu, ssd_conv_w, ssd_conv_b, ssd_dt_bias, ssd_a_log, ssd_d, ssd_norm, w_out, ffn2_norm, ffn2_w_gate, ffn2_w_up, ffn2_w_down, final_norm):
    depth = w_in.shape[0]
    assert depth == 1, "single-layer stack"
    i = 0
    d = x_prompt.shape[-1]
    g, p = s5_lambda_re.shape[1:]
    s5_width = g * S5_GROUP
    nheads = ssd_a_log.shape[1]
    ssd_width = nheads * SSD_HEAD_DIM
    xbc_w = ssd_conv_w.shape[-1]
    c0, c1, c2 = s5_width, s5_width + ssd_width, s5_width + ssd_width + xbc_w
    row = lambda v: v.astype(F32).reshape(1, -1)

    are, aim, bbre, bbim = _s5_params(s5_lambda_re[i], s5_lambda_im[i], s5_log_step[i],
                                      s5_b_re[i], s5_b_im[i])
    nblk = s5_width // LANES
    wb = jnp.concatenate([_block_diag(bbre.transpose(0, 2, 1), nblk),
                          _block_diag(bbim.transpose(0, 2, 1), nblk)], axis=-1).astype(BF16)
    wcre = _block_diag(s5_c_re[i].astype(F32).transpose(0, 2, 1), nblk).astype(BF16)
    wcim = _block_diag(s5_c_im[i].astype(F32).transpose(0, 2, 1), nblk).astype(BF16)

    win = w_in[i]
    w = {
        "ffn1_norm": row(ffn1_norm[i]), "mix_norm": row(mix_norm[i]),
        "ffn2_norm": row(ffn2_norm[i]), "final_norm": row(final_norm),
        "ffn1_wg": ffn1_w_gate[i].astype(BF16), "ffn1_wu": ffn1_w_up[i].astype(BF16),
        "ffn1_wd": ffn1_w_down[i].astype(BF16),
        "ffn2_wg": ffn2_w_gate[i].astype(BF16), "ffn2_wu": ffn2_w_up[i].astype(BF16),
        "ffn2_wd": ffn2_w_down[i].astype(BF16),
        "w_u": win[:, :c0].astype(BF16), "w_z": win[:, c0:c1].astype(BF16),
        "w_x": win[:, c1:c2].astype(BF16), "w_dt": _pad_lanes(win[:, c2:]).astype(BF16),
        "s5_wb": wb, "s5_wcre": wcre, "s5_wcim": wcim, "s5_are": are, "s5_aim": aim,
        "s5_d": row(s5_d[i]), "s5_wglu": s5_w_glu[i].astype(BF16), "s5_bglu": row(s5_b_glu[i]),
        "conv_w": ssd_conv_w[i].astype(F32), "conv_b": row(ssd_conv_b[i]),
        "dt_bias": _pad_lanes(row(ssd_dt_bias[i])), "a_log": _pad_lanes(row(ssd_a_log[i])),
        "ssd_d": jnp.repeat(ssd_d[i].astype(F32), SSD_HEAD_DIM).reshape(1, -1),
        "ssd_norm": row(ssd_norm[i]),
        "w_out5": w_out[i][:s5_width].astype(BF16), "w_outs": w_out[i][s5_width:].astype(BF16),
    }

    bp, lp = x_prompt.shape[:2]
    bs, ls = x_sample.shape[:2]
    zeros = lambda *s: jnp.zeros(s, F32)
    yp, p_re, p_im, p_ssd, p_conv = _layer(
        x_prompt, zeros(bp, g, p), zeros(bp, g, p),
        zeros(bp, nheads, SSD_HEAD_DIM, SSD_STATE), zeros(bp, SSD_CONV - 1, xbc_w), w,
        tm=512, s5_tl=min(lp, 128), ssd_nb=1, ssd_nch=min(lp // SSD_CHUNK, 4), ssd_t=SSD_CHUNK,
        ssd_t_real=SSD_CHUNK)
    ys, s_re, s_im, s_ssd, s_conv = _layer(
        x_sample, state_s5_re[i], state_s5_im[i], state_ssd[i], state_conv[i], w,
        tm=min(bs * ls, 512), s5_tl=ls, ssd_nb=8, ssd_nch=1, ssd_t=SUBLANES, ssd_t_real=ls)
    st = lambda v: v[None]
    return (yp, ys, st(p_re), st(p_im), st(p_ssd), st(p_conv),
            st(s_re), st(s_im), st(s_ssd), st(s_conv))
```

```python
import functools
import math

import jax
import jax.numpy as jnp
from jax import lax
from jax.experimental import pallas as pl
from jax.experimental.pallas import tpu as pltpu

F32 = jnp.float32
BF16 = jnp.bfloat16
EPS = 1e-6

LANES = 128
SUBLANES = 8
VMEM_LIMIT = 56 * 1024 * 1024

S5_GROUP = 16
S5_STATE = 64
SSD_HEAD_DIM = 64
SSD_GROUPS = 4
SSD_STATE = 128
SSD_CONV = 4
SSD_CHUNK = 128
CONV_PAD = SUBLANES


def _sigmoid(x):
    return 1.0 / (1.0 + jnp.exp(-x))


def _silu(x):
    return x * _sigmoid(x)


def _gelu_tanh(x):
    c = math.sqrt(2.0 / math.pi)
    return 0.5 * x * (1.0 + jnp.tanh(c * (x + 0.044715 * (x * x * x))))


def _softplus(x):
    return jnp.maximum(x, 0.0) + jnp.log1p(jnp.exp(-jnp.abs(x)))


def _rms(x, w):
    return x * lax.rsqrt(jnp.mean(x * x, axis=-1, keepdims=True) + EPS) * w


def _dot(a, b):
    return jnp.dot(a, b, preferred_element_type=F32)


def _dot_nt(a, b):
    return lax.dot_general(a, b, (((1,), (1,)), ((), ())), preferred_element_type=F32)


def _const_spec(shape):
    nd = len(shape)
    return pl.BlockSpec(shape, lambda *_: (0,) * nd, pipeline_mode=pl.Buffered(1))


def _params(sem):
    return pltpu.CompilerParams(dimension_semantics=sem, vmem_limit_bytes=VMEM_LIMIT)


def _swiglu_half(x, nw, wg_ref, wu_ref, wd_ref):
    hn = _rms(x, nw).astype(BF16)
    g = _dot(hn, wg_ref[...])
    u = _dot(hn, wu_ref[...])
    a = (_silu(g) * u).astype(BF16)
    return x + 0.5 * _dot(a, wd_ref[...])


def _ffn1_kernel(x_ref, nw_ref, wg_ref, wu_ref, wd_ref, mixw_ref, x1_ref, h_ref):
    x1 = _swiglu_half(x_ref[...], nw_ref[...], wg_ref, wu_ref, wd_ref)
    x1_ref[...] = x1
    h_ref[...] = _rms(x1, mixw_ref[...]).astype(BF16)


def _ffn1(x, nw, wg, wu, wd, mixw, tm):
    m, d = x.shape
    dff = wg.shape[1]
    row = lambda w: pl.BlockSpec((tm, w), lambda i: (i, 0))
    return pl.pallas_call(
        _ffn1_kernel,
        grid=(m // tm,),
        in_specs=[row(d), _const_spec((1, d)), _const_spec((d, dff)), _const_spec((d, dff)),
                  _const_spec((dff, d)), _const_spec((1, d))],
        out_specs=[row(d), row(d)],
        out_shape=[jax.ShapeDtypeStruct((m, d), F32), jax.ShapeDtypeStruct((m, d), BF16)],
        compiler_params=_params(("arbitrary",)),
        name="ffn1",
    )(x, nw, wg, wu, wd, mixw)


def _out_ffn2_kernel(x1_ref, o5_ref, os_ref, wo5_ref, wos_ref, nw_ref, wg_ref, wu_ref, wd_ref,
                     fin_ref, y_ref):
    x2 = x1_ref[...] + (_dot(o5_ref[...], wo5_ref[...]) + _dot(os_ref[...], wos_ref[...]))
    x3 = _swiglu_half(x2, nw_ref[...], wg_ref, wu_ref, wd_ref)
    y_ref[...] = _rms(x3, fin_ref[...])


def _out_ffn2(x1, o5, o_ssd, wo5, wos, nw, wg, wu, wd, fin, tm):
    m, d = x1.shape
    dff = wg.shape[1]
    row = lambda w: pl.BlockSpec((tm, w), lambda i: (i, 0))
    return pl.pallas_call(
        _out_ffn2_kernel,
        grid=(m // tm,),
        in_specs=[row(d), row(o5.shape[1]), row(o_ssd.shape[1]),
                  _const_spec(wo5.shape), _const_spec(wos.shape), _const_spec((1, d)),
                  _const_spec((d, dff)), _const_spec((d, dff)), _const_spec((dff, d)),
                  _const_spec((1, d))],
        out_specs=row(d),
        out_shape=jax.ShapeDtypeStruct((m, d), F32),
        compiler_params=_params(("arbitrary",)),
        name="out_ffn2",
    )(x1, o5, o_ssd, wo5, wos, nw, wg, wu, wd, fin)


def _s5_param_kernel(lre_ref, lim_ref, step_ref, bre_ref, bim_ref,
                     are_ref, aim_ref, bbre_ref, bbim_ref):
    lre, lim, step = lre_ref[...], lim_ref[...], step_ref[...]
    mag = jnp.exp(lre * step)
    ang = lim * step
    are = mag * jnp.cos(ang)
    aim = mag * jnp.sin(ang)
    den = lre * lre + lim * lim
    nre, nim = are - 1.0, aim
    cre = (nre * lre + nim * lim) / den
    cim = (nim * lre - nre * lim) / den
    are_ref[...] = are
    aim_ref[...] = aim
    bre, bim = bre_ref[...], bim_ref[...]
    bbre_ref[...] = cre * bre - cim * bim
    bbim_ref[...] = cre * bim + cim * bre


def _s5_params(lam_re, lam_im, log_step, b_re, b_im):
    g, p = lam_re.shape
    hh = b_re.shape[-1]
    gp = g * p
    step = jnp.exp(log_step.astype(F32))
    step_col = jnp.broadcast_to(step[:, None], (g, p)).reshape(gp, 1)
    full = lambda shape: pl.BlockSpec(shape, lambda: (0,) * len(shape))
    are, aim, bbre, bbim = pl.pallas_call(
        _s5_param_kernel,
        in_specs=[full((gp, 1)), full((gp, 1)), full((gp, 1)), full((gp, hh)), full((gp, hh))],
        out_specs=[full((gp, 1)), full((gp, 1)), full((gp, hh)), full((gp, hh))],
        out_shape=[jax.ShapeDtypeStruct((gp, 1), F32), jax.ShapeDtypeStruct((gp, 1), F32),
                   jax.ShapeDtypeStruct((gp, hh), F32), jax.ShapeDtypeStruct((gp, hh), F32)],
        name="s5_params",
    )(lam_re.astype(F32).reshape(gp, 1), lam_im.astype(F32).reshape(gp, 1), step_col,
      b_re.astype(F32).reshape(gp, hh), b_im.astype(F32).reshape(gp, hh))
    return (are.reshape(1, gp), aim.reshape(1, gp),
            bbre.reshape(g, p, hh), bbim.reshape(g, p, hh))


def _block_diag(w, nblk):
    g, a, b = w.shape
    gl = g // nblk
    w = w.reshape(nblk, gl, a, b)
    eye = jnp.eye(gl, dtype=w.dtype)
    out = w[:, :, :, None, :] * eye[None, :, None, :, None]
    return out.reshape(nblk, gl * a, gl * b)


def _s5_kernel(h_ref, wu_ref, wb_ref, wcre_ref, wcim_ref, are_ref, aim_ref, d_ref, wglu_ref,
               bglu_ref, s0re_ref, s0im_ref, o5_ref, sre_ref, sim_ref, u_s, xre_s, xim_s, y_s,
               *, nb, tl, lane_chunk):
    width = u_s.shape[1]
    nstate = xre_s.shape[1]
    nblk = wb_ref.shape[0]
    cin = width // nblk
    cst = nstate // nblk

    @pl.when(pl.program_id(0) == 0)
    def _():
        sre_ref[...] = s0re_ref[...]
        sim_ref[...] = s0im_ref[...]

    u = _dot(h_ref[...], wu_ref[...])
    u_s[...] = u
    ub = u.astype(BF16)
    for k in range(nblk):
        r = _dot(ub[:, k * cin:(k + 1) * cin], wb_ref[k])
        xre_s[:, k * cst:(k + 1) * cst] = r[:, :cst]
        xim_s[:, k * cst:(k + 1) * cst] = r[:, cst:]

    for c in range(nstate // lane_chunk):
        lanes = slice(c * lane_chunk, (c + 1) * lane_chunk)
        a_re = jnp.broadcast_to(are_ref[:, lanes], (SUBLANES, lane_chunk))
        a_im = jnp.broadcast_to(aim_ref[:, lanes], (SUBLANES, lane_chunk))

        def group_body(sg, carry, lanes=lanes, a_re=a_re, a_im=a_im):
            srow = pl.ds(pl.multiple_of(sg * SUBLANES, SUBLANES), SUBLANES)

            def step(t, st):
                s_re, s_im = st
                rows = pl.ds(pl.multiple_of(t * nb + sg * SUBLANES, SUBLANES), SUBLANES)
                n_re = a_re * s_re - a_im * s_im + xre_s[rows, lanes]
                n_im = a_re * s_im + a_im * s_re + xim_s[rows, lanes]
                xre_s[rows, lanes] = n_re
                xim_s[rows, lanes] = n_im
                return n_re, n_im

            s_re, s_im = lax.fori_loop(0, tl, step, (sre_ref[srow, lanes], sim_ref[srow, lanes]),
                                       unroll=min(tl, 8))
            sre_ref[srow, lanes] = s_re
            sim_ref[srow, lanes] = s_im
            return carry

        lax.fori_loop(0, nb // SUBLANES, group_body, 0)

    for k in range(nblk):
        st = slice(k * cst, (k + 1) * cst)
        y_s[:, k * cin:(k + 1) * cin] = (_dot(xre_s[:, st].astype(BF16), wcre_ref[k])
                                         - _dot(xim_s[:, st].astype(BF16), wcim_ref[k]))
    y = y_s[...] + d_ref[...] * u_s[...]
    v = _gelu_tanh(y)
    gate = _dot(v.astype(BF16), wglu_ref[...]) + bglu_ref[...]
    o5_ref[...] = (v * _sigmoid(gate)).astype(BF16)


def _s5(h_tm, wu, wb, wcre, wcim, are, aim, dsk, wglu, bglu, s0re, s0im, nb, tl):
    m, d = h_tm.shape
    width = wu.shape[1]
    nstate = are.shape[1]
    r = nb * tl
    row = lambda w: pl.BlockSpec((r, w), lambda i: (i, 0))
    kern = functools.partial(_s5_kernel, nb=nb, tl=tl, lane_chunk=4 * LANES)
    return pl.pallas_call(
        kern,
        grid=(m // r,),
        in_specs=[row(d), _const_spec(wu.shape), _const_spec(wb.shape), _const_spec(wcre.shape),
                  _const_spec(wcim.shape), _const_spec(are.shape), _const_spec(aim.shape),
                  _const_spec(dsk.shape), _const_spec(wglu.shape), _const_spec(bglu.shape),
                  _const_spec(s0re.shape), _const_spec(s0im.shape)],
        out_specs=[row(width), pl.BlockSpec((nb, nstate), lambda i: (0, 0)),
                   pl.BlockSpec((nb, nstate), lambda i: (0, 0))],
        out_shape=[jax.ShapeDtypeStruct((m, width), BF16),
                   jax.ShapeDtypeStruct((nb, nstate), F32),
                   jax.ShapeDtypeStruct((nb, nstate), F32)],
        scratch_shapes=[pltpu.VMEM((r, width), F32), pltpu.VMEM((r, nstate), F32),
                        pltpu.VMEM((r, nstate), F32), pltpu.VMEM((r, width), F32)],
        compiler_params=_params(("arbitrary",)),
        name="s5",
    )(h_tm, wu, wb, wcre, wcim, are, aim, dsk, wglu, bglu, s0re, s0im)


def _ssd_kernel(h_ref, wz_ref, wx_ref, wdt_ref, cw_ref, cb_ref, dtb_ref, alog_ref, dsk_ref,
                nw_ref, ssd0_ref, conv0_ref, o_ref, st_ref, cv_ref, xbc_s, xc_s, z_s, dt_s,
                *, nb, nch, t, t_real):
    r = nch * t
    nheads = st_ref.shape[1]
    hd_dim = st_ref.shape[2]
    nst = st_ref.shape[3]
    width = nheads * hd_dim
    hpg = nheads // SSD_GROUPS
    gw = width // SSD_GROUPS
    last = (nch - 1) * t + t_real

    z_s[...] = _dot(h_ref[...], wz_ref[...])
    dt_s[...] = _dot(h_ref[...], wdt_ref[...])

    @pl.when(pl.program_id(1) == 0)
    def _():
        st_ref[...] = ssd0_ref[...]
        for b in range(nb):
            xbc_s[b, CONV_PAD - (SSD_CONV - 1):CONV_PAD, :] = conv0_ref[b]

    xbc = _dot(h_ref[...], wx_ref[...])
    for b in range(nb):
        xbc_s[b, CONV_PAD:CONV_PAD + r, :] = xbc[b * r:(b + 1) * r, :]

    piece = min(r, SSD_CHUNK)
    cw = cw_ref[...]
    cbias = cb_ref[...]
    for b in range(nb):
        for i in range(r // piece):
            acc = cbias
            for k in range(SSD_CONV):
                off = CONV_PAD - (SSD_CONV - 1) + k + i * piece
                acc = acc + cw[k:k + 1, :] * xbc_s[b, off:off + piece, :]
            xc_s[b * r + i * piece:b * r + (i + 1) * piece, :] = _silu(acc)

    for b in range(nb):
        tail = xbc_s[b, CONV_PAD + last - (SSD_CONV - 1):CONV_PAD + last, :]
        cv_ref[b] = tail
        xbc_s[b, CONV_PAD - (SSD_CONV - 1):CONV_PAD, :] = tail

    rid = lax.broadcasted_iota(jnp.int32, (t, t), 0)
    cid = lax.broadcasted_iota(jnp.int32, (t, t), 1)
    causal = rid >= cid
    tril = jnp.where(causal, 1.0, 0.0).astype(BF16)
    a_neg = -jnp.exp(alog_ref[...])
    dtb = dtb_ref[...]
    dsk = dsk_ref[...]
    nw = nw_ref[...]

    for b in range(nb):
        def chunk_body(c, carry, b=b):
            rows = pl.ds(pl.multiple_of(b * r + c * t, SUBLANES), t)
            xc = xc_s[rows, :]
            xs = xc[:, :width]
            bmat = xc[:, width:width + SSD_GROUPS * nst]
            cmat = xc[:, width + SSD_GROUPS * nst:]
            dt = _softplus(dt_s[rows, :] + dtb)
            if t_real < t:
                dt = jnp.where(lax.broadcasted_iota(jnp.int32, dt.shape, 0) < t_real, dt, 0.0)
            la = dt * a_neg
            hi = la.astype(BF16)
            r1 = la - hi.astype(F32)
            mid = r1.astype(BF16)
            lo = (r1 - mid.astype(F32)).astype(BF16)
            acum = _dot(tril, hi) + _dot(tril, mid) + _dot(tril, lo)
            acum_t = acum.T
            tot = acum[t - 1:t, :]
            dstate = jnp.exp(tot - acum)
            eacum = jnp.exp(acum)
            cdec = jnp.exp(tot)
            ys = []
            for g in range(SSD_GROUPS):
                cg = cmat[:, g * nst:(g + 1) * nst].astype(BF16)
                bg = bmat[:, g * nst:(g + 1) * nst].astype(BF16)
                cb = _dot_nt(cg, bg)
                for jh in range(hpg):
                    hd = g * hpg + jh
                    diff = acum[:, hd:hd + 1] - acum_t[hd:hd + 1, :]
                    decay = jnp.exp(jnp.where(causal, diff, -jnp.inf))
                    xh = xs[:, hd * hd_dim:(hd + 1) * hd_dim]
                    xd = xh * dt[:, hd:hd + 1]
                    y = _dot((cb * decay).astype(BF16), xd.astype(BF16))
                    hst = st_ref[b, hd]
                    y = y + _dot_nt(cg, hst.astype(BF16)) * eacum[:, hd:hd + 1]
                    xdd = xd * dstate[:, hd:hd + 1]
                    st_ref[b, hd] = hst * cdec[:, hd:hd + 1] + _dot(xdd.T.astype(BF16), bg)
                    ys.append(y + dsk[:, hd * hd_dim:(hd + 1) * hd_dim] * xh)
            yall = jnp.concatenate(ys, axis=1) * _silu(z_s[rows, :])
            outs = []
            for g in range(SSD_GROUPS):
                seg = yall[:, g * gw:(g + 1) * gw]
                outs.append(seg * lax.rsqrt(jnp.mean(seg * seg, axis=-1, keepdims=True) + EPS))
            o_ref[rows, :] = (jnp.concatenate(outs, axis=1) * nw).astype(BF16)
            return carry
        lax.fori_loop(0, nch, chunk_body, 0)


def _ssd(h, wz, wx, wdt, cw, cb, dtb, alog, dsk, nw, ssd0, conv0, nb, nch, t, t_real):
    m, d = h.shape
    nseq, nheads, hd_dim, nst = ssd0.shape
    width = nheads * hd_dim
    xbc_w = wx.shape[1]
    r = nch * t
    nblk = m // (nseq * r)
    assert nb == 1 or nblk == 1
    row = lambda w: pl.BlockSpec((nb * r, w), lambda i, j: (i * nblk + j, 0))
    seq4 = pl.BlockSpec((nb, nheads, hd_dim, nst), lambda i, j: (i, 0, 0, 0))
    seq3 = pl.BlockSpec((nb, SSD_CONV - 1, xbc_w), lambda i, j: (i, 0, 0))
    kern = functools.partial(_ssd_kernel, nb=nb, nch=nch, t=t, t_real=t_real)
    return pl.pallas_call(
        kern,
        grid=(nseq // nb, nblk),
        in_specs=[row(d), _const_spec(wz.shape), _const_spec(wx.shape), _const_spec(wdt.shape),
                  _const_spec(cw.shape), _const_spec(cb.shape), _const_spec(dtb.shape),
                  _const_spec(alog.shape), _const_spec(dsk.shape), _const_spec(nw.shape),
                  seq4, seq3],
        out_specs=[row(width), seq4, seq3],
        out_shape=[jax.ShapeDtypeStruct((m, width), BF16),
                   jax.ShapeDtypeStruct(ssd0.shape, F32),
                   jax.ShapeDtypeStruct(conv0.shape, F32)],
        scratch_shapes=[pltpu.VMEM((nb, CONV_PAD + r, xbc_w), F32),
                        pltpu.VMEM((nb * r, xbc_w), F32),
                        pltpu.VMEM((nb * r, width), F32),
                        pltpu.VMEM((nb * r, LANES), F32)],
        compiler_params=_params(("arbitrary", "arbitrary")),
        name="ssd",
    )(h, wz, wx, wdt, cw, cb, dtb, alog, dsk, nw, ssd0, conv0)


def _split2(x):
    hi = x.astype(BF16)
    lo = (x - hi.astype(F32)).astype(BF16)
    return jnp.concatenate([hi, lo], axis=1)


def _split3(x):
    hi = x.astype(BF16)
    r1 = x - hi.astype(F32)
    mid = r1.astype(BF16)
    lo = (r1 - mid.astype(F32)).astype(BF16)
    return jnp.concatenate([hi, mid, lo], axis=1)


def _ssdp_kernel(h_ref, wz_ref, wx_ref, wdt_ref, cw_ref, cb_ref, dtb_ref, alog_ref, dsk_ref,
                 nw_ref, btril_ref, sel_ref, exp_ref, o_ref, st_ref, cv_ref,
                 xbc_s, zg_s, acol_s, acumt_s, e1_s, e2_s, e3_s, xd_s, xdd_s, ysk_s, b_s, c_s,
                 cdec_s, ht_s, *, nch, col_chunk):
    t = SSD_CHUNK
    r = nch * t
    j = pl.program_id(1)
    nheads, hd_dim, nst = st_ref.shape[1:]
    width = nheads * hd_dim
    gw = width // SSD_GROUPS
    hpg = nheads // SSD_GROUPS
    xbc_w = xbc_s.shape[1]
    hist = slice(CONV_PAD - (SSD_CONV - 1), CONV_PAD)

    @pl.when(j == 0)
    def _():
        ht_s[...] = jnp.zeros(ht_s.shape, F32)
        xbc_s[0:CONV_PAD, :] = jnp.zeros((CONV_PAD, xbc_w), F32)

    hb = h_ref[...]
    zg_s[...] = _silu(_dot(hb, wz_ref[...]))
    dt = _softplus(_dot(hb, wdt_ref[...]) + dtb_ref[...])
    la = dt * (-jnp.exp(alog_ref[...]))
    r3 = _dot(btril_ref[...], _split3(la))
    acum = r3[:, :LANES] + r3[:, LANES:2 * LANES] + r3[:, 2 * LANES:]
    tots = [acum[(c + 1) * t - 1:(c + 1) * t, :] for c in range(nch)]
    for c in range(nch):
        acumt_s[c] = acum[c * t:(c + 1) * t, :].T
    tot_rows = jnp.concatenate([jnp.broadcast_to(v, (t, LANES)) for v in tots], axis=0)
    dstate = jnp.exp(tot_rows - acum)
    cdec = jnp.exp(jnp.concatenate([jnp.broadcast_to(v, (SUBLANES, LANES)) for v in tots], axis=0))
    e1_s[...] = _dot(_split2(dt), exp_ref[...])
    e2_s[...] = _dot(_split2(dt * dstate), exp_ref[...])
    e3_s[...] = _dot(_split2(jnp.exp(acum)), exp_ref[...])
    cdx = _dot(_split2(cdec), exp_ref[...])
    for c in range(nch):
        cdec_s[c] = cdx[c * SUBLANES:(c + 1) * SUBLANES, :]
    acol_s[...] = _dot(_split3(acum), sel_ref[...])

    for jc in range(xbc_w // col_chunk):
        lanes = slice(jc * col_chunk, (jc + 1) * col_chunk)
        xbc_s[CONV_PAD:CONV_PAD + r, lanes] = _dot(hb, wx_ref[:, lanes])
        acc = cb_ref[:, lanes]
        for k in range(SSD_CONV):
            off = CONV_PAD - (SSD_CONV - 1) + k
            acc = acc + cw_ref[k:k + 1, lanes] * xbc_s[off:off + r, lanes]
        xc = _silu(acc)
        lo = jc * col_chunk
        if lo < width:
            xd_s[:, lanes] = (xc * e1_s[:, lanes]).astype(BF16)
            xdd_s[:, lanes] = (xc * e2_s[:, lanes]).astype(BF16)
            ysk_s[:, lanes] = xc * dsk_ref[:, lanes]
        elif lo < width + SSD_GROUPS * nst:
            b_s[:, lo - width:lo - width + col_chunk] = xc.astype(BF16)
        else:
            cl = lo - width - SSD_GROUPS * nst
            c_s[:, cl:cl + col_chunk] = xc.astype(BF16)

    tail = xbc_s[CONV_PAD + r - (SSD_CONV - 1):CONV_PAD + r, :]
    cv_ref[0] = tail
    xbc_s[hist, :] = tail

    causal = (lax.broadcasted_iota(jnp.int32, (t, t), 0)
              >= lax.broadcasted_iota(jnp.int32, (t, t), 1))
    lane = lax.broadcasted_iota(jnp.int32, (t, LANES), 1)
    keep_lo = jnp.where(lane < hd_dim, 1.0, 0.0).astype(BF16)
    keep_hi = jnp.where(lane < hd_dim, 0.0, 1.0).astype(BF16)
    nw = nw_ref[...]

    def chunk_body(c, carry):
        rows = pl.ds(pl.multiple_of(c * t, t), t)
        ys = []
        for g in range(SSD_GROUPS):
            gl = slice(g * gw, (g + 1) * gw)
            cg = c_s[rows, g * nst:(g + 1) * nst]
            bg = b_s[rows, g * nst:(g + 1) * nst]
            cbm = jnp.where(causal, _dot_nt(cg, bg), 0.0)
            pairs = []
            for jp in range(hpg // 2):
                ha = g * hpg + 2 * jp
                ms = []
                for hd in (ha, ha + 1):
                    diff = acol_s[rows, hd * t:(hd + 1) * t] - acumt_s[c, hd:hd + 1, :]
                    ms.append((cbm * jnp.exp(jnp.where(causal, diff, -jnp.inf))).astype(BF16))
                xdp = xd_s[rows, ha * hd_dim:ha * hd_dim + LANES]
                rhs = jnp.concatenate([xdp * keep_lo, xdp * keep_hi], axis=0)
                pairs.append(_dot(jnp.concatenate(ms, axis=1), rhs))
            htg = ht_s[:, gl]
            y = (jnp.concatenate(pairs, axis=1) + _dot(cg, htg.astype(BF16)) * e3_s[rows, gl]
                 + ysk_s[rows, gl])
            ys.append(y)
            upd = lax.dot_general(bg, xdd_s[rows, gl], (((0,), (0,)), ((), ())),
                                  preferred_element_type=F32)
            ht_s[:, gl] = htg * cdec_s[c, 0:1, gl] + upd
        outs = []
        for g in range(SSD_GROUPS):
            seg = ys[g] * zg_s[rows, g * gw:(g + 1) * gw]
            outs.append(seg * lax.rsqrt(jnp.mean(seg * seg, axis=-1, keepdims=True) + EPS))
        o_ref[rows, :] = (jnp.concatenate(outs, axis=1) * nw).astype(BF16)
        return carry

    lax.fori_loop(0, nch, chunk_body, 0)

    @pl.when(j == pl.num_programs(1) - 1)
    def _():
        for pr in range(nheads // 2):
            tt = ht_s[:, pr * LANES:(pr + 1) * LANES].T
            st_ref[0, 2 * pr] = tt[:hd_dim]
            st_ref[0, 2 * pr + 1] = tt[hd_dim:]


def _ssdp(h, wz, wx, wdt, cw, cb, dtb, alog, dsk, nw, nseq, nheads, nch):
    m, d = h.shape
    width = wz.shape[1]
    xbc_w = wx.shape[1]
    hd_dim = width // nheads
    nst = SSD_STATE
    t = SSD_CHUNK
    r = nch * t
    nblk = m // (nseq * r)
    ri = jnp.arange(r)
    btril = ((ri[:, None] >= ri[None, :]) & (ri[:, None] // t == ri[None, :] // t)).astype(BF16)
    head = jnp.arange(LANES)
    exp1 = (head[:, None] == (jnp.arange(width) // hd_dim)[None, :]).astype(BF16)
    sel1 = ((head[:, None] == (jnp.arange(nheads * t) // t)[None, :])).astype(BF16)
    exp_m = jnp.concatenate([exp1, exp1], axis=0)
    sel_m = jnp.concatenate([sel1, sel1, sel1], axis=0)
    row = lambda w: pl.BlockSpec((r, w), lambda i, j: (i * nblk + j, 0))
    kern = functools.partial(_ssdp_kernel, nch=nch, col_chunk=4 * LANES)
    vm = lambda shape, dt: pltpu.VMEM(shape, dt)
    return pl.pallas_call(
        kern,
        grid=(nseq, nblk),
        in_specs=[row(d), _const_spec(wz.shape), _const_spec(wx.shape), _const_spec(wdt.shape),
                  _const_spec(cw.shape), _const_spec(cb.shape), _const_spec(dtb.shape),
                  _const_spec(alog.shape), _const_spec(dsk.shape), _const_spec(nw.shape),
                  _const_spec(btril.shape), _const_spec(sel_m.shape), _const_spec(exp_m.shape)],
        out_specs=[row(width),
                   pl.BlockSpec((1, nheads, hd_dim, nst), lambda i, j: (i, 0, 0, 0)),
                   pl.BlockSpec((1, SSD_CONV - 1, xbc_w), lambda i, j: (i, 0, 0))],
        out_shape=[jax.ShapeDtypeStruct((m, width), BF16),
                   jax.ShapeDtypeStruct((nseq, nheads, hd_dim, nst), F32),
                   jax.ShapeDtypeStruct((nseq, SSD_CONV - 1, xbc_w), F32)],
        scratch_shapes=[vm((CONV_PAD + r, xbc_w), F32), vm((r, width), F32),
                        vm((r, nheads * t), F32), vm((nch, LANES, t), F32),
                        vm((r, width), F32), vm((r, width), F32), vm((r, width), F32),
                        vm((r, width), BF16), vm((r, width), BF16), vm((r, width), F32),
                        vm((r, SSD_GROUPS * nst), BF16), vm((r, SSD_GROUPS * nst), BF16),
                        vm((nch, SUBLANES, width), F32), vm((nst, width), F32)],
        compiler_params=_params(("arbitrary", "arbitrary")),
        name="ssd_prompt",
    )(h, wz, wx, wdt, cw, cb, dtb, alog, dsk, nw, btril, sel_m, exp_m)


def _pad_lanes(v, n=LANES):
    return jnp.pad(v, [(0, 0)] * (v.ndim - 1) + [(0, n - v.shape[-1])])


def _layer(x, s5re0, s5im0, ssd0, conv0, w, *, tm, s5_tl, ssd_nb, ssd_nch, ssd_t, ssd_t_real):
    bsz, seq, d = x.shape
    m = bsz * seq
    x1, h = _ffn1(x.reshape(m, d), w["ffn1_norm"], w["ffn1_wg"], w["ffn1_wu"], w["ffn1_wd"],
                  w["mix_norm"], tm)

    h_tm = h.reshape(bsz, seq, d).transpose(1, 0, 2).reshape(m, d)
    g, p = s5re0.shape[1:]
    o5_tm, s5re1, s5im1 = _s5(h_tm, w["w_u"], w["s5_wb"], w["s5_wcre"], w["s5_wcim"],
                              w["s5_are"], w["s5_aim"], w["s5_d"], w["s5_wglu"], w["s5_bglu"],
                              s5re0.reshape(bsz, g * p), s5im0.reshape(bsz, g * p), bsz, s5_tl)
    o5 = o5_tm.reshape(seq, bsz, -1).transpose(1, 0, 2).reshape(m, -1)

    if ssd0 is None:
        nheads = w["ssd_d"].shape[1] // SSD_HEAD_DIM
        o_ssd, ssd1, conv1 = _ssdp(h, w["w_z"], w["w_x"], w["w_dt"], w["conv_w"], w["conv_b"],
                                   w["dt_bias"], w["a_log"], w["ssd_d"], w["ssd_norm"],
                                   bsz, nheads, ssd_nch)
    else:
        pad_t = ssd_t - ssd_t_real if seq < ssd_t else 0
        hs = h
        if pad_t:
            hs = jnp.pad(h.reshape(bsz, seq, d), ((0, 0), (0, pad_t), (0, 0))).reshape(-1, d)
        o_ssd, ssd1, conv1 = _ssd(hs, w["w_z"], w["w_x"], w["w_dt"], w["conv_w"], w["conv_b"],
                                  w["dt_bias"], w["a_log"], w["ssd_d"], w["ssd_norm"], ssd0,
                                  conv0, ssd_nb, ssd_nch, ssd_t, ssd_t_real)
        if pad_t:
            o_ssd = o_ssd.reshape(bsz, seq + pad_t, -1)[:, :seq].reshape(m, -1)

    y = _out_ffn2(x1, o5, o_ssd, w["w_out5"], w["w_outs"], w["ffn2_norm"], w["ffn2_wg"],
                  w["ffn2_wu"], w["ffn2_wd"], w["final_norm"], tm)
    return (y.reshape(bsz, seq, d), s5re1.reshape(bsz, g, p), s5im1.reshape(bsz, g, p),
            ssd1, conv1)


def kernel(x_prompt, x_sample, state_s5_re, state_s5_im, state_ssd, state_conv, ffn1_norm, ffn1_w_gate, ffn1_w_up, ffn1_w_down, mix_norm, w_in, s5_lambda_re, s5_lambda_im, s5_log_step, s5_b_re, s5_b_im, s5_c_re, s5_c_im, s5_d, s5_w_glu, s5_b_glu, ssd_conv_w, ssd_conv_b, ssd_dt_bias, ssd_a_log, ssd_d, ssd_norm, w_out, ffn2_norm, ffn2_w_gate, ffn2_w_up, ffn2_w_down, final_norm):
    depth = w_in.shape[0]
    assert depth == 1, "single-layer stack"
    i = 0
    d = x_prompt.shape[-1]
    g, p = s5_lambda_re.shape[1:]
    s5_width = g * S5_GROUP
    nheads = ssd_a_log.shape[1]
    ssd_width = nheads * SSD_HEAD_DIM
    xbc_w = ssd_conv_w.shape[-1]
    c0, c1, c2 = s5_width, s5_width + ssd_width, s5_width + ssd_width + xbc_w
    row = lambda v: v.astype(F32).reshape(1, -1)

    are, aim, bbre, bbim = _s5_params(s5_lambda_re[i], s5_lambda_im[i], s5_log_step[i],
                                      s5_b_re[i], s5_b_im[i])
    nblk = s5_width // LANES
    wb = jnp.concatenate([_block_diag(bbre.transpose(0, 2, 1), nblk),
                          _block_diag(bbim.transpose(0, 2, 1), nblk)], axis=-1).astype(BF16)
    wcre = _block_diag(s5_c_re[i].astype(F32).transpose(0, 2, 1), nblk).astype(BF16)
    wcim = _block_diag(s5_c_im[i].astype(F32).transpose(0, 2, 1), nblk).astype(BF16)

    win = w_in[i]
    w = {
        "ffn1_norm": row(ffn1_norm[i]), "mix_norm": row(mix_norm[i]),
        "ffn2_norm": row(ffn2_norm[i]), "final_norm": row(final_norm),
        "ffn1_wg": ffn1_w_gate[i].astype(BF16), "ffn1_wu": ffn1_w_up[i].astype(BF16),
        "ffn1_wd": ffn1_w_down[i].astype(BF16),
        "ffn2_wg": ffn2_w_gate[i].astype(BF16), "ffn2_wu": ffn2_w_up[i].astype(BF16),
        "ffn2_wd": ffn2_w_down[i].astype(BF16),
        "w_u": win[:, :c0].astype(BF16), "w_z": win[:, c0:c1].astype(BF16),
        "w_x": win[:, c1:c2].astype(BF16), "w_dt": _pad_lanes(win[:, c2:]).astype(BF16),
        "s5_wb": wb, "s5_wcre": wcre, "s5_wcim": wcim, "s5_are": are, "s5_aim": aim,
        "s5_d": row(s5_d[i]), "s5_wglu": s5_w_glu[i].astype(BF16), "s5_bglu": row(s5_b_glu[i]),
        "conv_w": ssd_conv_w[i].astype(F32), "conv_b": row(ssd_conv_b[i]),
        "dt_bias": _pad_lanes(row(ssd_dt_bias[i])), "a_log": _pad_lanes(row(ssd_a_log[i])),
        "ssd_d": jnp.repeat(ssd_d[i].astype(F32), SSD_HEAD_DIM).reshape(1, -1),
        "ssd_norm": row(ssd_norm[i]),
        "w_out5": w_out[i][:s5_width].astype(BF16), "w_outs": w_out[i][s5_width:].astype(BF16),
    }

    bp, lp = x_prompt.shape[:2]
    bs, ls = x_sample.shape[:2]
    zeros = lambda *s: jnp.zeros(s, F32)
    yp, p_re, p_im, p_ssd, p_conv = _layer(
        x_prompt, zeros(bp, g, p), zeros(bp, g, p), None, None, w,
        tm=512, s5_tl=min(lp, 128), ssd_nb=1, ssd_nch=min(lp // SSD_CHUNK, 4), ssd_t=SSD_CHUNK,
        ssd_t_real=SSD_CHUNK)
    ys, s_re, s_im, s_ssd, s_conv = _layer(
        x_sample, state_s5_re[i], state_s5_im[i], state_ssd[i], state_conv[i], w,
        tm=min(bs * ls, 512), s5_tl=ls, ssd_nb=8, ssd_nch=1, ssd_t=SUBLANES, ssd_t_real=ls)
    st = lambda v: v[None]
    return (yp, ys, st(p_re), st(p_im), st(p_ssd), st(p_conv),
            st(s_re), st(s_im), st(s_ssd), st(s_conv))
```

```python
import functools
import math

import jax
import jax.numpy as jnp
from jax import lax
from jax.experimental import pallas as pl
from jax.experimental.pallas import tpu as pltpu

F32 = jnp.float32
BF16 = jnp.bfloat16
EPS = 1e-6

LANES = 128
SUBLANES = 8
VMEM_LIMIT = 56 * 1024 * 1024

S5_GROUP = 16
S5_STATE = 64
SSD_HEAD_DIM = 64
SSD_GROUPS = 4
SSD_STATE = 128
SSD_CONV = 4
SSD_CHUNK = 128
CONV_PAD = SUBLANES


def _sigmoid(x):
    return 1.0 / (1.0 + jnp.exp(-x))


def _silu(x):
    return x * _sigmoid(x)


def _gelu_tanh(x):
    c = math.sqrt(2.0 / math.pi)
    return 0.5 * x * (1.0 + jnp.tanh(c * (x + 0.044715 * (x * x * x))))


def _softplus(x):
    return jnp.maximum(x, 0.0) + jnp.log1p(jnp.exp(-jnp.abs(x)))


def _rms(x, w):
    return x * lax.rsqrt(jnp.mean(x * x, axis=-1, keepdims=True) + EPS) * w


def _dot(a, b):
    return jnp.dot(a, b, preferred_element_type=F32)


def _dot_nt(a, b):
    return lax.dot_general(a, b, (((1,), (1,)), ((), ())), preferred_element_type=F32)


def _const_spec(shape):
    nd = len(shape)
    return pl.BlockSpec(shape, lambda *_: (0,) * nd, pipeline_mode=pl.Buffered(1))


def _params(sem):
    return pltpu.CompilerParams(dimension_semantics=sem, vmem_limit_bytes=VMEM_LIMIT)


def _swiglu_half(x, nw, wg_ref, wu_ref, wd_ref):
    hn = _rms(x, nw).astype(BF16)
    g = _dot(hn, wg_ref[...])
    u = _dot(hn, wu_ref[...])
    a = (_silu(g) * u).astype(BF16)
    return x + 0.5 * _dot(a, wd_ref[...])


def _ffn1_kernel(x_ref, nw_ref, wg_ref, wu_ref, wd_ref, mixw_ref, x1_ref, h_ref):
    x1 = _swiglu_half(x_ref[...], nw_ref[...], wg_ref, wu_ref, wd_ref)
    x1_ref[...] = x1
    h_ref[...] = _rms(x1, mixw_ref[...]).astype(BF16)


def _ffn1(x, nw, wg, wu, wd, mixw, tm):
    m, d = x.shape
    dff = wg.shape[1]
    row = lambda w: pl.BlockSpec((tm, w), lambda i: (i, 0))
    return pl.pallas_call(
        _ffn1_kernel,
        grid=(m // tm,),
        in_specs=[row(d), _const_spec((1, d)), _const_spec((d, dff)), _const_spec((d, dff)),
                  _const_spec((dff, d)), _const_spec((1, d))],
        out_specs=[row(d), row(d)],
        out_shape=[jax.ShapeDtypeStruct((m, d), F32), jax.ShapeDtypeStruct((m, d), BF16)],
        compiler_params=_params(("arbitrary",)),
        name="ffn1",
    )(x, nw, wg, wu, wd, mixw)


def _out_ffn2_kernel(x1_ref, o5_ref, os_ref, wo5_ref, wos_ref, nw_ref, wg_ref, wu_ref, wd_ref,
                     fin_ref, y_ref):
    x2 = x1_ref[...] + (_dot(o5_ref[...].astype(BF16), wo5_ref[...])
                        + _dot(os_ref[...].astype(BF16), wos_ref[...]))
    x3 = _swiglu_half(x2, nw_ref[...], wg_ref, wu_ref, wd_ref)
    y_ref[...] = _rms(x3, fin_ref[...])


def _out_ffn2(x1, o5, o_ssd, wo5, wos, nw, wg, wu, wd, fin, tm):
    m, d = x1.shape
    dff = wg.shape[1]
    row = lambda w: pl.BlockSpec((tm, w), lambda i: (i, 0))
    return pl.pallas_call(
        _out_ffn2_kernel,
        grid=(m // tm,),
        in_specs=[row(d), row(o5.shape[1]), row(o_ssd.shape[1]),
                  _const_spec(wo5.shape), _const_spec(wos.shape), _const_spec((1, d)),
                  _const_spec((d, dff)), _const_spec((d, dff)), _const_spec((dff, d)),
                  _const_spec((1, d))],
        out_specs=row(d),
        out_shape=jax.ShapeDtypeStruct((m, d), F32),
        compiler_params=_params(("arbitrary",)),
        name="out_ffn2",
    )(x1, o5, o_ssd, wo5, wos, nw, wg, wu, wd, fin)


def _s5_param_kernel(lre_ref, lim_ref, step_ref, bre_ref, bim_ref,
                     are_ref, aim_ref, bbre_ref, bbim_ref):
    lre, lim, step = lre_ref[...], lim_ref[...], step_ref[...]
    mag = jnp.exp(lre * step)
    ang = lim * step
    are = mag * jnp.cos(ang)
    aim = mag * jnp.sin(ang)
    den = lre * lre + lim * lim
    nre, nim = are - 1.0, aim
    cre = (nre * lre + nim * lim) / den
    cim = (nim * lre - nre * lim) / den
    are_ref[...] = are
    aim_ref[...] = aim
    bre, bim = bre_ref[...], bim_ref[...]
    bbre_ref[...] = cre * bre - cim * bim
    bbim_ref[...] = cre * bim + cim * bre


def _s5_params(lam_re, lam_im, log_step, b_re, b_im):
    g, p = lam_re.shape
    hh = b_re.shape[-1]
    gp = g * p
    step = jnp.exp(log_step.astype(F32))
    step_col = jnp.broadcast_to(step[:, None], (g, p)).reshape(gp, 1)
    full = lambda shape: pl.BlockSpec(shape, lambda: (0,) * len(shape))
    are, aim, bbre, bbim = pl.pallas_call(
        _s5_param_kernel,
        in_specs=[full((gp, 1)), full((gp, 1)), full((gp, 1)), full((gp, hh)), full((gp, hh))],
        out_specs=[full((gp, 1)), full((gp, 1)), full((gp, hh)), full((gp, hh))],
        out_shape=[jax.ShapeDtypeStruct((gp, 1), F32), jax.ShapeDtypeStruct((gp, 1), F32),
                   jax.ShapeDtypeStruct((gp, hh), F32), jax.ShapeDtypeStruct((gp, hh), F32)],
        name="s5_params",
    )(lam_re.astype(F32).reshape(gp, 1), lam_im.astype(F32).reshape(gp, 1), step_col,
      b_re.astype(F32).reshape(gp, hh), b_im.astype(F32).reshape(gp, hh))
    return (are.reshape(1, gp), aim.reshape(1, gp),
            bbre.reshape(g, p, hh), bbim.reshape(g, p, hh))


def _block_diag(w, nblk):
    g, a, b = w.shape
    gl = g // nblk
    w = w.reshape(nblk, gl, a, b)
    eye = jnp.eye(gl, dtype=w.dtype)
    out = w[:, :, :, None, :] * eye[None, :, None, :, None]
    return out.reshape(nblk, gl * a, gl * b)


def _s5_kernel(h_ref, wu_ref, wb_ref, wcre_ref, wcim_ref, are_ref, aim_ref, d_ref, wglu_ref,
               bglu_ref, s0re_ref, s0im_ref, o5_ref, sre_ref, sim_ref, u_s, xre_s, xim_s, y_s,
               *, nb, tl, lane_chunk):
    width = u_s.shape[1]
    nstate = xre_s.shape[1]
    nblk = wb_ref.shape[0]
    cin = width // nblk
    cst = nstate // nblk

    @pl.when(pl.program_id(0) == 0)
    def _():
        sre_ref[...] = s0re_ref[...]
        sim_ref[...] = s0im_ref[...]

    u = _dot(h_ref[...], wu_ref[...])
    u_s[...] = u
    ub = u.astype(BF16)
    for k in range(nblk):
        r = _dot(ub[:, k * cin:(k + 1) * cin], wb_ref[k])
        xre_s[:, k * cst:(k + 1) * cst] = r[:, :cst]
        xim_s[:, k * cst:(k + 1) * cst] = r[:, cst:]

    for c in range(nstate // lane_chunk):
        lanes = slice(c * lane_chunk, (c + 1) * lane_chunk)
        a_re = jnp.broadcast_to(are_ref[:, lanes], (SUBLANES, lane_chunk))
        a_im = jnp.broadcast_to(aim_ref[:, lanes], (SUBLANES, lane_chunk))

        def group_body(sg, carry, lanes=lanes, a_re=a_re, a_im=a_im):
            srow = pl.ds(pl.multiple_of(sg * SUBLANES, SUBLANES), SUBLANES)

            def step(t, st):
                s_re, s_im = st
                rows = pl.ds(pl.multiple_of(t * nb + sg * SUBLANES, SUBLANES), SUBLANES)
                n_re = a_re * s_re - a_im * s_im + xre_s[rows, lanes]
                n_im = a_re * s_im + a_im * s_re + xim_s[rows, lanes]
                xre_s[rows, lanes] = n_re
                xim_s[rows, lanes] = n_im
                return n_re, n_im

            s_re, s_im = lax.fori_loop(0, tl, step, (sre_ref[srow, lanes], sim_ref[srow, lanes]),
                                       unroll=min(tl, 8))
            sre_ref[srow, lanes] = s_re
            sim_ref[srow, lanes] = s_im
            return carry

        lax.fori_loop(0, nb // SUBLANES, group_body, 0)

    for k in range(nblk):
        st = slice(k * cst, (k + 1) * cst)
        y_s[:, k * cin:(k + 1) * cin] = (_dot(xre_s[:, st].astype(BF16), wcre_ref[k])
                                         - _dot(xim_s[:, st].astype(BF16), wcim_ref[k]))
    y = y_s[...] + d_ref[...] * u_s[...]
    v = _gelu_tanh(y)
    gate = _dot(v.astype(BF16), wglu_ref[...]) + bglu_ref[...]
    o5_ref[...] = (v * _sigmoid(gate)).astype(BF16)


def _s5(h_tm, wu, wb, wcre, wcim, are, aim, dsk, wglu, bglu, s0re, s0im, nb, tl):
    m, d = h_tm.shape
    width = wu.shape[1]
    nstate = are.shape[1]
    r = nb * tl
    row = lambda w: pl.BlockSpec((r, w), lambda i: (i, 0))
    kern = functools.partial(_s5_kernel, nb=nb, tl=tl, lane_chunk=4 * LANES)
    return pl.pallas_call(
        kern,
        grid=(m // r,),
        in_specs=[row(d), _const_spec(wu.shape), _const_spec(wb.shape), _const_spec(wcre.shape),
                  _const_spec(wcim.shape), _const_spec(are.shape), _const_spec(aim.shape),
                  _const_spec(dsk.shape), _const_spec(wglu.shape), _const_spec(bglu.shape),
                  _const_spec(s0re.shape), _const_spec(s0im.shape)],
        out_specs=[row(width), pl.BlockSpec((nb, nstate), lambda i: (0, 0)),
                   pl.BlockSpec((nb, nstate), lambda i: (0, 0))],
        out_shape=[jax.ShapeDtypeStruct((m, width), BF16),
                   jax.ShapeDtypeStruct((nb, nstate), F32),
                   jax.ShapeDtypeStruct((nb, nstate), F32)],
        scratch_shapes=[pltpu.VMEM((r, width), F32), pltpu.VMEM((r, nstate), F32),
                        pltpu.VMEM((r, nstate), F32), pltpu.VMEM((r, width), F32)],
        compiler_params=_params(("arbitrary",)),
        name="s5",
    )(h_tm, wu, wb, wcre, wcim, are, aim, dsk, wglu, bglu, s0re, s0im)


def _split2(x):
    hi = x.astype(BF16)
    lo = (x - hi.astype(F32)).astype(BF16)
    return jnp.concatenate([hi, lo], axis=1)


def _split3(x):
    hi = x.astype(BF16)
    r1 = x - hi.astype(F32)
    mid = r1.astype(BF16)
    lo = (r1 - mid.astype(F32)).astype(BF16)
    return jnp.concatenate([hi, mid, lo], axis=1)


def _group_norm(y, zg, nw, gw):
    outs = []
    for g in range(SSD_GROUPS):
        seg = y[:, g * gw:(g + 1) * gw] * zg[:, g * gw:(g + 1) * gw]
        outs.append(seg * lax.rsqrt(jnp.mean(seg * seg, axis=-1, keepdims=True) + EPS))
    return jnp.concatenate(outs, axis=1) * nw


def _ssds_pre_kernel(h_ref, wz_ref, wx_ref, wdt_ref, cw_ref, cb_ref, dtb_ref, alog_ref, dsk_ref,
                     conv0_ref, g2_ref, exp_ref, c_o, b_o, xdd_o, yp_o, eac_o, zg_o, cdec_o, cv_o):
    nbt = conv0_ref.shape[1]
    seq = h_ref.shape[0] // nbt
    width = zg_o.shape[1]
    nbc = c_o.shape[1]
    hb = h_ref[...]
    zg_o[...] = _silu(_dot(hb, wz_ref[...]))
    dt = _softplus(_dot(hb, wdt_ref[...]) + dtb_ref[...])
    la = dt * (-jnp.exp(alog_ref[...]))
    xbc = _dot(hb, wx_ref[...])
    rows = [slice(t * nbt, (t + 1) * nbt) for t in range(seq)]
    full = [conv0_ref[k] for k in range(SSD_CONV - 1)] + [xbc[r] for r in rows]
    for k in range(SSD_CONV - 1):
        cv_o[k] = full[seq + k]
    acums = []
    for t in range(seq):
        acums.append(la[rows[t]] if t == 0 else acums[-1] + la[rows[t]])
    tot = acums[-1]
    cdec_o[...] = jnp.exp(tot)
    xs, bq, cq = [], [], []
    for t in range(seq):
        acc = cb_ref[...]
        for k in range(SSD_CONV):
            acc = acc + cw_ref[k:k + 1, :] * full[t + k]
        xc = _silu(acc)
        xs.append(xc[:, :width])
        bq.append(xc[:, width:width + nbc].astype(BF16).astype(F32))
        cq.append(xc[:, width + nbc:].astype(BF16).astype(F32))
    for t in range(seq):
        r = rows[t]
        b_o[r, :] = bq[t]
        c_o[r, :] = cq[t]
        xdd_o[r, :] = xs[t] * _dot(_split2(dt[r] * jnp.exp(tot - acums[t])), exp_ref[...])
        eac_o[r, :] = _dot(_split2(jnp.exp(acums[t])), exp_ref[...])
        yp = dsk_ref[...] * xs[t]
        for s in range(t + 1):
            cbx = _dot(_split2(cq[t] * bq[s]), g2_ref[...])
            coef = cbx * jnp.exp(acums[t] - acums[s]) * dt[rows[s]]
            yp = yp + _dot(_split2(coef), exp_ref[...]) * xs[s]
        yp_o[r, :] = yp


def _ssds_state_kernel(c_ref, b_ref, xdd_ref, yp_ref, eac_ref, zg_ref, cdec_ref, st_ref, nw_ref,
                       o_ref, sto_ref, yoff_s, xddt_s):
    seq, nb, width = xdd_ref.shape
    nheads, hd_dim, nst = st_ref.shape[1:]
    hpg = nheads // SSD_GROUPS
    gw = width // SSD_GROUPS
    rws = seq * nb
    cst = c_ref[...].reshape(rws, c_ref.shape[2]).astype(BF16)
    bst = b_ref[...].reshape(rws, b_ref.shape[2])
    xdd = xdd_ref[...].reshape(rws, width)
    xddt_s[...] = jnp.concatenate([xdd, jnp.zeros((LANES - rws, width), F32)], axis=0).T.astype(BF16)
    rowb = lax.broadcasted_iota(jnp.int32, (rws, gw), 0) % nb
    zpad = jnp.zeros((LANES - rws, nst), BF16)
    for b in range(nb):
        mine = rowb == b
        for g in range(SSD_GROUPS):
            gl = slice(g * gw, (g + 1) * gw)
            hg = st_ref[b, g * hpg:(g + 1) * hpg].reshape(gw, nst)
            res = _dot_nt(cst[:, g * nst:(g + 1) * nst], hg.astype(BF16))
            yoff_s[:, gl] = jnp.where(mine, res, 0.0) if b == 0 else jnp.where(mine, res, yoff_s[:, gl])
            bm = jnp.where(mine[:, :nst], bst[:, g * nst:(g + 1) * nst], 0.0).astype(BF16)
            upd = _dot(xddt_s[gl, :], jnp.concatenate([bm, zpad], axis=0))
            for jh in range(hpg):
                hd = g * hpg + jh
                hrows = slice(jh * hd_dim, (jh + 1) * hd_dim)
                sto_ref[b, hd] = hg[hrows] * cdec_ref[b, hd] + upd[hrows]
    y = yp_ref[...].reshape(rws, width) + yoff_s[...] * eac_ref[...].reshape(rws, width)
    o = _group_norm(y, zg_ref[...].reshape(rws, width), nw_ref[...], gw)
    o_ref[...] = o.reshape(seq, nb, width)


def _ssds(h_tm, wz, wx, wdt, cw, cb, dtb, alog, dsk, nw, ssd0, conv0_tm, nb):
    m, d = h_tm.shape
    nseq, nheads, hd_dim, nst = ssd0.shape
    seq = m // nseq
    width = wz.shape[1]
    xbc_w = wx.shape[1]
    nbc = SSD_GROUPS * nst
    hpg = nheads // SSD_GROUPS
    head = jnp.arange(LANES)
    exp1 = (head[:, None] == (jnp.arange(width) // hd_dim)[None, :]).astype(BF16)
    g1 = ((jnp.arange(nbc) // nst)[:, None] == (head // hpg)[None, :]) & (head < nheads)[None, :]
    exp_m = jnp.concatenate([exp1, exp1], axis=0)
    g2 = jnp.concatenate([g1, g1], axis=0).astype(BF16)
    full = lambda shape: pl.BlockSpec(shape, lambda: (0,) * len(shape))
    ins = (h_tm, wz, wx, wdt, cw, cb, dtb, alog, dsk, conv0_tm, g2, exp_m)
    outs = [jax.ShapeDtypeStruct((m, nbc), F32), jax.ShapeDtypeStruct((m, nbc), F32),
            jax.ShapeDtypeStruct((m, width), F32), jax.ShapeDtypeStruct((m, width), F32),
            jax.ShapeDtypeStruct((m, width), F32), jax.ShapeDtypeStruct((m, width), F32),
            jax.ShapeDtypeStruct((nseq, LANES), F32),
            jax.ShapeDtypeStruct((SSD_CONV - 1, nseq, xbc_w), F32)]
    c, b, xdd, yp, eac, zg, cdec, conv1_tm = pl.pallas_call(
        _ssds_pre_kernel,
        in_specs=[full(a.shape) for a in ins],
        out_specs=[full(o.shape) for o in outs],
        out_shape=outs,
        compiler_params=pltpu.CompilerParams(vmem_limit_bytes=VMEM_LIMIT),
        name="ssd_sample_pre",
    )(*ins)

    tm3 = lambda a: a.reshape(seq, nseq, a.shape[1])
    blk = lambda w: pl.BlockSpec((seq, nb, w), lambda i: (0, i, 0))
    st_spec = pl.BlockSpec((nb, nheads, hd_dim, nst), lambda i: (i, 0, 0, 0))
    o, ssd1 = pl.pallas_call(
        _ssds_state_kernel,
        grid=(nseq // nb,),
        in_specs=[blk(nbc), blk(nbc), blk(width), blk(width), blk(width), blk(width),
                  pl.BlockSpec((nb, LANES), lambda i: (i, 0), memory_space=pltpu.SMEM),
                  st_spec, _const_spec(nw.shape)],
        out_specs=[blk(width), st_spec],
        out_shape=[jax.ShapeDtypeStruct((seq, nseq, width), F32),
                   jax.ShapeDtypeStruct(ssd0.shape, F32)],
        scratch_shapes=[pltpu.VMEM((seq * nb, width), F32), pltpu.VMEM((width, LANES), BF16)],
        compiler_params=_params(("arbitrary",)),
        name="ssd_sample_state",
    )(tm3(c), tm3(b), tm3(xdd), tm3(yp), tm3(eac), tm3(zg), cdec, ssd0, nw)
    return o.reshape(m, width), ssd1, conv1_tm


def _ssdp_kernel(h_ref, wz_ref, wx_ref, wdt_ref, cw_ref, cb_ref, dtb_ref, alog_ref, dsk_ref,
                 nw_ref, btril_ref, sel_ref, exp_ref, o_ref, st_ref, cv_ref,
                 xbc_s, zg_s, acol_s, acumt_s, e1_s, e2_s, e3_s, xd_s, xdd_s, ysk_s, b_s, c_s,
                 cdec_s, ht_s, *, nch, col_chunk):
    t = SSD_CHUNK
    r = nch * t
    j = pl.program_id(1)
    nheads, hd_dim, nst = st_ref.shape[1:]
    width = nheads * hd_dim
    gw = width // SSD_GROUPS
    hpg = nheads // SSD_GROUPS
    xbc_w = xbc_s.shape[1]
    hist = slice(CONV_PAD - (SSD_CONV - 1), CONV_PAD)

    @pl.when(j == 0)
    def _():
        ht_s[...] = jnp.zeros(ht_s.shape, F32)
        xbc_s[0:CONV_PAD, :] = jnp.zeros((CONV_PAD, xbc_w), F32)

    hb = h_ref[...]
    zg_s[...] = _silu(_dot(hb, wz_ref[...]))
    dt = _softplus(_dot(hb, wdt_ref[...]) + dtb_ref[...])
    la = dt * (-jnp.exp(alog_ref[...]))
    r3 = _dot(btril_ref[...], _split3(la))
    acum = r3[:, :LANES] + r3[:, LANES:2 * LANES] + r3[:, 2 * LANES:]
    tots = [acum[(c + 1) * t - 1:(c + 1) * t, :] for c in range(nch)]
    for c in range(nch):
        acumt_s[c] = acum[c * t:(c + 1) * t, :].T
    tot_rows = jnp.concatenate([jnp.broadcast_to(v, (t, LANES)) for v in tots], axis=0)
    dstate = jnp.exp(tot_rows - acum)
    cdec = jnp.exp(jnp.concatenate([jnp.broadcast_to(v, (SUBLANES, LANES)) for v in tots], axis=0))
    e1_s[...] = _dot(_split2(dt), exp_ref[...])
    e2_s[...] = _dot(_split2(dt * dstate), exp_ref[...])
    e3_s[...] = _dot(_split2(jnp.exp(acum)), exp_ref[...])
    cdx = _dot(_split2(cdec), exp_ref[...])
    for c in range(nch):
        cdec_s[c] = cdx[c * SUBLANES:(c + 1) * SUBLANES, :]
    acol_s[...] = _dot(_split3(acum), sel_ref[...])

    for jc in range(xbc_w // col_chunk):
        lanes = slice(jc * col_chunk, (jc + 1) * col_chunk)
        xbc_s[CONV_PAD:CONV_PAD + r, lanes] = _dot(hb, wx_ref[:, lanes])
        acc = cb_ref[:, lanes]
        for k in range(SSD_CONV):
            off = CONV_PAD - (SSD_CONV - 1) + k
            acc = acc + cw_ref[k:k + 1, lanes] * xbc_s[off:off + r, lanes]
        xc = _silu(acc)
        lo = jc * col_chunk
        if lo < width:
            xd_s[:, lanes] = (xc * e1_s[:, lanes]).astype(BF16)
            xdd_s[:, lanes] = (xc * e2_s[:, lanes]).astype(BF16)
            ysk_s[:, lanes] = xc * dsk_ref[:, lanes]
        elif lo < width + SSD_GROUPS * nst:
            b_s[:, lo - width:lo - width + col_chunk] = xc.astype(BF16)
        else:
            cl = lo - width - SSD_GROUPS * nst
            c_s[:, cl:cl + col_chunk] = xc.astype(BF16)

    tail = xbc_s[CONV_PAD + r - (SSD_CONV - 1):CONV_PAD + r, :]
    cv_ref[0] = tail
    xbc_s[hist, :] = tail

    causal = (lax.broadcasted_iota(jnp.int32, (t, t), 0)
              >= lax.broadcasted_iota(jnp.int32, (t, t), 1))
    lane = lax.broadcasted_iota(jnp.int32, (t, LANES), 1)
    keep_lo = jnp.where(lane < hd_dim, 1.0, 0.0).astype(BF16)
    keep_hi = jnp.where(lane < hd_dim, 0.0, 1.0).astype(BF16)
    nw = nw_ref[...]

    def chunk_body(c, carry):
        rows = pl.ds(pl.multiple_of(c * t, t), t)
        ys = []
        for g in range(SSD_GROUPS):
            gl = slice(g * gw, (g + 1) * gw)
            cg = c_s[rows, g * nst:(g + 1) * nst]
            bg = b_s[rows, g * nst:(g + 1) * nst]
            cbm = jnp.where(causal, _dot_nt(cg, bg), 0.0)
            pairs = []
            for jp in range(hpg // 2):
                ha = g * hpg + 2 * jp
                ms = []
                for hd in (ha, ha + 1):
                    diff = acol_s[rows, hd * t:(hd + 1) * t] - acumt_s[c, hd:hd + 1, :]
                    ms.append((cbm * jnp.exp(jnp.where(causal, diff, -jnp.inf))).astype(BF16))
                xdp = xd_s[rows, ha * hd_dim:ha * hd_dim + LANES]
                rhs = jnp.concatenate([xdp * keep_lo, xdp * keep_hi], axis=0)
                pairs.append(_dot(jnp.concatenate(ms, axis=1), rhs))
            htg = ht_s[:, gl]
            y = (jnp.concatenate(pairs, axis=1) + _dot(cg, htg.astype(BF16)) * e3_s[rows, gl]
                 + ysk_s[rows, gl])
            ys.append(y)
            upd = lax.dot_general(bg, xdd_s[rows, gl], (((0,), (0,)), ((), ())),
                                  preferred_element_type=F32)
            ht_s[:, gl] = htg * cdec_s[c, 0:1, gl] + upd
        o_ref[rows, :] = _group_norm(jnp.concatenate(ys, axis=1), zg_s[rows, :], nw,
                                     gw).astype(BF16)
        return carry

    lax.fori_loop(0, nch, chunk_body, 0)

    @pl.when(j == pl.num_programs(1) - 1)
    def _():
        for pr in range(nheads // 2):
            tt = ht_s[:, pr * LANES:(pr + 1) * LANES].T
            st_ref[0, 2 * pr] = tt[:hd_dim]
            st_ref[0, 2 * pr + 1] = tt[hd_dim:]


def _ssdp(h, wz, wx, wdt, cw, cb, dtb, alog, dsk, nw, nseq, nheads, nch):
    m, d = h.shape
    width = wz.shape[1]
    xbc_w = wx.shape[1]
    hd_dim = width // nheads
    nst = SSD_STATE
    t = SSD_CHUNK
    r = nch * t
    nblk = m // (nseq * r)
    ri = jnp.arange(r)
    btril = ((ri[:, None] >= ri[None, :]) & (ri[:, None] // t == ri[None, :] // t)).astype(BF16)
    head = jnp.arange(LANES)
    exp1 = (head[:, None] == (jnp.arange(width) // hd_dim)[None, :]).astype(BF16)
    sel1 = ((head[:, None] == (jnp.arange(nheads * t) // t)[None, :])).astype(BF16)
    exp_m = jnp.concatenate([exp1, exp1], axis=0)
    sel_m = jnp.concatenate([sel1, sel1, sel1], axis=0)
    row = lambda w: pl.BlockSpec((r, w), lambda i, j: (i * nblk + j, 0))
    kern = functools.partial(_ssdp_kernel, nch=nch, col_chunk=4 * LANES)
    vm = lambda shape, dt: pltpu.VMEM(shape, dt)
    return pl.pallas_call(
        kern,
        grid=(nseq, nblk),
        in_specs=[row(d), _const_spec(wz.shape), _const_spec(wx.shape), _const_spec(wdt.shape),
                  _const_spec(cw.shape), _const_spec(cb.shape), _const_spec(dtb.shape),
                  _const_spec(alog.shape), _const_spec(dsk.shape), _const_spec(nw.shape),
                  _const_spec(btril.shape), _const_spec(sel_m.shape), _const_spec(exp_m.shape)],
        out_specs=[row(width),
                   pl.BlockSpec((1, nheads, hd_dim, nst), lambda i, j: (i, 0, 0, 0)),
                   pl.BlockSpec((1, SSD_CONV - 1, xbc_w), lambda i, j: (i, 0, 0))],
        out_shape=[jax.ShapeDtypeStruct((m, width), BF16),
                   jax.ShapeDtypeStruct((nseq, nheads, hd_dim, nst), F32),
                   jax.ShapeDtypeStruct((nseq, SSD_CONV - 1, xbc_w), F32)],
        scratch_shapes=[vm((CONV_PAD + r, xbc_w), F32), vm((r, width), F32),
                        vm((r, nheads * t), F32), vm((nch, LANES, t), F32),
                        vm((r, width), F32), vm((r, width), F32), vm((r, width), F32),
                        vm((r, width), BF16), vm((r, width), BF16), vm((r, width), F32),
                        vm((r, SSD_GROUPS * nst), BF16), vm((r, SSD_GROUPS * nst), BF16),
                        vm((nch, SUBLANES, width), F32), vm((nst, width), F32)],
        compiler_params=_params(("arbitrary", "arbitrary")),
        name="ssd_prompt",
    )(h, wz, wx, wdt, cw, cb, dtb, alog, dsk, nw, btril, sel_m, exp_m)


def _pad_lanes(v, n=LANES):
    return jnp.pad(v, [(0, 0)] * (v.ndim - 1) + [(0, n - v.shape[-1])])


def _s5_call(h_tm, s5re0, s5im0, w, nb, tl):
    g, p = s5re0.shape[1:]
    o5_tm, re1, im1 = _s5(h_tm, w["w_u"], w["s5_wb"], w["s5_wcre"], w["s5_wcim"], w["s5_are"],
                          w["s5_aim"], w["s5_d"], w["s5_wglu"], w["s5_bglu"],
                          s5re0.reshape(nb, g * p), s5im0.reshape(nb, g * p), nb, tl)
    return o5_tm, re1.reshape(nb, g, p), im1.reshape(nb, g, p)


def _ffn1_call(x2d, w, tm):
    return _ffn1(x2d, w["ffn1_norm"], w["ffn1_wg"], w["ffn1_wu"], w["ffn1_wd"], w["mix_norm"], tm)


def _out_call(x1, o5, o_ssd, w, tm):
    return _out_ffn2(x1, o5, o_ssd, w["w_out5"], w["w_outs"], w["ffn2_norm"], w["ffn2_wg"],
                     w["ffn2_wu"], w["ffn2_wd"], w["final_norm"], tm)


def _prompt_layer(x, w, *, tm, s5_tl, ssd_nch):
    bsz, seq, d = x.shape
    m = bsz * seq
    g, p = w["s5_are"].shape[1] // S5_STATE, S5_STATE
    x1, h = _ffn1_call(x.reshape(m, d), w, tm)
    h_tm = h.reshape(bsz, seq, d).transpose(1, 0, 2).reshape(m, d)
    zeros = jnp.zeros((bsz, g, p), F32)
    o5_tm, s5re1, s5im1 = _s5_call(h_tm, zeros, zeros, w, bsz, s5_tl)
    o5 = o5_tm.reshape(seq, bsz, -1).transpose(1, 0, 2).reshape(m, -1)
    nheads = w["ssd_d"].shape[1] // SSD_HEAD_DIM
    o_ssd, ssd1, conv1 = _ssdp(h, w["w_z"], w["w_x"], w["w_dt"], w["conv_w"], w["conv_b"],
                               w["dt_bias"], w["a_log"], w["ssd_d"], w["ssd_norm"],
                               bsz, nheads, ssd_nch)
    y = _out_call(x1, o5, o_ssd, w, tm)
    return y.reshape(bsz, seq, d), s5re1, s5im1, ssd1, conv1


def _sample_layer(x, s5re0, s5im0, ssd0, conv0, w, *, ssd_nb):
    bsz, seq, d = x.shape
    m = bsz * seq
    x_tm = x.transpose(1, 0, 2).reshape(m, d)
    x1, h = _ffn1_call(x_tm, w, m)
    o5, s5re1, s5im1 = _s5_call(h, s5re0, s5im0, w, bsz, seq)
    o_ssd, ssd1, conv1_tm = _ssds(h, w["w_z"], w["w_x"], w["w_dt"], w["conv_w"], w["conv_b"],
                                  w["dt_bias"], w["a_log"], w["ssd_d"], w["ssd_norm"], ssd0,
                                  conv0.transpose(1, 0, 2), ssd_nb)
    y = _out_call(x1, o5, o_ssd, w, m)
    return (y.reshape(seq, bsz, d).transpose(1, 0, 2), s5re1, s5im1, ssd1,
            conv1_tm.transpose(1, 0, 2))


def kernel(x_prompt, x_sample, state_s5_re, state_s5_im, state_ssd, state_conv, ffn1_norm, ffn1_w_gate, ffn1_w_up, ffn1_w_down, mix_norm, w_in, s5_lambda_re, s5_lambda_im, s5_log_step, s5_b_re, s5_b_im, s5_c_re, s5_c_im, s5_d, s5_w_glu, s5_b_glu, ssd_conv_w, ssd_conv_b, ssd_dt_bias, ssd_a_log, ssd_d, ssd_norm, w_out, ffn2_norm, ffn2_w_gate, ffn2_w_up, ffn2_w_down, final_norm):
    depth = w_in.shape[0]
    assert depth == 1, "single-layer stack"
    i = 0
    d = x_prompt.shape[-1]
    g, p = s5_lambda_re.shape[1:]
    s5_width = g * S5_GROUP
    nheads = ssd_a_log.shape[1]
    ssd_width = nheads * SSD_HEAD_DIM
    xbc_w = ssd_conv_w.shape[-1]
    c0, c1, c2 = s5_width, s5_width + ssd_width, s5_width + ssd_width + xbc_w
    row = lambda v: v.astype(F32).reshape(1, -1)

    are, aim, bbre, bbim = _s5_params(s5_lambda_re[i], s5_lambda_im[i], s5_log_step[i],
                                      s5_b_re[i], s5_b_im[i])
    nblk = s5_width // LANES
    wb = jnp.concatenate([_block_diag(bbre.transpose(0, 2, 1), nblk),
                          _block_diag(bbim.transpose(0, 2, 1), nblk)], axis=-1).astype(BF16)
    wcre = _block_diag(s5_c_re[i].astype(F32).transpose(0, 2, 1), nblk).astype(BF16)
    wcim = _block_diag(s5_c_im[i].astype(F32).transpose(0, 2, 1), nblk).astype(BF16)

    win = w_in[i]
    w = {
        "ffn1_norm": row(ffn1_norm[i]), "mix_norm": row(mix_norm[i]),
        "ffn2_norm": row(ffn2_norm[i]), "final_norm": row(final_norm),
        "ffn1_wg": ffn1_w_gate[i].astype(BF16), "ffn1_wu": ffn1_w_up[i].astype(BF16),
        "ffn1_wd": ffn1_w_down[i].astype(BF16),
        "ffn2_wg": ffn2_w_gate[i].astype(BF16), "ffn2_wu": ffn2_w_up[i].astype(BF16),
        "ffn2_wd": ffn2_w_down[i].astype(BF16),
        "w_u": win[:, :c0].astype(BF16), "w_z": win[:, c0:c1].astype(BF16),
        "w_x": win[:, c1:c2].astype(BF16), "w_dt": _pad_lanes(win[:, c2:]).astype(BF16),
        "s5_wb": wb, "s5_wcre": wcre, "s5_wcim": wcim, "s5_are": are, "s5_aim": aim,
        "s5_d": row(s5_d[i]), "s5_wglu": s5_w_glu[i].astype(BF16), "s5_bglu": row(s5_b_glu[i]),
        "conv_w": ssd_conv_w[i].astype(F32), "conv_b": row(ssd_conv_b[i]),
        "dt_bias": _pad_lanes(row(ssd_dt_bias[i])), "a_log": _pad_lanes(row(ssd_a_log[i])),
        "ssd_d": jnp.repeat(ssd_d[i].astype(F32), SSD_HEAD_DIM).reshape(1, -1),
        "ssd_norm": row(ssd_norm[i]),
        "w_out5": w_out[i][:s5_width].astype(BF16), "w_outs": w_out[i][s5_width:].astype(BF16),
    }

    lp = x_prompt.shape[1]
    yp, p_re, p_im, p_ssd, p_conv = _prompt_layer(
        x_prompt, w, tm=512, s5_tl=min(lp, 128), ssd_nch=min(lp // SSD_CHUNK, 4))
    ys, s_re, s_im, s_ssd, s_conv = _sample_layer(
        x_sample, state_s5_re[i], state_s5_im[i], state_ssd[i], state_conv[i], w,
        ssd_nb=SUBLANES)
    st = lambda v: v[None]
    return (yp, ys, st(p_re), st(p_im), st(p_ssd), st(p_conv),
            st(s_re), st(s_im), st(s_ssd), st(s_conv))
```

```python
import functools
import math

import jax
import jax.numpy as jnp
from jax import lax
from jax.experimental import pallas as pl
from jax.experimental.pallas import tpu as pltpu

F32 = jnp.float32
BF16 = jnp.bfloat16
EPS = 1e-6

LANES = 128
SUBLANES = 8
VMEM_LIMIT = 56 * 1024 * 1024

S5_GROUP = 16
S5_STATE = 64
SSD_HEAD_DIM = 64
SSD_GROUPS = 4
SSD_STATE = 128
SSD_CONV = 4
SSD_CHUNK = 128


def _sigmoid(x):
    return 0.5 + 0.5 * jnp.tanh(0.5 * x)


def _silu(x):
    hx = 0.5 * x
    return hx + hx * jnp.tanh(hx)


def _gelu_tanh(x):
    c = math.sqrt(2.0 / math.pi)
    return 0.5 * x * (1.0 + jnp.tanh(c * (x + 0.044715 * (x * x * x))))


def _softplus(x):
    return jnp.maximum(x, 0.0) + jnp.log1p(jnp.exp(-jnp.abs(x)))


def _rms(x, w):
    return x * lax.rsqrt(jnp.mean(x * x, axis=-1, keepdims=True) + EPS) * w


def _dot(a, b):
    return jnp.dot(a, b, preferred_element_type=F32)


def _dot_nt(a, b):
    return lax.dot_general(a, b, (((1,), (1,)), ((), ())), preferred_element_type=F32)


def _const_spec(shape):
    nd = len(shape)
    return pl.BlockSpec(shape, lambda *_: (0,) * nd, pipeline_mode=pl.Buffered(1))


def _params(sem):
    return pltpu.CompilerParams(dimension_semantics=sem, vmem_limit_bytes=VMEM_LIMIT)


def _swiglu_half(x, nw, wg_ref, wu_ref, wd_ref):
    hn = _rms(x, nw).astype(BF16)
    g = _dot(hn, wg_ref[...])
    u = _dot(hn, wu_ref[...])
    a = (_silu(g) * u).astype(BF16)
    return x + 0.5 * _dot(a, wd_ref[...])


def _ffn1_kernel(x_ref, nw_ref, wg_ref, wu_ref, wd_ref, mixw_ref, x1_ref, h_ref):
    x1 = _swiglu_half(x_ref[...], nw_ref[...], wg_ref, wu_ref, wd_ref)
    x1_ref[...] = x1
    h_ref[...] = _rms(x1, mixw_ref[...]).astype(BF16)


def _ffn1(x, nw, wg, wu, wd, mixw, tm):
    m, d = x.shape
    dff = wg.shape[1]
    row = lambda w: pl.BlockSpec((tm, w), lambda i: (i, 0))
    return pl.pallas_call(
        _ffn1_kernel,
        grid=(m // tm,),
        in_specs=[row(d), _const_spec((1, d)), _const_spec((d, dff)), _const_spec((d, dff)),
                  _const_spec((dff, d)), _const_spec((1, d))],
        out_specs=[row(d), row(d)],
        out_shape=[jax.ShapeDtypeStruct((m, d), F32), jax.ShapeDtypeStruct((m, d), BF16)],
        compiler_params=_params(("arbitrary",)),
        name="ffn1",
    )(x, nw, wg, wu, wd, mixw)


def _out_ffn2_kernel(x1_ref, o5_ref, os_ref, wo5_ref, wos_ref, nw_ref, wg_ref, wu_ref, wd_ref,
                     fin_ref, y_ref):
    x2 = x1_ref[...] + (_dot(o5_ref[...].astype(BF16), wo5_ref[...])
                        + _dot(os_ref[...].astype(BF16), wos_ref[...]))
    x3 = _swiglu_half(x2, nw_ref[...], wg_ref, wu_ref, wd_ref)
    y_ref[...] = _rms(x3, fin_ref[...])


def _out_ffn2(x1, o5, o_ssd, wo5, wos, nw, wg, wu, wd, fin, tm):
    m, d = x1.shape
    dff = wg.shape[1]
    row = lambda w: pl.BlockSpec((tm, w), lambda i: (i, 0))
    return pl.pallas_call(
        _out_ffn2_kernel,
        grid=(m // tm,),
        in_specs=[row(d), row(o5.shape[1]), row(o_ssd.shape[1]),
                  _const_spec(wo5.shape), _const_spec(wos.shape), _const_spec((1, d)),
                  _const_spec((d, dff)), _const_spec((d, dff)), _const_spec((dff, d)),
                  _const_spec((1, d))],
        out_specs=row(d),
        out_shape=jax.ShapeDtypeStruct((m, d), F32),
        compiler_params=_params(("arbitrary",)),
        name="out_ffn2",
    )(x1, o5, o_ssd, wo5, wos, nw, wg, wu, wd, fin)


def _s5_param_kernel(lre_ref, lim_ref, step_ref, bre_ref, bim_ref,
                     are_ref, aim_ref, bbre_ref, bbim_ref):
    lre, lim, step = lre_ref[...], lim_ref[...], step_ref[...]
    mag = jnp.exp(lre * step)
    ang = lim * step
    are = mag * jnp.cos(ang)
    aim = mag * jnp.sin(ang)
    den = lre * lre + lim * lim
    nre, nim = are - 1.0, aim
    cre = (nre * lre + nim * lim) / den
    cim = (nim * lre - nre * lim) / den
    are_ref[...] = are
    aim_ref[...] = aim
    bre, bim = bre_ref[...], bim_ref[...]
    bbre_ref[...] = cre * bre - cim * bim
    bbim_ref[...] = cre * bim + cim * bre


def _s5_params(lam_re, lam_im, log_step, b_re, b_im):
    g, p = lam_re.shape
    hh = b_re.shape[-1]
    gp = g * p
    step = jnp.exp(log_step.astype(F32))
    step_col = jnp.broadcast_to(step[:, None], (g, p)).reshape(gp, 1)
    full = lambda shape: pl.BlockSpec(shape, lambda: (0,) * len(shape))
    are, aim, bbre, bbim = pl.pallas_call(
        _s5_param_kernel,
        in_specs=[full((gp, 1)), full((gp, 1)), full((gp, 1)), full((gp, hh)), full((gp, hh))],
        out_specs=[full((gp, 1)), full((gp, 1)), full((gp, hh)), full((gp, hh))],
        out_shape=[jax.ShapeDtypeStruct((gp, 1), F32), jax.ShapeDtypeStruct((gp, 1), F32),
                   jax.ShapeDtypeStruct((gp, hh), F32), jax.ShapeDtypeStruct((gp, hh), F32)],
        name="s5_params",
    )(lam_re.astype(F32).reshape(gp, 1), lam_im.astype(F32).reshape(gp, 1), step_col,
      b_re.astype(F32).reshape(gp, hh), b_im.astype(F32).reshape(gp, hh))
    return (are.reshape(1, gp), aim.reshape(1, gp),
            bbre.reshape(g, p, hh), bbim.reshape(g, p, hh))


def _block_diag(w, nblk):
    g, a, b = w.shape
    gl = g // nblk
    w = w.reshape(nblk, gl, a, b)
    eye = jnp.eye(gl, dtype=w.dtype)
    out = w[:, :, :, None, :] * eye[None, :, None, :, None]
    return out.reshape(nblk, gl * a, gl * b)


def _s5_kernel(h_ref, wu_ref, wb_ref, wcre_ref, wcim_ref, are_ref, aim_ref, d_ref, wglu_ref,
               bglu_ref, s0re_ref, s0im_ref, o5_ref, sre_ref, sim_ref, u_s, xre_s, xim_s, y_s,
               *, nb, tl, lane_chunk):
    width = u_s.shape[1]
    nstate = xre_s.shape[1]
    nblk = wb_ref.shape[0]
    cin = width // nblk
    cst = nstate // nblk

    @pl.when(pl.program_id(0) == 0)
    def _():
        sre_ref[...] = s0re_ref[...]
        sim_ref[...] = s0im_ref[...]

    u = _dot(h_ref[...], wu_ref[...])
    u_s[...] = u
    ub = u.astype(BF16)
    for k in range(nblk):
        r = _dot(ub[:, k * cin:(k + 1) * cin], wb_ref[k])
        xre_s[:, k * cst:(k + 1) * cst] = r[:, :cst]
        xim_s[:, k * cst:(k + 1) * cst] = r[:, cst:]

    for c in range(nstate // lane_chunk):
        lanes = slice(c * lane_chunk, (c + 1) * lane_chunk)
        a_re = jnp.broadcast_to(are_ref[:, lanes], (SUBLANES, lane_chunk))
        a_im = jnp.broadcast_to(aim_ref[:, lanes], (SUBLANES, lane_chunk))

        def group_body(sg, carry, lanes=lanes, a_re=a_re, a_im=a_im):
            srow = pl.ds(pl.multiple_of(sg * SUBLANES, SUBLANES), SUBLANES)

            def step(t, st):
                s_re, s_im = st
                rows = pl.ds(pl.multiple_of(t * nb + sg * SUBLANES, SUBLANES), SUBLANES)
                n_re = a_re * s_re - a_im * s_im + xre_s[rows, lanes]
                n_im = a_re * s_im + a_im * s_re + xim_s[rows, lanes]
                xre_s[rows, lanes] = n_re
                xim_s[rows, lanes] = n_im
                return n_re, n_im

            s_re, s_im = lax.fori_loop(0, tl, step, (sre_ref[srow, lanes], sim_ref[srow, lanes]),
                                       unroll=min(tl, 8))
            sre_ref[srow, lanes] = s_re
            sim_ref[srow, lanes] = s_im
            return carry

        lax.fori_loop(0, nb // SUBLANES, group_body, 0)

    for k in range(nblk):
        st = slice(k * cst, (k + 1) * cst)
        y_s[:, k * cin:(k + 1) * cin] = (_dot(xre_s[:, st].astype(BF16), wcre_ref[k])
                                         - _dot(xim_s[:, st].astype(BF16), wcim_ref[k]))
    y = y_s[...] + d_ref[...] * u_s[...]
    v = _gelu_tanh(y)
    gate = _dot(v.astype(BF16), wglu_ref[...]) + bglu_ref[...]
    o5_ref[...] = (v * _sigmoid(gate)).astype(BF16)


def _s5(h_tm, wu, wb, wcre, wcim, are, aim, dsk, wglu, bglu, s0re, s0im, nb, tl):
    m, d = h_tm.shape
    width = wu.shape[1]
    nstate = are.shape[1]
    r = nb * tl
    row = lambda w: pl.BlockSpec((r, w), lambda i: (i, 0))
    kern = functools.partial(_s5_kernel, nb=nb, tl=tl, lane_chunk=4 * LANES)
    return pl.pallas_call(
        kern,
        grid=(m // r,),
        in_specs=[row(d), _const_spec(wu.shape), _const_spec(wb.shape), _const_spec(wcre.shape),
                  _const_spec(wcim.shape), _const_spec(are.shape), _const_spec(aim.shape),
                  _const_spec(dsk.shape), _const_spec(wglu.shape), _const_spec(bglu.shape),
                  _const_spec(s0re.shape), _const_spec(s0im.shape)],
        out_specs=[row(width), pl.BlockSpec((nb, nstate), lambda i: (0, 0)),
                   pl.BlockSpec((nb, nstate), lambda i: (0, 0))],
        out_shape=[jax.ShapeDtypeStruct((m, width), BF16),
                   jax.ShapeDtypeStruct((nb, nstate), F32),
                   jax.ShapeDtypeStruct((nb, nstate), F32)],
        scratch_shapes=[pltpu.VMEM((r, width), F32), pltpu.VMEM((r, nstate), F32),
                        pltpu.VMEM((r, nstate), F32), pltpu.VMEM((r, width), F32)],
        compiler_params=_params(("arbitrary",)),
        name="s5",
    )(h_tm, wu, wb, wcre, wcim, are, aim, dsk, wglu, bglu, s0re, s0im)


def _split2(x):
    hi = x.astype(BF16)
    lo = (x - hi.astype(F32)).astype(BF16)
    return jnp.concatenate([hi, lo], axis=1)


def _split3(x):
    hi = x.astype(BF16)
    r1 = x - hi.astype(F32)
    mid = r1.astype(BF16)
    lo = (r1 - mid.astype(F32)).astype(BF16)
    return jnp.concatenate([hi, mid, lo], axis=1)


def _group_norm(y, zg, nw, gw):
    outs = []
    for g in range(SSD_GROUPS):
        seg = y[:, g * gw:(g + 1) * gw] * zg[:, g * gw:(g + 1) * gw]
        outs.append(seg * lax.rsqrt(jnp.mean(seg * seg, axis=-1, keepdims=True) + EPS))
    return jnp.concatenate(outs, axis=1) * nw


def _ssds_pre_kernel(h_ref, wz_ref, wx_ref, wdt_ref, cw_ref, cb_ref, dtb_ref, alog_ref, dsk_ref,
                     conv0_ref, g2_ref, exp_ref, c_o, b_o, xdd_o, yp_o, eac_o, zg_o, cdec_o, cv_o):
    nbt = conv0_ref.shape[1]
    seq = h_ref.shape[0] // nbt
    width = zg_o.shape[1]
    nbc = c_o.shape[1]
    hb = h_ref[...]
    zg_o[...] = _silu(_dot(hb, wz_ref[...]))
    dt = _softplus(_dot(hb, wdt_ref[...]) + dtb_ref[...])
    la = dt * (-jnp.exp(alog_ref[...]))
    xbc = _dot(hb, wx_ref[...])
    rows = [slice(t * nbt, (t + 1) * nbt) for t in range(seq)]
    full = [conv0_ref[k] for k in range(SSD_CONV - 1)] + [xbc[r] for r in rows]
    for k in range(SSD_CONV - 1):
        cv_o[k] = full[seq + k]
    acums = []
    for t in range(seq):
        acums.append(la[rows[t]] if t == 0 else acums[-1] + la[rows[t]])
    tot = acums[-1]
    cdec_o[...] = jnp.exp(tot)
    xs, bq, cq = [], [], []
    for t in range(seq):
        acc = cb_ref[...]
        for k in range(SSD_CONV):
            acc = acc + cw_ref[k:k + 1, :] * full[t + k]
        xc = _silu(acc)
        xs.append(xc[:, :width])
        bq.append(xc[:, width:width + nbc].astype(BF16).astype(F32))
        cq.append(xc[:, width + nbc:].astype(BF16).astype(F32))
    for t in range(seq):
        r = rows[t]
        b_o[r, :] = bq[t]
        c_o[r, :] = cq[t]
        xdd_o[r, :] = xs[t] * _dot(_split2(dt[r] * jnp.exp(tot - acums[t])), exp_ref[...])
        eac_o[r, :] = _dot(_split2(jnp.exp(acums[t])), exp_ref[...])
        yp = dsk_ref[...] * xs[t]
        for s in range(t + 1):
            cbx = _dot(_split2(cq[t] * bq[s]), g2_ref[...])
            coef = cbx * jnp.exp(acums[t] - acums[s]) * dt[rows[s]]
            yp = yp + _dot(_split2(coef), exp_ref[...]) * xs[s]
        yp_o[r, :] = yp


def _ssds_state_kernel(c_ref, b_ref, xdd_ref, yp_ref, eac_ref, zg_ref, cdec_ref, st_ref, nw_ref,
                       o_ref, sto_ref, yoff_s, xddt_s):
    seq, nb, width = xdd_ref.shape
    nheads, hd_dim, nst = st_ref.shape[1:]
    hpg = nheads // SSD_GROUPS
    gw = width // SSD_GROUPS
    rws = seq * nb
    cst = c_ref[...].reshape(rws, c_ref.shape[2]).astype(BF16)
    bst = b_ref[...].reshape(rws, b_ref.shape[2])
    xdd = xdd_ref[...].reshape(rws, width)
    xddt_s[...] = jnp.concatenate([xdd, jnp.zeros((LANES - rws, width), F32)], axis=0).T.astype(BF16)
    rowb = lax.broadcasted_iota(jnp.int32, (rws, gw), 0) % nb
    zpad = jnp.zeros((LANES - rws, nst), BF16)
    for b in range(nb):
        mine = rowb == b
        for g in range(SSD_GROUPS):
            gl = slice(g * gw, (g + 1) * gw)
            hg = st_ref[b, g * hpg:(g + 1) * hpg].reshape(gw, nst)
            res = _dot_nt(cst[:, g * nst:(g + 1) * nst], hg.astype(BF16))
            yoff_s[:, gl] = jnp.where(mine, res, 0.0) if b == 0 else jnp.where(mine, res, yoff_s[:, gl])
            bm = jnp.where(mine[:, :nst], bst[:, g * nst:(g + 1) * nst], 0.0).astype(BF16)
            upd = _dot(xddt_s[gl, :], jnp.concatenate([bm, zpad], axis=0))
            for jh in range(hpg):
                hd = g * hpg + jh
                hrows = slice(jh * hd_dim, (jh + 1) * hd_dim)
                sto_ref[b, hd] = hg[hrows] * cdec_ref[b, hd] + upd[hrows]
    y = yp_ref[...].reshape(rws, width) + yoff_s[...] * eac_ref[...].reshape(rws, width)
    o = _group_norm(y, zg_ref[...].reshape(rws, width), nw_ref[...], gw)
    o_ref[...] = o.reshape(seq, nb, width)


def _ssds(h_tm, wz, wx, wdt, cw, cb, dtb, alog, dsk, nw, ssd0, conv0_tm, nb):
    m, d = h_tm.shape
    nseq, nheads, hd_dim, nst = ssd0.shape
    seq = m // nseq
    width = wz.shape[1]
    xbc_w = wx.shape[1]
    nbc = SSD_GROUPS * nst
    hpg = nheads // SSD_GROUPS
    head = jnp.arange(LANES)
    exp1 = (head[:, None] == (jnp.arange(width) // hd_dim)[None, :]).astype(BF16)
    g1 = ((jnp.arange(nbc) // nst)[:, None] == (head // hpg)[None, :]) & (head < nheads)[None, :]
    exp_m = jnp.concatenate([exp1, exp1], axis=0)
    g2 = jnp.concatenate([g1, g1], axis=0).astype(BF16)
    full = lambda shape: pl.BlockSpec(shape, lambda: (0,) * len(shape))
    ins = (h_tm, wz, wx, wdt, cw, cb, dtb, alog, dsk, conv0_tm, g2, exp_m)
    outs = [jax.ShapeDtypeStruct((m, nbc), F32), jax.ShapeDtypeStruct((m, nbc), F32),
            jax.ShapeDtypeStruct((m, width), F32), jax.ShapeDtypeStruct((m, width), F32),
            jax.ShapeDtypeStruct((m, width), F32), jax.ShapeDtypeStruct((m, width), F32),
            jax.ShapeDtypeStruct((nseq, LANES), F32),
            jax.ShapeDtypeStruct((SSD_CONV - 1, nseq, xbc_w), F32)]
    c, b, xdd, yp, eac, zg, cdec, conv1_tm = pl.pallas_call(
        _ssds_pre_kernel,
        in_specs=[full(a.shape) for a in ins],
        out_specs=[full(o.shape) for o in outs],
        out_shape=outs,
        compiler_params=pltpu.CompilerParams(vmem_limit_bytes=VMEM_LIMIT),
        name="ssd_sample_pre",
    )(*ins)

    tm3 = lambda a: a.reshape(seq, nseq, a.shape[1])
    blk = lambda w: pl.BlockSpec((seq, nb, w), lambda i: (0, i, 0))
    st_spec = pl.BlockSpec((nb, nheads, hd_dim, nst), lambda i: (i, 0, 0, 0))
    o, ssd1 = pl.pallas_call(
        _ssds_state_kernel,
        grid=(nseq // nb,),
        in_specs=[blk(nbc), blk(nbc), blk(width), blk(width), blk(width), blk(width),
                  pl.BlockSpec((nb, LANES), lambda i: (i, 0), memory_space=pltpu.SMEM),
                  st_spec, _const_spec(nw.shape)],
        out_specs=[blk(width), st_spec],
        out_shape=[jax.ShapeDtypeStruct((seq, nseq, width), F32),
                   jax.ShapeDtypeStruct(ssd0.shape, F32)],
        scratch_shapes=[pltpu.VMEM((seq * nb, width), F32), pltpu.VMEM((width, LANES), BF16)],
        compiler_params=_params(("arbitrary",)),
        name="ssd_sample_state",
    )(tm3(c), tm3(b), tm3(xdd), tm3(yp), tm3(eac), tm3(zg), cdec, ssd0, nw)
    return o.reshape(m, width), ssd1, conv1_tm


def _ssdp_kernel(h_ref, wz_ref, wx_ref, wdt_ref, cw_ref, cb_ref, dtb_ref, alog_ref, dsk_ref,
                 nw_ref, btril_ref, sel_ref, exp_ref, perm_ref, permt_ref, o_ref, st_ref, cv_ref,
                 xbc_s, zg_s, acol_s, acumt_s, e2_s, e3_s, xd_s, xdd_s, ysk_s, b_s, c_s,
                 cdec_s, ht_s, tail_s, *, nch, col_chunk):
    t = SSD_CHUNK
    r = nch * t
    j = pl.program_id(1)
    nheads, hd_dim, nst = st_ref.shape[1:]
    width = nheads * hd_dim
    gw = width // SSD_GROUPS
    hpg = nheads // SSD_GROUPS
    xbc_w = xbc_s.shape[1]
    ntap = SSD_CONV - 1
    pre = ntap * SUBLANES
    ext = pre + t
    steps = t // SUBLANES
    last_rows = [t - 1 - SUBLANES * (ntap - 1 - v) for v in range(ntap)]

    @pl.when(j == 0)
    def _():
        ht_s[...] = jnp.zeros(ht_s.shape, F32)
        tail_s[...] = jnp.zeros(tail_s.shape, F32)

    hb = jnp.concatenate([_dot(perm_ref[...], h_ref[c * t:(c + 1) * t, :]) for c in range(nch)],
                         axis=0).astype(BF16)
    dt = _softplus(_dot(hb, wdt_ref[...]) + dtb_ref[...])
    la = dt * (-jnp.exp(alog_ref[...]))

    def xbc_cols(jc):
        lanes = slice(jc * col_chunk, (jc + 1) * col_chunk)
        xall = _dot(hb, wx_ref[:, lanes])
        for c in range(nch):
            xbc_s[c * ext + pre:(c + 1) * ext, lanes] = xall[c * t:(c + 1) * t, :]
        for c in range(nch):
            base = c * ext
            xbc_s[base:base + pre, lanes] = xbc_s[base + t - 1:base + pre + t - 1, lanes]
            for v in range(ntap):
                if c == 0:
                    prev = tail_s[v:v + 1, lanes]
                else:
                    prow = base - ext + pre + last_rows[v]
                    prev = xbc_s[prow:prow + 1, lanes]
                xbc_s[base + v * SUBLANES:base + v * SUBLANES + 1, lanes] = prev
            acc = cb_ref[:, lanes]
            for k in range(SSD_CONV):
                acc = acc + cw_ref[k:k + 1, lanes] * xbc_s[base + k * SUBLANES:
                                                           base + k * SUBLANES + t, lanes]
            xc = _silu(acc)
            rows = slice(c * t, (c + 1) * t)
            lo = jc * col_chunk
            if lo < width:
                xd_s[rows, lanes] = xc.astype(BF16)
                xdd_s[rows, lanes] = (xc * e2_s[rows, lanes]).astype(BF16)
                ysk_s[rows, lanes] = xc * dsk_ref[:, lanes]
            elif lo < width + SSD_GROUPS * nst:
                b_s[rows, lo - width:lo - width + col_chunk] = xc.astype(BF16)
            else:
                cl = lo - width - SSD_GROUPS * nst
                c_s[rows, cl:cl + col_chunk] = xc.astype(BF16)

    x_cols = width // col_chunk
    for jc in range(x_cols, xbc_w // col_chunk):
        xbc_cols(jc)

    r3 = _dot(btril_ref[...], _split3(la))
    acum = r3[:, :LANES] + r3[:, LANES:2 * LANES] + r3[:, 2 * LANES:]
    tots = [acum[(c + 1) * t - 1:(c + 1) * t, :] for c in range(nch)]
    arow = acum - jnp.log(dt)
    for c in range(nch):
        acumt_s[c] = arow[c * t:(c + 1) * t, :].T
    tot_rows = jnp.concatenate([jnp.broadcast_to(v, (t, LANES)) for v in tots], axis=0)
    dstate = jnp.exp(tot_rows - acum)
    cdec = jnp.exp(jnp.concatenate([jnp.broadcast_to(v, (SUBLANES, LANES)) for v in tots], axis=0))
    e2_s[...] = _dot(_split2(dt * dstate), exp_ref[...])
    e3_s[...] = _dot(_split2(jnp.exp(acum)), exp_ref[...])
    cdx = _dot(_split2(cdec), exp_ref[...])
    for c in range(nch):
        cdec_s[c] = cdx[c * SUBLANES:(c + 1) * SUBLANES, :]
    acol_s[...] = _dot(_split3(acum), sel_ref[...])

    for jc in range(x_cols):
        lanes = slice(jc * col_chunk, (jc + 1) * col_chunk)
        xbc_cols(jc)
        zg_s[:, lanes] = _silu(_dot(hb, wz_ref[:, lanes]))

    for v in range(ntap):
        row = (nch - 1) * ext + pre + last_rows[v]
        tail_s[v:v + 1, :] = xbc_s[row:row + 1, :]
    cv_ref[0] = tail_s[...]

    def token(i):
        return (i % SUBLANES) * steps + i // SUBLANES

    causal = (token(lax.broadcasted_iota(jnp.int32, (t, t), 0))
              >= token(lax.broadcasted_iota(jnp.int32, (t, t), 1)))
    lane = lax.broadcasted_iota(jnp.int32, (t, LANES), 1)
    keep_lo = jnp.where(lane < hd_dim, 1.0, 0.0).astype(BF16)
    keep_hi = jnp.where(lane < hd_dim, 0.0, 1.0).astype(BF16)
    nw = nw_ref[...]

    def chunk_body(c, carry):
        rows = pl.ds(pl.multiple_of(c * t, t), t)
        ys = []
        for g in range(SSD_GROUPS):
            gl = slice(g * gw, (g + 1) * gw)
            cg = c_s[rows, g * nst:(g + 1) * nst]
            bg = b_s[rows, g * nst:(g + 1) * nst]
            cbm = jnp.where(causal, _dot_nt(cg, bg), 0.0)
            pairs = []
            for jp in range(hpg // 2):
                ha = g * hpg + 2 * jp
                ms = []
                for hd in (ha, ha + 1):
                    diff = acol_s[rows, hd * t:(hd + 1) * t] - acumt_s[c, hd:hd + 1, :]
                    ms.append((cbm * jnp.exp(jnp.where(causal, diff, -jnp.inf))).astype(BF16))
                xdp = xd_s[rows, ha * hd_dim:ha * hd_dim + LANES]
                rhs = jnp.concatenate([xdp * keep_lo, xdp * keep_hi], axis=0)
                pairs.append(_dot(jnp.concatenate(ms, axis=1), rhs))
            htg = ht_s[:, gl]
            y = (jnp.concatenate(pairs, axis=1) + _dot(cg, htg.astype(BF16)) * e3_s[rows, gl]
                 + ysk_s[rows, gl])
            ys.append(y)
            upd = lax.dot_general(bg, xdd_s[rows, gl], (((0,), (0,)), ((), ())),
                                  preferred_element_type=F32)
            ht_s[:, gl] = htg * cdec_s[c, 0:1, gl] + upd
        o = _group_norm(jnp.concatenate(ys, axis=1), zg_s[rows, :], nw, gw).astype(BF16)
        o_ref[rows, :] = _dot(permt_ref[...], o).astype(BF16)
        return carry

    lax.fori_loop(0, nch, chunk_body, 0)

    @pl.when(j == pl.num_programs(1) - 1)
    def _():
        for pr in range(nheads // 2):
            tt = ht_s[:, pr * LANES:(pr + 1) * LANES].T
            st_ref[0, 2 * pr] = tt[:hd_dim]
            st_ref[0, 2 * pr + 1] = tt[hd_dim:]


def _ssdp(h, wz, wx, wdt, cw, cb, dtb, alog, dsk, nw, nseq, nheads, nch):
    m, d = h.shape
    width = wz.shape[1]
    xbc_w = wx.shape[1]
    hd_dim = width // nheads
    nst = SSD_STATE
    t = SSD_CHUNK
    r = nch * t
    nblk = m // (nseq * r)
    ri = jnp.arange(r)
    steps = t // SUBLANES
    tok = (ri % t % SUBLANES) * steps + ri % t // SUBLANES
    btril = ((tok[:, None] >= tok[None, :]) & (ri[:, None] // t == ri[None, :] // t)).astype(BF16)
    perm = (tok[:t, None] == jnp.arange(t)[None, :]).astype(BF16)
    head = jnp.arange(LANES)
    exp1 = (head[:, None] == (jnp.arange(width) // hd_dim)[None, :]).astype(BF16)
    sel1 = ((head[:, None] == (jnp.arange(nheads * t) // t)[None, :])).astype(BF16)
    exp_m = jnp.concatenate([exp1, exp1], axis=0)
    sel_m = jnp.concatenate([sel1, sel1, sel1], axis=0)
    row = lambda w: pl.BlockSpec((r, w), lambda i, j: (i * nblk + j, 0))
    kern = functools.partial(_ssdp_kernel, nch=nch, col_chunk=4 * LANES)
    vm = lambda shape, dt: pltpu.VMEM(shape, dt)
    return pl.pallas_call(
        kern,
        grid=(nseq, nblk),
        in_specs=[row(d), _const_spec(wz.shape), _const_spec(wx.shape), _const_spec(wdt.shape),
                  _const_spec(cw.shape), _const_spec(cb.shape), _const_spec(dtb.shape),
                  _const_spec(alog.shape), _const_spec(dsk.shape), _const_spec(nw.shape),
                  _const_spec(btril.shape), _const_spec(sel_m.shape), _const_spec(exp_m.shape),
                  _const_spec(perm.shape), _const_spec(perm.shape)],
        out_specs=[row(width),
                   pl.BlockSpec((1, nheads, hd_dim, nst), lambda i, j: (i, 0, 0, 0)),
                   pl.BlockSpec((1, SSD_CONV - 1, xbc_w), lambda i, j: (i, 0, 0))],
        out_shape=[jax.ShapeDtypeStruct((m, width), BF16),
                   jax.ShapeDtypeStruct((nseq, nheads, hd_dim, nst), F32),
                   jax.ShapeDtypeStruct((nseq, SSD_CONV - 1, xbc_w), F32)],
        scratch_shapes=[vm((nch * (t + (SSD_CONV - 1) * SUBLANES), xbc_w), F32),
                        vm((r, width), F32),
                        vm((r, nheads * t), F32), vm((nch, LANES, t), F32),
                        vm((r, width), F32), vm((r, width), F32),
                        vm((r, width), BF16), vm((r, width), BF16), vm((r, width), F32),
                        vm((r, SSD_GROUPS * nst), BF16), vm((r, SSD_GROUPS * nst), BF16),
                        vm((nch, SUBLANES, width), F32), vm((nst, width), F32),
                        vm((SSD_CONV - 1, xbc_w), F32)],
        compiler_params=_params(("arbitrary", "arbitrary")),
        name="ssd_prompt",
    )(h, wz, wx, wdt, cw, cb, dtb, alog, dsk, nw, btril, sel_m, exp_m, perm, perm.T)


def _pad_lanes(v, n=LANES):
    return jnp.pad(v, [(0, 0)] * (v.ndim - 1) + [(0, n - v.shape[-1])])


def _s5_call(h_tm, s5re0, s5im0, w, nb, tl):
    g, p = s5re0.shape[1:]
    o5_tm, re1, im1 = _s5(h_tm, w["w_u"], w["s5_wb"], w["s5_wcre"], w["s5_wcim"], w["s5_are"],
                          w["s5_aim"], w["s5_d"], w["s5_wglu"], w["s5_bglu"],
                          s5re0.reshape(nb, g * p), s5im0.reshape(nb, g * p), nb, tl)
    return o5_tm, re1.reshape(nb, g, p), im1.reshape(nb, g, p)


def _ffn1_call(x2d, w, tm):
    return _ffn1(x2d, w["ffn1_norm"], w["ffn1_wg"], w["ffn1_wu"], w["ffn1_wd"], w["mix_norm"], tm)


def _out_call(x1, o5, o_ssd, w, tm):
    return _out_ffn2(x1, o5, o_ssd, w["w_out5"], w["w_outs"], w["ffn2_norm"], w["ffn2_wg"],
                     w["ffn2_wu"], w["ffn2_wd"], w["final_norm"], tm)


def _prompt_layer(x, w, *, tm, s5_tl, ssd_nch):
    bsz, seq, d = x.shape
    m = bsz * seq
    g, p = w["s5_are"].shape[1] // S5_STATE, S5_STATE
    x1, h = _ffn1_call(x.reshape(m, d), w, tm)
    h_tm = h.reshape(bsz, seq, d).transpose(1, 0, 2).reshape(m, d)
    zeros = jnp.zeros((bsz, g, p), F32)
    o5_tm, s5re1, s5im1 = _s5_call(h_tm, zeros, zeros, w, bsz, s5_tl)
    o5 = o5_tm.reshape(seq, bsz, -1).transpose(1, 0, 2).reshape(m, -1)
    nheads = w["ssd_d"].shape[1] // SSD_HEAD_DIM
    o_ssd, ssd1, conv1 = _ssdp(h, w["w_z"], w["w_x"], w["w_dt"], w["conv_w"], w["conv_b"],
                               w["dt_bias"], w["a_log"], w["ssd_d"], w["ssd_norm"],
                               bsz, nheads, ssd_nch)
    y = _out_call(x1, o5, o_ssd, w, tm)
    return y.reshape(bsz, seq, d), s5re1, s5im1, ssd1, conv1


def _sample_layer(x, s5re0, s5im0, ssd0, conv0, w, *, ssd_nb):
    bsz, seq, d = x.shape
    m = bsz * seq
    x_tm = x.transpose(1, 0, 2).reshape(m, d)
    x1, h = _ffn1_call(x_tm, w, m)
    o5, s5re1, s5im1 = _s5_call(h, s5re0, s5im0, w, bsz, seq)
    o_ssd, ssd1, conv1_tm = _ssds(h, w["w_z"], w["w_x"], w["w_dt"], w["conv_w"], w["conv_b"],
                                  w["dt_bias"], w["a_log"], w["ssd_d"], w["ssd_norm"], ssd0,
                                  conv0.transpose(1, 0, 2), ssd_nb)
    y = _out_call(x1, o5, o_ssd, w, m)
    return (y.reshape(seq, bsz, d).transpose(1, 0, 2), s5re1, s5im1, ssd1,
            conv1_tm.transpose(1, 0, 2))


def kernel(x_prompt, x_sample, state_s5_re, state_s5_im, state_ssd, state_conv, ffn1_norm, ffn1_w_gate, ffn1_w_up, ffn1_w_down, mix_norm, w_in, s5_lambda_re, s5_lambda_im, s5_log_step, s5_b_re, s5_b_im, s5_c_re, s5_c_im, s5_d, s5_w_glu, s5_b_glu, ssd_conv_w, ssd_conv_b, ssd_dt_bias, ssd_a_log, ssd_d, ssd_norm, w_out, ffn2_norm, ffn2_w_gate, ffn2_w_up, ffn2_w_down, final_norm):
    depth = w_in.shape[0]
    assert depth == 1, "single-layer stack"
    i = 0
    d = x_prompt.shape[-1]
    g, p = s5_lambda_re.shape[1:]
    s5_width = g * S5_GROUP
    nheads = ssd_a_log.shape[1]
    ssd_width = nheads * SSD_HEAD_DIM
    xbc_w = ssd_conv_w.shape[-1]
    c0, c1, c2 = s5_width, s5_width + ssd_width, s5_width + ssd_width + xbc_w
    row = lambda v: v.astype(F32).reshape(1, -1)

    are, aim, bbre, bbim = _s5_params(s5_lambda_re[i], s5_lambda_im[i], s5_log_step[i],
                                      s5_b_re[i], s5_b_im[i])
    nblk = s5_width // LANES
    wb = jnp.concatenate([_block_diag(bbre.transpose(0, 2, 1), nblk),
                          _block_diag(bbim.transpose(0, 2, 1), nblk)], axis=-1).astype(BF16)
    wcre = _block_diag(s5_c_re[i].astype(F32).transpose(0, 2, 1), nblk).astype(BF16)
    wcim = _block_diag(s5_c_im[i].astype(F32).transpose(0, 2, 1), nblk).astype(BF16)

    win = w_in[i]
    w = {
        "ffn1_norm": row(ffn1_norm[i]), "mix_norm": row(mix_norm[i]),
        "ffn2_norm": row(ffn2_norm[i]), "final_norm": row(final_norm),
        "ffn1_wg": ffn1_w_gate[i].astype(BF16), "ffn1_wu": ffn1_w_up[i].astype(BF16),
        "ffn1_wd": ffn1_w_down[i].astype(BF16),
        "ffn2_wg": ffn2_w_gate[i].astype(BF16), "ffn2_wu": ffn2_w_up[i].astype(BF16),
        "ffn2_wd": ffn2_w_down[i].astype(BF16),
        "w_u": win[:, :c0].astype(BF16), "w_z": win[:, c0:c1].astype(BF16),
        "w_x": win[:, c1:c2].astype(BF16), "w_dt": _pad_lanes(win[:, c2:]).astype(BF16),
        "s5_wb": wb, "s5_wcre": wcre, "s5_wcim": wcim, "s5_are": are, "s5_aim": aim,
        "s5_d": row(s5_d[i]), "s5_wglu": s5_w_glu[i].astype(BF16), "s5_bglu": row(s5_b_glu[i]),
        "conv_w": ssd_conv_w[i].astype(F32), "conv_b": row(ssd_conv_b[i]),
        "dt_bias": _pad_lanes(row(ssd_dt_bias[i])), "a_log": _pad_lanes(row(ssd_a_log[i])),
        "ssd_d": jnp.repeat(ssd_d[i].astype(F32), SSD_HEAD_DIM).reshape(1, -1),
        "ssd_norm": row(ssd_norm[i]),
        "w_out5": w_out[i][:s5_width].astype(BF16), "w_outs": w_out[i][s5_width:].astype(BF16),
    }

    lp = x_prompt.shape[1]
    yp, p_re, p_im, p_ssd, p_conv = _prompt_layer(
        x_prompt, w, tm=512, s5_tl=min(lp, 128), ssd_nch=min(lp // SSD_CHUNK, 4))
    ys, s_re, s_im, s_ssd, s_conv = _sample_layer(
        x_sample, state_s5_re[i], state_s5_im[i], state_ssd[i], state_conv[i], w,
        ssd_nb=SUBLANES)
    st = lambda v: v[None]
    return (yp, ys, st(p_re), st(p_im), st(p_ssd), st(p_conv),
            st(s_re), st(s_im), st(s_ssd), st(s_conv))
```

```python
import functools
import math

import jax
import jax.numpy as jnp
from jax import lax
from jax.experimental import pallas as pl
from jax.experimental.pallas import tpu as pltpu

F32 = jnp.float32
BF16 = jnp.bfloat16
EPS = 1e-6

LANES = 128
SUBLANES = 8
VMEM_LIMIT = 56 * 1024 * 1024

S5_GROUP = 16
S5_STATE = 64
SSD_HEAD_DIM = 64
SSD_GROUPS = 4
SSD_STATE = 128
SSD_CONV = 4
SSD_CHUNK = 128


def _sigmoid(x):
    return 0.5 + 0.5 * jnp.tanh(0.5 * x)


def _silu(x):
    hx = 0.5 * x
    return hx + hx * jnp.tanh(hx)


def _gelu_tanh(x):
    c = math.sqrt(2.0 / math.pi)
    return 0.5 * x * (1.0 + jnp.tanh(c * (x + 0.044715 * (x * x * x))))


def _softplus(x):
    return jnp.maximum(x, 0.0) + jnp.log1p(jnp.exp(-jnp.abs(x)))


def _rms(x, w):
    return x * lax.rsqrt(jnp.mean(x * x, axis=-1, keepdims=True) + EPS) * w


def _dot(a, b):
    return jnp.dot(a, b, preferred_element_type=F32)


def _dot_nt(a, b):
    return lax.dot_general(a, b, (((1,), (1,)), ((), ())), preferred_element_type=F32)


def _const_spec(shape):
    nd = len(shape)
    return pl.BlockSpec(shape, lambda *_: (0,) * nd, pipeline_mode=pl.Buffered(1))


def _params(sem):
    return pltpu.CompilerParams(dimension_semantics=sem, vmem_limit_bytes=VMEM_LIMIT)


def _swiglu_half(x, nw, wg_ref, wu_ref, wd_ref):
    hn = _rms(x, nw).astype(BF16)
    g = _dot(hn, wg_ref[...])
    u = _dot(hn, wu_ref[...])
    a = (_silu(g) * u).astype(BF16)
    return x + 0.5 * _dot(a, wd_ref[...])


def _ffn1_kernel(x_ref, nw_ref, wg_ref, wu_ref, wd_ref, mixw_ref, x1_ref, h_ref):
    x1 = _swiglu_half(x_ref[...], nw_ref[...], wg_ref, wu_ref, wd_ref)
    x1_ref[...] = x1
    h_ref[...] = _rms(x1, mixw_ref[...]).astype(BF16)


def _ffn1(x, nw, wg, wu, wd, mixw, tm):
    m, d = x.shape
    dff = wg.shape[1]
    row = lambda w: pl.BlockSpec((tm, w), lambda i: (i, 0))
    return pl.pallas_call(
        _ffn1_kernel,
        grid=(m // tm,),
        in_specs=[row(d), _const_spec((1, d)), _const_spec((d, dff)), _const_spec((d, dff)),
                  _const_spec((dff, d)), _const_spec((1, d))],
        out_specs=[row(d), row(d)],
        out_shape=[jax.ShapeDtypeStruct((m, d), F32), jax.ShapeDtypeStruct((m, d), BF16)],
        compiler_params=_params(("arbitrary",)),
        name="ffn1",
    )(x, nw, wg, wu, wd, mixw)


def _out_ffn2_kernel(x1_ref, o5_ref, os_ref, wo5_ref, wos_ref, nw_ref, wg_ref, wu_ref, wd_ref,
                     fin_ref, y_ref):
    x2 = x1_ref[...] + (_dot(o5_ref[...].astype(BF16), wo5_ref[...])
                        + _dot(os_ref[...].astype(BF16), wos_ref[...]))
    x3 = _swiglu_half(x2, nw_ref[...], wg_ref, wu_ref, wd_ref)
    y_ref[...] = _rms(x3, fin_ref[...])


def _out_ffn2(x1, o5, o_ssd, wo5, wos, nw, wg, wu, wd, fin, tm):
    m, d = x1.shape
    dff = wg.shape[1]
    row = lambda w: pl.BlockSpec((tm, w), lambda i: (i, 0))
    return pl.pallas_call(
        _out_ffn2_kernel,
        grid=(m // tm,),
        in_specs=[row(d), row(o5.shape[1]), row(o_ssd.shape[1]),
                  _const_spec(wo5.shape), _const_spec(wos.shape), _const_spec((1, d)),
                  _const_spec((d, dff)), _const_spec((d, dff)), _const_spec((dff, d)),
                  _const_spec((1, d))],
        out_specs=row(d),
        out_shape=jax.ShapeDtypeStruct((m, d), F32),
        compiler_params=_params(("arbitrary",)),
        name="out_ffn2",
    )(x1, o5, o_ssd, wo5, wos, nw, wg, wu, wd, fin)


def _s5_param_kernel(lre_ref, lim_ref, step_ref, bre_ref, bim_ref,
                     are_ref, aim_ref, bbre_ref, bbim_ref):
    lre, lim, step = lre_ref[...], lim_ref[...], step_ref[...]
    mag = jnp.exp(lre * step)
    ang = lim * step
    are = mag * jnp.cos(ang)
    aim = mag * jnp.sin(ang)
    den = lre * lre + lim * lim
    nre, nim = are - 1.0, aim
    cre = (nre * lre + nim * lim) / den
    cim = (nim * lre - nre * lim) / den
    are_ref[...] = are
    aim_ref[...] = aim
    bre, bim = bre_ref[...], bim_ref[...]
    bbre_ref[...] = cre * bre - cim * bim
    bbim_ref[...] = cre * bim + cim * bre


def _s5_params(lam_re, lam_im, log_step, b_re, b_im):
    g, p = lam_re.shape
    hh = b_re.shape[-1]
    gp = g * p
    step = jnp.exp(log_step.astype(F32))
    step_col = jnp.broadcast_to(step[:, None], (g, p)).reshape(gp, 1)
    full = lambda shape: pl.BlockSpec(shape, lambda: (0,) * len(shape))
    are, aim, bbre, bbim = pl.pallas_call(
        _s5_param_kernel,
        in_specs=[full((gp, 1)), full((gp, 1)), full((gp, 1)), full((gp, hh)), full((gp, hh))],
        out_specs=[full((gp, 1)), full((gp, 1)), full((gp, hh)), full((gp, hh))],
        out_shape=[jax.ShapeDtypeStruct((gp, 1), F32), jax.ShapeDtypeStruct((gp, 1), F32),
                   jax.ShapeDtypeStruct((gp, hh), F32), jax.ShapeDtypeStruct((gp, hh), F32)],
        name="s5_params",
    )(lam_re.astype(F32).reshape(gp, 1), lam_im.astype(F32).reshape(gp, 1), step_col,
      b_re.astype(F32).reshape(gp, hh), b_im.astype(F32).reshape(gp, hh))
    return (are.reshape(1, gp), aim.reshape(1, gp),
            bbre.reshape(g, p, hh), bbim.reshape(g, p, hh))


def _block_diag(w, nblk):
    g, a, b = w.shape
    gl = g // nblk
    w = w.reshape(nblk, gl, a, b)
    eye = jnp.eye(gl, dtype=w.dtype)
    out = w[:, :, :, None, :] * eye[None, :, None, :, None]
    return out.reshape(nblk, gl * a, gl * b)


def _s5_kernel(h_ref, wu_ref, wb_ref, wcre_ref, wcim_ref, are_ref, aim_ref, d_ref, wglu_ref,
               bglu_ref, s0re_ref, s0im_ref, o5_ref, sre_ref, sim_ref, u_s, xre_s, xim_s, y_s,
               *perm_s, nb, tl):
    width = u_s.shape[1]
    nstate = xre_s.shape[1]
    nblk = wb_ref.shape[0]
    cin = width // nblk
    cst = nstate // nblk
    r = nb * tl
    pack = 2 * SUBLANES

    @pl.when(pl.program_id(0) == 0)
    def _():
        sre_ref[...] = s0re_ref[...]
        sim_ref[...] = s0im_ref[...]

    if perm_s:
        slab_s, = perm_s
        assert cin == LANES and nb == SUBLANES
        pitch = tl + SUBLANES
        u_bt = _dot(h_ref[...].reshape(r, h_ref.shape[2]), wu_ref[...])
        for b in range(nb):
            for k in range(nblk):
                slab_s[k, b * pitch:b * pitch + tl, :] = u_bt[b * tl:(b + 1) * tl,
                                                              k * cin:(k + 1) * cin]

        def to_time_major(t, carry):
            rows = pl.ds(pl.multiple_of(t * nb, nb), nb)
            for k in range(nblk):
                u_s[rows, k * cin:(k + 1) * cin] = slab_s[k, pl.ds(t, nb, stride=pitch), :]
            return carry

        lax.fori_loop(0, tl, to_time_major, 0, unroll=8)
        u = u_s[...]
    else:
        u = _dot(h_ref[...], wu_ref[...])
        u_s[...] = u
    ub = u.astype(BF16)

    def drive(k):
        bu = _dot(ub[:, k * cin:(k + 1) * cin], wb_ref[k])
        xre_s[:, k * cst:(k + 1) * cst] = bu[:, :cst]
        xim_s[:, k * cst:(k + 1) * cst] = bu[:, cst:]

    def scan(k):
        lanes = slice(k * cst, (k + 1) * cst)
        a_re = jnp.broadcast_to(are_ref[:, lanes], (SUBLANES, cst))
        a_im = jnp.broadcast_to(aim_ref[:, lanes], (SUBLANES, cst))

        def step(rows, s_re, s_im):
            n_re = a_re * s_re - a_im * s_im + xre_s[rows, lanes]
            n_im = a_re * s_im + a_im * s_re + xim_s[rows, lanes]
            xre_s[rows, lanes] = n_re
            xim_s[rows, lanes] = n_im
            return n_re, n_im

        if nb == SUBLANES:
            s_re, s_im = sre_ref[:, lanes], sim_ref[:, lanes]
            for t in range(tl):
                s_re, s_im = step(slice(t * nb, (t + 1) * nb), s_re, s_im)
            sre_ref[:, lanes] = s_re
            sim_ref[:, lanes] = s_im
        else:
            def group_body(sg, carry):
                srow = pl.ds(pl.multiple_of(sg * SUBLANES, SUBLANES), SUBLANES)
                st = (sre_ref[srow, lanes], sim_ref[srow, lanes])
                for t in range(tl):
                    st = step(pl.ds(pl.multiple_of(t * nb + sg * SUBLANES, SUBLANES), SUBLANES), *st)
                sre_ref[srow, lanes], sim_ref[srow, lanes] = st
                return carry

            lax.fori_loop(0, nb // SUBLANES, group_body, 0)

    def readout(k):
        st = slice(k * cst, (k + 1) * cst)
        y_s[:, k * cin:(k + 1) * cin] = (_dot(xre_s[:, st].astype(BF16), wcre_ref[k])
                                         - _dot(xim_s[:, st].astype(BF16), wcim_ref[k]))

    for k in range(nblk + 2):
        if k < nblk:
            drive(k)
        if 1 <= k <= nblk:
            scan(k - 1)
        if k >= 2:
            readout(k - 2)
    y = y_s[...] + d_ref[...] * u_s[...]
    v = _gelu_tanh(y)
    gate = _dot(v.astype(BF16), wglu_ref[...]) + bglu_ref[...]
    o5 = v * _sigmoid(gate)
    if perm_s:
        for k in range(nblk):
            slab_s[k, 0:r, :] = o5[:, k * cin:(k + 1) * cin]
        per_b = tl // pack

        def to_batch_major(i, carry):
            b = i // per_b
            t0 = (i % per_b) * pack
            for k in range(nblk):
                lo = slab_s[k, pl.ds(t0 * nb + b, SUBLANES, stride=nb), :]
                hi = slab_s[k, pl.ds((t0 + SUBLANES) * nb + b, SUBLANES, stride=nb), :]
                o5_ref[b, pl.ds(pl.multiple_of(t0, pack), pack), k * cin:(k + 1) * cin] = (
                    jnp.concatenate([lo, hi], axis=0).astype(BF16))
            return carry

        lax.fori_loop(0, nb * per_b, to_batch_major, 0, unroll=4)
    else:
        o5_ref[...] = o5.astype(BF16)


def _s5(h, wu, wb, wcre, wcim, are, aim, dsk, wglu, bglu, s0re, s0im, nb, tl):
    width = wu.shape[1]
    nstate = are.shape[1]
    r = nb * tl
    batch_major = h.ndim == 3
    if batch_major:
        steps = h.shape[1] // tl
        blk = lambda w: pl.BlockSpec((nb, tl, w), lambda i: (0, i, 0))
        o_shape = (nb, h.shape[1], width)
        extra = [pltpu.VMEM((width // LANES, nb * (tl + SUBLANES), LANES), F32)]
    else:
        steps = h.shape[0] // r
        blk = lambda w: pl.BlockSpec((r, w), lambda i: (i, 0))
        o_shape = (h.shape[0], width)
        extra = []
    kern = functools.partial(_s5_kernel, nb=nb, tl=tl)
    return pl.pallas_call(
        kern,
        grid=(steps,),
        in_specs=[blk(h.shape[-1]), _const_spec(wu.shape), _const_spec(wb.shape),
                  _const_spec(wcre.shape), _const_spec(wcim.shape), _const_spec(are.shape),
                  _const_spec(aim.shape), _const_spec(dsk.shape), _const_spec(wglu.shape),
                  _const_spec(bglu.shape), _const_spec(s0re.shape), _const_spec(s0im.shape)],
        out_specs=[blk(width), pl.BlockSpec((nb, nstate), lambda i: (0, 0)),
                   pl.BlockSpec((nb, nstate), lambda i: (0, 0))],
        out_shape=[jax.ShapeDtypeStruct(o_shape, BF16),
                   jax.ShapeDtypeStruct((nb, nstate), F32),
                   jax.ShapeDtypeStruct((nb, nstate), F32)],
        scratch_shapes=[pltpu.VMEM((r, width), F32), pltpu.VMEM((r, nstate), F32),
                        pltpu.VMEM((r, nstate), F32), pltpu.VMEM((r, width), F32)] + extra,
        compiler_params=_params(("arbitrary",)),
        name="s5",
    )(h, wu, wb, wcre, wcim, are, aim, dsk, wglu, bglu, s0re, s0im)


def _split2(x):
    hi = x.astype(BF16)
    lo = (x - hi.astype(F32)).astype(BF16)
    return jnp.concatenate([hi, lo], axis=1)


def _split3(x):
    hi = x.astype(BF16)
    r1 = x - hi.astype(F32)
    mid = r1.astype(BF16)
    lo = (r1 - mid.astype(F32)).astype(BF16)
    return jnp.concatenate([hi, mid, lo], axis=1)


def _group_norm(y, zg, nw, gw):
    outs = []
    for g in range(SSD_GROUPS):
        seg = y[:, g * gw:(g + 1) * gw] * zg[:, g * gw:(g + 1) * gw]
        outs.append(seg * lax.rsqrt(jnp.mean(seg * seg, axis=-1, keepdims=True) + EPS))
    return jnp.concatenate(outs, axis=1) * nw


def _ssds_pre_kernel(h_ref, wz_ref, wx_ref, wdt_ref, cw_ref, cb_ref, dtb_ref, alog_ref, dsk_ref,
                     conv0_ref, g2_ref, exp_ref, c_o, b_o, xdd_o, yp_o, eac_o, zg_o, cdec_o, cv_o):
    nbt = conv0_ref.shape[1]
    seq = h_ref.shape[0] // nbt
    width = zg_o.shape[1]
    nbc = c_o.shape[1]
    hb = h_ref[...]
    zg_o[...] = _silu(_dot(hb, wz_ref[...]))
    dt = _softplus(_dot(hb, wdt_ref[...]) + dtb_ref[...])
    la = dt * (-jnp.exp(alog_ref[...]))
    xbc = _dot(hb, wx_ref[...])
    rows = [slice(t * nbt, (t + 1) * nbt) for t in range(seq)]
    full = [conv0_ref[k] for k in range(SSD_CONV - 1)] + [xbc[r] for r in rows]
    for k in range(SSD_CONV - 1):
        cv_o[k] = full[seq + k]
    acums = []
    for t in range(seq):
        acums.append(la[rows[t]] if t == 0 else acums[-1] + la[rows[t]])
    tot = acums[-1]
    cdec_o[...] = jnp.exp(tot)
    xs, bq, cq = [], [], []
    for t in range(seq):
        acc = cb_ref[...]
        for k in range(SSD_CONV):
            acc = acc + cw_ref[k:k + 1, :] * full[t + k]
        xc = _silu(acc)
        xs.append(xc[:, :width])
        bq.append(xc[:, width:width + nbc].astype(BF16).astype(F32))
        cq.append(xc[:, width + nbc:].astype(BF16).astype(F32))
    for t in range(seq):
        r = rows[t]
        b_o[r, :] = bq[t]
        c_o[r, :] = cq[t]
        xdd_o[r, :] = xs[t] * _dot(_split2(dt[r] * jnp.exp(tot - acums[t])), exp_ref[...])
        eac_o[r, :] = _dot(_split2(jnp.exp(acums[t])), exp_ref[...])
        yp = dsk_ref[...] * xs[t]
        for s in range(t + 1):
            cbx = _dot(_split2(cq[t] * bq[s]), g2_ref[...])
            coef = cbx * jnp.exp(acums[t] - acums[s]) * dt[rows[s]]
            yp = yp + _dot(_split2(coef), exp_ref[...]) * xs[s]
        yp_o[r, :] = yp


def _ssds_state_kernel(c_ref, b_ref, xdd_ref, yp_ref, eac_ref, zg_ref, cdec_ref, st_ref, nw_ref,
                       o_ref, sto_ref, yoff_s, xddt_s):
    seq, nb, width = xdd_ref.shape
    nheads, hd_dim, nst = st_ref.shape[1:]
    hpg = nheads // SSD_GROUPS
    gw = width // SSD_GROUPS
    rws = seq * nb
    cst = c_ref[...].reshape(rws, c_ref.shape[2]).astype(BF16)
    bst = b_ref[...].reshape(rws, b_ref.shape[2])
    xdd = xdd_ref[...].reshape(rws, width)
    xddt_s[...] = jnp.concatenate([xdd, jnp.zeros((LANES - rws, width), F32)], axis=0).T.astype(BF16)
    rowb = lax.broadcasted_iota(jnp.int32, (rws, gw), 0) % nb
    zpad = jnp.zeros((LANES - rws, nst), BF16)
    for b in range(nb):
        mine = rowb == b
        for g in range(SSD_GROUPS):
            gl = slice(g * gw, (g + 1) * gw)
            hg = st_ref[b, g * hpg:(g + 1) * hpg].reshape(gw, nst)
            res = _dot_nt(cst[:, g * nst:(g + 1) * nst], hg.astype(BF16))
            yoff_s[:, gl] = jnp.where(mine, res, 0.0) if b == 0 else jnp.where(mine, res, yoff_s[:, gl])
            bm = jnp.where(mine[:, :nst], bst[:, g * nst:(g + 1) * nst], 0.0).astype(BF16)
            upd = _dot(xddt_s[gl, :], jnp.concatenate([bm, zpad], axis=0))
            for jh in range(hpg):
                hd = g * hpg + jh
                hrows = slice(jh * hd_dim, (jh + 1) * hd_dim)
                sto_ref[b, hd] = hg[hrows] * cdec_ref[b, hd] + upd[hrows]
    y = yp_ref[...].reshape(rws, width) + yoff_s[...] * eac_ref[...].reshape(rws, width)
    o = _group_norm(y, zg_ref[...].reshape(rws, width), nw_ref[...], gw)
    o_ref[...] = o.reshape(seq, nb, width)


def _ssds(h_tm, wz, wx, wdt, cw, cb, dtb, alog, dsk, nw, ssd0, conv0_tm, nb):
    m, d = h_tm.shape
    nseq, nheads, hd_dim, nst = ssd0.shape
    seq = m // nseq
    width = wz.shape[1]
    xbc_w = wx.shape[1]
    nbc = SSD_GROUPS * nst
    hpg = nheads // SSD_GROUPS
    head = jnp.arange(LANES)
    exp1 = (head[:, None] == (jnp.arange(width) // hd_dim)[None, :]).astype(BF16)
    g1 = ((jnp.arange(nbc) // nst)[:, None] == (head // hpg)[None, :]) & (head < nheads)[None, :]
    exp_m = jnp.concatenate([exp1, exp1], axis=0)
    g2 = jnp.concatenate([g1, g1], axis=0).astype(BF16)
    full = lambda shape: pl.BlockSpec(shape, lambda: (0,) * len(shape))
    ins = (h_tm, wz, wx, wdt, cw, cb, dtb, alog, dsk, conv0_tm, g2, exp_m)
    outs = [jax.ShapeDtypeStruct((m, nbc), F32), jax.ShapeDtypeStruct((m, nbc), F32),
            jax.ShapeDtypeStruct((m, width), F32), jax.ShapeDtypeStruct((m, width), F32),
            jax.ShapeDtypeStruct((m, width), F32), jax.ShapeDtypeStruct((m, width), F32),
            jax.ShapeDtypeStruct((nseq, LANES), F32),
            jax.ShapeDtypeStruct((SSD_CONV - 1, nseq, xbc_w), F32)]
    c, b, xdd, yp, eac, zg, cdec, conv1_tm = pl.pallas_call(
        _ssds_pre_kernel,
        in_specs=[full(a.shape) for a in ins],
        out_specs=[full(o.shape) for o in outs],
        out_shape=outs,
        compiler_params=pltpu.CompilerParams(vmem_limit_bytes=VMEM_LIMIT),
        name="ssd_sample_pre",
    )(*ins)

    tm3 = lambda a: a.reshape(seq, nseq, a.shape[1])
    blk = lambda w: pl.BlockSpec((seq, nb, w), lambda i: (0, i, 0))
    st_spec = pl.BlockSpec((nb, nheads, hd_dim, nst), lambda i: (i, 0, 0, 0))
    o, ssd1 = pl.pallas_call(
        _ssds_state_kernel,
        grid=(nseq // nb,),
        in_specs=[blk(nbc), blk(nbc), blk(width), blk(width), blk(width), blk(width),
                  pl.BlockSpec((nb, LANES), lambda i: (i, 0), memory_space=pltpu.SMEM),
                  st_spec, _const_spec(nw.shape)],
        out_specs=[blk(width), st_spec],
        out_shape=[jax.ShapeDtypeStruct((seq, nseq, width), F32),
                   jax.ShapeDtypeStruct(ssd0.shape, F32)],
        scratch_shapes=[pltpu.VMEM((seq * nb, width), F32), pltpu.VMEM((width, LANES), BF16)],
        compiler_params=_params(("arbitrary",)),
        name="ssd_sample_state",
    )(tm3(c), tm3(b), tm3(xdd), tm3(yp), tm3(eac), tm3(zg), cdec, ssd0, nw)
    return o.reshape(m, width), ssd1, conv1_tm


def _ssdp_kernel(h_ref, wz_ref, wx_ref, wdt_ref, cw_ref, cb_ref, dtb_ref, alog_ref, dsk_ref,
                 nw_ref, btril_ref, sel_ref, exp_ref, perm_ref, permt_ref, o_ref, st_ref, cv_ref,
                 xbc_s, zg_s, acol_s, acumt_s, e2_s, e3_s, xd_s, xdd_s, ysk_s, b_s, c_s,
                 cdec_s, ht_s, tail_s, *, nch, col_chunk):
    t = SSD_CHUNK
    r = nch * t
    j = pl.program_id(1)
    nheads, hd_dim, nst = st_ref.shape[1:]
    width = nheads * hd_dim
    gw = width // SSD_GROUPS
    hpg = nheads // SSD_GROUPS
    xbc_w = xbc_s.shape[1]
    ntap = SSD_CONV - 1
    pre = ntap * SUBLANES
    ext = pre + t
    steps = t // SUBLANES
    last_rows = [t - 1 - SUBLANES * (ntap - 1 - v) for v in range(ntap)]

    @pl.when(j == 0)
    def _():
        ht_s[...] = jnp.zeros(ht_s.shape, F32)
        tail_s[...] = jnp.zeros(tail_s.shape, F32)

    hb = jnp.concatenate([_dot(perm_ref[...], h_ref[c * t:(c + 1) * t, :]) for c in range(nch)],
                         axis=0).astype(BF16)
    dt = _softplus(_dot(hb, wdt_ref[...]) + dtb_ref[...])
    la = dt * (-jnp.exp(alog_ref[...]))

    def xbc_cols(jc):
        lanes = slice(jc * col_chunk, (jc + 1) * col_chunk)
        xall = _dot(hb, wx_ref[:, lanes])
        for c in range(nch):
            xbc_s[c * ext + pre:(c + 1) * ext, lanes] = xall[c * t:(c + 1) * t, :]
        for c in range(nch):
            base = c * ext
            xbc_s[base:base + pre, lanes] = xbc_s[base + t - 1:base + pre + t - 1, lanes]
            for v in range(ntap):
                if c == 0:
                    prev = tail_s[v:v + 1, lanes]
                else:
                    prow = base - ext + pre + last_rows[v]
                    prev = xbc_s[prow:prow + 1, lanes]
                xbc_s[base + v * SUBLANES:base + v * SUBLANES + 1, lanes] = prev
            acc = cb_ref[:, lanes]
            for k in range(SSD_CONV):
                acc = acc + cw_ref[k:k + 1, lanes] * xbc_s[base + k * SUBLANES:
                                                           base + k * SUBLANES + t, lanes]
            xc = _silu(acc)
            rows = slice(c * t, (c + 1) * t)
            lo = jc * col_chunk
            if lo < width:
                xd_s[rows, lanes] = xc.astype(BF16)
                xdd_s[rows, lanes] = (xc * e2_s[rows, lanes]).astype(BF16)
                ysk_s[rows, lanes] = xc * dsk_ref[:, lanes]
            elif lo < width + SSD_GROUPS * nst:
                b_s[rows, lo - width:lo - width + col_chunk] = xc.astype(BF16)
            else:
                cl = lo - width - SSD_GROUPS * nst
                c_s[rows, cl:cl + col_chunk] = xc.astype(BF16)

    x_cols = width // col_chunk
    for jc in range(x_cols, xbc_w // col_chunk):
        xbc_cols(jc)

    r3 = _dot(btril_ref[...], _split3(la))
    acum = r3[:, :LANES] + r3[:, LANES:2 * LANES] + r3[:, 2 * LANES:]
    tots = [acum[(c + 1) * t - 1:(c + 1) * t, :] for c in range(nch)]
    arow = acum - jnp.log(dt)
    for c in range(nch):
        acumt_s[c] = arow[c * t:(c + 1) * t, :].T
    tot_rows = jnp.concatenate([jnp.broadcast_to(v, (t, LANES)) for v in tots], axis=0)
    dstate = jnp.exp(tot_rows - acum)
    cdec = jnp.exp(jnp.concatenate([jnp.broadcast_to(v, (SUBLANES, LANES)) for v in tots], axis=0))
    e2_s[...] = _dot(_split2(dt * dstate), exp_ref[...])
    e3_s[...] = _dot(_split2(jnp.exp(acum)), exp_ref[...])
    cdx = _dot(_split2(cdec), exp_ref[...])
    for c in range(nch):
        cdec_s[c] = cdx[c * SUBLANES:(c + 1) * SUBLANES, :]
    acol_s[...] = _dot(_split3(acum), sel_ref[...])

    for jc in range(x_cols):
        lanes = slice(jc * col_chunk, (jc + 1) * col_chunk)
        xbc_cols(jc)
        zg_s[:, lanes] = _silu(_dot(hb, wz_ref[:, lanes]))

    for v in range(ntap):
        row = (nch - 1) * ext + pre + last_rows[v]
        tail_s[v:v + 1, :] = xbc_s[row:row + 1, :]
    cv_ref[0] = tail_s[...]

    def token(i):
        return (i % SUBLANES) * steps + i // SUBLANES

    causal = (token(lax.broadcasted_iota(jnp.int32, (t, t), 0))
              >= token(lax.broadcasted_iota(jnp.int32, (t, t), 1)))
    lane = lax.broadcasted_iota(jnp.int32, (t, LANES), 1)
    keep_lo = jnp.where(lane < hd_dim, 1.0, 0.0).astype(BF16)
    keep_hi = jnp.where(lane < hd_dim, 0.0, 1.0).astype(BF16)
    nw = nw_ref[...]

    def chunk_body(c, carry):
        rows = pl.ds(pl.multiple_of(c * t, t), t)
        ys = []
        for g in range(SSD_GROUPS):
            gl = slice(g * gw, (g + 1) * gw)
            cg = c_s[rows, g * nst:(g + 1) * nst]
            bg = b_s[rows, g * nst:(g + 1) * nst]
            cbm = jnp.where(causal, _dot_nt(cg, bg), 0.0)
            pairs = []
            for jp in range(hpg // 2):
                ha = g * hpg + 2 * jp
                ms = []
                for hd in (ha, ha + 1):
                    diff = acol_s[rows, hd * t:(hd + 1) * t] - acumt_s[c, hd:hd + 1, :]
                    ms.append((cbm * jnp.exp(jnp.where(causal, diff, -jnp.inf))).astype(BF16))
                xdp = xd_s[rows, ha * hd_dim:ha * hd_dim + LANES]
                rhs = jnp.concatenate([xdp * keep_lo, xdp * keep_hi], axis=0)
                pairs.append(_dot(jnp.concatenate(ms, axis=1), rhs))
            htg = ht_s[:, gl]
            y = (jnp.concatenate(pairs, axis=1) + _dot(cg, htg.astype(BF16)) * e3_s[rows, gl]
                 + ysk_s[rows, gl])
            ys.append(y)
            upd = lax.dot_general(bg, xdd_s[rows, gl], (((0,), (0,)), ((), ())),
                                  preferred_element_type=F32)
            ht_s[:, gl] = htg * cdec_s[c, 0:1, gl] + upd
        o = _group_norm(jnp.concatenate(ys, axis=1), zg_s[rows, :], nw, gw).astype(BF16)
        o_ref[rows, :] = _dot(permt_ref[...], o).astype(BF16)
        return carry

    lax.fori_loop(0, nch, chunk_body, 0)

    @pl.when(j == pl.num_programs(1) - 1)
    def _():
        for pr in range(nheads // 2):
            tt = ht_s[:, pr * LANES:(pr + 1) * LANES].T
            st_ref[0, 2 * pr] = tt[:hd_dim]
            st_ref[0, 2 * pr + 1] = tt[hd_dim:]


def _ssdp(h, wz, wx, wdt, cw, cb, dtb, alog, dsk, nw, nseq, nheads, nch):
    m, d = h.shape
    width = wz.shape[1]
    xbc_w = wx.shape[1]
    hd_dim = width // nheads
    nst = SSD_STATE
    t = SSD_CHUNK
    r = nch * t
    nblk = m // (nseq * r)
    ri = jnp.arange(r)
    steps = t // SUBLANES
    tok = (ri % t % SUBLANES) * steps + ri % t // SUBLANES
    btril = ((tok[:, None] >= tok[None, :]) & (ri[:, None] // t == ri[None, :] // t)).astype(BF16)
    perm = (tok[:t, None] == jnp.arange(t)[None, :]).astype(BF16)
    head = jnp.arange(LANES)
    exp1 = (head[:, None] == (jnp.arange(width) // hd_dim)[None, :]).astype(BF16)
    sel1 = ((head[:, None] == (jnp.arange(nheads * t) // t)[None, :])).astype(BF16)
    exp_m = jnp.concatenate([exp1, exp1], axis=0)
    sel_m = jnp.concatenate([sel1, sel1, sel1], axis=0)
    row = lambda w: pl.BlockSpec((r, w), lambda i, j: (i * nblk + j, 0))
    kern = functools.partial(_ssdp_kernel, nch=nch, col_chunk=4 * LANES)
    vm = lambda shape, dt: pltpu.VMEM(shape, dt)
    return pl.pallas_call(
        kern,
        grid=(nseq, nblk),
        in_specs=[row(d), _const_spec(wz.shape), _const_spec(wx.shape), _const_spec(wdt.shape),
                  _const_spec(cw.shape), _const_spec(cb.shape), _const_spec(dtb.shape),
                  _const_spec(alog.shape), _const_spec(dsk.shape), _const_spec(nw.shape),
                  _const_spec(btril.shape), _const_spec(sel_m.shape), _const_spec(exp_m.shape),
                  _const_spec(perm.shape), _const_spec(perm.shape)],
        out_specs=[row(width),
                   pl.BlockSpec((1, nheads, hd_dim, nst), lambda i, j: (i, 0, 0, 0)),
                   pl.BlockSpec((1, SSD_CONV - 1, xbc_w), lambda i, j: (i, 0, 0))],
        out_shape=[jax.ShapeDtypeStruct((m, width), BF16),
                   jax.ShapeDtypeStruct((nseq, nheads, hd_dim, nst), F32),
                   jax.ShapeDtypeStruct((nseq, SSD_CONV - 1, xbc_w), F32)],
        scratch_shapes=[vm((nch * (t + (SSD_CONV - 1) * SUBLANES), xbc_w), F32),
                        vm((r, width), F32),
                        vm((r, nheads * t), F32), vm((nch, LANES, t), F32),
                        vm((r, width), F32), vm((r, width), F32),
                        vm((r, width), BF16), vm((r, width), BF16), vm((r, width), F32),
                        vm((r, SSD_GROUPS * nst), BF16), vm((r, SSD_GROUPS * nst), BF16),
                        vm((nch, SUBLANES, width), F32), vm((nst, width), F32),
                        vm((SSD_CONV - 1, xbc_w), F32)],
        compiler_params=_params(("arbitrary", "arbitrary")),
        name="ssd_prompt",
    )(h, wz, wx, wdt, cw, cb, dtb, alog, dsk, nw, btril, sel_m, exp_m, perm, perm.T)


def _pad_lanes(v, n=LANES):
    return jnp.pad(v, [(0, 0)] * (v.ndim - 1) + [(0, n - v.shape[-1])])


def _s5_call(h_tm, s5re0, s5im0, w, nb, tl):
    g, p = s5re0.shape[1:]
    o5_tm, re1, im1 = _s5(h_tm, w["w_u"], w["s5_wb"], w["s5_wcre"], w["s5_wcim"], w["s5_are"],
                          w["s5_aim"], w["s5_d"], w["s5_wglu"], w["s5_bglu"],
                          s5re0.reshape(nb, g * p), s5im0.reshape(nb, g * p), nb, tl)
    return o5_tm, re1.reshape(nb, g, p), im1.reshape(nb, g, p)


def _ffn1_call(x2d, w, tm):
    return _ffn1(x2d, w["ffn1_norm"], w["ffn1_wg"], w["ffn1_wu"], w["ffn1_wd"], w["mix_norm"], tm)


def _out_call(x1, o5, o_ssd, w, tm):
    return _out_ffn2(x1, o5, o_ssd, w["w_out5"], w["w_outs"], w["ffn2_norm"], w["ffn2_wg"],
                     w["ffn2_wu"], w["ffn2_wd"], w["final_norm"], tm)


def _prompt_layer(x, w, *, tm, s5_tl, ssd_nch):
    bsz, seq, d = x.shape
    m = bsz * seq
    g, p = w["s5_are"].shape[1] // S5_STATE, S5_STATE
    x1, h = _ffn1_call(x.reshape(m, d), w, tm)
    zeros = jnp.zeros((bsz, g, p), F32)
    o5, s5re1, s5im1 = _s5_call(h.reshape(bsz, seq, d), zeros, zeros, w, bsz, s5_tl)
    o5 = o5.reshape(m, -1)
    nheads = w["ssd_d"].shape[1] // SSD_HEAD_DIM
    o_ssd, ssd1, conv1 = _ssdp(h, w["w_z"], w["w_x"], w["w_dt"], w["conv_w"], w["conv_b"],
                               w["dt_bias"], w["a_log"], w["ssd_d"], w["ssd_norm"],
                               bsz, nheads, ssd_nch)
    y = _out_call(x1, o5, o_ssd, w, tm)
    return y.reshape(bsz, seq, d), s5re1, s5im1, ssd1, conv1


def _sample_layer(x, s5re0, s5im0, ssd0, conv0, w, *, ssd_nb):
    bsz, seq, d = x.shape
    m = bsz * seq
    x_tm = x.transpose(1, 0, 2).reshape(m, d)
    x1, h = _ffn1_call(x_tm, w, m)
    o5, s5re1, s5im1 = _s5_call(h, s5re0, s5im0, w, bsz, seq)
    o_ssd, ssd1, conv1_tm = _ssds(h, w["w_z"], w["w_x"], w["w_dt"], w["conv_w"], w["conv_b"],
                                  w["dt_bias"], w["a_log"], w["ssd_d"], w["ssd_norm"], ssd0,
                                  conv0.transpose(1, 0, 2), ssd_nb)
    y = _out_call(x1, o5, o_ssd, w, m)
    return (y.reshape(seq, bsz, d).transpose(1, 0, 2), s5re1, s5im1, ssd1,
            conv1_tm.transpose(1, 0, 2))


def kernel(x_prompt, x_sample, state_s5_re, state_s5_im, state_ssd, state_conv, ffn1_norm, ffn1_w_gate, ffn1_w_up, ffn1_w_down, mix_norm, w_in, s5_lambda_re, s5_lambda_im, s5_log_step, s5_b_re, s5_b_im, s5_c_re, s5_c_im, s5_d, s5_w_glu, s5_b_glu, ssd_conv_w, ssd_conv_b, ssd_dt_bias, ssd_a_log, ssd_d, ssd_norm, w_out, ffn2_norm, ffn2_w_gate, ffn2_w_up, ffn2_w_down, final_norm):
    depth = w_in.shape[0]
    assert depth == 1, "single-layer stack"
    i = 0
    d = x_prompt.shape[-1]
    g, p = s5_lambda_re.shape[1:]
    s5_width = g * S5_GROUP
    nheads = ssd_a_log.shape[1]
    ssd_width = nheads * SSD_HEAD_DIM
    xbc_w = ssd_conv_w.shape[-1]
    c0, c1, c2 = s5_width, s5_width + ssd_width, s5_width + ssd_width + xbc_w
    row = lambda v: v.astype(F32).reshape(1, -1)

    are, aim, bbre, bbim = _s5_params(s5_lambda_re[i], s5_lambda_im[i], s5_log_step[i],
                                      s5_b_re[i], s5_b_im[i])
    nblk = s5_width // LANES
    wb = jnp.concatenate([_block_diag(bbre.transpose(0, 2, 1), nblk),
                          _block_diag(bbim.transpose(0, 2, 1), nblk)], axis=-1).astype(BF16)
    wcre = _block_diag(s5_c_re[i].astype(F32).transpose(0, 2, 1), nblk).astype(BF16)
    wcim = _block_diag(s5_c_im[i].astype(F32).transpose(0, 2, 1), nblk).astype(BF16)

    win = w_in[i]
    w = {
        "ffn1_norm": row(ffn1_norm[i]), "mix_norm": row(mix_norm[i]),
        "ffn2_norm": row(ffn2_norm[i]), "final_norm": row(final_norm),
        "ffn1_wg": ffn1_w_gate[i].astype(BF16), "ffn1_wu": ffn1_w_up[i].astype(BF16),
        "ffn1_wd": ffn1_w_down[i].astype(BF16),
        "ffn2_wg": ffn2_w_gate[i].astype(BF16), "ffn2_wu": ffn2_w_up[i].astype(BF16),
        "ffn2_wd": ffn2_w_down[i].astype(BF16),
        "w_u": win[:, :c0].astype(BF16), "w_z": win[:, c0:c1].astype(BF16),
        "w_x": win[:, c1:c2].astype(BF16), "w_dt": _pad_lanes(win[:, c2:]).astype(BF16),
        "s5_wb": wb, "s5_wcre": wcre, "s5_wcim": wcim, "s5_are": are, "s5_aim": aim,
        "s5_d": row(s5_d[i]), "s5_wglu": s5_w_glu[i].astype(BF16), "s5_bglu": row(s5_b_glu[i]),
        "conv_w": ssd_conv_w[i].astype(F32), "conv_b": row(ssd_conv_b[i]),
        "dt_bias": _pad_lanes(row(ssd_dt_bias[i])), "a_log": _pad_lanes(row(ssd_a_log[i])),
        "ssd_d": jnp.repeat(ssd_d[i].astype(F32), SSD_HEAD_DIM).reshape(1, -1),
        "ssd_norm": row(ssd_norm[i]),
        "w_out5": w_out[i][:s5_width].astype(BF16), "w_outs": w_out[i][s5_width:].astype(BF16),
    }

    lp = x_prompt.shape[1]
    yp, p_re, p_im, p_ssd, p_conv = _prompt_layer(
        x_prompt, w, tm=512, s5_tl=min(lp, 128), ssd_nch=min(lp // SSD_CHUNK, 4))
    ys, s_re, s_im, s_ssd, s_conv = _sample_layer(
        x_sample, state_s5_re[i], state_s5_im[i], state_ssd[i], state_conv[i], w,
        ssd_nb=SUBLANES)
    st = lambda v: v[None]
    return (yp, ys, st(p_re), st(p_im), st(p_ssd), st(p_conv),
            st(s_re), st(s_im), st(s_ssd), st(s_conv))
```

```python
import functools
import math

import jax
import jax.numpy as jnp
from jax import lax
from jax.experimental import pallas as pl
from jax.experimental.pallas import tpu as pltpu

F32 = jnp.float32
BF16 = jnp.bfloat16
EPS = 1e-6

LANES = 128
SUBLANES = 8
VMEM_LIMIT = 56 * 1024 * 1024

S5_GROUP = 16
S5_STATE = 64
SSD_HEAD_DIM = 64
SSD_GROUPS = 4
SSD_STATE = 128
SSD_CONV = 4
SSD_CHUNK = 128


def _sigmoid(x):
    return 0.5 + 0.5 * jnp.tanh(0.5 * x)


def _silu(x):
    hx = 0.5 * x
    return hx + hx * jnp.tanh(hx)


def _gelu_tanh(x):
    c = math.sqrt(2.0 / math.pi)
    return 0.5 * x * (1.0 + jnp.tanh(c * (x + 0.044715 * (x * x * x))))


def _softplus(x):
    return jnp.maximum(x, 0.0) + jnp.log1p(jnp.exp(-jnp.abs(x)))


def _rms(x, w):
    return x * lax.rsqrt(jnp.mean(x * x, axis=-1, keepdims=True) + EPS) * w


def _dot(a, b):
    return jnp.dot(a, b, preferred_element_type=F32)


def _dot_nt(a, b):
    return lax.dot_general(a, b, (((1,), (1,)), ((), ())), preferred_element_type=F32)


def _const_spec(shape):
    nd = len(shape)
    return pl.BlockSpec(shape, lambda *_: (0,) * nd, pipeline_mode=pl.Buffered(1))


def _params(sem):
    return pltpu.CompilerParams(dimension_semantics=sem, vmem_limit_bytes=VMEM_LIMIT)


def _swiglu_half(x, nw, wg_ref, wu_ref, wd_ref):
    hn = _rms(x, nw).astype(BF16)
    g = _dot(hn, wg_ref[...])
    u = _dot(hn, wu_ref[...])
    a = (_silu(g) * u).astype(BF16)
    return x + 0.5 * _dot(a, wd_ref[...])


def _ffn1_kernel(x_ref, nw_ref, wg_ref, wu_ref, wd_ref, mixw_ref, x1_ref, h_ref):
    x1 = _swiglu_half(x_ref[...], nw_ref[...], wg_ref, wu_ref, wd_ref)
    x1_ref[...] = x1
    h_ref[...] = _rms(x1, mixw_ref[...]).astype(BF16)


def _ffn1(x, nw, wg, wu, wd, mixw, tm):
    m, d = x.shape
    dff = wg.shape[1]
    row = lambda w: pl.BlockSpec((tm, w), lambda i: (i, 0))
    return pl.pallas_call(
        _ffn1_kernel,
        grid=(m // tm,),
        in_specs=[row(d), _const_spec((1, d)), _const_spec((d, dff)), _const_spec((d, dff)),
                  _const_spec((dff, d)), _const_spec((1, d))],
        out_specs=[row(d), row(d)],
        out_shape=[jax.ShapeDtypeStruct((m, d), F32), jax.ShapeDtypeStruct((m, d), BF16)],
        compiler_params=_params(("arbitrary",)),
        name="ffn1",
    )(x, nw, wg, wu, wd, mixw)


def _out_ffn2_kernel(x1_ref, o5_ref, os_ref, wo5_ref, wos_ref, nw_ref, wg_ref, wu_ref, wd_ref,
                     fin_ref, y_ref):
    x2 = x1_ref[...] + (_dot(o5_ref[...].astype(BF16), wo5_ref[...])
                        + _dot(os_ref[...].astype(BF16), wos_ref[...]))
    x3 = _swiglu_half(x2, nw_ref[...], wg_ref, wu_ref, wd_ref)
    y_ref[...] = _rms(x3, fin_ref[...])


def _out_ffn2(x1, o5, o_ssd, wo5, wos, nw, wg, wu, wd, fin, tm):
    m, d = x1.shape
    dff = wg.shape[1]
    row = lambda w: pl.BlockSpec((tm, w), lambda i: (i, 0))
    return pl.pallas_call(
        _out_ffn2_kernel,
        grid=(m // tm,),
        in_specs=[row(d), row(o5.shape[1]), row(o_ssd.shape[1]),
                  _const_spec(wo5.shape), _const_spec(wos.shape), _const_spec((1, d)),
                  _const_spec((d, dff)), _const_spec((d, dff)), _const_spec((dff, d)),
                  _const_spec((1, d))],
        out_specs=row(d),
        out_shape=jax.ShapeDtypeStruct((m, d), F32),
        compiler_params=_params(("arbitrary",)),
        name="out_ffn2",
    )(x1, o5, o_ssd, wo5, wos, nw, wg, wu, wd, fin)


def _s5_param_kernel(lre_ref, lim_ref, step_ref, bre_ref, bim_ref,
                     are_ref, aim_ref, bbre_ref, bbim_ref):
    lre, lim, step = lre_ref[...], lim_ref[...], step_ref[...]
    mag = jnp.exp(lre * step)
    ang = lim * step
    are = mag * jnp.cos(ang)
    aim = mag * jnp.sin(ang)
    den = lre * lre + lim * lim
    nre, nim = are - 1.0, aim
    cre = (nre * lre + nim * lim) / den
    cim = (nim * lre - nre * lim) / den
    are_ref[...] = are
    aim_ref[...] = aim
    bre, bim = bre_ref[...], bim_ref[...]
    bbre_ref[...] = cre * bre - cim * bim
    bbim_ref[...] = cre * bim + cim * bre


def _s5_params(lam_re, lam_im, log_step, b_re, b_im):
    g, p = lam_re.shape
    hh = b_re.shape[-1]
    gp = g * p
    step = jnp.exp(log_step.astype(F32))
    step_col = jnp.broadcast_to(step[:, None], (g, p)).reshape(gp, 1)
    full = lambda shape: pl.BlockSpec(shape, lambda: (0,) * len(shape))
    are, aim, bbre, bbim = pl.pallas_call(
        _s5_param_kernel,
        in_specs=[full((gp, 1)), full((gp, 1)), full((gp, 1)), full((gp, hh)), full((gp, hh))],
        out_specs=[full((gp, 1)), full((gp, 1)), full((gp, hh)), full((gp, hh))],
        out_shape=[jax.ShapeDtypeStruct((gp, 1), F32), jax.ShapeDtypeStruct((gp, 1), F32),
                   jax.ShapeDtypeStruct((gp, hh), F32), jax.ShapeDtypeStruct((gp, hh), F32)],
        name="s5_params",
    )(lam_re.astype(F32).reshape(gp, 1), lam_im.astype(F32).reshape(gp, 1), step_col,
      b_re.astype(F32).reshape(gp, hh), b_im.astype(F32).reshape(gp, hh))
    return (are.reshape(1, gp), aim.reshape(1, gp),
            bbre.reshape(g, p, hh), bbim.reshape(g, p, hh))


def _block_diag(w, nblk):
    g, a, b = w.shape
    gl = g // nblk
    w = w.reshape(nblk, gl, a, b)
    eye = jnp.eye(gl, dtype=w.dtype)
    out = w[:, :, :, None, :] * eye[None, :, None, :, None]
    return out.reshape(nblk, gl * a, gl * b)


def _s5_kernel(h_ref, wu_ref, wb_ref, wcre_ref, wcim_ref, are_ref, aim_ref, d_ref, wglu_ref,
               bglu_ref, s0re_ref, s0im_ref, o5_ref, sre_ref, sim_ref, u_s, xre_s, xim_s, y_s,
               *perm_s, nb, tl):
    width = u_s.shape[1]
    nstate = xre_s.shape[1]
    nblk = wb_ref.shape[0]
    cin = width // nblk
    cst = nstate // nblk
    r = nb * tl
    pack = 2 * SUBLANES

    @pl.when(pl.program_id(0) == 0)
    def _():
        sre_ref[...] = s0re_ref[...]
        sim_ref[...] = s0im_ref[...]

    if perm_s:
        slab_s, = perm_s
        assert cin == LANES and nb == SUBLANES
        pitch = tl + SUBLANES
        u_bt = _dot(h_ref[...].reshape(r, h_ref.shape[2]), wu_ref[...])
        for b in range(nb):
            for k in range(nblk):
                slab_s[k, b * pitch:b * pitch + tl, :] = u_bt[b * tl:(b + 1) * tl,
                                                              k * cin:(k + 1) * cin]

        def to_time_major(t, carry):
            rows = pl.ds(pl.multiple_of(t * nb, nb), nb)
            for k in range(nblk):
                u_s[rows, k * cin:(k + 1) * cin] = slab_s[k, pl.ds(t, nb, stride=pitch), :]
            return carry

        lax.fori_loop(0, tl, to_time_major, 0, unroll=8)
        u = u_s[...]
    else:
        u = _dot(h_ref[...], wu_ref[...])
        u_s[...] = u
    ub = u.astype(BF16)

    def drive(k):
        bu = _dot(ub[:, k * cin:(k + 1) * cin], wb_ref[k])
        xre_s[:, k * cst:(k + 1) * cst] = bu[:, :cst]
        xim_s[:, k * cst:(k + 1) * cst] = bu[:, cst:]

    def scan(k):
        lanes = slice(k * cst, (k + 1) * cst)
        a_re = jnp.broadcast_to(are_ref[:, lanes], (SUBLANES, cst))
        a_im = jnp.broadcast_to(aim_ref[:, lanes], (SUBLANES, cst))

        def step(rows, s_re, s_im):
            n_re = a_re * s_re - a_im * s_im + xre_s[rows, lanes]
            n_im = a_re * s_im + a_im * s_re + xim_s[rows, lanes]
            xre_s[rows, lanes] = n_re
            xim_s[rows, lanes] = n_im
            return n_re, n_im

        if nb == SUBLANES:
            s_re, s_im = sre_ref[:, lanes], sim_ref[:, lanes]
            for t in range(tl):
                s_re, s_im = step(slice(t * nb, (t + 1) * nb), s_re, s_im)
            sre_ref[:, lanes] = s_re
            sim_ref[:, lanes] = s_im
        else:
            def group_body(sg, carry):
                srow = pl.ds(pl.multiple_of(sg * SUBLANES, SUBLANES), SUBLANES)
                st = (sre_ref[srow, lanes], sim_ref[srow, lanes])
                for t in range(tl):
                    st = step(pl.ds(pl.multiple_of(t * nb + sg * SUBLANES, SUBLANES), SUBLANES), *st)
                sre_ref[srow, lanes], sim_ref[srow, lanes] = st
                return carry

            lax.fori_loop(0, nb // SUBLANES, group_body, 0)

    def readout(k):
        st = slice(k * cst, (k + 1) * cst)
        y_s[:, k * cin:(k + 1) * cin] = (_dot(xre_s[:, st].astype(BF16), wcre_ref[k])
                                         - _dot(xim_s[:, st].astype(BF16), wcim_ref[k]))

    for k in range(nblk + 2):
        if k < nblk:
            drive(k)
        if 1 <= k <= nblk:
            scan(k - 1)
        if k >= 2:
            readout(k - 2)
    y = y_s[...] + d_ref[...] * u_s[...]
    v = _gelu_tanh(y)
    gate = _dot(v.astype(BF16), wglu_ref[...]) + bglu_ref[...]
    o5 = v * _sigmoid(gate)
    if perm_s:
        for k in range(nblk):
            slab_s[k, 0:r, :] = o5[:, k * cin:(k + 1) * cin]
        per_b = tl // pack

        def to_batch_major(i, carry):
            b = i // per_b
            t0 = (i % per_b) * pack
            for k in range(nblk):
                lo = slab_s[k, pl.ds(t0 * nb + b, SUBLANES, stride=nb), :]
                hi = slab_s[k, pl.ds((t0 + SUBLANES) * nb + b, SUBLANES, stride=nb), :]
                o5_ref[b, pl.ds(pl.multiple_of(t0, pack), pack), k * cin:(k + 1) * cin] = (
                    jnp.concatenate([lo, hi], axis=0).astype(BF16))
            return carry

        lax.fori_loop(0, nb * per_b, to_batch_major, 0, unroll=4)
    else:
        o5_ref[...] = o5.astype(BF16)


def _s5(h, wu, wb, wcre, wcim, are, aim, dsk, wglu, bglu, s0re, s0im, nb, tl):
    width = wu.shape[1]
    nstate = are.shape[1]
    r = nb * tl
    batch_major = h.ndim == 3
    if batch_major:
        steps = h.shape[1] // tl
        blk = lambda w: pl.BlockSpec((nb, tl, w), lambda i: (0, i, 0))
        o_shape = (nb, h.shape[1], width)
        extra = [pltpu.VMEM((width // LANES, nb * (tl + SUBLANES), LANES), F32)]
    else:
        steps = h.shape[0] // r
        blk = lambda w: pl.BlockSpec((r, w), lambda i: (i, 0))
        o_shape = (h.shape[0], width)
        extra = []
    kern = functools.partial(_s5_kernel, nb=nb, tl=tl)
    return pl.pallas_call(
        kern,
        grid=(steps,),
        in_specs=[blk(h.shape[-1]), _const_spec(wu.shape), _const_spec(wb.shape),
                  _const_spec(wcre.shape), _const_spec(wcim.shape), _const_spec(are.shape),
                  _const_spec(aim.shape), _const_spec(dsk.shape), _const_spec(wglu.shape),
                  _const_spec(bglu.shape), _const_spec(s0re.shape), _const_spec(s0im.shape)],
        out_specs=[blk(width), pl.BlockSpec((nb, nstate), lambda i: (0, 0)),
                   pl.BlockSpec((nb, nstate), lambda i: (0, 0))],
        out_shape=[jax.ShapeDtypeStruct(o_shape, BF16),
                   jax.ShapeDtypeStruct((nb, nstate), F32),
                   jax.ShapeDtypeStruct((nb, nstate), F32)],
        scratch_shapes=[pltpu.VMEM((r, width), F32), pltpu.VMEM((r, nstate), F32),
                        pltpu.VMEM((r, nstate), F32), pltpu.VMEM((r, width), F32)] + extra,
        compiler_params=_params(("arbitrary",)),
        name="s5",
    )(h, wu, wb, wcre, wcim, are, aim, dsk, wglu, bglu, s0re, s0im)


def _split2(x):
    hi = x.astype(BF16)
    lo = (x - hi.astype(F32)).astype(BF16)
    return jnp.concatenate([hi, lo], axis=1)


def _split3(x):
    hi = x.astype(BF16)
    r1 = x - hi.astype(F32)
    mid = r1.astype(BF16)
    lo = (r1 - mid.astype(F32)).astype(BF16)
    return jnp.concatenate([hi, mid, lo], axis=1)


HEAD_REP = 32


def _pack2(x):
    grp = lax.broadcasted_iota(jnp.int32, x.shape, 1) // HEAD_REP
    hi = x.astype(BF16).astype(F32)
    return jnp.where(grp == 0, hi, x - hi).astype(BF16)


def _pack3(x):
    grp = lax.broadcasted_iota(jnp.int32, x.shape, 1) // HEAD_REP
    hi = x.astype(BF16).astype(F32)
    r1 = x - hi
    mid = r1.astype(BF16).astype(F32)
    return jnp.where(grp == 0, hi, jnp.where(grp == 1, mid, r1 - mid)).astype(BF16)


def _group_norm(y, zg, nw, gw):
    outs = []
    for g in range(SSD_GROUPS):
        seg = y[:, g * gw:(g + 1) * gw] * zg[:, g * gw:(g + 1) * gw]
        outs.append(seg * lax.rsqrt(jnp.mean(seg * seg, axis=-1, keepdims=True) + EPS))
    return jnp.concatenate(outs, axis=1) * nw


def _ssds_pre_kernel(h_ref, wz_ref, wx_ref, wdt_ref, cw_ref, cb_ref, dtb_ref, alog_ref, dsk_ref,
                     conv0_ref, g2_ref, exp_ref, c_o, b_o, xdd_o, yp_o, eac_o, zg_o, cdec_o, cv_o):
    nbt = conv0_ref.shape[1]
    seq = h_ref.shape[0] // nbt
    width = zg_o.shape[1]
    nbc = c_o.shape[1]
    hb = h_ref[...]
    zg_o[...] = _silu(_dot(hb, wz_ref[...]))
    dt = _softplus(_dot(hb, wdt_ref[...]) + dtb_ref[...])
    la = dt * (-jnp.exp(alog_ref[...]))
    xbc = _dot(hb, wx_ref[...])
    rows = [slice(t * nbt, (t + 1) * nbt) for t in range(seq)]
    full = [conv0_ref[k] for k in range(SSD_CONV - 1)] + [xbc[r] for r in rows]
    for k in range(SSD_CONV - 1):
        cv_o[k] = full[seq + k]
    acums = []
    for t in range(seq):
        acums.append(la[rows[t]] if t == 0 else acums[-1] + la[rows[t]])
    tot = acums[-1]
    cdec_o[...] = jnp.exp(tot)
    xs, bq, cq = [], [], []
    for t in range(seq):
        acc = cb_ref[...]
        for k in range(SSD_CONV):
            acc = acc + cw_ref[k:k + 1, :] * full[t + k]
        xc = _silu(acc)
        xs.append(xc[:, :width])
        bq.append(xc[:, width:width + nbc].astype(BF16).astype(F32))
        cq.append(xc[:, width + nbc:].astype(BF16).astype(F32))
    for t in range(seq):
        r = rows[t]
        b_o[r, :] = bq[t]
        c_o[r, :] = cq[t]
        xdd_o[r, :] = xs[t] * _dot(_split2(dt[r] * jnp.exp(tot - acums[t])), exp_ref[...])
        eac_o[r, :] = _dot(_split2(jnp.exp(acums[t])), exp_ref[...])
        yp = dsk_ref[...] * xs[t]
        for s in range(t + 1):
            cbx = _dot(_split2(cq[t] * bq[s]), g2_ref[...])
            coef = cbx * jnp.exp(acums[t] - acums[s]) * dt[rows[s]]
            yp = yp + _dot(_split2(coef), exp_ref[...]) * xs[s]
        yp_o[r, :] = yp


def _ssds_state_kernel(c_ref, b_ref, xdd_ref, yp_ref, eac_ref, zg_ref, cdec_ref, st_ref, nw_ref,
                       o_ref, sto_ref, yoff_s, xddt_s):
    seq, nb, width = xdd_ref.shape
    nheads, hd_dim, nst = st_ref.shape[1:]
    hpg = nheads // SSD_GROUPS
    gw = width // SSD_GROUPS
    rws = seq * nb
    cst = c_ref[...].reshape(rws, c_ref.shape[2]).astype(BF16)
    bst = b_ref[...].reshape(rws, b_ref.shape[2])
    xdd = xdd_ref[...].reshape(rws, width)
    xddt_s[...] = jnp.concatenate([xdd, jnp.zeros((LANES - rws, width), F32)], axis=0).T.astype(BF16)
    rowb = lax.broadcasted_iota(jnp.int32, (rws, gw), 0) % nb
    zpad = jnp.zeros((LANES - rws, nst), BF16)
    for b in range(nb):
        mine = rowb == b
        for g in range(SSD_GROUPS):
            gl = slice(g * gw, (g + 1) * gw)
            hg = st_ref[b, g * hpg:(g + 1) * hpg].reshape(gw, nst)
            res = _dot_nt(cst[:, g * nst:(g + 1) * nst], hg.astype(BF16))
            yoff_s[:, gl] = jnp.where(mine, res, 0.0) if b == 0 else jnp.where(mine, res, yoff_s[:, gl])
            bm = jnp.where(mine[:, :nst], bst[:, g * nst:(g + 1) * nst], 0.0).astype(BF16)
            upd = _dot(xddt_s[gl, :], jnp.concatenate([bm, zpad], axis=0))
            for jh in range(hpg):
                hd = g * hpg + jh
                hrows = slice(jh * hd_dim, (jh + 1) * hd_dim)
                sto_ref[b, hd] = hg[hrows] * cdec_ref[b, hd] + upd[hrows]
    y = yp_ref[...].reshape(rws, width) + yoff_s[...] * eac_ref[...].reshape(rws, width)
    o = _group_norm(y, zg_ref[...].reshape(rws, width), nw_ref[...], gw)
    o_ref[...] = o.reshape(seq, nb, width)


def _ssds(h_tm, wz, wx, wdt, cw, cb, dtb, alog, dsk, nw, ssd0, conv0_tm, nb):
    m, d = h_tm.shape
    nseq, nheads, hd_dim, nst = ssd0.shape
    seq = m // nseq
    width = wz.shape[1]
    xbc_w = wx.shape[1]
    nbc = SSD_GROUPS * nst
    hpg = nheads // SSD_GROUPS
    head = jnp.arange(LANES)
    exp1 = (head[:, None] == (jnp.arange(width) // hd_dim)[None, :]).astype(BF16)
    g1 = ((jnp.arange(nbc) // nst)[:, None] == (head // hpg)[None, :]) & (head < nheads)[None, :]
    exp_m = jnp.concatenate([exp1, exp1], axis=0)
    g2 = jnp.concatenate([g1, g1], axis=0).astype(BF16)
    full = lambda shape: pl.BlockSpec(shape, lambda: (0,) * len(shape))
    ins = (h_tm, wz, wx, wdt, cw, cb, dtb, alog, dsk, conv0_tm, g2, exp_m)
    outs = [jax.ShapeDtypeStruct((m, nbc), F32), jax.ShapeDtypeStruct((m, nbc), F32),
            jax.ShapeDtypeStruct((m, width), F32), jax.ShapeDtypeStruct((m, width), F32),
            jax.ShapeDtypeStruct((m, width), F32), jax.ShapeDtypeStruct((m, width), F32),
            jax.ShapeDtypeStruct((nseq, LANES), F32),
            jax.ShapeDtypeStruct((SSD_CONV - 1, nseq, xbc_w), F32)]
    c, b, xdd, yp, eac, zg, cdec, conv1_tm = pl.pallas_call(
        _ssds_pre_kernel,
        in_specs=[full(a.shape) for a in ins],
        out_specs=[full(o.shape) for o in outs],
        out_shape=outs,
        compiler_params=pltpu.CompilerParams(vmem_limit_bytes=VMEM_LIMIT),
        name="ssd_sample_pre",
    )(*ins)

    tm3 = lambda a: a.reshape(seq, nseq, a.shape[1])
    blk = lambda w: pl.BlockSpec((seq, nb, w), lambda i: (0, i, 0))
    st_spec = pl.BlockSpec((nb, nheads, hd_dim, nst), lambda i: (i, 0, 0, 0))
    o, ssd1 = pl.pallas_call(
        _ssds_state_kernel,
        grid=(nseq // nb,),
        in_specs=[blk(nbc), blk(nbc), blk(width), blk(width), blk(width), blk(width),
                  pl.BlockSpec((nb, LANES), lambda i: (i, 0), memory_space=pltpu.SMEM),
                  st_spec, _const_spec(nw.shape)],
        out_specs=[blk(width), st_spec],
        out_shape=[jax.ShapeDtypeStruct((seq, nseq, width), F32),
                   jax.ShapeDtypeStruct(ssd0.shape, F32)],
        scratch_shapes=[pltpu.VMEM((seq * nb, width), F32), pltpu.VMEM((width, LANES), BF16)],
        compiler_params=_params(("arbitrary",)),
        name="ssd_sample_state",
    )(tm3(c), tm3(b), tm3(xdd), tm3(yp), tm3(eac), tm3(zg), cdec, ssd0, nw)
    return o.reshape(m, width), ssd1, conv1_tm


def _ssdp_kernel(h_ref, wz_ref, wx_ref, wdt_ref, cw_ref, cb_ref, dtb_ref, alog_ref, dsk_ref,
                 nw_ref, btril_ref, sel_ref, exp_ref, perm_ref, permt_ref, o_ref, st_ref, cv_ref,
                 xbc_s, zg_s, acol_s, acumt_s, e2_s, e3_s, xd_s, xdd_s, ysk_s, b_s, c_s,
                 cdec_s, ht_s, tail_s, *, nch, col_chunk):
    t = SSD_CHUNK
    r = nch * t
    j = pl.program_id(1)
    nheads, hd_dim, nst = st_ref.shape[1:]
    width = nheads * hd_dim
    gw = width // SSD_GROUPS
    hpg = nheads // SSD_GROUPS
    xbc_w = xbc_s.shape[1]
    ntap = SSD_CONV - 1
    pre = ntap * SUBLANES
    ext = pre + t
    steps = t // SUBLANES
    last_rows = [t - 1 - SUBLANES * (ntap - 1 - v) for v in range(ntap)]

    @pl.when(j == 0)
    def _():
        ht_s[...] = jnp.zeros(ht_s.shape, F32)
        tail_s[...] = jnp.zeros(tail_s.shape, F32)

    hb = jnp.concatenate([_dot(perm_ref[...], h_ref[c * t:(c + 1) * t, :]) for c in range(nch)],
                         axis=0).astype(BF16)
    dt = _softplus(_dot(hb, wdt_ref[...]) + dtb_ref[...])
    la = dt * (-jnp.exp(alog_ref[...]))

    def xbc_cols(jc):
        lanes = slice(jc * col_chunk, (jc + 1) * col_chunk)
        xall = _dot(hb, wx_ref[:, lanes])
        for c in range(nch):
            xbc_s[c * ext + pre:(c + 1) * ext, lanes] = xall[c * t:(c + 1) * t, :]
        for c in range(nch):
            base = c * ext
            xbc_s[base:base + pre, lanes] = xbc_s[base + t - 1:base + pre + t - 1, lanes]
            for v in range(ntap):
                if c == 0:
                    prev = tail_s[v:v + 1, lanes]
                else:
                    prow = base - ext + pre + last_rows[v]
                    prev = xbc_s[prow:prow + 1, lanes]
                xbc_s[base + v * SUBLANES:base + v * SUBLANES + 1, lanes] = prev
            acc = cb_ref[:, lanes]
            for k in range(SSD_CONV):
                acc = acc + cw_ref[k:k + 1, lanes] * xbc_s[base + k * SUBLANES:
                                                           base + k * SUBLANES + t, lanes]
            xc = _silu(acc)
            rows = slice(c * t, (c + 1) * t)
            lo = jc * col_chunk
            if lo < width:
                xd_s[rows, lanes] = xc.astype(BF16)
                xdd_s[rows, lanes] = (xc * e2_s[rows, lanes]).astype(BF16)
                ysk_s[rows, lanes] = xc * dsk_ref[:, lanes]
            elif lo < width + SSD_GROUPS * nst:
                b_s[rows, lo - width:lo - width + col_chunk] = xc.astype(BF16)
            else:
                cl = lo - width - SSD_GROUPS * nst
                c_s[rows, cl:cl + col_chunk] = xc.astype(BF16)

    x_cols = width // col_chunk
    for jc in range(x_cols, xbc_w // col_chunk):
        xbc_cols(jc)

    r3 = _dot(btril_ref[...], _split3(la))
    acum = r3[:, :LANES] + r3[:, LANES:2 * LANES] + r3[:, 2 * LANES:]
    tots = [acum[(c + 1) * t - 1:(c + 1) * t, :] for c in range(nch)]
    arow = acum - jnp.log(dt)
    for c in range(nch):
        acumt_s[c] = arow[c * t:(c + 1) * t, :].T
    tot_rows = jnp.concatenate([jnp.broadcast_to(v, (t, LANES)) for v in tots], axis=0)
    dstate = jnp.exp(tot_rows - acum)
    cdec = jnp.exp(jnp.concatenate([jnp.broadcast_to(v, (SUBLANES, LANES)) for v in tots], axis=0))
    e2_s[...] = _dot(_pack2(dt * dstate), exp_ref[...])
    e3_s[...] = _dot(_pack2(jnp.exp(acum)), exp_ref[...])
    cdx = _dot(_pack2(cdec), exp_ref[...])
    for c in range(nch):
        cdec_s[c] = cdx[c * SUBLANES:(c + 1) * SUBLANES, :]
    acol_s[...] = _dot(_pack3(acum), sel_ref[...])

    for jc in range(x_cols):
        lanes = slice(jc * col_chunk, (jc + 1) * col_chunk)
        xbc_cols(jc)
        zg_s[:, lanes] = _silu(_dot(hb, wz_ref[:, lanes]))

    for v in range(ntap):
        row = (nch - 1) * ext + pre + last_rows[v]
        tail_s[v:v + 1, :] = xbc_s[row:row + 1, :]
    cv_ref[0] = tail_s[...]

    def token(i):
        return (i % SUBLANES) * steps + i // SUBLANES

    causal = (token(lax.broadcasted_iota(jnp.int32, (t, t), 0))
              >= token(lax.broadcasted_iota(jnp.int32, (t, t), 1)))
    lane = lax.broadcasted_iota(jnp.int32, (t, LANES), 1)
    keep_lo = jnp.where(lane < hd_dim, 1.0, 0.0).astype(BF16)
    keep_hi = jnp.where(lane < hd_dim, 0.0, 1.0).astype(BF16)
    nw = nw_ref[...]

    def chunk_body(c, carry):
        rows = pl.ds(pl.multiple_of(c * t, t), t)
        ys = []
        for g in range(SSD_GROUPS):
            gl = slice(g * gw, (g + 1) * gw)
            cg = c_s[rows, g * nst:(g + 1) * nst]
            bg = b_s[rows, g * nst:(g + 1) * nst]
            cbm = jnp.where(causal, _dot_nt(cg, bg), 0.0)
            pairs = []
            for jp in range(hpg // 2):
                ha = g * hpg + 2 * jp
                ms = []
                for hd in (ha, ha + 1):
                    diff = acol_s[rows, hd * t:(hd + 1) * t] - acumt_s[c, hd:hd + 1, :]
                    ms.append((cbm * jnp.exp(jnp.where(causal, diff, -jnp.inf))).astype(BF16))
                xdp = xd_s[rows, ha * hd_dim:ha * hd_dim + LANES]
                rhs = jnp.concatenate([xdp * keep_lo, xdp * keep_hi], axis=0)
                pairs.append(_dot(jnp.concatenate(ms, axis=1), rhs))
            htg = ht_s[:, gl]
            y = (jnp.concatenate(pairs, axis=1) + _dot(cg, htg.astype(BF16)) * e3_s[rows, gl]
                 + ysk_s[rows, gl])
            ys.append(y)
            upd = lax.dot_general(bg, xdd_s[rows, gl], (((0,), (0,)), ((), ())),
                                  preferred_element_type=F32)
            ht_s[:, gl] = htg * cdec_s[c, 0:1, gl] + upd
        o = _group_norm(jnp.concatenate(ys, axis=1), zg_s[rows, :], nw, gw).astype(BF16)
        o_ref[rows, :] = _dot(permt_ref[...], o).astype(BF16)
        return carry

    lax.fori_loop(0, nch, chunk_body, 0, unroll=2)

    @pl.when(j == pl.num_programs(1) - 1)
    def _():
        for pr in range(nheads // 2):
            tt = ht_s[:, pr * LANES:(pr + 1) * LANES].T
            st_ref[0, 2 * pr] = tt[:hd_dim]
            st_ref[0, 2 * pr + 1] = tt[hd_dim:]


def _ssdp(h, wz, wx, wdt, cw, cb, dtb, alog, dsk, nw, nseq, nheads, nch):
    m, d = h.shape
    width = wz.shape[1]
    xbc_w = wx.shape[1]
    hd_dim = width // nheads
    nst = SSD_STATE
    t = SSD_CHUNK
    r = nch * t
    nblk = m // (nseq * r)
    ri = jnp.arange(r)
    steps = t // SUBLANES
    tok = (ri % t % SUBLANES) * steps + ri % t // SUBLANES
    btril = ((tok[:, None] >= tok[None, :]) & (ri[:, None] // t == ri[None, :] // t)).astype(BF16)
    perm = (tok[:t, None] == jnp.arange(t)[None, :]).astype(BF16)
    lane = jnp.arange(LANES)
    head = (lane % HEAD_REP)[:, None]
    exp_m = ((head == (jnp.arange(width) // hd_dim)[None, :])
             & (lane < 2 * HEAD_REP)[:, None]).astype(BF16)
    sel_m = ((head == (jnp.arange(nheads * t) // t)[None, :])
             & (lane < 3 * HEAD_REP)[:, None]).astype(BF16)
    row = lambda w: pl.BlockSpec((r, w), lambda i, j: (i * nblk + j, 0))
    kern = functools.partial(_ssdp_kernel, nch=nch, col_chunk=4 * LANES)
    vm = lambda shape, dt: pltpu.VMEM(shape, dt)
    return pl.pallas_call(
        kern,
        grid=(nseq, nblk),
        in_specs=[row(d), _const_spec(wz.shape), _const_spec(wx.shape), _const_spec(wdt.shape),
                  _const_spec(cw.shape), _const_spec(cb.shape), _const_spec(dtb.shape),
                  _const_spec(alog.shape), _const_spec(dsk.shape), _const_spec(nw.shape),
                  _const_spec(btril.shape), _const_spec(sel_m.shape), _const_spec(exp_m.shape),
                  _const_spec(perm.shape), _const_spec(perm.shape)],
        out_specs=[row(width),
                   pl.BlockSpec((1, nheads, hd_dim, nst), lambda i, j: (i, 0, 0, 0)),
                   pl.BlockSpec((1, SSD_CONV - 1, xbc_w), lambda i, j: (i, 0, 0))],
        out_shape=[jax.ShapeDtypeStruct((m, width), BF16),
                   jax.ShapeDtypeStruct((nseq, nheads, hd_dim, nst), F32),
                   jax.ShapeDtypeStruct((nseq, SSD_CONV - 1, xbc_w), F32)],
        scratch_shapes=[vm((nch * (t + (SSD_CONV - 1) * SUBLANES), xbc_w), F32),
                        vm((r, width), F32),
                        vm((r, nheads * t), F32), vm((nch, LANES, t), F32),
                        vm((r, width), F32), vm((r, width), F32),
                        vm((r, width), BF16), vm((r, width), BF16), vm((r, width), F32),
                        vm((r, SSD_GROUPS * nst), BF16), vm((r, SSD_GROUPS * nst), BF16),
                        vm((nch, SUBLANES, width), F32), vm((nst, width), F32),
                        vm((SSD_CONV - 1, xbc_w), F32)],
        compiler_params=_params(("arbitrary", "arbitrary")),
        name="ssd_prompt",
    )(h, wz, wx, wdt, cw, cb, dtb, alog, dsk, nw, btril, sel_m, exp_m, perm, perm.T)


def _rep_heads(v):
    grp = jnp.pad(v, [(0, 0)] * (v.ndim - 1) + [(0, HEAD_REP - v.shape[-1])])
    tail = jnp.zeros(v.shape[:-1] + (LANES - 3 * HEAD_REP,), v.dtype)
    return jnp.concatenate([grp, grp, grp, tail], axis=-1)


def _s5_call(h_tm, s5re0, s5im0, w, nb, tl):
    g, p = s5re0.shape[1:]
    o5_tm, re1, im1 = _s5(h_tm, w["w_u"], w["s5_wb"], w["s5_wcre"], w["s5_wcim"], w["s5_are"],
                          w["s5_aim"], w["s5_d"], w["s5_wglu"], w["s5_bglu"],
                          s5re0.reshape(nb, g * p), s5im0.reshape(nb, g * p), nb, tl)
    return o5_tm, re1.reshape(nb, g, p), im1.reshape(nb, g, p)


def _ffn1_call(x2d, w, tm):
    return _ffn1(x2d, w["ffn1_norm"], w["ffn1_wg"], w["ffn1_wu"], w["ffn1_wd"], w["mix_norm"], tm)


def _out_call(x1, o5, o_ssd, w, tm):
    return _out_ffn2(x1, o5, o_ssd, w["w_out5"], w["w_outs"], w["ffn2_norm"], w["ffn2_wg"],
                     w["ffn2_wu"], w["ffn2_wd"], w["final_norm"], tm)


def _prompt_layer(x, w, *, tm, s5_tl, ssd_nch):
    bsz, seq, d = x.shape
    m = bsz * seq
    g, p = w["s5_are"].shape[1] // S5_STATE, S5_STATE
    x1, h = _ffn1_call(x.reshape(m, d), w, tm)
    zeros = jnp.zeros((bsz, g, p), F32)
    o5, s5re1, s5im1 = _s5_call(h.reshape(bsz, seq, d), zeros, zeros, w, bsz, s5_tl)
    o5 = o5.reshape(m, -1)
    nheads = w["ssd_d"].shape[1] // SSD_HEAD_DIM
    o_ssd, ssd1, conv1 = _ssdp(h, w["w_z"], w["w_x"], w["w_dt"], w["conv_w"], w["conv_b"],
                               w["dt_bias"], w["a_log"], w["ssd_d"], w["ssd_norm"],
                               bsz, nheads, ssd_nch)
    y = _out_call(x1, o5, o_ssd, w, tm)
    return y.reshape(bsz, seq, d), s5re1, s5im1, ssd1, conv1


def _sample_layer(x, s5re0, s5im0, ssd0, conv0, w, *, ssd_nb):
    bsz, seq, d = x.shape
    m = bsz * seq
    x_tm = x.transpose(1, 0, 2).reshape(m, d)
    x1, h = _ffn1_call(x_tm, w, m)
    o5, s5re1, s5im1 = _s5_call(h, s5re0, s5im0, w, bsz, seq)
    o_ssd, ssd1, conv1_tm = _ssds(h, w["w_z"], w["w_x"], w["w_dt"], w["conv_w"], w["conv_b"],
                                  w["dt_bias"], w["a_log"], w["ssd_d"], w["ssd_norm"], ssd0,
                                  conv0.transpose(1, 0, 2), ssd_nb)
    y = _out_call(x1, o5, o_ssd, w, m)
    return (y.reshape(seq, bsz, d).transpose(1, 0, 2), s5re1, s5im1, ssd1,
            conv1_tm.transpose(1, 0, 2))


def kernel(x_prompt, x_sample, state_s5_re, state_s5_im, state_ssd, state_conv, ffn1_norm, ffn1_w_gate, ffn1_w_up, ffn1_w_down, mix_norm, w_in, s5_lambda_re, s5_lambda_im, s5_log_step, s5_b_re, s5_b_im, s5_c_re, s5_c_im, s5_d, s5_w_glu, s5_b_glu, ssd_conv_w, ssd_conv_b, ssd_dt_bias, ssd_a_log, ssd_d, ssd_norm, w_out, ffn2_norm, ffn2_w_gate, ffn2_w_up, ffn2_w_down, final_norm):
    depth = w_in.shape[0]
    assert depth == 1, "single-layer stack"
    i = 0
    d = x_prompt.shape[-1]
    g, p = s5_lambda_re.shape[1:]
    s5_width = g * S5_GROUP
    nheads = ssd_a_log.shape[1]
    ssd_width = nheads * SSD_HEAD_DIM
    xbc_w = ssd_conv_w.shape[-1]
    c0, c1, c2 = s5_width, s5_width + ssd_width, s5_width + ssd_width + xbc_w
    row = lambda v: v.astype(F32).reshape(1, -1)

    are, aim, bbre, bbim = _s5_params(s5_lambda_re[i], s5_lambda_im[i], s5_log_step[i],
                                      s5_b_re[i], s5_b_im[i])
    nblk = s5_width // LANES
    wb = jnp.concatenate([_block_diag(bbre.transpose(0, 2, 1), nblk),
                          _block_diag(bbim.transpose(0, 2, 1), nblk)], axis=-1).astype(BF16)
    wcre = _block_diag(s5_c_re[i].astype(F32).transpose(0, 2, 1), nblk).astype(BF16)
    wcim = _block_diag(s5_c_im[i].astype(F32).transpose(0, 2, 1), nblk).astype(BF16)

    win = w_in[i]
    w = {
        "ffn1_norm": row(ffn1_norm[i]), "mix_norm": row(mix_norm[i]),
        "ffn2_norm": row(ffn2_norm[i]), "final_norm": row(final_norm),
        "ffn1_wg": ffn1_w_gate[i].astype(BF16), "ffn1_wu": ffn1_w_up[i].astype(BF16),
        "ffn1_wd": ffn1_w_down[i].astype(BF16),
        "ffn2_wg": ffn2_w_gate[i].astype(BF16), "ffn2_wu": ffn2_w_up[i].astype(BF16),
        "ffn2_wd": ffn2_w_down[i].astype(BF16),
        "w_u": win[:, :c0].astype(BF16), "w_z": win[:, c0:c1].astype(BF16),
        "w_x": win[:, c1:c2].astype(BF16), "w_dt": _rep_heads(win[:, c2:]).astype(BF16),
        "s5_wb": wb, "s5_wcre": wcre, "s5_wcim": wcim, "s5_are": are, "s5_aim": aim,
        "s5_d": row(s5_d[i]), "s5_wglu": s5_w_glu[i].astype(BF16), "s5_bglu": row(s5_b_glu[i]),
        "conv_w": ssd_conv_w[i].astype(F32), "conv_b": row(ssd_conv_b[i]),
        "dt_bias": _rep_heads(row(ssd_dt_bias[i])), "a_log": _rep_heads(row(ssd_a_log[i])),
        "ssd_d": jnp.repeat(ssd_d[i].astype(F32), SSD_HEAD_DIM).reshape(1, -1),
        "ssd_norm": row(ssd_norm[i]),
        "w_out5": w_out[i][:s5_width].astype(BF16), "w_outs": w_out[i][s5_width:].astype(BF16),
    }

    lp = x_prompt.shape[1]
    yp, p_re, p_im, p_ssd, p_conv = _prompt_layer(
        x_prompt, w, tm=512, s5_tl=min(lp, 128), ssd_nch=min(lp // SSD_CHUNK, 4))
    ys, s_re, s_im, s_ssd, s_conv = _sample_layer(
        x_sample, state_s5_re[i], state_s5_im[i], state_ssd[i], state_conv[i], w,
        ssd_nb=SUBLANES)
    st = lambda v: v[None]
    return (yp, ys, st(p_re), st(p_im), st(p_ssd), st(p_conv),
            st(s_re), st(s_im), st(s_ssd), st(s_conv))
```

```python
import functools
import math

import jax
import jax.numpy as jnp
from jax import lax
from jax.experimental import pallas as pl
from jax.experimental.pallas import tpu as pltpu

F32 = jnp.float32
BF16 = jnp.bfloat16
EPS = 1e-6

LANES = 128
SUBLANES = 8
VMEM_LIMIT = 56 * 1024 * 1024

S5_GROUP = 16
S5_STATE = 64
SSD_HEAD_DIM = 64
SSD_GROUPS = 4
SSD_STATE = 128
SSD_CONV = 4
SSD_CHUNK = 128


def _sigmoid(x):
    return 0.5 + 0.5 * jnp.tanh(0.5 * x)


def _silu(x):
    hx = 0.5 * x
    return hx + hx * jnp.tanh(hx)


def _gelu_tanh(x):
    c = math.sqrt(2.0 / math.pi)
    return 0.5 * x * (1.0 + jnp.tanh(c * (x + 0.044715 * (x * x * x))))


def _softplus(x):
    return jnp.maximum(x, 0.0) + jnp.log1p(jnp.exp(-jnp.abs(x)))


def _rms(x, w):
    return x * lax.rsqrt(jnp.mean(x * x, axis=-1, keepdims=True) + EPS) * w


def _dot(a, b):
    return jnp.dot(a, b, preferred_element_type=F32)


def _dot_nt(a, b):
    return lax.dot_general(a, b, (((1,), (1,)), ((), ())), preferred_element_type=F32)


def _const_spec(shape):
    nd = len(shape)
    return pl.BlockSpec(shape, lambda *_: (0,) * nd, pipeline_mode=pl.Buffered(1))


def _params(sem):
    return pltpu.CompilerParams(dimension_semantics=sem, vmem_limit_bytes=VMEM_LIMIT)


def _swiglu_half(x, nw, wg_ref, wu_ref, wd_ref):
    hn = _rms(x, nw).astype(BF16)
    g = _dot(hn, wg_ref[...])
    u = _dot(hn, wu_ref[...])
    a = (_silu(g) * u).astype(BF16)
    return x + 0.5 * _dot(a, wd_ref[...])


def _cast_specs(ws, steps):
    ins, outs, shapes = [], [], []
    for w in ws:
        rows = w.shape[0] // steps
        assert rows * steps == w.shape[0] and rows % (2 * SUBLANES) == 0
        spec = pl.BlockSpec((rows, w.shape[1]), lambda i: (jnp.minimum(i, steps - 1), 0))
        ins.append(spec)
        outs.append(spec)
        shapes.append(jax.ShapeDtypeStruct(w.shape, BF16))
    return ins, outs, shapes


def _ffn1_kernel(xp_ref, xs_ref, nw_ref, wg_ref, wu_ref, wd_ref, mixw_ref, win_ref,
                 x1p_ref, hp_ref, x1s_ref, hs_ref, *wsplit_refs, n_p):
    is_p = pl.program_id(0) < n_p
    x = jnp.where(is_p, xp_ref[...], xs_ref[...])
    x1 = _swiglu_half(x, nw_ref[...], wg_ref, wu_ref, wd_ref)
    h = _rms(x1, mixw_ref[...]).astype(BF16)

    @pl.when(is_p)
    def _():
        x1p_ref[...] = x1
        hp_ref[...] = h

    @pl.when(jnp.logical_not(is_p))
    def _():
        x1s_ref[...] = x1
        hs_ref[...] = h

    lo = 0
    for o_ref in wsplit_refs:
        o_ref[...] = win_ref[:, lo:lo + o_ref.shape[1]].astype(BF16)
        lo += o_ref.shape[1]


def _ffn1(xp, xs, nw, wg, wu, wd, mixw, w_in, splits, tm):
    mp, d = xp.shape
    ms = xs.shape[0]
    assert ms == tm
    n_p = mp // tm
    dff = wg.shape[1]
    rowp = lambda w: pl.BlockSpec((tm, w), lambda i: (jnp.minimum(i, n_p - 1), 0))
    rows = lambda w: pl.BlockSpec((ms, w), lambda i: (0, 0))
    wrows = w_in.shape[0] // n_p
    wblk = lambda w: pl.BlockSpec((wrows, w), lambda i: (jnp.minimum(i, n_p - 1), 0))
    return pl.pallas_call(
        functools.partial(_ffn1_kernel, n_p=n_p),
        grid=(n_p + 1,),
        in_specs=[rowp(d), rows(d), _const_spec((1, d)), _const_spec((d, dff)),
                  _const_spec((d, dff)), _const_spec((dff, d)), _const_spec((1, d)),
                  wblk(w_in.shape[1])],
        out_specs=[rowp(d), rowp(d), rows(d), rows(d)] + [wblk(c) for c in splits],
        out_shape=[jax.ShapeDtypeStruct((mp, d), F32), jax.ShapeDtypeStruct((mp, d), BF16),
                   jax.ShapeDtypeStruct((ms, d), F32), jax.ShapeDtypeStruct((ms, d), BF16)]
                  + [jax.ShapeDtypeStruct((w_in.shape[0], c), BF16) for c in splits],
        compiler_params=_params(("arbitrary",)),
        name="ffn1",
    )(xp, xs, nw, wg, wu, wd, mixw, w_in)


def _out_ffn2_kernel(x1p_ref, o5p_ref, osp_ref, x1s_ref, o5s_ref, oss_ref, wo_ref, nw_ref,
                     wg_ref, wu_ref, wd_ref, fin_ref, yp_ref, ys_ref, *, n_p):
    is_p = pl.program_id(0) < n_p
    x1 = jnp.where(is_p, x1p_ref[...], x1s_ref[...])
    o5 = jnp.where(is_p, o5p_ref[...], o5s_ref[...])
    o_ssd = jnp.where(is_p, osp_ref[...], oss_ref[...].astype(BF16))
    k5 = o5.shape[1]
    x2 = x1 + (_dot(o5, wo_ref[:k5, :]) + _dot(o_ssd, wo_ref[k5:, :]))
    x3 = _swiglu_half(x2, nw_ref[...], wg_ref, wu_ref, wd_ref)
    y = _rms(x3, fin_ref[...])

    @pl.when(is_p)
    def _():
        yp_ref[...] = y

    @pl.when(jnp.logical_not(is_p))
    def _():
        ys_ref[...] = y


def _out_ffn2(x1p, o5p, osp, x1s, o5s, oss, wo, nw, wg, wu, wd, fin, tm):
    mp, d = x1p.shape
    ms = x1s.shape[0]
    assert ms == tm
    n_p = mp // tm
    dff = wg.shape[1]
    rowp = lambda w: pl.BlockSpec((tm, w), lambda i: (jnp.minimum(i, n_p - 1), 0))
    rows = lambda w: pl.BlockSpec((ms, w), lambda i: (0, 0))
    return pl.pallas_call(
        functools.partial(_out_ffn2_kernel, n_p=n_p),
        grid=(n_p + 1,),
        in_specs=[rowp(d), rowp(o5p.shape[1]), rowp(osp.shape[1]),
                  rows(d), rows(o5s.shape[1]), rows(oss.shape[1]),
                  _const_spec(wo.shape), _const_spec((1, d)),
                  _const_spec((d, dff)), _const_spec((d, dff)), _const_spec((dff, d)),
                  _const_spec((1, d))],
        out_specs=[rowp(d), rows(d)],
        out_shape=[jax.ShapeDtypeStruct((mp, d), F32), jax.ShapeDtypeStruct((ms, d), F32)],
        compiler_params=_params(("arbitrary",)),
        name="out_ffn2",
    )(x1p, o5p, osp, x1s, o5s, oss, wo, nw, wg, wu, wd, fin)


def _s5_param_kernel(lre_ref, lim_ref, step_ref, bre_ref, bim_ref,
                     are_ref, aim_ref, bbre_ref, bbim_ref):
    lre, lim, step = lre_ref[...], lim_ref[...], step_ref[...]
    mag = jnp.exp(lre * step)
    ang = lim * step
    are = mag * jnp.cos(ang)
    aim = mag * jnp.sin(ang)
    den = lre * lre + lim * lim
    nre, nim = are - 1.0, aim
    cre = (nre * lre + nim * lim) / den
    cim = (nim * lre - nre * lim) / den
    are_ref[...] = are
    aim_ref[...] = aim
    bre, bim = bre_ref[...], bim_ref[...]
    bbre_ref[...] = cre * bre - cim * bim
    bbim_ref[...] = cre * bim + cim * bre


def _s5_params(lam_re, lam_im, log_step, b_re, b_im):
    g, p = lam_re.shape
    hh = b_re.shape[-1]
    gp = g * p
    step = jnp.exp(log_step.astype(F32))
    step_col = jnp.broadcast_to(step[:, None], (g, p)).reshape(gp, 1)
    full = lambda shape: pl.BlockSpec(shape, lambda: (0,) * len(shape))
    are, aim, bbre, bbim = pl.pallas_call(
        _s5_param_kernel,
        in_specs=[full((gp, 1)), full((gp, 1)), full((gp, 1)), full((gp, hh)), full((gp, hh))],
        out_specs=[full((gp, 1)), full((gp, 1)), full((gp, hh)), full((gp, hh))],
        out_shape=[jax.ShapeDtypeStruct((gp, 1), F32), jax.ShapeDtypeStruct((gp, 1), F32),
                   jax.ShapeDtypeStruct((gp, hh), F32), jax.ShapeDtypeStruct((gp, hh), F32)],
        name="s5_params",
    )(lam_re.astype(F32).reshape(gp, 1), lam_im.astype(F32).reshape(gp, 1), step_col,
      b_re.astype(F32).reshape(gp, hh), b_im.astype(F32).reshape(gp, hh))
    return (are.reshape(1, gp), aim.reshape(1, gp),
            bbre.reshape(g, p, hh), bbim.reshape(g, p, hh))


def _block_diag(w, nblk):
    g, a, b = w.shape
    gl = g // nblk
    w = w.reshape(nblk, gl, a, b)
    eye = jnp.eye(gl, dtype=w.dtype)
    out = w[:, :, :, None, :] * eye[None, :, None, :, None]
    return out.reshape(nblk, gl * a, gl * b)


def _s5_kernel(*refs, nb, tl, n_cast):
    (h_ref, wu_ref, wb_ref, wcre_ref, wcim_ref, are_ref, aim_ref, d_ref, wglu_ref, bglu_ref,
     s0re_ref, s0im_ref) = refs[:12]
    cast_in = refs[12:12 + n_cast]
    o5_ref, sre_ref, sim_ref = refs[12 + n_cast:15 + n_cast]
    cast_out = refs[15 + n_cast:15 + 2 * n_cast]
    u_s, xre_s, xim_s, y_s, *perm_s = refs[15 + 2 * n_cast:]
    for src, dst in zip(cast_in, cast_out):
        dst[...] = src[...].astype(BF16)
    width = u_s.shape[1]
    nstate = xre_s.shape[1]
    nblk = wb_ref.shape[0]
    cin = width // nblk
    cst = nstate // nblk
    r = nb * tl
    pack = 2 * SUBLANES

    @pl.when(pl.program_id(0) == 0)
    def _():
        sre_ref[...] = s0re_ref[...]
        sim_ref[...] = s0im_ref[...]

    if perm_s:
        slab_s, = perm_s
        assert cin == LANES and nb == SUBLANES
        pitch = tl + SUBLANES
        u_bt = _dot(h_ref[...].reshape(r, h_ref.shape[2]), wu_ref[...])
        for b in range(nb):
            for k in range(nblk):
                slab_s[k, b * pitch:b * pitch + tl, :] = u_bt[b * tl:(b + 1) * tl,
                                                              k * cin:(k + 1) * cin]

        def to_time_major(t, carry):
            rows = pl.ds(pl.multiple_of(t * nb, nb), nb)
            for k in range(nblk):
                u_s[rows, k * cin:(k + 1) * cin] = slab_s[k, pl.ds(t, nb, stride=pitch), :]
            return carry

        lax.fori_loop(0, tl, to_time_major, 0, unroll=8)
        u = u_s[...]
    else:
        u = _dot(h_ref[...], wu_ref[...])
        u_s[...] = u
    ub = u.astype(BF16)

    def drive(k):
        bu = _dot(ub[:, k * cin:(k + 1) * cin], wb_ref[k])
        xre_s[:, k * cst:(k + 1) * cst] = bu[:, :cst]
        xim_s[:, k * cst:(k + 1) * cst] = bu[:, cst:]

    def scan(k):
        lanes = slice(k * cst, (k + 1) * cst)
        a_re = jnp.broadcast_to(are_ref[:, lanes], (SUBLANES, cst))
        a_im = jnp.broadcast_to(aim_ref[:, lanes], (SUBLANES, cst))

        def step(rows, s_re, s_im):
            n_re = a_re * s_re - a_im * s_im + xre_s[rows, lanes]
            n_im = a_re * s_im + a_im * s_re + xim_s[rows, lanes]
            xre_s[rows, lanes] = n_re
            xim_s[rows, lanes] = n_im
            return n_re, n_im

        if nb == SUBLANES:
            s_re, s_im = sre_ref[:, lanes], sim_ref[:, lanes]
            for t in range(tl):
                s_re, s_im = step(slice(t * nb, (t + 1) * nb), s_re, s_im)
            sre_ref[:, lanes] = s_re
            sim_ref[:, lanes] = s_im
        else:
            def group_body(sg, carry):
                srow = pl.ds(pl.multiple_of(sg * SUBLANES, SUBLANES), SUBLANES)
                st = (sre_ref[srow, lanes], sim_ref[srow, lanes])
                for t in range(tl):
                    st = step(pl.ds(pl.multiple_of(t * nb + sg * SUBLANES, SUBLANES), SUBLANES), *st)
                sre_ref[srow, lanes], sim_ref[srow, lanes] = st
                return carry

            lax.fori_loop(0, nb // SUBLANES, group_body, 0)

    def readout(k):
        st = slice(k * cst, (k + 1) * cst)
        y_s[:, k * cin:(k + 1) * cin] = (_dot(xre_s[:, st].astype(BF16), wcre_ref[k])
                                         - _dot(xim_s[:, st].astype(BF16), wcim_ref[k]))

    for k in range(nblk + 2):
        if k < nblk:
            drive(k)
        if 1 <= k <= nblk:
            scan(k - 1)
        if k >= 2:
            readout(k - 2)
    y = y_s[...] + d_ref[...] * u_s[...]
    v = _gelu_tanh(y)
    gate = _dot(v.astype(BF16), wglu_ref[...]) + bglu_ref[...]
    o5 = v * _sigmoid(gate)
    if perm_s:
        for k in range(nblk):
            slab_s[k, 0:r, :] = o5[:, k * cin:(k + 1) * cin]
        per_b = tl // pack

        def to_batch_major(i, carry):
            b = i // per_b
            t0 = (i % per_b) * pack
            for k in range(nblk):
                lo = slab_s[k, pl.ds(t0 * nb + b, SUBLANES, stride=nb), :]
                hi = slab_s[k, pl.ds((t0 + SUBLANES) * nb + b, SUBLANES, stride=nb), :]
                o5_ref[b, pl.ds(pl.multiple_of(t0, pack), pack), k * cin:(k + 1) * cin] = (
                    jnp.concatenate([lo, hi], axis=0).astype(BF16))
            return carry

        lax.fori_loop(0, nb * per_b, to_batch_major, 0, unroll=4)
    else:
        o5_ref[...] = o5.astype(BF16)


def _s5(h, wu, wb, wcre, wcim, are, aim, dsk, wglu, bglu, s0re, s0im, nb, tl, casts=()):
    width = wu.shape[1]
    nstate = are.shape[1]
    r = nb * tl
    batch_major = h.ndim == 3
    if batch_major:
        steps = h.shape[1] // tl
        blk = lambda w: pl.BlockSpec((nb, tl, w), lambda i: (0, i, 0))
        o_shape = (nb, h.shape[1], width)
        extra = [pltpu.VMEM((width // LANES, nb * (tl + SUBLANES), LANES), F32)]
    else:
        steps = h.shape[0] // r
        blk = lambda w: pl.BlockSpec((r, w), lambda i: (i, 0))
        o_shape = (h.shape[0], width)
        extra = []
    kern = functools.partial(_s5_kernel, nb=nb, tl=tl, n_cast=len(casts))
    c_in, c_out, c_shapes = _cast_specs(casts, steps)
    return pl.pallas_call(
        kern,
        grid=(steps,),
        in_specs=[blk(h.shape[-1]), _const_spec(wu.shape), _const_spec(wb.shape),
                  _const_spec(wcre.shape), _const_spec(wcim.shape), _const_spec(are.shape),
                  _const_spec(aim.shape), _const_spec(dsk.shape), _const_spec(wglu.shape),
                  _const_spec(bglu.shape), _const_spec(s0re.shape), _const_spec(s0im.shape)]
                 + c_in,
        out_specs=[blk(width), pl.BlockSpec((nb, nstate), lambda i: (0, 0)),
                   pl.BlockSpec((nb, nstate), lambda i: (0, 0))] + c_out,
        out_shape=[jax.ShapeDtypeStruct(o_shape, BF16),
                   jax.ShapeDtypeStruct((nb, nstate), F32),
                   jax.ShapeDtypeStruct((nb, nstate), F32)] + c_shapes,
        scratch_shapes=[pltpu.VMEM((r, width), F32), pltpu.VMEM((r, nstate), F32),
                        pltpu.VMEM((r, nstate), F32), pltpu.VMEM((r, width), F32)] + extra,
        compiler_params=_params(("arbitrary",)),
        name="s5",
    )(h, wu, wb, wcre, wcim, are, aim, dsk, wglu, bglu, s0re, s0im, *casts)


def _split2(x):
    hi = x.astype(BF16)
    lo = (x - hi.astype(F32)).astype(BF16)
    return jnp.concatenate([hi, lo], axis=1)


def _split3(x):
    hi = x.astype(BF16)
    r1 = x - hi.astype(F32)
    mid = r1.astype(BF16)
    lo = (r1 - mid.astype(F32)).astype(BF16)
    return jnp.concatenate([hi, mid, lo], axis=1)


HEAD_REP = 32


def _pack2(x):
    grp = lax.broadcasted_iota(jnp.int32, x.shape, 1) // HEAD_REP
    hi = x.astype(BF16).astype(F32)
    return jnp.where(grp == 0, hi, x - hi).astype(BF16)


def _pack3(x):
    grp = lax.broadcasted_iota(jnp.int32, x.shape, 1) // HEAD_REP
    hi = x.astype(BF16).astype(F32)
    r1 = x - hi
    mid = r1.astype(BF16).astype(F32)
    return jnp.where(grp == 0, hi, jnp.where(grp == 1, mid, r1 - mid)).astype(BF16)


def _group_norm(y, zg, nw, gw):
    outs = []
    for g in range(SSD_GROUPS):
        seg = y[:, g * gw:(g + 1) * gw] * zg[:, g * gw:(g + 1) * gw]
        outs.append(seg * lax.rsqrt(jnp.mean(seg * seg, axis=-1, keepdims=True) + EPS))
    return jnp.concatenate(outs, axis=1) * nw


def _ssds_pre_kernel(h_ref, wz_ref, wx_ref, wdt_ref, cw_ref, cb_ref, dtb_ref, alog_ref, dsk_ref,
                     conv0_ref, g2_ref, exp_ref, c_o, b_o, xdd_o, yp_o, eac_o, zg_o, cdec_o, cv_o):
    nbt = conv0_ref.shape[1]
    seq = h_ref.shape[0] // nbt
    width = zg_o.shape[1]
    nbc = c_o.shape[1]
    hb = h_ref[...]
    zg_o[...] = _silu(_dot(hb, wz_ref[...]))
    dt = _softplus(_dot(hb, wdt_ref[...]) + dtb_ref[...])
    la = dt * (-jnp.exp(alog_ref[...]))
    xbc = _dot(hb, wx_ref[...])
    rows = [slice(t * nbt, (t + 1) * nbt) for t in range(seq)]
    full = [conv0_ref[k] for k in range(SSD_CONV - 1)] + [xbc[r] for r in rows]
    for k in range(SSD_CONV - 1):
        cv_o[k] = full[seq + k]
    acums = []
    for t in range(seq):
        acums.append(la[rows[t]] if t == 0 else acums[-1] + la[rows[t]])
    tot = acums[-1]
    cdec_o[...] = jnp.exp(tot)
    xs, bq, cq = [], [], []
    for t in range(seq):
        acc = cb_ref[...]
        for k in range(SSD_CONV):
            acc = acc + cw_ref[k:k + 1, :] * full[t + k]
        xc = _silu(acc)
        xs.append(xc[:, :width])
        bq.append(xc[:, width:width + nbc].astype(BF16).astype(F32))
        cq.append(xc[:, width + nbc:].astype(BF16).astype(F32))
    for t in range(seq):
        r = rows[t]
        b_o[r, :] = bq[t]
        c_o[r, :] = cq[t]
        xdd_o[r, :] = xs[t] * _dot(_split2(dt[r] * jnp.exp(tot - acums[t])), exp_ref[...])
        eac_o[r, :] = _dot(_split2(jnp.exp(acums[t])), exp_ref[...])
        yp = dsk_ref[...] * xs[t]
        for s in range(t + 1):
            cbx = _dot(_split2(cq[t] * bq[s]), g2_ref[...])
            coef = cbx * jnp.exp(acums[t] - acums[s]) * dt[rows[s]]
            yp = yp + _dot(_split2(coef), exp_ref[...]) * xs[s]
        yp_o[r, :] = yp


def _ssds_state_kernel(c_ref, b_ref, xdd_ref, yp_ref, eac_ref, zg_ref, cdec_ref, st_ref, nw_ref,
                       o_ref, sto_ref, yoff_s, xddt_s):
    seq, nb, width = xdd_ref.shape
    nheads, hd_dim, nst = st_ref.shape[1:]
    hpg = nheads // SSD_GROUPS
    gw = width // SSD_GROUPS
    rws = seq * nb
    cst = c_ref[...].reshape(rws, c_ref.shape[2]).astype(BF16)
    bst = b_ref[...].reshape(rws, b_ref.shape[2])
    xdd = xdd_ref[...].reshape(rws, width)
    xddt_s[...] = jnp.concatenate([xdd, jnp.zeros((LANES - rws, width), F32)], axis=0).T.astype(BF16)
    rowb = lax.broadcasted_iota(jnp.int32, (rws, gw), 0) % nb
    zpad = jnp.zeros((LANES - rws, nst), BF16)
    for b in range(nb):
        mine = rowb == b
        for g in range(SSD_GROUPS):
            gl = slice(g * gw, (g + 1) * gw)
            hg = st_ref[b, g * hpg:(g + 1) * hpg].reshape(gw, nst)
            res = _dot_nt(cst[:, g * nst:(g + 1) * nst], hg.astype(BF16))
            yoff_s[:, gl] = jnp.where(mine, res, 0.0) if b == 0 else jnp.where(mine, res, yoff_s[:, gl])
            bm = jnp.where(mine[:, :nst], bst[:, g * nst:(g + 1) * nst], 0.0).astype(BF16)
            upd = _dot(xddt_s[gl, :], jnp.concatenate([bm, zpad], axis=0))
            for jh in range(hpg):
                hd = g * hpg + jh
                hrows = slice(jh * hd_dim, (jh + 1) * hd_dim)
                sto_ref[b, hd] = hg[hrows] * cdec_ref[b, hd] + upd[hrows]
    y = yp_ref[...].reshape(rws, width) + yoff_s[...] * eac_ref[...].reshape(rws, width)
    o = _group_norm(y, zg_ref[...].reshape(rws, width), nw_ref[...], gw)
    o_ref[...] = o.reshape(seq, nb, width)


def _ssds(h_tm, wz, wx, wdt, cw, cb, dtb, alog, dsk, nw, ssd0, conv0_tm, nb):
    m, d = h_tm.shape
    nseq, nheads, hd_dim, nst = ssd0.shape
    seq = m // nseq
    width = wz.shape[1]
    xbc_w = wx.shape[1]
    nbc = SSD_GROUPS * nst
    hpg = nheads // SSD_GROUPS
    head = jnp.arange(LANES)
    exp1 = (head[:, None] == (jnp.arange(width) // hd_dim)[None, :]).astype(BF16)
    g1 = ((jnp.arange(nbc) // nst)[:, None] == (head // hpg)[None, :]) & (head < nheads)[None, :]
    exp_m = jnp.concatenate([exp1, exp1], axis=0)
    g2 = jnp.concatenate([g1, g1], axis=0).astype(BF16)
    full = lambda shape: pl.BlockSpec(shape, lambda: (0,) * len(shape))
    ins = (h_tm, wz, wx, wdt, cw, cb, dtb, alog, dsk, conv0_tm, g2, exp_m)
    outs = [jax.ShapeDtypeStruct((m, nbc), F32), jax.ShapeDtypeStruct((m, nbc), F32),
            jax.ShapeDtypeStruct((m, width), F32), jax.ShapeDtypeStruct((m, width), F32),
            jax.ShapeDtypeStruct((m, width), F32), jax.ShapeDtypeStruct((m, width), F32),
            jax.ShapeDtypeStruct((nseq, LANES), F32),
            jax.ShapeDtypeStruct((SSD_CONV - 1, nseq, xbc_w), F32)]
    c, b, xdd, yp, eac, zg, cdec, conv1_tm = pl.pallas_call(
        _ssds_pre_kernel,
        in_specs=[full(a.shape) for a in ins],
        out_specs=[full(o.shape) for o in outs],
        out_shape=outs,
        compiler_params=pltpu.CompilerParams(vmem_limit_bytes=VMEM_LIMIT),
        name="ssd_sample_pre",
    )(*ins)

    tm3 = lambda a: a.reshape(seq, nseq, a.shape[1])
    blk = lambda w: pl.BlockSpec((seq, nb, w), lambda i: (0, i, 0))
    st_spec = pl.BlockSpec((nb, nheads, hd_dim, nst), lambda i: (i, 0, 0, 0))
    o, ssd1 = pl.pallas_call(
        _ssds_state_kernel,
        grid=(nseq // nb,),
        in_specs=[blk(nbc), blk(nbc), blk(width), blk(width), blk(width), blk(width),
                  pl.BlockSpec((nb, LANES), lambda i: (i, 0), memory_space=pltpu.SMEM),
                  st_spec, _const_spec(nw.shape)],
        out_specs=[blk(width), st_spec],
        out_shape=[jax.ShapeDtypeStruct((seq, nseq, width), F32),
                   jax.ShapeDtypeStruct(ssd0.shape, F32)],
        scratch_shapes=[pltpu.VMEM((seq * nb, width), F32), pltpu.VMEM((width, LANES), BF16)],
        compiler_params=_params(("arbitrary",)),
        name="ssd_sample_state",
    )(tm3(c), tm3(b), tm3(xdd), tm3(yp), tm3(eac), tm3(zg), cdec, ssd0, nw)
    return o.reshape(m, width), ssd1, conv1_tm


def _ssdp_kernel(h_ref, wz_ref, wx_ref, wdt_ref, cw_ref, cb_ref, dtb_ref, alog_ref, dsk_ref,
                 nw_ref, btril_ref, sel_ref, exp_ref, perm_ref, permt_ref, o_ref, st_ref, cv_ref,
                 xbc_s, zg_s, acol_s, acumt_s, e2_s, e3_s, xd_s, xdd_s, ysk_s, b_s, c_s,
                 cdec_s, ht_s, tail_s, *, nch, col_chunk):
    t = SSD_CHUNK
    r = nch * t
    j = pl.program_id(1)
    nheads, hd_dim, nst = st_ref.shape[1:]
    width = nheads * hd_dim
    gw = width // SSD_GROUPS
    hpg = nheads // SSD_GROUPS
    xbc_w = xbc_s.shape[1]
    ntap = SSD_CONV - 1
    pre = ntap * SUBLANES
    ext = pre + t
    steps = t // SUBLANES
    last_rows = [t - 1 - SUBLANES * (ntap - 1 - v) for v in range(ntap)]

    @pl.when(j == 0)
    def _():
        ht_s[...] = jnp.zeros(ht_s.shape, F32)
        tail_s[...] = jnp.zeros(tail_s.shape, F32)

    hb = jnp.concatenate([_dot(perm_ref[...], h_ref[c * t:(c + 1) * t, :]) for c in range(nch)],
                         axis=0).astype(BF16)
    dt = _softplus(_dot(hb, wdt_ref[...]) + dtb_ref[...])
    la = dt * (-jnp.exp(alog_ref[...]))

    def xbc_cols(jc):
        lanes = slice(jc * col_chunk, (jc + 1) * col_chunk)
        xall = _dot(hb, wx_ref[:, lanes])
        for c in range(nch):
            xbc_s[c * ext + pre:(c + 1) * ext, lanes] = xall[c * t:(c + 1) * t, :]
        for c in range(nch):
            base = c * ext
            xbc_s[base:base + pre, lanes] = xbc_s[base + t - 1:base + pre + t - 1, lanes]
            for v in range(ntap):
                if c == 0:
                    prev = tail_s[v:v + 1, lanes]
                else:
                    prow = base - ext + pre + last_rows[v]
                    prev = xbc_s[prow:prow + 1, lanes]
                xbc_s[base + v * SUBLANES:base + v * SUBLANES + 1, lanes] = prev
            acc = cb_ref[:, lanes]
            for k in range(SSD_CONV):
                acc = acc + cw_ref[k:k + 1, lanes] * xbc_s[base + k * SUBLANES:
                                                           base + k * SUBLANES + t, lanes]
            xc = _silu(acc)
            rows = slice(c * t, (c + 1) * t)
            lo = jc * col_chunk
            if lo < width:
                xd_s[rows, lanes] = xc.astype(BF16)
                xdd_s[rows, lanes] = (xc * e2_s[rows, lanes]).astype(BF16)
                ysk_s[rows, lanes] = xc * dsk_ref[:, lanes]
            elif lo < width + SSD_GROUPS * nst:
                b_s[rows, lo - width:lo - width + col_chunk] = xc.astype(BF16)
            else:
                cl = lo - width - SSD_GROUPS * nst
                c_s[rows, cl:cl + col_chunk] = xc.astype(BF16)

    x_cols = width // col_chunk
    for jc in range(x_cols, xbc_w // col_chunk):
        xbc_cols(jc)

    r3 = _dot(btril_ref[...], _split3(la))
    acum = r3[:, :LANES] + r3[:, LANES:2 * LANES] + r3[:, 2 * LANES:]
    tots = [acum[(c + 1) * t - 1:(c + 1) * t, :] for c in range(nch)]
    arow = acum - jnp.log(dt)
    for c in range(nch):
        acumt_s[c] = arow[c * t:(c + 1) * t, :].T
    tot_rows = jnp.concatenate([jnp.broadcast_to(v, (t, LANES)) for v in tots], axis=0)
    dstate = jnp.exp(tot_rows - acum)
    cdec = jnp.exp(jnp.concatenate([jnp.broadcast_to(v, (SUBLANES, LANES)) for v in tots], axis=0))
    e2_s[...] = _dot(_pack2(dt * dstate), exp_ref[...])
    e3_s[...] = _dot(_pack2(jnp.exp(acum)), exp_ref[...])
    cdx = _dot(_pack2(cdec), exp_ref[...])
    for c in range(nch):
        cdec_s[c] = cdx[c * SUBLANES:(c + 1) * SUBLANES, :]
    acol_s[...] = _dot(_pack3(acum), sel_ref[...])

    for jc in range(x_cols):
        lanes = slice(jc * col_chunk, (jc + 1) * col_chunk)
        xbc_cols(jc)
        zg_s[:, lanes] = _silu(_dot(hb, wz_ref[:, lanes]))

    for v in range(ntap):
        row = (nch - 1) * ext + pre + last_rows[v]
        tail_s[v:v + 1, :] = xbc_s[row:row + 1, :]
    cv_ref[0] = tail_s[...]

    def token(i):
        return (i % SUBLANES) * steps + i // SUBLANES

    causal = (token(lax.broadcasted_iota(jnp.int32, (t, t), 0))
              >= token(lax.broadcasted_iota(jnp.int32, (t, t), 1)))
    lane = lax.broadcasted_iota(jnp.int32, (t, LANES), 1)
    keep_lo = jnp.where(lane < hd_dim, 1.0, 0.0).astype(BF16)
    keep_hi = jnp.where(lane < hd_dim, 0.0, 1.0).astype(BF16)
    nw = nw_ref[...]

    def chunk_body(c, carry):
        rows = pl.ds(pl.multiple_of(c * t, t), t)
        ys = []
        for g in range(SSD_GROUPS):
            gl = slice(g * gw, (g + 1) * gw)
            cg = c_s[rows, g * nst:(g + 1) * nst]
            bg = b_s[rows, g * nst:(g + 1) * nst]
            cbm = jnp.where(causal, _dot_nt(cg, bg), 0.0)
            pairs = []
            for jp in range(hpg // 2):
                ha = g * hpg + 2 * jp
                ms = []
                for hd in (ha, ha + 1):
                    diff = acol_s[rows, hd * t:(hd + 1) * t] - acumt_s[c, hd:hd + 1, :]
                    ms.append((cbm * jnp.exp(jnp.where(causal, diff, -jnp.inf))).astype(BF16))
                xdp = xd_s[rows, ha * hd_dim:ha * hd_dim + LANES]
                rhs = jnp.concatenate([xdp * keep_lo, xdp * keep_hi], axis=0)
                pairs.append(_dot(jnp.concatenate(ms, axis=1), rhs))
            htg = ht_s[:, gl]
            y = (jnp.concatenate(pairs, axis=1) + _dot(cg, htg.astype(BF16)) * e3_s[rows, gl]
                 + ysk_s[rows, gl])
            ys.append(y)
            upd = lax.dot_general(bg, xdd_s[rows, gl], (((0,), (0,)), ((), ())),
                                  preferred_element_type=F32)
            ht_s[:, gl] = htg * cdec_s[c, 0:1, gl] + upd
        o = _group_norm(jnp.concatenate(ys, axis=1), zg_s[rows, :], nw, gw).astype(BF16)
        o_ref[rows, :] = _dot(permt_ref[...], o).astype(BF16)
        return carry

    lax.fori_loop(0, nch, chunk_body, 0, unroll=2)

    @pl.when(j == pl.num_programs(1) - 1)
    def _():
        for pr in range(nheads // 2):
            tt = ht_s[:, pr * LANES:(pr + 1) * LANES].T
            st_ref[0, 2 * pr] = tt[:hd_dim]
            st_ref[0, 2 * pr + 1] = tt[hd_dim:]


def _ssdp(h, wz, wx, wdt, cw, cb, dtb, alog, dsk, nw, nseq, nheads, nch):
    m, d = h.shape
    width = wz.shape[1]
    xbc_w = wx.shape[1]
    hd_dim = width // nheads
    nst = SSD_STATE
    t = SSD_CHUNK
    r = nch * t
    nblk = m // (nseq * r)
    ri = jnp.arange(r)
    steps = t // SUBLANES
    tok = (ri % t % SUBLANES) * steps + ri % t // SUBLANES
    btril = ((tok[:, None] >= tok[None, :]) & (ri[:, None] // t == ri[None, :] // t)).astype(BF16)
    perm = (tok[:t, None] == jnp.arange(t)[None, :]).astype(BF16)
    lane = jnp.arange(LANES)
    head = (lane % HEAD_REP)[:, None]
    exp_m = ((head == (jnp.arange(width) // hd_dim)[None, :])
             & (lane < 2 * HEAD_REP)[:, None]).astype(BF16)
    sel_m = ((head == (jnp.arange(nheads * t) // t)[None, :])
             & (lane < 3 * HEAD_REP)[:, None]).astype(BF16)
    row = lambda w: pl.BlockSpec((r, w), lambda i, j: (i * nblk + j, 0))
    kern = functools.partial(_ssdp_kernel, nch=nch, col_chunk=4 * LANES)
    vm = lambda shape, dt: pltpu.VMEM(shape, dt)
    return pl.pallas_call(
        kern,
        grid=(nseq, nblk),
        in_specs=[row(d), _const_spec(wz.shape), _const_spec(wx.shape), _const_spec(wdt.shape),
                  _const_spec(cw.shape), _const_spec(cb.shape), _const_spec(dtb.shape),
                  _const_spec(alog.shape), _const_spec(dsk.shape), _const_spec(nw.shape),
                  _const_spec(btril.shape), _const_spec(sel_m.shape), _const_spec(exp_m.shape),
                  _const_spec(perm.shape), _const_spec(perm.shape)],
        out_specs=[row(width),
                   pl.BlockSpec((1, nheads, hd_dim, nst), lambda i, j: (i, 0, 0, 0)),
                   pl.BlockSpec((1, SSD_CONV - 1, xbc_w), lambda i, j: (i, 0, 0))],
        out_shape=[jax.ShapeDtypeStruct((m, width), BF16),
                   jax.ShapeDtypeStruct((nseq, nheads, hd_dim, nst), F32),
                   jax.ShapeDtypeStruct((nseq, SSD_CONV - 1, xbc_w), F32)],
        scratch_shapes=[vm((nch * (t + (SSD_CONV - 1) * SUBLANES), xbc_w), F32),
                        vm((r, width), F32),
                        vm((r, nheads * t), F32), vm((nch, LANES, t), F32),
                        vm((r, width), F32), vm((r, width), F32),
                        vm((r, width), BF16), vm((r, width), BF16), vm((r, width), F32),
                        vm((r, SSD_GROUPS * nst), BF16), vm((r, SSD_GROUPS * nst), BF16),
                        vm((nch, SUBLANES, width), F32), vm((nst, width), F32),
                        vm((SSD_CONV - 1, xbc_w), F32)],
        compiler_params=_params(("arbitrary", "arbitrary")),
        name="ssd_prompt",
    )(h, wz, wx, wdt, cw, cb, dtb, alog, dsk, nw, btril, sel_m, exp_m, perm, perm.T)


def _rep_heads(v):
    grp = jnp.pad(v, [(0, 0)] * (v.ndim - 1) + [(0, HEAD_REP - v.shape[-1])])
    tail = jnp.zeros(v.shape[:-1] + (LANES - 3 * HEAD_REP,), v.dtype)
    return jnp.concatenate([grp, grp, grp, tail], axis=-1)


def _s5_call(h, s5re0, s5im0, w, nb, tl, casts=()):
    g, p = s5re0.shape[1:]
    o5, re1, im1, *cast = _s5(h, w["w_u"], w["s5_wb"], w["s5_wcre"], w["s5_wcim"], w["s5_are"],
                              w["s5_aim"], w["s5_d"], w["s5_wglu"], w["s5_bglu"],
                              s5re0.reshape(nb, g * p), s5im0.reshape(nb, g * p), nb, tl, casts)
    return o5, re1.reshape(nb, g, p), im1.reshape(nb, g, p), cast


def _layer(xp, xs, s5re0, s5im0, ssd0, conv0, w, *, tm, s5_tl, ssd_nch, ssd_nb):
    bp, lp, d = xp.shape
    bs, ls, _ = xs.shape
    mp, ms = bp * lp, bs * ls
    g, p = s5re0.shape[1:]
    nheads = ssd0.shape[1]

    xs_tm = xs.transpose(1, 0, 2).reshape(ms, d)
    x1p, hp, x1s, hs, w_u, w_z, w_x, w_dt = _ffn1(
        xp.reshape(mp, d), xs_tm, w["ffn1_norm"], w["ffn1_wg"], w["ffn1_wu"], w["ffn1_wd"],
        w["mix_norm"], w["w_in"], w["w_in_splits"], tm)
    w = dict(w, w_u=w_u, w_z=w_z, w_x=w_x, w_dt=_rep_heads(w_dt))

    zeros = jnp.zeros((bp, g, p), F32)
    o5p, p_re, p_im, (wg2, wu2, wd2, wo) = _s5_call(
        hp.reshape(bp, lp, d), zeros, zeros, w, bp, s5_tl,
        (w["ffn2_wg"], w["ffn2_wu"], w["ffn2_wd"], w["w_out"]))
    osp, p_ssd, p_conv = _ssdp(hp, w["w_z"], w["w_x"], w["w_dt"], w["conv_w"], w["conv_b"],
                               w["dt_bias"], w["a_log"], w["ssd_d"], w["ssd_norm"],
                               bp, nheads, ssd_nch)

    o5s, s_re, s_im, _ = _s5_call(hs, s5re0, s5im0, w, bs, ls)
    oss, s_ssd, s_conv_tm = _ssds(hs, w["w_z"], w["w_x"], w["w_dt"], w["conv_w"], w["conv_b"],
                                  w["dt_bias"], w["a_log"], w["ssd_d"], w["ssd_norm"], ssd0,
                                  conv0.transpose(1, 0, 2), ssd_nb)

    yp, ys = _out_ffn2(x1p, o5p.reshape(mp, -1), osp, x1s, o5s, oss, wo, w["ffn2_norm"],
                       wg2, wu2, wd2, w["final_norm"], tm)
    return (yp.reshape(bp, lp, d), ys.reshape(ls, bs, d).transpose(1, 0, 2),
            p_re, p_im, p_ssd, p_conv, s_re, s_im, s_ssd, s_conv_tm.transpose(1, 0, 2))


def kernel(x_prompt, x_sample, state_s5_re, state_s5_im, state_ssd, state_conv, ffn1_norm, ffn1_w_gate, ffn1_w_up, ffn1_w_down, mix_norm, w_in, s5_lambda_re, s5_lambda_im, s5_log_step, s5_b_re, s5_b_im, s5_c_re, s5_c_im, s5_d, s5_w_glu, s5_b_glu, ssd_conv_w, ssd_conv_b, ssd_dt_bias, ssd_a_log, ssd_d, ssd_norm, w_out, ffn2_norm, ffn2_w_gate, ffn2_w_up, ffn2_w_down, final_norm):
    depth = w_in.shape[0]
    assert depth == 1, "single-layer stack"
    i = 0
    d = x_prompt.shape[-1]
    g, p = s5_lambda_re.shape[1:]
    s5_width = g * S5_GROUP
    nheads = ssd_a_log.shape[1]
    ssd_width = nheads * SSD_HEAD_DIM
    xbc_w = ssd_conv_w.shape[-1]
    c0, c1, c2 = s5_width, s5_width + ssd_width, s5_width + ssd_width + xbc_w
    row = lambda v: v.astype(F32).reshape(1, -1)

    are, aim, bbre, bbim = _s5_params(s5_lambda_re[i], s5_lambda_im[i], s5_log_step[i],
                                      s5_b_re[i], s5_b_im[i])
    nblk = s5_width // LANES
    wb = jnp.concatenate([_block_diag(bbre.transpose(0, 2, 1), nblk),
                          _block_diag(bbim.transpose(0, 2, 1), nblk)], axis=-1).astype(BF16)
    wcre = _block_diag(s5_c_re[i].astype(F32).transpose(0, 2, 1), nblk).astype(BF16)
    wcim = _block_diag(s5_c_im[i].astype(F32).transpose(0, 2, 1), nblk).astype(BF16)

    w = {
        "ffn1_norm": row(ffn1_norm[i]), "mix_norm": row(mix_norm[i]),
        "ffn2_norm": row(ffn2_norm[i]), "final_norm": row(final_norm),
        "ffn1_wg": ffn1_w_gate[i].astype(BF16), "ffn1_wu": ffn1_w_up[i].astype(BF16),
        "ffn1_wd": ffn1_w_down[i].astype(BF16),
        "w_in": w_in[i].astype(F32), "w_in_splits": (c0, c1 - c0, c2 - c1, nheads),
        "ffn2_wg": ffn2_w_gate[i].astype(F32), "ffn2_wu": ffn2_w_up[i].astype(F32),
        "ffn2_wd": ffn2_w_down[i].astype(F32), "w_out": w_out[i].astype(F32),
        "s5_wb": wb, "s5_wcre": wcre, "s5_wcim": wcim, "s5_are": are, "s5_aim": aim,
        "s5_d": row(s5_d[i]), "s5_wglu": s5_w_glu[i].astype(BF16), "s5_bglu": row(s5_b_glu[i]),
        "conv_w": ssd_conv_w[i].astype(F32), "conv_b": row(ssd_conv_b[i]),
        "dt_bias": _rep_heads(row(ssd_dt_bias[i])), "a_log": _rep_heads(row(ssd_a_log[i])),
        "ssd_d": jnp.repeat(ssd_d[i].astype(F32), SSD_HEAD_DIM).reshape(1, -1),
        "ssd_norm": row(ssd_norm[i]),
    }

    lp = x_prompt.shape[1]
    yp, ys, *states = _layer(
        x_prompt, x_sample, state_s5_re[i], state_s5_im[i], state_ssd[i], state_conv[i], w,
        tm=512, s5_tl=min(lp, 128), ssd_nch=min(lp // SSD_CHUNK, 4), ssd_nb=SUBLANES)
    return (yp, ys) + tuple(v[None] for v in states)
```

```python
import functools
import math

import jax
import jax.numpy as jnp
from jax import lax
from jax.experimental import pallas as pl
from jax.experimental.pallas import tpu as pltpu

F32 = jnp.float32
BF16 = jnp.bfloat16
EPS = 1e-6

LANES = 128
SUBLANES = 8
VMEM_LIMIT = 56 * 1024 * 1024

S5_GROUP = 16
S5_STATE = 64
SSD_HEAD_DIM = 64
SSD_GROUPS = 4
SSD_STATE = 128
SSD_CONV = 4
SSD_CHUNK = 128


def _sigmoid(x):
    return 0.5 + 0.5 * jnp.tanh(0.5 * x)


def _silu(x):
    hx = 0.5 * x
    return hx + hx * jnp.tanh(hx)


def _gelu_tanh(x):
    c = math.sqrt(2.0 / math.pi)
    return 0.5 * x * (1.0 + jnp.tanh(c * (x + 0.044715 * (x * x * x))))


def _softplus(x):
    return jnp.maximum(x, 0.0) + jnp.log1p(jnp.exp(-jnp.abs(x)))


def _rms(x, w):
    return x * lax.rsqrt(jnp.mean(x * x, axis=-1, keepdims=True) + EPS) * w


def _dot(a, b):
    return jnp.dot(a, b, preferred_element_type=F32)


def _dot_nt(a, b):
    return lax.dot_general(a, b, (((1,), (1,)), ((), ())), preferred_element_type=F32)


def _const_spec(shape):
    nd = len(shape)
    return pl.BlockSpec(shape, lambda *_: (0,) * nd, pipeline_mode=pl.Buffered(1))


def _params(sem):
    return pltpu.CompilerParams(dimension_semantics=sem, vmem_limit_bytes=VMEM_LIMIT)


def _swiglu_half(x, nw, wg_ref, wu_ref, wd_ref):
    hn = _rms(x, nw).astype(BF16)
    g = _dot(hn, wg_ref[...])
    u = _dot(hn, wu_ref[...])
    a = (_silu(g) * u).astype(BF16)
    return x + 0.5 * _dot(a, wd_ref[...])


def _cast_specs(ws, steps, layer=0):
    ins, outs, shapes = [], [], []
    for w in ws:
        _, nrow, ncol = w.shape
        rows = nrow // steps
        assert rows * steps == nrow and rows % (2 * SUBLANES) == 0
        ins.append(pl.BlockSpec((None, rows, ncol),
                                lambda i: (layer, jnp.minimum(i, steps - 1), 0)))
        outs.append(pl.BlockSpec((rows, ncol), lambda i: (jnp.minimum(i, steps - 1), 0)))
        shapes.append(jax.ShapeDtypeStruct((nrow, ncol), BF16))
    return ins, outs, shapes


def _ffn1_kernel(xp_ref, xs_ref, nw_ref, wg_ref, wu_ref, wd_ref, mixw_ref, win_ref,
                 x1p_ref, hp_ref, x1s_ref, hs_ref, *wsplit_refs, n_p):
    is_p = pl.program_id(0) < n_p
    x = jnp.where(is_p, xp_ref[...], xs_ref[...])
    x1 = _swiglu_half(x, nw_ref[...], wg_ref, wu_ref, wd_ref)
    h = _rms(x1, mixw_ref[...]).astype(BF16)

    @pl.when(is_p)
    def _():
        x1p_ref[...] = x1
        hp_ref[...] = h

    @pl.when(jnp.logical_not(is_p))
    def _():
        x1s_ref[...] = x1
        hs_ref[...] = h

    lo = 0
    for o_ref in wsplit_refs:
        o_ref[...] = win_ref[:, lo:lo + o_ref.shape[1]].astype(BF16)
        lo += o_ref.shape[1]


def _ffn1(xp, xs, nw, wg, wu, wd, mixw, w_in, splits, tm):
    mp, d = xp.shape
    ms = xs.shape[0]
    assert ms == tm
    n_p = mp // tm
    dff = wg.shape[1]
    rowp = lambda w: pl.BlockSpec((tm, w), lambda i: (jnp.minimum(i, n_p - 1), 0))
    rows = lambda w: pl.BlockSpec((ms, w), lambda i: (0, 0))
    layer = 0
    wrows = w_in.shape[1] // n_p
    wblk = lambda w: pl.BlockSpec((wrows, w), lambda i: (jnp.minimum(i, n_p - 1), 0))
    w_in_blk = pl.BlockSpec((None, wrows, w_in.shape[2]),
                            lambda i: (layer, jnp.minimum(i, n_p - 1), 0))
    return pl.pallas_call(
        functools.partial(_ffn1_kernel, n_p=n_p),
        grid=(n_p + 1,),
        in_specs=[rowp(d), rows(d), _const_spec((1, d)), _const_spec((d, dff)),
                  _const_spec((d, dff)), _const_spec((dff, d)), _const_spec((1, d)),
                  w_in_blk],
        out_specs=[rowp(d), rowp(d), rows(d), rows(d)] + [wblk(c) for c in splits],
        out_shape=[jax.ShapeDtypeStruct((mp, d), F32), jax.ShapeDtypeStruct((mp, d), BF16),
                   jax.ShapeDtypeStruct((ms, d), F32), jax.ShapeDtypeStruct((ms, d), BF16)]
                  + [jax.ShapeDtypeStruct((w_in.shape[1], c), BF16) for c in splits],
        compiler_params=_params(("arbitrary",)),
        name="ffn1",
    )(xp, xs, nw, wg, wu, wd, mixw, w_in)


def _out_ffn2_kernel(x1p_ref, o5p_ref, osp_ref, x1s_ref, o5s_ref, oss_ref, wo_ref, nw_ref,
                     wg_ref, wu_ref, wd_ref, fin_ref, yp_ref, ys_ref, *, n_p):
    is_p = pl.program_id(0) < n_p
    x1 = jnp.where(is_p, x1p_ref[...], x1s_ref[...])
    o5 = jnp.where(is_p, o5p_ref[...], o5s_ref[...])
    o_ssd = jnp.where(is_p, osp_ref[...], oss_ref[...].astype(BF16))
    k5 = o5.shape[1]
    x2 = x1 + (_dot(o5, wo_ref[:k5, :]) + _dot(o_ssd, wo_ref[k5:, :]))
    x3 = _swiglu_half(x2, nw_ref[...], wg_ref, wu_ref, wd_ref)
    y = _rms(x3, fin_ref[...])

    @pl.when(is_p)
    def _():
        yp_ref[...] = y

    @pl.when(jnp.logical_not(is_p))
    def _():
        ys_ref[...] = y


def _out_ffn2(x1p, o5p, osp, x1s, o5s, oss, wo, nw, wg, wu, wd, fin, tm):
    mp, d = x1p.shape
    ms = x1s.shape[0]
    assert ms == tm
    n_p = mp // tm
    dff = wg.shape[1]
    rowp = lambda w: pl.BlockSpec((tm, w), lambda i: (jnp.minimum(i, n_p - 1), 0))
    rows = lambda w: pl.BlockSpec((ms, w), lambda i: (0, 0))
    return pl.pallas_call(
        functools.partial(_out_ffn2_kernel, n_p=n_p),
        grid=(n_p + 1,),
        in_specs=[rowp(d), rowp(o5p.shape[1]), rowp(osp.shape[1]),
                  rows(d), rows(o5s.shape[1]), rows(oss.shape[1]),
                  _const_spec(wo.shape), _const_spec((1, d)),
                  _const_spec((d, dff)), _const_spec((d, dff)), _const_spec((dff, d)),
                  _const_spec((1, d))],
        out_specs=[rowp(d), rows(d)],
        out_shape=[jax.ShapeDtypeStruct((mp, d), F32), jax.ShapeDtypeStruct((ms, d), F32)],
        compiler_params=_params(("arbitrary",)),
        name="out_ffn2",
    )(x1p, o5p, osp, x1s, o5s, oss, wo, nw, wg, wu, wd, fin)


def _s5_param_kernel(lre_ref, lim_ref, step_ref, bre_ref, bim_ref,
                     are_ref, aim_ref, bbre_ref, bbim_ref):
    lre, lim, step = lre_ref[...], lim_ref[...], step_ref[...]
    mag = jnp.exp(lre * step)
    ang = lim * step
    are = mag * jnp.cos(ang)
    aim = mag * jnp.sin(ang)
    den = lre * lre + lim * lim
    nre, nim = are - 1.0, aim
    cre = (nre * lre + nim * lim) / den
    cim = (nim * lre - nre * lim) / den
    are_ref[...] = are
    aim_ref[...] = aim
    bre, bim = bre_ref[...], bim_ref[...]
    bbre_ref[...] = cre * bre - cim * bim
    bbim_ref[...] = cre * bim + cim * bre


def _s5_params(lam_re, lam_im, log_step, b_re, b_im):
    g, p = lam_re.shape
    hh = b_re.shape[-1]
    gp = g * p
    step = jnp.exp(log_step.astype(F32))
    step_row = jnp.broadcast_to(step[:, None], (g, p)).reshape(1, gp)
    to_lanes = lambda b: b.astype(F32).reshape(gp, hh).T
    full = lambda shape: pl.BlockSpec(shape, lambda: (0,) * len(shape))
    are, aim, bbre, bbim = pl.pallas_call(
        _s5_param_kernel,
        in_specs=[full((1, gp)), full((1, gp)), full((1, gp)), full((hh, gp)), full((hh, gp))],
        out_specs=[full((1, gp)), full((1, gp)), full((hh, gp)), full((hh, gp))],
        out_shape=[jax.ShapeDtypeStruct((1, gp), F32), jax.ShapeDtypeStruct((1, gp), F32),
                   jax.ShapeDtypeStruct((hh, gp), F32), jax.ShapeDtypeStruct((hh, gp), F32)],
        name="s5_params",
    )(lam_re.astype(F32).reshape(1, gp), lam_im.astype(F32).reshape(1, gp), step_row,
      to_lanes(b_re), to_lanes(b_im))
    to_blocks = lambda b: b.reshape(hh, g, p).transpose(1, 0, 2)
    return are, aim, to_blocks(bbre), to_blocks(bbim)


def _block_diag(w, nblk):
    g, a, b = w.shape
    gl = g // nblk
    w = w.reshape(nblk, gl, a, b)
    eye = jnp.eye(gl, dtype=w.dtype)
    out = w[:, :, :, None, :] * eye[None, :, None, :, None]
    return out.reshape(nblk, gl * a, gl * b)


def _s5_kernel(*refs, nb, tl, n_cast):
    (h_ref, wu_ref, wb_ref, wcre_ref, wcim_ref, are_ref, aim_ref, d_ref, wglu_ref, bglu_ref,
     s0re_ref, s0im_ref) = refs[:12]
    cast_in = refs[12:12 + n_cast]
    o5_ref, sre_ref, sim_ref = refs[12 + n_cast:15 + n_cast]
    cast_out = refs[15 + n_cast:15 + 2 * n_cast]
    u_s, xre_s, xim_s, y_s, *perm_s = refs[15 + 2 * n_cast:]
    for src, dst in zip(cast_in, cast_out):
        dst[...] = src[...].astype(BF16)
    width = u_s.shape[1]
    nstate = xre_s.shape[1]
    nblk = wb_ref.shape[0]
    cin = width // nblk
    cst = nstate // nblk
    r = nb * tl
    pack = 2 * SUBLANES

    @pl.when(pl.program_id(0) == 0)
    def _():
        sre_ref[...] = s0re_ref[...]
        sim_ref[...] = s0im_ref[...]

    if perm_s:
        slab_s, = perm_s
        assert cin == LANES and nb == SUBLANES
        pitch = tl + SUBLANES
        u_bt = _dot(h_ref[...].reshape(r, h_ref.shape[2]), wu_ref[...])
        for b in range(nb):
            for k in range(nblk):
                slab_s[k, b * pitch:b * pitch + tl, :] = u_bt[b * tl:(b + 1) * tl,
                                                              k * cin:(k + 1) * cin]

        for t in range(tl):
            for k in range(nblk):
                u_s[t * nb:(t + 1) * nb, k * cin:(k + 1) * cin] = (
                    slab_s[k, pl.ds(t, nb, stride=pitch), :])
        u = u_s[...]
    else:
        u = _dot(h_ref[...], wu_ref[...])
        u_s[...] = u
    ub = u.astype(BF16)

    def drive(k):
        bu = _dot(ub[:, k * cin:(k + 1) * cin], wb_ref[k])
        xre_s[:, k * cst:(k + 1) * cst] = bu[:, :cst]
        xim_s[:, k * cst:(k + 1) * cst] = bu[:, cst:]

    def scan(k):
        lanes = slice(k * cst, (k + 1) * cst)
        a_re = jnp.broadcast_to(are_ref[:, lanes], (SUBLANES, cst))
        a_im = jnp.broadcast_to(aim_ref[:, lanes], (SUBLANES, cst))

        def step(rows, s_re, s_im):
            n_re = a_re * s_re - a_im * s_im + xre_s[rows, lanes]
            n_im = a_re * s_im + a_im * s_re + xim_s[rows, lanes]
            xre_s[rows, lanes] = n_re
            xim_s[rows, lanes] = n_im
            return n_re, n_im

        if nb == SUBLANES:
            s_re, s_im = sre_ref[:, lanes], sim_ref[:, lanes]
            for t in range(tl):
                s_re, s_im = step(slice(t * nb, (t + 1) * nb), s_re, s_im)
            sre_ref[:, lanes] = s_re
            sim_ref[:, lanes] = s_im
        else:
            def group_body(sg, carry):
                srow = pl.ds(pl.multiple_of(sg * SUBLANES, SUBLANES), SUBLANES)
                st = (sre_ref[srow, lanes], sim_ref[srow, lanes])
                for t in range(tl):
                    st = step(pl.ds(pl.multiple_of(t * nb + sg * SUBLANES, SUBLANES), SUBLANES), *st)
                sre_ref[srow, lanes], sim_ref[srow, lanes] = st
                return carry

            lax.fori_loop(0, nb // SUBLANES, group_body, 0)

    def readout(k):
        st = slice(k * cst, (k + 1) * cst)
        y_s[:, k * cin:(k + 1) * cin] = (_dot(xre_s[:, st].astype(BF16), wcre_ref[k])
                                         - _dot(xim_s[:, st].astype(BF16), wcim_ref[k]))

    for k in range(nblk + 2):
        if k < nblk:
            drive(k)
        if 1 <= k <= nblk:
            scan(k - 1)
        if k >= 2:
            readout(k - 2)
    y = y_s[...] + d_ref[...] * u_s[...]
    v = _gelu_tanh(y)
    gate = _dot(v.astype(BF16), wglu_ref[...]) + bglu_ref[...]
    o5 = v * _sigmoid(gate)
    if perm_s:
        for k in range(nblk):
            slab_s[k, 0:r, :] = o5[:, k * cin:(k + 1) * cin]
        for b in range(nb):
            for t0 in range(0, tl, pack):
                for k in range(nblk):
                    lo = slab_s[k, pl.ds(t0 * nb + b, SUBLANES, stride=nb), :]
                    hi = slab_s[k, pl.ds((t0 + SUBLANES) * nb + b, SUBLANES, stride=nb), :]
                    o5_ref[b, t0:t0 + pack, k * cin:(k + 1) * cin] = (
                        jnp.concatenate([lo, hi], axis=0).astype(BF16))
    else:
        o5_ref[...] = o5.astype(BF16)


def _s5(h, wu, wb, wcre, wcim, are, aim, dsk, wglu, bglu, s0re, s0im, nb, tl, casts=()):
    width = wu.shape[1]
    nstate = are.shape[1]
    r = nb * tl
    batch_major = h.ndim == 3
    if batch_major:
        steps = h.shape[1] // tl
        blk = lambda w: pl.BlockSpec((nb, tl, w), lambda i: (0, i, 0))
        o_shape = (nb, h.shape[1], width)
        extra = [pltpu.VMEM((width // LANES, nb * (tl + SUBLANES), LANES), F32)]
    else:
        steps = h.shape[0] // r
        blk = lambda w: pl.BlockSpec((r, w), lambda i: (i, 0))
        o_shape = (h.shape[0], width)
        extra = []
    kern = functools.partial(_s5_kernel, nb=nb, tl=tl, n_cast=len(casts))
    c_in, c_out, c_shapes = _cast_specs(casts, steps)
    return pl.pallas_call(
        kern,
        grid=(steps,),
        in_specs=[blk(h.shape[-1]), _const_spec(wu.shape), _const_spec(wb.shape),
                  _const_spec(wcre.shape), _const_spec(wcim.shape), _const_spec(are.shape),
                  _const_spec(aim.shape), _const_spec(dsk.shape), _const_spec(wglu.shape),
                  _const_spec(bglu.shape), _const_spec(s0re.shape), _const_spec(s0im.shape)]
                 + c_in,
        out_specs=[blk(width), pl.BlockSpec((nb, nstate), lambda i: (0, 0)),
                   pl.BlockSpec((nb, nstate), lambda i: (0, 0))] + c_out,
        out_shape=[jax.ShapeDtypeStruct(o_shape, BF16),
                   jax.ShapeDtypeStruct((nb, nstate), F32),
                   jax.ShapeDtypeStruct((nb, nstate), F32)] + c_shapes,
        scratch_shapes=[pltpu.VMEM((r, width), F32), pltpu.VMEM((r, nstate), F32),
                        pltpu.VMEM((r, nstate), F32), pltpu.VMEM((r, width), F32)] + extra,
        compiler_params=_params(("arbitrary",)),
        name="s5",
    )(h, wu, wb, wcre, wcim, are, aim, dsk, wglu, bglu, s0re, s0im, *casts)


def _split2(x):
    hi = x.astype(BF16)
    lo = (x - hi.astype(F32)).astype(BF16)
    return jnp.concatenate([hi, lo], axis=1)


def _split3(x):
    hi = x.astype(BF16)
    r1 = x - hi.astype(F32)
    mid = r1.astype(BF16)
    lo = (r1 - mid.astype(F32)).astype(BF16)
    return jnp.concatenate([hi, mid, lo], axis=1)


HEAD_REP = 32


def _pack2(x):
    grp = lax.broadcasted_iota(jnp.int32, x.shape, 1) // HEAD_REP
    hi = x.astype(BF16).astype(F32)
    return jnp.where(grp == 0, hi, x - hi).astype(BF16)


def _pack3(x):
    grp = lax.broadcasted_iota(jnp.int32, x.shape, 1) // HEAD_REP
    hi = x.astype(BF16).astype(F32)
    r1 = x - hi
    mid = r1.astype(BF16).astype(F32)
    return jnp.where(grp == 0, hi, jnp.where(grp == 1, mid, r1 - mid)).astype(BF16)


def _group_norm(y, zg, nw, gw):
    outs = []
    for g in range(SSD_GROUPS):
        seg = y[:, g * gw:(g + 1) * gw] * zg[:, g * gw:(g + 1) * gw]
        outs.append(seg * lax.rsqrt(jnp.mean(seg * seg, axis=-1, keepdims=True) + EPS))
    return jnp.concatenate(outs, axis=1) * nw


def _ssds_pre_kernel(h_ref, wz_ref, wx_ref, wdt_ref, cw_ref, cb_ref, dtb_ref, alog_ref, dsk_ref,
                     conv0_ref, g2_ref, exp_ref, c_o, b_o, xdd_o, yp_o, eac_o, zg_o, cdec_o, cv_o):
    nbt = conv0_ref.shape[1]
    seq = h_ref.shape[0] // nbt
    width = zg_o.shape[1]
    nbc = c_o.shape[1]
    hb = h_ref[...]
    zg_o[...] = _silu(_dot(hb, wz_ref[...]))
    dt = _softplus(_dot(hb, wdt_ref[...]) + dtb_ref[...])
    la = dt * (-jnp.exp(alog_ref[...]))
    xbc = _dot(hb, wx_ref[...])
    rows = [slice(t * nbt, (t + 1) * nbt) for t in range(seq)]
    full = [conv0_ref[k] for k in range(SSD_CONV - 1)] + [xbc[r] for r in rows]
    for k in range(SSD_CONV - 1):
        cv_o[k] = full[seq + k]
    acums = []
    for t in range(seq):
        acums.append(la[rows[t]] if t == 0 else acums[-1] + la[rows[t]])
    tot = acums[-1]
    cdec_o[...] = jnp.exp(tot)
    xs, bq, cq = [], [], []
    for t in range(seq):
        acc = cb_ref[...]
        for k in range(SSD_CONV):
            acc = acc + cw_ref[k:k + 1, :] * full[t + k]
        xc = _silu(acc)
        xs.append(xc[:, :width])
        bq.append(xc[:, width:width + nbc].astype(BF16).astype(F32))
        cq.append(xc[:, width + nbc:].astype(BF16).astype(F32))
    for t in range(seq):
        r = rows[t]
        b_o[r, :] = bq[t]
        c_o[r, :] = cq[t]
        xdd_o[r, :] = xs[t] * _dot(_split2(dt[r] * jnp.exp(tot - acums[t])), exp_ref[...])
        eac_o[r, :] = _dot(_split2(jnp.exp(acums[t])), exp_ref[...])
        yp = dsk_ref[...] * xs[t]
        for s in range(t + 1):
            cbx = _dot(_split2(cq[t] * bq[s]), g2_ref[...])
            coef = cbx * jnp.exp(acums[t] - acums[s]) * dt[rows[s]]
            yp = yp + _dot(_split2(coef), exp_ref[...]) * xs[s]
        yp_o[r, :] = yp


def _ssds_state_kernel(c_ref, b_ref, xdd_ref, yp_ref, eac_ref, zg_ref, cdec_ref, st_ref, nw_ref,
                       o_ref, sto_ref, yoff_s, xddt_s):
    seq, nb, width = xdd_ref.shape
    nheads, hd_dim, nst = st_ref.shape[1:]
    hpg = nheads // SSD_GROUPS
    gw = width // SSD_GROUPS
    rws = seq * nb
    cst = c_ref[...].reshape(rws, c_ref.shape[2]).astype(BF16)
    bst = b_ref[...].reshape(rws, b_ref.shape[2])
    xdd = xdd_ref[...].reshape(rws, width)
    xddt_s[...] = jnp.concatenate([xdd, jnp.zeros((LANES - rws, width), F32)], axis=0).T.astype(BF16)
    rowb = lax.broadcasted_iota(jnp.int32, (rws, gw), 0) % nb
    zpad = jnp.zeros((LANES - rws, nst), BF16)
    for b in range(nb):
        mine = rowb == b
        for g in range(SSD_GROUPS):
            gl = slice(g * gw, (g + 1) * gw)
            hg = st_ref[b, g * hpg:(g + 1) * hpg].reshape(gw, nst)
            res = _dot_nt(cst[:, g * nst:(g + 1) * nst], hg.astype(BF16))
            yoff_s[:, gl] = jnp.where(mine, res, 0.0) if b == 0 else jnp.where(mine, res, yoff_s[:, gl])
            bm = jnp.where(mine[:, :nst], bst[:, g * nst:(g + 1) * nst], 0.0).astype(BF16)
            upd = _dot(xddt_s[gl, :], jnp.concatenate([bm, zpad], axis=0))
            for jh in range(hpg):
                hd = g * hpg + jh
                hrows = slice(jh * hd_dim, (jh + 1) * hd_dim)
                sto_ref[b, hd] = hg[hrows] * cdec_ref[b, hd] + upd[hrows]
    y = yp_ref[...].reshape(rws, width) + yoff_s[...] * eac_ref[...].reshape(rws, width)
    o = _group_norm(y, zg_ref[...].reshape(rws, width), nw_ref[...], gw)
    o_ref[...] = o.reshape(seq, nb, width)


def _ssds(h_tm, wz, wx, wdt, cw, cb, dtb, alog, dsk, nw, ssd0, conv0_tm, nb):
    m, d = h_tm.shape
    nseq, nheads, hd_dim, nst = ssd0.shape
    seq = m // nseq
    width = wz.shape[1]
    xbc_w = wx.shape[1]
    nbc = SSD_GROUPS * nst
    hpg = nheads // SSD_GROUPS
    head = jnp.arange(LANES)
    exp1 = (head[:, None] == (jnp.arange(width) // hd_dim)[None, :]).astype(BF16)
    g1 = ((jnp.arange(nbc) // nst)[:, None] == (head // hpg)[None, :]) & (head < nheads)[None, :]
    exp_m = jnp.concatenate([exp1, exp1], axis=0)
    g2 = jnp.concatenate([g1, g1], axis=0).astype(BF16)
    full = lambda shape: pl.BlockSpec(shape, lambda: (0,) * len(shape))
    ins = (h_tm, wz, wx, wdt, cw, cb, dtb, alog, dsk, conv0_tm, g2, exp_m)
    outs = [jax.ShapeDtypeStruct((m, nbc), F32), jax.ShapeDtypeStruct((m, nbc), F32),
            jax.ShapeDtypeStruct((m, width), F32), jax.ShapeDtypeStruct((m, width), F32),
            jax.ShapeDtypeStruct((m, width), F32), jax.ShapeDtypeStruct((m, width), F32),
            jax.ShapeDtypeStruct((nseq, LANES), F32),
            jax.ShapeDtypeStruct((SSD_CONV - 1, nseq, xbc_w), F32)]
    c, b, xdd, yp, eac, zg, cdec, conv1_tm = pl.pallas_call(
        _ssds_pre_kernel,
        in_specs=[full(a.shape) for a in ins],
        out_specs=[full(o.shape) for o in outs],
        out_shape=outs,
        compiler_params=pltpu.CompilerParams(vmem_limit_bytes=VMEM_LIMIT),
        name="ssd_sample_pre",
    )(*ins)

    tm3 = lambda a: a.reshape(seq, nseq, a.shape[1])
    blk = lambda w: pl.BlockSpec((seq, nb, w), lambda i: (0, i, 0))
    st_spec = pl.BlockSpec((nb, nheads, hd_dim, nst), lambda i: (i, 0, 0, 0))
    o, ssd1 = pl.pallas_call(
        _ssds_state_kernel,
        grid=(nseq // nb,),
        in_specs=[blk(nbc), blk(nbc), blk(width), blk(width), blk(width), blk(width),
                  pl.BlockSpec((nb, LANES), lambda i: (i, 0), memory_space=pltpu.SMEM),
                  st_spec, _const_spec(nw.shape)],
        out_specs=[blk(width), st_spec],
        out_shape=[jax.ShapeDtypeStruct((seq, nseq, width), F32),
                   jax.ShapeDtypeStruct(ssd0.shape, F32)],
        scratch_shapes=[pltpu.VMEM((seq * nb, width), F32), pltpu.VMEM((width, LANES), BF16)],
        compiler_params=_params(("arbitrary",)),
        name="ssd_sample_state",
    )(tm3(c), tm3(b), tm3(xdd), tm3(yp), tm3(eac), tm3(zg), cdec, ssd0, nw)
    return o.reshape(m, width), ssd1, conv1_tm


def _ssdp_kernel(h_ref, wz_ref, wx_ref, wdt_ref, cw_ref, cb_ref, dtb_ref, alog_ref, dsk_ref,
                 nw_ref, btril_ref, sel_ref, exp_ref, perm_ref, permt_ref, o_ref, st_ref, cv_ref,
                 xbc_s, zg_s, acol_s, acumt_s, e2_s, e3_s, xd_s, xdd_s, ysk_s, b_s, c_s,
                 cdec_s, ht_s, tail_s, *, nch, col_chunk):
    t = SSD_CHUNK
    r = nch * t
    j = pl.program_id(1)
    nheads, hd_dim, nst = st_ref.shape[1:]
    width = nheads * hd_dim
    gw = width // SSD_GROUPS
    hpg = nheads // SSD_GROUPS
    xbc_w = xbc_s.shape[1]
    ntap = SSD_CONV - 1
    pre = ntap * SUBLANES
    ext = pre + t
    steps = t // SUBLANES
    last_rows = [t - 1 - SUBLANES * (ntap - 1 - v) for v in range(ntap)]

    @pl.when(j == 0)
    def _():
        ht_s[...] = jnp.zeros(ht_s.shape, F32)
        tail_s[...] = jnp.zeros(tail_s.shape, F32)

    hb = jnp.concatenate([_dot(perm_ref[...], h_ref[c * t:(c + 1) * t, :]) for c in range(nch)],
                         axis=0).astype(BF16)
    dt = _softplus(_dot(hb, wdt_ref[...]) + dtb_ref[...])
    la = dt * (-jnp.exp(alog_ref[...]))

    def xbc_cols(jc):
        lanes = slice(jc * col_chunk, (jc + 1) * col_chunk)
        xall = _dot(hb, wx_ref[:, lanes])
        for c in range(nch):
            xbc_s[c * ext + pre:(c + 1) * ext, lanes] = xall[c * t:(c + 1) * t, :]
        for c in range(nch):
            base = c * ext
            xbc_s[base:base + pre, lanes] = xbc_s[base + t - 1:base + pre + t - 1, lanes]
            for v in range(ntap):
                if c == 0:
                    prev = tail_s[v:v + 1, lanes]
                else:
                    prow = base - ext + pre + last_rows[v]
                    prev = xbc_s[prow:prow + 1, lanes]
                xbc_s[base + v * SUBLANES:base + v * SUBLANES + 1, lanes] = prev
            acc = cb_ref[:, lanes]
            for k in range(SSD_CONV):
                acc = acc + cw_ref[k:k + 1, lanes] * xbc_s[base + k * SUBLANES:
                                                           base + k * SUBLANES + t, lanes]
            xc = _silu(acc)
            rows = slice(c * t, (c + 1) * t)
            lo = jc * col_chunk
            if lo < width:
                xd_s[rows, lanes] = xc.astype(BF16)
                xdd_s[rows, lanes] = (xc * e2_s[rows, lanes]).astype(BF16)
                ysk_s[rows, lanes] = xc * dsk_ref[:, lanes]
            elif lo < width + SSD_GROUPS * nst:
                b_s[rows, lo - width:lo - width + col_chunk] = xc.astype(BF16)
            else:
                cl = lo - width - SSD_GROUPS * nst
                c_s[rows, cl:cl + col_chunk] = xc.astype(BF16)

    x_cols = width // col_chunk
    for jc in range(x_cols, xbc_w // col_chunk):
        xbc_cols(jc)

    r3 = _dot(btril_ref[...], _split3(la))
    acum = r3[:, :LANES] + r3[:, LANES:2 * LANES] + r3[:, 2 * LANES:]
    tots = [acum[(c + 1) * t - 1:(c + 1) * t, :] for c in range(nch)]
    arow = acum - jnp.log(dt)
    for c in range(nch):
        acumt_s[c] = arow[c * t:(c + 1) * t, :].T
    tot_rows = jnp.concatenate([jnp.broadcast_to(v, (t, LANES)) for v in tots], axis=0)
    dstate = jnp.exp(tot_rows - acum)
    cdec = jnp.exp(jnp.concatenate([jnp.broadcast_to(v, (SUBLANES, LANES)) for v in tots], axis=0))
    e2_s[...] = _dot(_pack2(dt * dstate), exp_ref[...])
    e3_s[...] = _dot(_pack2(jnp.exp(acum)), exp_ref[...])
    cdx = _dot(_pack2(cdec), exp_ref[...])
    for c in range(nch):
        cdec_s[c] = cdx[c * SUBLANES:(c + 1) * SUBLANES, :]
    acol_s[...] = _dot(_pack3(acum), sel_ref[...])

    for jc in range(x_cols):
        lanes = slice(jc * col_chunk, (jc + 1) * col_chunk)
        xbc_cols(jc)
        zg_s[:, lanes] = _silu(_dot(hb, wz_ref[:, lanes]))

    for v in range(ntap):
        row = (nch - 1) * ext + pre + last_rows[v]
        tail_s[v:v + 1, :] = xbc_s[row:row + 1, :]
    cv_ref[0] = tail_s[...]

    def token(i):
        return (i % SUBLANES) * steps + i // SUBLANES

    causal = (token(lax.broadcasted_iota(jnp.int32, (t, t), 0))
              >= token(lax.broadcasted_iota(jnp.int32, (t, t), 1)))
    lane = lax.broadcasted_iota(jnp.int32, (t, LANES), 1)
    keep_lo = jnp.where(lane < hd_dim, 1.0, 0.0).astype(BF16)
    keep_hi = jnp.where(lane < hd_dim, 0.0, 1.0).astype(BF16)
    nw = nw_ref[...]

    def chunk_body(c, carry):
        rows = pl.ds(pl.multiple_of(c * t, t), t)
        ys = []
        for g in range(SSD_GROUPS):
            gl = slice(g * gw, (g + 1) * gw)
            cg = c_s[rows, g * nst:(g + 1) * nst]
            bg = b_s[rows, g * nst:(g + 1) * nst]
            cbm = jnp.where(causal, _dot_nt(cg, bg), 0.0)
            pairs = []
            for jp in range(hpg // 2):
                ha = g * hpg + 2 * jp
                ms = []
                for hd in (ha, ha + 1):
                    diff = acol_s[rows, hd * t:(hd + 1) * t] - acumt_s[c, hd:hd + 1, :]
                    ms.append((cbm * jnp.exp(jnp.where(causal, diff, -jnp.inf))).astype(BF16))
                xdp = xd_s[rows, ha * hd_dim:ha * hd_dim + LANES]
                rhs = jnp.concatenate([xdp * keep_lo, xdp * keep_hi], axis=0)
                pairs.append(_dot(jnp.concatenate(ms, axis=1), rhs))
            htg = ht_s[:, gl]
            y = (jnp.concatenate(pairs, axis=1) + _dot(cg, htg.astype(BF16)) * e3_s[rows, gl]
                 + ysk_s[rows, gl])
            ys.append(y)
            upd = lax.dot_general(bg, xdd_s[rows, gl], (((0,), (0,)), ((), ())),
                                  preferred_element_type=F32)
            ht_s[:, gl] = htg * cdec_s[c, 0:1, gl] + upd
        o = _group_norm(jnp.concatenate(ys, axis=1), zg_s[rows, :], nw, gw).astype(BF16)
        o_ref[rows, :] = _dot(permt_ref[...], o).astype(BF16)
        return carry

    lax.fori_loop(0, nch, chunk_body, 0, unroll=2)

    @pl.when(j == pl.num_programs(1) - 1)
    def _():
        for pr in range(nheads // 2):
            tt = ht_s[:, pr * LANES:(pr + 1) * LANES].T
            st_ref[0, 2 * pr] = tt[:hd_dim]
            st_ref[0, 2 * pr + 1] = tt[hd_dim:]


def _ssdp(h, wz, wx, wdt, cw, cb, dtb, alog, dsk, nw, nseq, nheads, nch):
    m, d = h.shape
    width = wz.shape[1]
    xbc_w = wx.shape[1]
    hd_dim = width // nheads
    nst = SSD_STATE
    t = SSD_CHUNK
    r = nch * t
    nblk = m // (nseq * r)
    ri = jnp.arange(r)
    steps = t // SUBLANES
    tok = (ri % t % SUBLANES) * steps + ri % t // SUBLANES
    btril = ((tok[:, None] >= tok[None, :]) & (ri[:, None] // t == ri[None, :] // t)).astype(BF16)
    perm = (tok[:t, None] == jnp.arange(t)[None, :]).astype(BF16)
    lane = jnp.arange(LANES)
    head = (lane % HEAD_REP)[:, None]
    exp_m = ((head == (jnp.arange(width) // hd_dim)[None, :])
             & (lane < 2 * HEAD_REP)[:, None]).astype(BF16)
    sel_m = ((head == (jnp.arange(nheads * t) // t)[None, :])
             & (lane < 3 * HEAD_REP)[:, None]).astype(BF16)
    row = lambda w: pl.BlockSpec((r, w), lambda i, j: (i * nblk + j, 0))
    kern = functools.partial(_ssdp_kernel, nch=nch, col_chunk=4 * LANES)
    vm = lambda shape, dt: pltpu.VMEM(shape, dt)
    return pl.pallas_call(
        kern,
        grid=(nseq, nblk),
        in_specs=[row(d), _const_spec(wz.shape), _const_spec(wx.shape), _const_spec(wdt.shape),
                  _const_spec(cw.shape), _const_spec(cb.shape), _const_spec(dtb.shape),
                  _const_spec(alog.shape), _const_spec(dsk.shape), _const_spec(nw.shape),
                  _const_spec(btril.shape), _const_spec(sel_m.shape), _const_spec(exp_m.shape),
                  _const_spec(perm.shape), _const_spec(perm.shape)],
        out_specs=[row(width),
                   pl.BlockSpec((1, nheads, hd_dim, nst), lambda i, j: (i, 0, 0, 0)),
                   pl.BlockSpec((1, SSD_CONV - 1, xbc_w), lambda i, j: (i, 0, 0))],
        out_shape=[jax.ShapeDtypeStruct((m, width), BF16),
                   jax.ShapeDtypeStruct((nseq, nheads, hd_dim, nst), F32),
                   jax.ShapeDtypeStruct((nseq, SSD_CONV - 1, xbc_w), F32)],
        scratch_shapes=[vm((nch * (t + (SSD_CONV - 1) * SUBLANES), xbc_w), F32),
                        vm((r, width), F32),
                        vm((r, nheads * t), F32), vm((nch, LANES, t), F32),
                        vm((r, width), F32), vm((r, width), F32),
                        vm((r, width), BF16), vm((r, width), BF16), vm((r, width), F32),
                        vm((r, SSD_GROUPS * nst), BF16), vm((r, SSD_GROUPS * nst), BF16),
                        vm((nch, SUBLANES, width), F32), vm((nst, width), F32),
                        vm((SSD_CONV - 1, xbc_w), F32)],
        compiler_params=_params(("arbitrary", "arbitrary")),
        name="ssd_prompt",
    )(h, wz, wx, wdt, cw, cb, dtb, alog, dsk, nw, btril, sel_m, exp_m, perm, perm.T)


def _rep_heads(v):
    grp = jnp.pad(v, [(0, 0)] * (v.ndim - 1) + [(0, HEAD_REP - v.shape[-1])])
    tail = jnp.zeros(v.shape[:-1] + (LANES - 3 * HEAD_REP,), v.dtype)
    return jnp.concatenate([grp, grp, grp, tail], axis=-1)


def _s5_call(h, s5re0, s5im0, w, nb, tl, casts=()):
    g, p = s5re0.shape[1:]
    o5, re1, im1, *cast = _s5(h, w["w_u"], w["s5_wb"], w["s5_wcre"], w["s5_wcim"], w["s5_are"],
                              w["s5_aim"], w["s5_d"], w["s5_wglu"], w["s5_bglu"],
                              s5re0.reshape(nb, g * p), s5im0.reshape(nb, g * p), nb, tl, casts)
    return o5, re1.reshape(nb, g, p), im1.reshape(nb, g, p), cast


def _layer(xp, xs, s5re0, s5im0, ssd0, conv0, w, *, tm, s5_tl, ssd_nch, ssd_nb):
    bp, lp, d = xp.shape
    bs, ls, _ = xs.shape
    mp, ms = bp * lp, bs * ls
    g, p = s5re0.shape[1:]
    nheads = ssd0.shape[1]

    xs_tm = xs.transpose(1, 0, 2).reshape(ms, d)
    x1p, hp, x1s, hs, w_u, w_z, w_x, w_dt = _ffn1(
        xp.reshape(mp, d), xs_tm, w["ffn1_norm"], w["ffn1_wg"], w["ffn1_wu"], w["ffn1_wd"],
        w["mix_norm"], w["w_in"], w["w_in_splits"], tm)
    w = dict(w, w_u=w_u, w_z=w_z, w_x=w_x, w_dt=_rep_heads(w_dt))

    zeros = jnp.zeros((bp, g, p), F32)
    o5p, p_re, p_im, (wg2, wu2, wd2, wo) = _s5_call(
        hp.reshape(bp, lp, d), zeros, zeros, w, bp, s5_tl,
        (w["ffn2_wg"], w["ffn2_wu"], w["ffn2_wd"], w["w_out"]))
    osp, p_ssd, p_conv = _ssdp(hp, w["w_z"], w["w_x"], w["w_dt"], w["conv_w"], w["conv_b"],
                               w["dt_bias"], w["a_log"], w["ssd_d"], w["ssd_norm"],
                               bp, nheads, ssd_nch)

    o5s, s_re, s_im, _ = _s5_call(hs, s5re0, s5im0, w, bs, ls)
    oss, s_ssd, s_conv_tm = _ssds(hs, w["w_z"], w["w_x"], w["w_dt"], w["conv_w"], w["conv_b"],
                                  w["dt_bias"], w["a_log"], w["ssd_d"], w["ssd_norm"], ssd0,
                                  conv0.transpose(1, 0, 2), ssd_nb)

    yp, ys = _out_ffn2(x1p, o5p.reshape(mp, -1), osp, x1s, o5s, oss, wo, w["ffn2_norm"],
                       wg2, wu2, wd2, w["final_norm"], tm)
    return (yp.reshape(bp, lp, d), ys.reshape(ls, bs, d).transpose(1, 0, 2),
            p_re, p_im, p_ssd, p_conv, s_re, s_im, s_ssd, s_conv_tm.transpose(1, 0, 2))


def kernel(x_prompt, x_sample, state_s5_re, state_s5_im, state_ssd, state_conv, ffn1_norm, ffn1_w_gate, ffn1_w_up, ffn1_w_down, mix_norm, w_in, s5_lambda_re, s5_lambda_im, s5_log_step, s5_b_re, s5_b_im, s5_c_re, s5_c_im, s5_d, s5_w_glu, s5_b_glu, ssd_conv_w, ssd_conv_b, ssd_dt_bias, ssd_a_log, ssd_d, ssd_norm, w_out, ffn2_norm, ffn2_w_gate, ffn2_w_up, ffn2_w_down, final_norm):
    depth = w_in.shape[0]
    assert depth == 1, "single-layer stack"
    i = 0
    d = x_prompt.shape[-1]
    g, p = s5_lambda_re.shape[1:]
    s5_width = g * S5_GROUP
    nheads = ssd_a_log.shape[1]
    ssd_width = nheads * SSD_HEAD_DIM
    xbc_w = ssd_conv_w.shape[-1]
    c0, c1, c2 = s5_width, s5_width + ssd_width, s5_width + ssd_width + xbc_w
    row = lambda v: v.astype(F32).reshape(1, -1)

    are, aim, bbre, bbim = _s5_params(s5_lambda_re[i], s5_lambda_im[i], s5_log_step[i],
                                      s5_b_re[i], s5_b_im[i])
    nblk = s5_width // LANES
    wb = jnp.concatenate([_block_diag(bbre, nblk), _block_diag(bbim, nblk)],
                         axis=-1).astype(BF16)
    wcre = _block_diag(s5_c_re[i].astype(F32).transpose(0, 2, 1), nblk).astype(BF16)
    wcim = _block_diag(s5_c_im[i].astype(F32).transpose(0, 2, 1), nblk).astype(BF16)

    w = {
        "ffn1_norm": row(ffn1_norm[i]), "mix_norm": row(mix_norm[i]),
        "ffn2_norm": row(ffn2_norm[i]), "final_norm": row(final_norm),
        "ffn1_wg": ffn1_w_gate[i].astype(BF16), "ffn1_wu": ffn1_w_up[i].astype(BF16),
        "ffn1_wd": ffn1_w_down[i].astype(BF16),
        "w_in": w_in.astype(F32), "w_in_splits": (c0, c1 - c0, c2 - c1, nheads),
        "ffn2_wg": ffn2_w_gate.astype(F32), "ffn2_wu": ffn2_w_up.astype(F32),
        "ffn2_wd": ffn2_w_down.astype(F32), "w_out": w_out.astype(F32),
        "s5_wb": wb, "s5_wcre": wcre, "s5_wcim": wcim, "s5_are": are, "s5_aim": aim,
        "s5_d": row(s5_d[i]), "s5_wglu": s5_w_glu[i].astype(BF16), "s5_bglu": row(s5_b_glu[i]),
        "conv_w": ssd_conv_w[i].astype(F32), "conv_b": row(ssd_conv_b[i]),
        "dt_bias": _rep_heads(row(ssd_dt_bias[i])), "a_log": _rep_heads(row(ssd_a_log[i])),
        "ssd_d": jnp.repeat(ssd_d[i].astype(F32), SSD_HEAD_DIM).reshape(1, -1),
        "ssd_norm": row(ssd_norm[i]),
    }

    lp = x_prompt.shape[1]
    yp, ys, *states = _layer(
        x_prompt, x_sample, state_s5_re[i], state_s5_im[i], state_ssd[i], state_conv[i], w,
        tm=512, s5_tl=min(lp, 128), ssd_nch=min(lp // SSD_CHUNK, 4), ssd_nb=SUBLANES)
    return (yp, ys) + tuple(v[None] for v in states)
```

```python
import functools
import math

import jax
import jax.numpy as jnp
from jax import lax
from jax.experimental import pallas as pl
from jax.experimental.pallas import tpu as pltpu

F32 = jnp.float32
BF16 = jnp.bfloat16
EPS = 1e-6

LANES = 128
SUBLANES = 8
VMEM_LIMIT = 56 * 1024 * 1024

S5_GROUP = 16
S5_STATE = 64
SSD_HEAD_DIM = 64
SSD_GROUPS = 4
SSD_STATE = 128
SSD_CONV = 4
SSD_CHUNK = 128


def _sigmoid(x):
    return 0.5 + 0.5 * jnp.tanh(0.5 * x)


def _silu(x):
    hx = 0.5 * x
    return hx + hx * jnp.tanh(hx)


def _gelu_tanh(x):
    c = math.sqrt(2.0 / math.pi)
    return 0.5 * x * (1.0 + jnp.tanh(c * (x + 0.044715 * (x * x * x))))


def _softplus(x):
    return jnp.maximum(x, 0.0) + jnp.log1p(jnp.exp(-jnp.abs(x)))


def _rms(x, w):
    return x * lax.rsqrt(jnp.mean(x * x, axis=-1, keepdims=True) + EPS) * w


def _dot(a, b):
    return jnp.dot(a, b, preferred_element_type=F32)


def _dot_nt(a, b):
    return lax.dot_general(a, b, (((1,), (1,)), ((), ())), preferred_element_type=F32)


def _const_spec(shape):
    nd = len(shape)
    return pl.BlockSpec(shape, lambda *_: (0,) * nd, pipeline_mode=pl.Buffered(1))


def _params(sem):
    return pltpu.CompilerParams(dimension_semantics=sem, vmem_limit_bytes=VMEM_LIMIT)


def _swiglu_half(x, nw, wg_ref, wu_ref, wd_ref):
    hn = _rms(x, nw).astype(BF16)
    g = _dot(hn, wg_ref[...])
    u = _dot(hn, wu_ref[...])
    a = (_silu(g) * u).astype(BF16)
    return x + 0.5 * _dot(a, wd_ref[...])


def _cast_specs(ws, steps, layer=0):
    ins, outs, shapes = [], [], []
    for w in ws:
        _, nrow, ncol = w.shape
        rows = nrow // steps
        assert rows * steps == nrow and rows % (2 * SUBLANES) == 0
        ins.append(pl.BlockSpec((None, rows, ncol),
                                lambda i: (layer, jnp.minimum(i, steps - 1), 0)))
        outs.append(pl.BlockSpec((rows, ncol), lambda i: (jnp.minimum(i, steps - 1), 0)))
        shapes.append(jax.ShapeDtypeStruct((nrow, ncol), BF16))
    return ins, outs, shapes


def _ffn1_kernel(xp_ref, xs_ref, nw_ref, wg_ref, wu_ref, wd_ref, mixw_ref, win_ref,
                 x1p_ref, hp_ref, x1s_ref, hs_ref, *wsplit_refs, n_p):
    is_p = pl.program_id(0) < n_p

    def tile(x_ref, x1_ref, h_ref):
        x1 = _swiglu_half(x_ref[...], nw_ref[...], wg_ref, wu_ref, wd_ref)
        x1_ref[...] = x1
        h_ref[...] = _rms(x1, mixw_ref[...]).astype(BF16)

    pl.when(is_p)(functools.partial(tile, xp_ref, x1p_ref, hp_ref))
    pl.when(jnp.logical_not(is_p))(functools.partial(tile, xs_ref, x1s_ref, hs_ref))

    lo = 0
    for o_ref in wsplit_refs:
        o_ref[...] = win_ref[:, lo:lo + o_ref.shape[1]].astype(BF16)
        lo += o_ref.shape[1]


def _ffn1(xp, xs, nw, wg, wu, wd, mixw, w_in, splits, tm):
    mp, d = xp.shape
    ms = xs.shape[0]
    assert ms == tm
    n_p = mp // tm
    dff = wg.shape[1]
    rowp = lambda w: pl.BlockSpec((tm, w), lambda i: (jnp.minimum(i, n_p - 1), 0))
    rows = lambda w: pl.BlockSpec((ms, w), lambda i: (0, 0))
    layer = 0
    wrows = w_in.shape[1] // n_p
    wblk = lambda w: pl.BlockSpec((wrows, w), lambda i: (jnp.minimum(i, n_p - 1), 0))
    w_in_blk = pl.BlockSpec((None, wrows, w_in.shape[2]),
                            lambda i: (layer, jnp.minimum(i, n_p - 1), 0))
    return pl.pallas_call(
        functools.partial(_ffn1_kernel, n_p=n_p),
        grid=(n_p + 1,),
        in_specs=[rowp(d), rows(d), _const_spec((1, d)), _const_spec((d, dff)),
                  _const_spec((d, dff)), _const_spec((dff, d)), _const_spec((1, d)),
                  w_in_blk],
        out_specs=[rowp(d), rowp(d), rows(d), rows(d)] + [wblk(c) for c in splits],
        out_shape=[jax.ShapeDtypeStruct((mp, d), F32), jax.ShapeDtypeStruct((mp, d), BF16),
                   jax.ShapeDtypeStruct((ms, d), F32), jax.ShapeDtypeStruct((ms, d), BF16)]
                  + [jax.ShapeDtypeStruct((w_in.shape[1], c), BF16) for c in splits],
        compiler_params=_params(("arbitrary",)),
        name="ffn1",
    )(xp, xs, nw, wg, wu, wd, mixw, w_in)


def _out_ffn2_kernel(x1p_ref, o5p_ref, osp_ref, x1s_ref, o5s_ref, oss_ref, wo_ref, nw_ref,
                     wg_ref, wu_ref, wd_ref, fin_ref, yp_ref, ys_ref, *, n_p):
    is_p = pl.program_id(0) < n_p
    k5 = o5p_ref.shape[1]

    def tile(x1_ref, o5_ref, os_ref, y_ref):
        x2 = x1_ref[...] + (_dot(o5_ref[...].astype(BF16), wo_ref[:k5, :])
                            + _dot(os_ref[...].astype(BF16), wo_ref[k5:, :]))
        x3 = _swiglu_half(x2, nw_ref[...], wg_ref, wu_ref, wd_ref)
        y_ref[...] = _rms(x3, fin_ref[...])

    pl.when(is_p)(functools.partial(tile, x1p_ref, o5p_ref, osp_ref, yp_ref))
    pl.when(jnp.logical_not(is_p))(functools.partial(tile, x1s_ref, o5s_ref, oss_ref, ys_ref))


def _out_ffn2(x1p, o5p, osp, x1s, o5s, oss, wo, nw, wg, wu, wd, fin, tm):
    mp, d = x1p.shape
    ms = x1s.shape[0]
    assert ms == tm
    n_p = mp // tm
    dff = wg.shape[1]
    rowp = lambda w: pl.BlockSpec((tm, w), lambda i: (jnp.minimum(i, n_p - 1), 0))
    rows = lambda w: pl.BlockSpec((ms, w), lambda i: (0, 0))
    return pl.pallas_call(
        functools.partial(_out_ffn2_kernel, n_p=n_p),
        grid=(n_p + 1,),
        in_specs=[rowp(d), rowp(o5p.shape[1]), rowp(osp.shape[1]),
                  rows(d), rows(o5s.shape[1]), rows(oss.shape[1]),
                  _const_spec(wo.shape), _const_spec((1, d)),
                  _const_spec((d, dff)), _const_spec((d, dff)), _const_spec((dff, d)),
                  _const_spec((1, d))],
        out_specs=[rowp(d), rows(d)],
        out_shape=[jax.ShapeDtypeStruct((mp, d), F32), jax.ShapeDtypeStruct((ms, d), F32)],
        compiler_params=_params(("arbitrary",)),
        name="out_ffn2",
    )(x1p, o5p, osp, x1s, o5s, oss, wo, nw, wg, wu, wd, fin)


def _s5_param_kernel(lre_ref, lim_ref, step_ref, bre_ref, bim_ref,
                     are_ref, aim_ref, bbre_ref, bbim_ref):
    lre, lim, step = lre_ref[...], lim_ref[...], step_ref[...]
    mag = jnp.exp(lre * step)
    ang = lim * step
    are = mag * jnp.cos(ang)
    aim = mag * jnp.sin(ang)
    den = lre * lre + lim * lim
    nre, nim = are - 1.0, aim
    cre = (nre * lre + nim * lim) / den
    cim = (nim * lre - nre * lim) / den
    are_ref[...] = are
    aim_ref[...] = aim
    bre, bim = bre_ref[...], bim_ref[...]
    bbre_ref[...] = cre * bre - cim * bim
    bbim_ref[...] = cre * bim + cim * bre


def _s5_params(lam_re, lam_im, log_step, b_re, b_im):
    g, p = lam_re.shape
    hh = b_re.shape[-1]
    gp = g * p
    step = jnp.exp(log_step.astype(F32))
    step_row = jnp.broadcast_to(step[:, None], (g, p)).reshape(1, gp)
    to_lanes = lambda b: b.astype(F32).reshape(gp, hh).T
    full = lambda shape: pl.BlockSpec(shape, lambda: (0,) * len(shape))
    are, aim, bbre, bbim = pl.pallas_call(
        _s5_param_kernel,
        in_specs=[full((1, gp)), full((1, gp)), full((1, gp)), full((hh, gp)), full((hh, gp))],
        out_specs=[full((1, gp)), full((1, gp)), full((hh, gp)), full((hh, gp))],
        out_shape=[jax.ShapeDtypeStruct((1, gp), F32), jax.ShapeDtypeStruct((1, gp), F32),
                   jax.ShapeDtypeStruct((hh, gp), F32), jax.ShapeDtypeStruct((hh, gp), F32)],
        name="s5_params",
    )(lam_re.astype(F32).reshape(1, gp), lam_im.astype(F32).reshape(1, gp), step_row,
      to_lanes(b_re), to_lanes(b_im))
    to_blocks = lambda b: b.reshape(hh, g, p).transpose(1, 0, 2)
    return are, aim, to_blocks(bbre), to_blocks(bbim)


def _block_diag(w, nblk):
    g, a, b = w.shape
    gl = g // nblk
    w = w.reshape(nblk, gl, a, b)
    eye = jnp.eye(gl, dtype=w.dtype)
    out = w[:, :, :, None, :] * eye[None, :, None, :, None]
    return out.reshape(nblk, gl * a, gl * b)


def _s5_kernel(*refs, nb, tl, n_cast):
    (h_ref, wu_ref, wb_ref, wcre_ref, wcim_ref, are_ref, aim_ref, d_ref, wglu_ref, bglu_ref,
     s0re_ref, s0im_ref) = refs[:12]
    cast_in = refs[12:12 + n_cast]
    o5_ref, sre_ref, sim_ref = refs[12 + n_cast:15 + n_cast]
    cast_out = refs[15 + n_cast:15 + 2 * n_cast]
    u_s, xre_s, xim_s, y_s, *perm_s = refs[15 + 2 * n_cast:]
    for src, dst in zip(cast_in, cast_out):
        dst[...] = src[...].astype(BF16)
    width = u_s.shape[1]
    nstate = xre_s.shape[1]
    nblk = wb_ref.shape[0]
    cin = width // nblk
    cst = nstate // nblk
    r = nb * tl
    pack = 2 * SUBLANES

    @pl.when(pl.program_id(0) == 0)
    def _():
        sre_ref[...] = s0re_ref[...]
        sim_ref[...] = s0im_ref[...]

    if perm_s:
        slab_s, = perm_s
        assert cin == LANES and nb == SUBLANES
        pitch = tl + SUBLANES
        u_bt = _dot(h_ref[...].reshape(r, h_ref.shape[2]), wu_ref[...])
        for b in range(nb):
            for k in range(nblk):
                slab_s[k, b * pitch:b * pitch + tl, :] = u_bt[b * tl:(b + 1) * tl,
                                                              k * cin:(k + 1) * cin]

        for t in range(tl):
            for k in range(nblk):
                u_s[t * nb:(t + 1) * nb, k * cin:(k + 1) * cin] = (
                    slab_s[k, pl.ds(t, nb, stride=pitch), :])
        u = u_s[...]
    else:
        u = _dot(h_ref[...], wu_ref[...])
        u_s[...] = u
    ub = u.astype(BF16)

    def drive(k):
        bu = _dot(ub[:, k * cin:(k + 1) * cin], wb_ref[k])
        xre_s[:, k * cst:(k + 1) * cst] = bu[:, :cst]
        xim_s[:, k * cst:(k + 1) * cst] = bu[:, cst:]

    def scan(k):
        lanes = slice(k * cst, (k + 1) * cst)
        a_re = jnp.broadcast_to(are_ref[:, lanes], (SUBLANES, cst))
        a_im = jnp.broadcast_to(aim_ref[:, lanes], (SUBLANES, cst))

        def step(rows, s_re, s_im):
            n_re = a_re * s_re - a_im * s_im + xre_s[rows, lanes]
            n_im = a_re * s_im + a_im * s_re + xim_s[rows, lanes]
            xre_s[rows, lanes] = n_re
            xim_s[rows, lanes] = n_im
            return n_re, n_im

        if nb == SUBLANES:
            s_re, s_im = sre_ref[:, lanes], sim_ref[:, lanes]
            for t in range(tl):
                s_re, s_im = step(slice(t * nb, (t + 1) * nb), s_re, s_im)
            sre_ref[:, lanes] = s_re
            sim_ref[:, lanes] = s_im
        else:
            def group_body(sg, carry):
                srow = pl.ds(pl.multiple_of(sg * SUBLANES, SUBLANES), SUBLANES)
                st = (sre_ref[srow, lanes], sim_ref[srow, lanes])
                for t in range(tl):
                    st = step(pl.ds(pl.multiple_of(t * nb + sg * SUBLANES, SUBLANES), SUBLANES), *st)
                sre_ref[srow, lanes], sim_ref[srow, lanes] = st
                return carry

            lax.fori_loop(0, nb // SUBLANES, group_body, 0)

    def readout(k):
        st = slice(k * cst, (k + 1) * cst)
        y_s[:, k * cin:(k + 1) * cin] = (_dot(xre_s[:, st].astype(BF16), wcre_ref[k])
                                         - _dot(xim_s[:, st].astype(BF16), wcim_ref[k]))

    for k in range(nblk + 2):
        if k < nblk:
            drive(k)
        if 1 <= k <= nblk:
            scan(k - 1)
        if k >= 2:
            readout(k - 2)
    y = y_s[...] + d_ref[...] * u_s[...]
    v = _gelu_tanh(y)
    gate = _dot(v.astype(BF16), wglu_ref[...]) + bglu_ref[...]
    o5 = v * _sigmoid(gate)
    if perm_s:
        for k in range(nblk):
            slab_s[k, 0:r, :] = o5[:, k * cin:(k + 1) * cin]
        for b in range(nb):
            for t0 in range(0, tl, pack):
                for k in range(nblk):
                    lo = slab_s[k, pl.ds(t0 * nb + b, SUBLANES, stride=nb), :]
                    hi = slab_s[k, pl.ds((t0 + SUBLANES) * nb + b, SUBLANES, stride=nb), :]
                    o5_ref[b, t0:t0 + pack, k * cin:(k + 1) * cin] = (
                        jnp.concatenate([lo, hi], axis=0).astype(BF16))
    else:
        o5_ref[...] = o5.astype(BF16)


def _s5(h, wu, wb, wcre, wcim, are, aim, dsk, wglu, bglu, s0re, s0im, nb, tl, casts=()):
    width = wu.shape[1]
    nstate = are.shape[1]
    r = nb * tl
    batch_major = h.ndim == 3
    if batch_major:
        steps = h.shape[1] // tl
        blk = lambda w: pl.BlockSpec((nb, tl, w), lambda i: (0, i, 0))
        o_shape = (nb, h.shape[1], width)
        extra = [pltpu.VMEM((width // LANES, nb * (tl + SUBLANES), LANES), F32)]
    else:
        steps = h.shape[0] // r
        blk = lambda w: pl.BlockSpec((r, w), lambda i: (i, 0))
        o_shape = (h.shape[0], width)
        extra = []
    kern = functools.partial(_s5_kernel, nb=nb, tl=tl, n_cast=len(casts))
    c_in, c_out, c_shapes = _cast_specs(casts, steps)
    return pl.pallas_call(
        kern,
        grid=(steps,),
        in_specs=[blk(h.shape[-1]), _const_spec(wu.shape), _const_spec(wb.shape),
                  _const_spec(wcre.shape), _const_spec(wcim.shape), _const_spec(are.shape),
                  _const_spec(aim.shape), _const_spec(dsk.shape), _const_spec(wglu.shape),
                  _const_spec(bglu.shape), _const_spec(s0re.shape), _const_spec(s0im.shape)]
                 + c_in,
        out_specs=[blk(width), pl.BlockSpec((nb, nstate), lambda i: (0, 0)),
                   pl.BlockSpec((nb, nstate), lambda i: (0, 0))] + c_out,
        out_shape=[jax.ShapeDtypeStruct(o_shape, BF16),
                   jax.ShapeDtypeStruct((nb, nstate), F32),
                   jax.ShapeDtypeStruct((nb, nstate), F32)] + c_shapes,
        scratch_shapes=[pltpu.VMEM((r, width), F32), pltpu.VMEM((r, nstate), F32),
                        pltpu.VMEM((r, nstate), F32), pltpu.VMEM((r, width), F32)] + extra,
        compiler_params=_params(("arbitrary",)),
        name="s5",
    )(h, wu, wb, wcre, wcim, are, aim, dsk, wglu, bglu, s0re, s0im, *casts)


def _split2(x):
    hi = x.astype(BF16)
    lo = (x - hi.astype(F32)).astype(BF16)
    return jnp.concatenate([hi, lo], axis=1)


def _split3(x):
    hi = x.astype(BF16)
    r1 = x - hi.astype(F32)
    mid = r1.astype(BF16)
    lo = (r1 - mid.astype(F32)).astype(BF16)
    return jnp.concatenate([hi, mid, lo], axis=1)


HEAD_REP = 32


def _pack2(x):
    grp = lax.broadcasted_iota(jnp.int32, x.shape, 1) // HEAD_REP
    hi = x.astype(BF16).astype(F32)
    return jnp.where(grp == 0, hi, x - hi).astype(BF16)


def _pack3(x):
    grp = lax.broadcasted_iota(jnp.int32, x.shape, 1) // HEAD_REP
    hi = x.astype(BF16).astype(F32)
    r1 = x - hi
    mid = r1.astype(BF16).astype(F32)
    return jnp.where(grp == 0, hi, jnp.where(grp == 1, mid, r1 - mid)).astype(BF16)


def _group_norm(y, zg, nw, gw):
    outs = []
    for g in range(SSD_GROUPS):
        seg = y[:, g * gw:(g + 1) * gw] * zg[:, g * gw:(g + 1) * gw]
        outs.append(seg * lax.rsqrt(jnp.mean(seg * seg, axis=-1, keepdims=True) + EPS))
    return jnp.concatenate(outs, axis=1) * nw


def _ssds_pre_kernel(h_ref, wz_ref, wx_ref, wdt_ref, cw_ref, cb_ref, dtb_ref, alog_ref, dsk_ref,
                     conv0_ref, g2_ref, exp_ref, c_o, b_o, xdd_o, yp_o, eac_o, zg_o, cdec_o, cv_o):
    nbt = conv0_ref.shape[1]
    seq = h_ref.shape[0] // nbt
    width = zg_o.shape[1]
    nbc = c_o.shape[1]
    hb = h_ref[...]
    zg_o[...] = _silu(_dot(hb, wz_ref[...]))
    dt = _softplus(_dot(hb, wdt_ref[...]) + dtb_ref[...])
    la = dt * (-jnp.exp(alog_ref[...]))
    xbc = _dot(hb, wx_ref[...])
    rows = [slice(t * nbt, (t + 1) * nbt) for t in range(seq)]
    full = [conv0_ref[k] for k in range(SSD_CONV - 1)] + [xbc[r] for r in rows]
    for k in range(SSD_CONV - 1):
        cv_o[k] = full[seq + k]
    acums = []
    for t in range(seq):
        acums.append(la[rows[t]] if t == 0 else acums[-1] + la[rows[t]])
    tot = acums[-1]
    cdec_o[...] = jnp.exp(tot)
    xs, bq, cq = [], [], []
    for t in range(seq):
        acc = cb_ref[...]
        for k in range(SSD_CONV):
            acc = acc + cw_ref[k:k + 1, :] * full[t + k]
        xc = _silu(acc)
        xs.append(xc[:, :width])
        bq.append(xc[:, width:width + nbc].astype(BF16).astype(F32))
        cq.append(xc[:, width + nbc:].astype(BF16).astype(F32))
    for t in range(seq):
        r = rows[t]
        b_o[r, :] = bq[t]
        c_o[r, :] = cq[t]
        xdd_o[r, :] = xs[t] * _dot(_split2(dt[r] * jnp.exp(tot - acums[t])), exp_ref[...])
        eac_o[r, :] = _dot(_split2(jnp.exp(acums[t])), exp_ref[...])
        yp = dsk_ref[...] * xs[t]
        for s in range(t + 1):
            cbx = _dot(_split2(cq[t] * bq[s]), g2_ref[...])
            coef = cbx * jnp.exp(acums[t] - acums[s]) * dt[rows[s]]
            yp = yp + _dot(_split2(coef), exp_ref[...]) * xs[s]
        yp_o[r, :] = yp


def _ssds_state_kernel(c_ref, b_ref, xdd_ref, yp_ref, eac_ref, zg_ref, cdec_ref, st_ref, nw_ref,
                       o_ref, sto_ref, yoff_s, xddt_s):
    seq, nb, width = xdd_ref.shape
    nheads, hd_dim, nst = st_ref.shape[1:]
    hpg = nheads // SSD_GROUPS
    gw = width // SSD_GROUPS
    rws = seq * nb
    cst = c_ref[...].reshape(rws, c_ref.shape[2]).astype(BF16)
    bst = b_ref[...].reshape(rws, b_ref.shape[2])
    xdd = xdd_ref[...].reshape(rws, width)
    xddt_s[...] = jnp.concatenate([xdd, jnp.zeros((LANES - rws, width), F32)], axis=0).T.astype(BF16)
    rowb = lax.broadcasted_iota(jnp.int32, (rws, gw), 0) % nb
    zpad = jnp.zeros((LANES - rws, nst), BF16)
    for b in range(nb):
        mine = rowb == b
        for g in range(SSD_GROUPS):
            gl = slice(g * gw, (g + 1) * gw)
            hg = st_ref[b, g * hpg:(g + 1) * hpg].reshape(gw, nst)
            res = _dot_nt(cst[:, g * nst:(g + 1) * nst], hg.astype(BF16))
            yoff_s[:, gl] = jnp.where(mine, res, 0.0) if b == 0 else jnp.where(mine, res, yoff_s[:, gl])
            bm = jnp.where(mine[:, :nst], bst[:, g * nst:(g + 1) * nst], 0.0).astype(BF16)
            upd = _dot(xddt_s[gl, :], jnp.concatenate([bm, zpad], axis=0))
            for jh in range(hpg):
                hd = g * hpg + jh
                hrows = slice(jh * hd_dim, (jh + 1) * hd_dim)
                sto_ref[b, hd] = hg[hrows] * cdec_ref[b, hd] + upd[hrows]
    y = yp_ref[...].reshape(rws, width) + yoff_s[...] * eac_ref[...].reshape(rws, width)
    o = _group_norm(y, zg_ref[...].reshape(rws, width), nw_ref[...], gw)
    o_ref[...] = o.reshape(seq, nb, width)


def _ssds(h_tm, wz, wx, wdt, cw, cb, dtb, alog, dsk, nw, ssd0, conv0_tm, nb):
    m, d = h_tm.shape
    nseq, nheads, hd_dim, nst = ssd0.shape
    seq = m // nseq
    width = wz.shape[1]
    xbc_w = wx.shape[1]
    nbc = SSD_GROUPS * nst
    hpg = nheads // SSD_GROUPS
    head = jnp.arange(LANES)
    exp1 = (head[:, None] == (jnp.arange(width) // hd_dim)[None, :]).astype(BF16)
    g1 = ((jnp.arange(nbc) // nst)[:, None] == (head // hpg)[None, :]) & (head < nheads)[None, :]
    exp_m = jnp.concatenate([exp1, exp1], axis=0)
    g2 = jnp.concatenate([g1, g1], axis=0).astype(BF16)
    full = lambda shape: pl.BlockSpec(shape, lambda: (0,) * len(shape))
    ins = (h_tm, wz, wx, wdt, cw, cb, dtb, alog, dsk, conv0_tm, g2, exp_m)
    outs = [jax.ShapeDtypeStruct((m, nbc), F32), jax.ShapeDtypeStruct((m, nbc), F32),
            jax.ShapeDtypeStruct((m, width), F32), jax.ShapeDtypeStruct((m, width), F32),
            jax.ShapeDtypeStruct((m, width), F32), jax.ShapeDtypeStruct((m, width), F32),
            jax.ShapeDtypeStruct((nseq, LANES), F32),
            jax.ShapeDtypeStruct((SSD_CONV - 1, nseq, xbc_w), F32)]
    c, b, xdd, yp, eac, zg, cdec, conv1_tm = pl.pallas_call(
        _ssds_pre_kernel,
        in_specs=[full(a.shape) for a in ins],
        out_specs=[full(o.shape) for o in outs],
        out_shape=outs,
        compiler_params=pltpu.CompilerParams(vmem_limit_bytes=VMEM_LIMIT),
        name="ssd_sample_pre",
    )(*ins)

    tm3 = lambda a: a.reshape(seq, nseq, a.shape[1])
    blk = lambda w: pl.BlockSpec((seq, nb, w), lambda i: (0, i, 0))
    st_spec = pl.BlockSpec((nb, nheads, hd_dim, nst), lambda i: (i, 0, 0, 0))
    o, ssd1 = pl.pallas_call(
        _ssds_state_kernel,
        grid=(nseq // nb,),
        in_specs=[blk(nbc), blk(nbc), blk(width), blk(width), blk(width), blk(width),
                  pl.BlockSpec((nb, LANES), lambda i: (i, 0), memory_space=pltpu.SMEM),
                  st_spec, _const_spec(nw.shape)],
        out_specs=[blk(width), st_spec],
        out_shape=[jax.ShapeDtypeStruct((seq, nseq, width), F32),
                   jax.ShapeDtypeStruct(ssd0.shape, F32)],
        scratch_shapes=[pltpu.VMEM((seq * nb, width), F32), pltpu.VMEM((width, LANES), BF16)],
        compiler_params=_params(("arbitrary",)),
        name="ssd_sample_state",
    )(tm3(c), tm3(b), tm3(xdd), tm3(yp), tm3(eac), tm3(zg), cdec, ssd0, nw)
    return o.reshape(m, width), ssd1, conv1_tm


def _ssdp_kernel(h_ref, wz_ref, wx_ref, wdt_ref, cw_ref, cb_ref, dtb_ref, alog_ref, dsk_ref,
                 nw_ref, btril_ref, exp_ref, perm_ref, permt_ref, o_ref, st_ref, cv_ref,
                 xbc_s, zg_s, acum_s, acumt_s, e2_s, e3_s, xd_s, xdd_s, ysk_s, b_s, c_s,
                 cdec_s, ht_s, tail_s, *, nch, col_chunk):
    t = SSD_CHUNK
    r = nch * t
    j = pl.program_id(1)
    nheads, hd_dim, nst = st_ref.shape[1:]
    width = nheads * hd_dim
    gw = width // SSD_GROUPS
    hpg = nheads // SSD_GROUPS
    xbc_w = xbc_s.shape[1]
    ntap = SSD_CONV - 1
    pre = ntap * SUBLANES
    ext = pre + t
    steps = t // SUBLANES
    last_rows = [t - 1 - SUBLANES * (ntap - 1 - v) for v in range(ntap)]

    @pl.when(j == 0)
    def _():
        ht_s[...] = jnp.zeros(ht_s.shape, F32)
        tail_s[...] = jnp.zeros(tail_s.shape, F32)

    hb = jnp.concatenate([_dot(perm_ref[...], h_ref[c * t:(c + 1) * t, :]) for c in range(nch)],
                         axis=0).astype(BF16)
    dt = _softplus(_dot(hb, wdt_ref[...]) + dtb_ref[...])
    la = dt * (-jnp.exp(alog_ref[...]))

    def xbc_cols(jc):
        lanes = slice(jc * col_chunk, (jc + 1) * col_chunk)
        xall = _dot(hb, wx_ref[:, lanes])
        for c in range(nch):
            xbc_s[c * ext + pre:(c + 1) * ext, lanes] = xall[c * t:(c + 1) * t, :]
        for c in range(nch):
            base = c * ext
            xbc_s[base:base + pre, lanes] = xbc_s[base + t - 1:base + pre + t - 1, lanes]
            for v in range(ntap):
                if c == 0:
                    prev = tail_s[v:v + 1, lanes]
                else:
                    prow = base - ext + pre + last_rows[v]
                    prev = xbc_s[prow:prow + 1, lanes]
                xbc_s[base + v * SUBLANES:base + v * SUBLANES + 1, lanes] = prev
            acc = cb_ref[:, lanes]
            for k in range(SSD_CONV):
                acc = acc + cw_ref[k:k + 1, lanes] * xbc_s[base + k * SUBLANES:
                                                           base + k * SUBLANES + t, lanes]
            xc = _silu(acc)
            rows = slice(c * t, (c + 1) * t)
            lo = jc * col_chunk
            if lo < width:
                xd_s[rows, lanes] = xc.astype(BF16)
                xdd_s[rows, lanes] = (xc * e2_s[rows, lanes]).astype(BF16)
                ysk_s[rows, lanes] = xc * dsk_ref[:, lanes]
            elif lo < width + SSD_GROUPS * nst:
                b_s[rows, lo - width:lo - width + col_chunk] = xc.astype(BF16)
            else:
                cl = lo - width - SSD_GROUPS * nst
                c_s[rows, cl:cl + col_chunk] = xc.astype(BF16)

    x_cols = width // col_chunk
    for jc in range(x_cols, xbc_w // col_chunk):
        xbc_cols(jc)

    r3 = _dot(btril_ref[...], _split3(la))
    acum = r3[:, :LANES] + r3[:, LANES:2 * LANES] + r3[:, 2 * LANES:]
    tots = [acum[(c + 1) * t - 1:(c + 1) * t, :] for c in range(nch)]
    arow = acum - jnp.log(dt)
    for c in range(nch):
        acumt_s[c] = arow[c * t:(c + 1) * t, :].T
    tot_rows = jnp.concatenate([jnp.broadcast_to(v, (t, LANES)) for v in tots], axis=0)
    dstate = jnp.exp(tot_rows - acum)
    cdec = jnp.exp(jnp.concatenate([jnp.broadcast_to(v, (SUBLANES, LANES)) for v in tots], axis=0))
    e2_s[...] = _dot(_pack2(dt * dstate), exp_ref[...])
    e3_s[...] = _dot(_pack2(jnp.exp(acum)), exp_ref[...])
    cdx = _dot(_pack2(cdec), exp_ref[...])
    for c in range(nch):
        cdec_s[c] = cdx[c * SUBLANES:(c + 1) * SUBLANES, :]
    acum_s[...] = acum

    for jc in range(x_cols):
        lanes = slice(jc * col_chunk, (jc + 1) * col_chunk)
        xbc_cols(jc)
        zg_s[:, lanes] = _silu(_dot(hb, wz_ref[:, lanes]))

    for v in range(ntap):
        row = (nch - 1) * ext + pre + last_rows[v]
        tail_s[v:v + 1, :] = xbc_s[row:row + 1, :]
    cv_ref[0] = tail_s[...]

    def token(i):
        return (i % SUBLANES) * steps + i // SUBLANES

    causal = (token(lax.broadcasted_iota(jnp.int32, (t, t), 0))
              >= token(lax.broadcasted_iota(jnp.int32, (t, t), 1)))
    lane = lax.broadcasted_iota(jnp.int32, (t, LANES), 1)
    keep_lo = jnp.where(lane < hd_dim, 1.0, 0.0).astype(BF16)
    keep_hi = jnp.where(lane < hd_dim, 0.0, 1.0).astype(BF16)
    nw = nw_ref[...]

    def chunk_body(c, carry):
        rows = pl.ds(pl.multiple_of(c * t, t), t)
        ys = []
        for g in range(SSD_GROUPS):
            gl = slice(g * gw, (g + 1) * gw)
            cg = c_s[rows, g * nst:(g + 1) * nst]
            bg = b_s[rows, g * nst:(g + 1) * nst]
            cbm = jnp.where(causal, _dot_nt(cg, bg), 0.0)
            pairs = []
            for jp in range(hpg // 2):
                ha = g * hpg + 2 * jp
                ms = []
                for hd in (ha, ha + 1):
                    diff = acum_s[rows, hd:hd + 1] - acumt_s[c, hd:hd + 1, :]
                    ms.append((cbm * jnp.exp(jnp.where(causal, diff, -jnp.inf))).astype(BF16))
                xdp = xd_s[rows, ha * hd_dim:ha * hd_dim + LANES]
                rhs = jnp.concatenate([xdp * keep_lo, xdp * keep_hi], axis=0)
                pairs.append(_dot(jnp.concatenate(ms, axis=1), rhs))
            htg = ht_s[:, gl]
            y = (jnp.concatenate(pairs, axis=1) + _dot(cg, htg.astype(BF16)) * e3_s[rows, gl]
                 + ysk_s[rows, gl])
            ys.append(y)
            upd = lax.dot_general(bg, xdd_s[rows, gl], (((0,), (0,)), ((), ())),
                                  preferred_element_type=F32)
            ht_s[:, gl] = htg * cdec_s[c, 0:1, gl] + upd
        o = _group_norm(jnp.concatenate(ys, axis=1), zg_s[rows, :], nw, gw).astype(BF16)
        o_ref[rows, :] = _dot(permt_ref[...], o).astype(BF16)
        return carry

    lax.fori_loop(0, nch, chunk_body, 0, unroll=2)

    @pl.when(j == pl.num_programs(1) - 1)
    def _():
        for pr in range(nheads // 2):
            tt = ht_s[:, pr * LANES:(pr + 1) * LANES].T
            st_ref[0, 2 * pr] = tt[:hd_dim]
            st_ref[0, 2 * pr + 1] = tt[hd_dim:]


def _ssdp(h, wz, wx, wdt, cw, cb, dtb, alog, dsk, nw, nseq, nheads, nch):
    m, d = h.shape
    width = wz.shape[1]
    xbc_w = wx.shape[1]
    hd_dim = width // nheads
    nst = SSD_STATE
    t = SSD_CHUNK
    r = nch * t
    nblk = m // (nseq * r)
    ri = jnp.arange(r)
    steps = t // SUBLANES
    tok = (ri % t % SUBLANES) * steps + ri % t // SUBLANES
    btril = ((tok[:, None] >= tok[None, :]) & (ri[:, None] // t == ri[None, :] // t)).astype(BF16)
    perm = (tok[:t, None] == jnp.arange(t)[None, :]).astype(BF16)
    lane = jnp.arange(LANES)
    head = (lane % HEAD_REP)[:, None]
    exp_m = ((head == (jnp.arange(width) // hd_dim)[None, :])
             & (lane < 2 * HEAD_REP)[:, None]).astype(BF16)
    row = lambda w: pl.BlockSpec((r, w), lambda i, j: (i * nblk + j, 0))
    kern = functools.partial(_ssdp_kernel, nch=nch, col_chunk=4 * LANES)
    vm = lambda shape, dt: pltpu.VMEM(shape, dt)
    return pl.pallas_call(
        kern,
        grid=(nseq, nblk),
        in_specs=[row(d), _const_spec(wz.shape), _const_spec(wx.shape), _const_spec(wdt.shape),
                  _const_spec(cw.shape), _const_spec(cb.shape), _const_spec(dtb.shape),
                  _const_spec(alog.shape), _const_spec(dsk.shape), _const_spec(nw.shape),
                  _const_spec(btril.shape), _const_spec(exp_m.shape),
                  _const_spec(perm.shape), _const_spec(perm.shape)],
        out_specs=[row(width),
                   pl.BlockSpec((1, nheads, hd_dim, nst), lambda i, j: (i, 0, 0, 0)),
                   pl.BlockSpec((1, SSD_CONV - 1, xbc_w), lambda i, j: (i, 0, 0))],
        out_shape=[jax.ShapeDtypeStruct((m, width), BF16),
                   jax.ShapeDtypeStruct((nseq, nheads, hd_dim, nst), F32),
                   jax.ShapeDtypeStruct((nseq, SSD_CONV - 1, xbc_w), F32)],
        scratch_shapes=[vm((nch * (t + (SSD_CONV - 1) * SUBLANES), xbc_w), F32),
                        vm((r, width), F32),
                        vm((r, LANES), F32), vm((nch, LANES, t), F32),
                        vm((r, width), F32), vm((r, width), F32),
                        vm((r, width), BF16), vm((r, width), BF16), vm((r, width), F32),
                        vm((r, SSD_GROUPS * nst), BF16), vm((r, SSD_GROUPS * nst), BF16),
                        vm((nch, SUBLANES, width), F32), vm((nst, width), F32),
                        vm((SSD_CONV - 1, xbc_w), F32)],
        compiler_params=_params(("arbitrary", "arbitrary")),
        name="ssd_prompt",
    )(h, wz, wx, wdt, cw, cb, dtb, alog, dsk, nw, btril, exp_m, perm, perm.T)


def _rep_heads(v):
    grp = jnp.pad(v, [(0, 0)] * (v.ndim - 1) + [(0, HEAD_REP - v.shape[-1])])
    tail = jnp.zeros(v.shape[:-1] + (LANES - 3 * HEAD_REP,), v.dtype)
    return jnp.concatenate([grp, grp, grp, tail], axis=-1)


def _s5_call(h, s5re0, s5im0, w, nb, tl, casts=()):
    g, p = s5re0.shape[1:]
    o5, re1, im1, *cast = _s5(h, w["w_u"], w["s5_wb"], w["s5_wcre"], w["s5_wcim"], w["s5_are"],
                              w["s5_aim"], w["s5_d"], w["s5_wglu"], w["s5_bglu"],
                              s5re0.reshape(nb, g * p), s5im0.reshape(nb, g * p), nb, tl, casts)
    return o5, re1.reshape(nb, g, p), im1.reshape(nb, g, p), cast


def _layer(xp, xs, s5re0, s5im0, ssd0, conv0, w, *, tm, s5_tl, ssd_nch, ssd_nb):
    bp, lp, d = xp.shape
    bs, ls, _ = xs.shape
    mp, ms = bp * lp, bs * ls
    g, p = s5re0.shape[1:]
    nheads = ssd0.shape[1]

    xs_tm = xs.transpose(1, 0, 2).reshape(ms, d)
    x1p, hp, x1s, hs, w_u, w_z, w_x, w_dt = _ffn1(
        xp.reshape(mp, d), xs_tm, w["ffn1_norm"], w["ffn1_wg"], w["ffn1_wu"], w["ffn1_wd"],
        w["mix_norm"], w["w_in"], w["w_in_splits"], tm)
    w = dict(w, w_u=w_u, w_z=w_z, w_x=w_x, w_dt=_rep_heads(w_dt))

    zeros = jnp.zeros((bp, g, p), F32)
    o5p, p_re, p_im, (wg2, wu2, wd2, wo) = _s5_call(
        hp.reshape(bp, lp, d), zeros, zeros, w, bp, s5_tl,
        (w["ffn2_wg"], w["ffn2_wu"], w["ffn2_wd"], w["w_out"]))
    osp, p_ssd, p_conv = _ssdp(hp, w["w_z"], w["w_x"], w["w_dt"], w["conv_w"], w["conv_b"],
                               w["dt_bias"], w["a_log"], w["ssd_d"], w["ssd_norm"],
                               bp, nheads, ssd_nch)

    o5s, s_re, s_im, _ = _s5_call(hs, s5re0, s5im0, w, bs, ls)
    oss, s_ssd, s_conv_tm = _ssds(hs, w["w_z"], w["w_x"], w["w_dt"], w["conv_w"], w["conv_b"],
                                  w["dt_bias"], w["a_log"], w["ssd_d"], w["ssd_norm"], ssd0,
                                  conv0.transpose(1, 0, 2), ssd_nb)

    yp, ys = _out_ffn2(x1p, o5p.reshape(mp, -1), osp, x1s, o5s, oss, wo, w["ffn2_norm"],
                       wg2, wu2, wd2, w["final_norm"], tm)
    return (yp.reshape(bp, lp, d), ys.reshape(ls, bs, d).transpose(1, 0, 2),
            p_re, p_im, p_ssd, p_conv, s_re, s_im, s_ssd, s_conv_tm.transpose(1, 0, 2))


def kernel(x_prompt, x_sample, state_s5_re, state_s5_im, state_ssd, state_conv, ffn1_norm, ffn1_w_gate, ffn1_w_up, ffn1_w_down, mix_norm, w_in, s5_lambda_re, s5_lambda_im, s5_log_step, s5_b_re, s5_b_im, s5_c_re, s5_c_im, s5_d, s5_w_glu, s5_b_glu, ssd_conv_w, ssd_conv_b, ssd_dt_bias, ssd_a_log, ssd_d, ssd_norm, w_out, ffn2_norm, ffn2_w_gate, ffn2_w_up, ffn2_w_down, final_norm):
    depth = w_in.shape[0]
    assert depth == 1, "single-layer stack"
    i = 0
    d = x_prompt.shape[-1]
    g, p = s5_lambda_re.shape[1:]
    s5_width = g * S5_GROUP
    nheads = ssd_a_log.shape[1]
    ssd_width = nheads * SSD_HEAD_DIM
    xbc_w = ssd_conv_w.shape[-1]
    c0, c1, c2 = s5_width, s5_width + ssd_width, s5_width + ssd_width + xbc_w
    row = lambda v: v.astype(F32).reshape(1, -1)

    are, aim, bbre, bbim = _s5_params(s5_lambda_re[i], s5_lambda_im[i], s5_log_step[i],
                                      s5_b_re[i], s5_b_im[i])
    nblk = s5_width // LANES
    wb = jnp.concatenate([_block_diag(bbre, nblk), _block_diag(bbim, nblk)],
                         axis=-1).astype(BF16)
    wcre = _block_diag(s5_c_re[i].astype(F32).transpose(0, 2, 1), nblk).astype(BF16)
    wcim = _block_diag(s5_c_im[i].astype(F32).transpose(0, 2, 1), nblk).astype(BF16)

    w = {
        "ffn1_norm": row(ffn1_norm[i]), "mix_norm": row(mix_norm[i]),
        "ffn2_norm": row(ffn2_norm[i]), "final_norm": row(final_norm),
        "ffn1_wg": ffn1_w_gate[i].astype(BF16), "ffn1_wu": ffn1_w_up[i].astype(BF16),
        "ffn1_wd": ffn1_w_down[i].astype(BF16),
        "w_in": w_in.astype(F32), "w_in_splits": (c0, c1 - c0, c2 - c1, nheads),
        "ffn2_wg": ffn2_w_gate.astype(F32), "ffn2_wu": ffn2_w_up.astype(F32),
        "ffn2_wd": ffn2_w_down.astype(F32), "w_out": w_out.astype(F32),
        "s5_wb": wb, "s5_wcre": wcre, "s5_wcim": wcim, "s5_are": are, "s5_aim": aim,
        "s5_d": row(s5_d[i]), "s5_wglu": s5_w_glu[i].astype(BF16), "s5_bglu": row(s5_b_glu[i]),
        "conv_w": ssd_conv_w[i].astype(F32), "conv_b": row(ssd_conv_b[i]),
        "dt_bias": _rep_heads(row(ssd_dt_bias[i])), "a_log": _rep_heads(row(ssd_a_log[i])),
        "ssd_d": jnp.repeat(ssd_d[i].astype(F32), SSD_HEAD_DIM).reshape(1, -1),
        "ssd_norm": row(ssd_norm[i]),
    }

    lp = x_prompt.shape[1]
    yp, ys, *states = _layer(
        x_prompt, x_sample, state_s5_re[i], state_s5_im[i], state_ssd[i], state_conv[i], w,
        tm=512, s5_tl=min(lp, 128), ssd_nch=min(lp // SSD_CHUNK, 4), ssd_nb=SUBLANES)
    return (yp, ys) + tuple(v[None] for v in states)
```

```python
import functools
import math

import jax
import jax.numpy as jnp
from jax import lax
from jax.experimental import pallas as pl
from jax.experimental.pallas import tpu as pltpu

F32 = jnp.float32
BF16 = jnp.bfloat16
EPS = 1e-6

LANES = 128
SUBLANES = 8
VMEM_LIMIT = 56 * 1024 * 1024

S5_GROUP = 16
S5_STATE = 64
SSD_HEAD_DIM = 64
SSD_GROUPS = 4
SSD_STATE = 128
SSD_CONV = 4
SSD_CHUNK = 128


def _sigmoid(x):
    return 0.5 + 0.5 * jnp.tanh(0.5 * x)


def _silu(x):
    hx = 0.5 * x
    return hx + hx * jnp.tanh(hx)


def _gelu_tanh(x):
    c = math.sqrt(2.0 / math.pi)
    return 0.5 * x * (1.0 + jnp.tanh(c * (x + 0.044715 * (x * x * x))))


def _softplus(x):
    return jnp.maximum(x, 0.0) + jnp.log1p(jnp.exp(-jnp.abs(x)))


def _rms(x, w):
    return x * lax.rsqrt(jnp.mean(x * x, axis=-1, keepdims=True) + EPS) * w


def _dot(a, b):
    return jnp.dot(a, b, preferred_element_type=F32)


def _dot_nt(a, b):
    return lax.dot_general(a, b, (((1,), (1,)), ((), ())), preferred_element_type=F32)


def _const_spec(shape):
    nd = len(shape)
    return pl.BlockSpec(shape, lambda *_: (0,) * nd, pipeline_mode=pl.Buffered(1))


def _params(sem):
    return pltpu.CompilerParams(dimension_semantics=sem, vmem_limit_bytes=VMEM_LIMIT)


def _swiglu_half(x, nw, wg_ref, wu_ref, wd_ref):
    hn = _rms(x, nw).astype(BF16)
    g = _dot(hn, wg_ref[...])
    u = _dot(hn, wu_ref[...])
    a = (_silu(g) * u).astype(BF16)
    return x + 0.5 * _dot(a, wd_ref[...])


def _cast_specs(ws, steps, layer=0):
    ins, outs, shapes = [], [], []
    for w in ws:
        _, nrow, ncol = w.shape
        rows = nrow // steps
        assert rows * steps == nrow and rows % (2 * SUBLANES) == 0
        ins.append(pl.BlockSpec((None, rows, ncol),
                                lambda i: (layer, jnp.minimum(i, steps - 1), 0)))
        outs.append(pl.BlockSpec((rows, ncol), lambda i: (jnp.minimum(i, steps - 1), 0)))
        shapes.append(jax.ShapeDtypeStruct((nrow, ncol), BF16))
    return ins, outs, shapes


def _ffn1_kernel(xp_ref, xs_ref, nw_ref, wg_ref, wu_ref, wd_ref, mixw_ref, win_ref,
                 x1p_ref, hp_ref, x1s_ref, hs_ref, *wsplit_refs, n_p):
    is_p = pl.program_id(0) < n_p

    def tile(x_ref, x1_ref, h_ref):
        x1 = _swiglu_half(x_ref[...], nw_ref[...], wg_ref, wu_ref, wd_ref)
        x1_ref[...] = x1
        h_ref[...] = _rms(x1, mixw_ref[...]).astype(BF16)

    pl.when(is_p)(functools.partial(tile, xp_ref, x1p_ref, hp_ref))
    pl.when(jnp.logical_not(is_p))(functools.partial(tile, xs_ref, x1s_ref, hs_ref))

    lo = 0
    for o_ref in wsplit_refs:
        o_ref[...] = win_ref[:, lo:lo + o_ref.shape[1]].astype(BF16)
        lo += o_ref.shape[1]


def _ffn1(xp, xs, nw, wg, wu, wd, mixw, w_in, splits, tm):
    mp, d = xp.shape
    ms = xs.shape[0]
    assert ms == tm
    n_p = mp // tm
    dff = wg.shape[1]
    rowp = lambda w: pl.BlockSpec((tm, w), lambda i: (jnp.minimum(i, n_p - 1), 0))
    rows = lambda w: pl.BlockSpec((ms, w), lambda i: (0, 0))
    layer = 0
    wrows = w_in.shape[1] // n_p
    wblk = lambda w: pl.BlockSpec((wrows, w), lambda i: (jnp.minimum(i, n_p - 1), 0))
    w_in_blk = pl.BlockSpec((None, wrows, w_in.shape[2]),
                            lambda i: (layer, jnp.minimum(i, n_p - 1), 0))
    return pl.pallas_call(
        functools.partial(_ffn1_kernel, n_p=n_p),
        grid=(n_p + 1,),
        in_specs=[rowp(d), rows(d), _const_spec((1, d)), _const_spec((d, dff)),
                  _const_spec((d, dff)), _const_spec((dff, d)), _const_spec((1, d)),
                  w_in_blk],
        out_specs=[rowp(d), rowp(d), rows(d), rows(d)] + [wblk(c) for c in splits],
        out_shape=[jax.ShapeDtypeStruct((mp, d), F32), jax.ShapeDtypeStruct((mp, d), BF16),
                   jax.ShapeDtypeStruct((ms, d), F32), jax.ShapeDtypeStruct((ms, d), BF16)]
                  + [jax.ShapeDtypeStruct((w_in.shape[1], c), BF16) for c in splits],
        compiler_params=_params(("arbitrary",)),
        name="ffn1",
    )(xp, xs, nw, wg, wu, wd, mixw, w_in)


def _out_ffn2_kernel(x1p_ref, o5p_ref, osp_ref, x1s_ref, o5s_ref, oss_ref, wo_ref, nw_ref,
                     wg_ref, wu_ref, wd_ref, fin_ref, yp_ref, ys_ref, *, n_p):
    is_p = pl.program_id(0) < n_p
    k5 = o5p_ref.shape[1]

    def tile(x1_ref, o5_ref, os_ref, y_ref):
        x2 = x1_ref[...] + (_dot(o5_ref[...].astype(BF16), wo_ref[:k5, :])
                            + _dot(os_ref[...].astype(BF16), wo_ref[k5:, :]))
        x3 = _swiglu_half(x2, nw_ref[...], wg_ref, wu_ref, wd_ref)
        y_ref[...] = _rms(x3, fin_ref[...])

    pl.when(is_p)(functools.partial(tile, x1p_ref, o5p_ref, osp_ref, yp_ref))
    pl.when(jnp.logical_not(is_p))(functools.partial(tile, x1s_ref, o5s_ref, oss_ref, ys_ref))


def _out_ffn2(x1p, o5p, osp, x1s, o5s, oss, wo, nw, wg, wu, wd, fin, tm):
    mp, d = x1p.shape
    ms = x1s.shape[0]
    assert ms == tm
    n_p = mp // tm
    dff = wg.shape[1]
    rowp = lambda w: pl.BlockSpec((tm, w), lambda i: (jnp.minimum(i, n_p - 1), 0))
    rows = lambda w: pl.BlockSpec((ms, w), lambda i: (0, 0))
    return pl.pallas_call(
        functools.partial(_out_ffn2_kernel, n_p=n_p),
        grid=(n_p + 1,),
        in_specs=[rowp(d), rowp(o5p.shape[1]), rowp(osp.shape[1]),
                  rows(d), rows(o5s.shape[1]), rows(oss.shape[1]),
                  _const_spec(wo.shape), _const_spec((1, d)),
                  _const_spec((d, dff)), _const_spec((d, dff)), _const_spec((dff, d)),
                  _const_spec((1, d))],
        out_specs=[rowp(d), rows(d)],
        out_shape=[jax.ShapeDtypeStruct((mp, d), F32), jax.ShapeDtypeStruct((ms, d), F32)],
        compiler_params=_params(("arbitrary",)),
        name="out_ffn2",
    )(x1p, o5p, osp, x1s, o5s, oss, wo, nw, wg, wu, wd, fin)


def _s5_param_kernel(lre_ref, lim_ref, step_ref, bre_ref, bim_ref,
                     are_ref, aim_ref, bbre_ref, bbim_ref):
    lre, lim, step = lre_ref[...], lim_ref[...], step_ref[...]
    mag = jnp.exp(lre * step)
    ang = lim * step
    are = mag * jnp.cos(ang)
    aim = mag * jnp.sin(ang)
    den = lre * lre + lim * lim
    nre, nim = are - 1.0, aim
    cre = (nre * lre + nim * lim) / den
    cim = (nim * lre - nre * lim) / den
    are_ref[...] = are
    aim_ref[...] = aim
    bre, bim = bre_ref[...], bim_ref[...]
    bbre_ref[...] = cre * bre - cim * bim
    bbim_ref[...] = cre * bim + cim * bre


def _s5_params(lam_re, lam_im, log_step, b_re, b_im):
    g, p = lam_re.shape
    hh = b_re.shape[-1]
    gp = g * p
    step = jnp.exp(log_step.astype(F32))
    step_row = jnp.broadcast_to(step[:, None], (g, p)).reshape(1, gp)
    to_lanes = lambda b: b.astype(F32).reshape(gp, hh).T
    full = lambda shape: pl.BlockSpec(shape, lambda: (0,) * len(shape))
    are, aim, bbre, bbim = pl.pallas_call(
        _s5_param_kernel,
        in_specs=[full((1, gp)), full((1, gp)), full((1, gp)), full((hh, gp)), full((hh, gp))],
        out_specs=[full((1, gp)), full((1, gp)), full((hh, gp)), full((hh, gp))],
        out_shape=[jax.ShapeDtypeStruct((1, gp), F32), jax.ShapeDtypeStruct((1, gp), F32),
                   jax.ShapeDtypeStruct((hh, gp), F32), jax.ShapeDtypeStruct((hh, gp), F32)],
        name="s5_params",
    )(lam_re.astype(F32).reshape(1, gp), lam_im.astype(F32).reshape(1, gp), step_row,
      to_lanes(b_re), to_lanes(b_im))
    to_blocks = lambda b: b.reshape(hh, g, p).transpose(1, 0, 2)
    return are, aim, to_blocks(bbre), to_blocks(bbim)


def _block_diag(w, nblk):
    g, a, b = w.shape
    gl = g // nblk
    w = w.reshape(nblk, gl, a, b)
    eye = jnp.eye(gl, dtype=w.dtype)
    out = w[:, :, :, None, :] * eye[None, :, None, :, None]
    return out.reshape(nblk, gl * a, gl * b)


def _s5_kernel(*refs, nb, tl, n_cast):
    (h_ref, wu_ref, wb_ref, wcre_ref, wcim_ref, are_ref, aim_ref, d_ref, wglu_ref, bglu_ref,
     s0re_ref, s0im_ref) = refs[:12]
    cast_in = refs[12:12 + n_cast]
    o5_ref, sre_ref, sim_ref = refs[12 + n_cast:15 + n_cast]
    cast_out = refs[15 + n_cast:15 + 2 * n_cast]
    u_s, xre_s, xim_s, y_s, *perm_s = refs[15 + 2 * n_cast:]
    for src, dst in zip(cast_in, cast_out):
        dst[...] = src[...].astype(BF16)
    width = u_s.shape[1]
    nstate = xre_s.shape[1]
    nblk = wb_ref.shape[0]
    cin = width // nblk
    cst = nstate // nblk
    r = nb * tl
    pack = 2 * SUBLANES

    @pl.when(pl.program_id(0) == 0)
    def _():
        sre_ref[...] = s0re_ref[...]
        sim_ref[...] = s0im_ref[...]

    if perm_s:
        slab_s, = perm_s
        assert cin == LANES and nb == SUBLANES
        pitch = tl + SUBLANES
        u_bt = _dot(h_ref[...].reshape(r, h_ref.shape[2]), wu_ref[...])
        for b in range(nb):
            for k in range(nblk):
                slab_s[k, b * pitch:b * pitch + tl, :] = u_bt[b * tl:(b + 1) * tl,
                                                              k * cin:(k + 1) * cin]

        for t in range(tl):
            for k in range(nblk):
                u_s[t * nb:(t + 1) * nb, k * cin:(k + 1) * cin] = (
                    slab_s[k, pl.ds(t, nb, stride=pitch), :])
        u = u_s[...]
    else:
        u = _dot(h_ref[...], wu_ref[...])
        u_s[...] = u
    ub = u.astype(BF16)

    def drive(k):
        bu = _dot(ub[:, k * cin:(k + 1) * cin], wb_ref[k])
        xre_s[:, k * cst:(k + 1) * cst] = bu[:, :cst]
        xim_s[:, k * cst:(k + 1) * cst] = bu[:, cst:]

    def scan(k):
        lanes = slice(k * cst, (k + 1) * cst)
        a_re = jnp.broadcast_to(are_ref[:, lanes], (SUBLANES, cst))
        a_im = jnp.broadcast_to(aim_ref[:, lanes], (SUBLANES, cst))

        def step(rows, s_re, s_im):
            n_re = a_re * s_re - a_im * s_im + xre_s[rows, lanes]
            n_im = a_re * s_im + a_im * s_re + xim_s[rows, lanes]
            xre_s[rows, lanes] = n_re
            xim_s[rows, lanes] = n_im
            return n_re, n_im

        if nb == SUBLANES:
            s_re, s_im = sre_ref[:, lanes], sim_ref[:, lanes]
            for t in range(tl):
                s_re, s_im = step(slice(t * nb, (t + 1) * nb), s_re, s_im)
            sre_ref[:, lanes] = s_re
            sim_ref[:, lanes] = s_im
        else:
            def group_body(sg, carry):
                srow = pl.ds(pl.multiple_of(sg * SUBLANES, SUBLANES), SUBLANES)
                st = (sre_ref[srow, lanes], sim_ref[srow, lanes])
                for t in range(tl):
                    st = step(pl.ds(pl.multiple_of(t * nb + sg * SUBLANES, SUBLANES), SUBLANES), *st)
                sre_ref[srow, lanes], sim_ref[srow, lanes] = st
                return carry

            lax.fori_loop(0, nb // SUBLANES, group_body, 0)

    def readout(k):
        st = slice(k * cst, (k + 1) * cst)
        y_s[:, k * cin:(k + 1) * cin] = (_dot(xre_s[:, st].astype(BF16), wcre_ref[k])
                                         - _dot(xim_s[:, st].astype(BF16), wcim_ref[k]))

    for k in range(nblk + 2):
        if k < nblk:
            drive(k)
        if 1 <= k <= nblk:
            scan(k - 1)
        if k >= 2:
            readout(k - 2)
    y = y_s[...] + d_ref[...] * u_s[...]
    v = _gelu_tanh(y)
    gate = _dot(v.astype(BF16), wglu_ref[...]) + bglu_ref[...]
    o5 = v * _sigmoid(gate)
    if perm_s:
        for k in range(nblk):
            slab_s[k, 0:r, :] = o5[:, k * cin:(k + 1) * cin]
        for b in range(nb):
            for t0 in range(0, tl, pack):
                for k in range(nblk):
                    lo = slab_s[k, pl.ds(t0 * nb + b, SUBLANES, stride=nb), :]
                    hi = slab_s[k, pl.ds((t0 + SUBLANES) * nb + b, SUBLANES, stride=nb), :]
                    o5_ref[b, t0:t0 + pack, k * cin:(k + 1) * cin] = (
                        jnp.concatenate([lo, hi], axis=0).astype(BF16))
    else:
        o5_ref[...] = o5.astype(BF16)


def _s5(h, wu, wb, wcre, wcim, are, aim, dsk, wglu, bglu, s0re, s0im, nb, tl, casts=()):
    width = wu.shape[1]
    nstate = are.shape[1]
    r = nb * tl
    batch_major = h.ndim == 3
    if batch_major:
        steps = h.shape[1] // tl
        blk = lambda w: pl.BlockSpec((nb, tl, w), lambda i: (0, i, 0))
        o_shape = (nb, h.shape[1], width)
        extra = [pltpu.VMEM((width // LANES, nb * (tl + SUBLANES), LANES), F32)]
    else:
        steps = h.shape[0] // r
        blk = lambda w: pl.BlockSpec((r, w), lambda i: (i, 0))
        o_shape = (h.shape[0], width)
        extra = []
    kern = functools.partial(_s5_kernel, nb=nb, tl=tl, n_cast=len(casts))
    c_in, c_out, c_shapes = _cast_specs(casts, steps)
    return pl.pallas_call(
        kern,
        grid=(steps,),
        in_specs=[blk(h.shape[-1]), _const_spec(wu.shape), _const_spec(wb.shape),
                  _const_spec(wcre.shape), _const_spec(wcim.shape), _const_spec(are.shape),
                  _const_spec(aim.shape), _const_spec(dsk.shape), _const_spec(wglu.shape),
                  _const_spec(bglu.shape), _const_spec(s0re.shape), _const_spec(s0im.shape)]
                 + c_in,
        out_specs=[blk(width), pl.BlockSpec((nb, nstate), lambda i: (0, 0)),
                   pl.BlockSpec((nb, nstate), lambda i: (0, 0))] + c_out,
        out_shape=[jax.ShapeDtypeStruct(o_shape, BF16),
                   jax.ShapeDtypeStruct((nb, nstate), F32),
                   jax.ShapeDtypeStruct((nb, nstate), F32)] + c_shapes,
        scratch_shapes=[pltpu.VMEM((r, width), F32), pltpu.VMEM((r, nstate), F32),
                        pltpu.VMEM((r, nstate), F32), pltpu.VMEM((r, width), F32)] + extra,
        compiler_params=_params(("arbitrary",)),
        name="s5",
    )(h, wu, wb, wcre, wcim, are, aim, dsk, wglu, bglu, s0re, s0im, *casts)


def _split2(x):
    hi = x.astype(BF16)
    lo = (x - hi.astype(F32)).astype(BF16)
    return jnp.concatenate([hi, lo], axis=1)


def _split3(x):
    hi = x.astype(BF16)
    r1 = x - hi.astype(F32)
    mid = r1.astype(BF16)
    lo = (r1 - mid.astype(F32)).astype(BF16)
    return jnp.concatenate([hi, mid, lo], axis=1)


HEAD_REP = 32


def _pack2(x):
    grp = lax.broadcasted_iota(jnp.int32, x.shape, 1) // HEAD_REP
    hi = x.astype(BF16).astype(F32)
    return jnp.where(grp == 0, hi, x - hi).astype(BF16)


def _pack3(x):
    grp = lax.broadcasted_iota(jnp.int32, x.shape, 1) // HEAD_REP
    hi = x.astype(BF16).astype(F32)
    r1 = x - hi
    mid = r1.astype(BF16).astype(F32)
    return jnp.where(grp == 0, hi, jnp.where(grp == 1, mid, r1 - mid)).astype(BF16)


def _group_norm(y, zg, nw, gw):
    outs = []
    for g in range(SSD_GROUPS):
        seg = y[:, g * gw:(g + 1) * gw] * zg[:, g * gw:(g + 1) * gw]
        outs.append(seg * lax.rsqrt(jnp.mean(seg * seg, axis=-1, keepdims=True) + EPS))
    return jnp.concatenate(outs, axis=1) * nw


def _ssds_pre_kernel(h_ref, wz_ref, wx_ref, wdt_ref, cw_ref, cb_ref, dtb_ref, alog_ref, dsk_ref,
                     conv0_ref, g2_ref, exp_ref, c_o, b_o, xdd_o, yp_o, eac_o, zg_o, cdec_o, cv_o):
    nbt = conv0_ref.shape[1]
    seq = h_ref.shape[0] // nbt
    width = zg_o.shape[1]
    nbc = c_o.shape[1]
    hb = h_ref[...]
    zg_o[...] = _silu(_dot(hb, wz_ref[...]))
    dt = _softplus(_dot(hb, wdt_ref[...]) + dtb_ref[...])
    la = dt * (-jnp.exp(alog_ref[...]))
    xbc = _dot(hb, wx_ref[...])
    rows = [slice(t * nbt, (t + 1) * nbt) for t in range(seq)]
    full = [conv0_ref[k] for k in range(SSD_CONV - 1)] + [xbc[r] for r in rows]
    for k in range(SSD_CONV - 1):
        cv_o[k] = full[seq + k]
    acums = []
    for t in range(seq):
        acums.append(la[rows[t]] if t == 0 else acums[-1] + la[rows[t]])
    tot = acums[-1]
    cdec_o[...] = jnp.exp(tot)
    xs, bq, cq = [], [], []
    for t in range(seq):
        acc = cb_ref[...]
        for k in range(SSD_CONV):
            acc = acc + cw_ref[k:k + 1, :] * full[t + k]
        xc = _silu(acc)
        xs.append(xc[:, :width])
        bq.append(xc[:, width:width + nbc].astype(BF16).astype(F32))
        cq.append(xc[:, width + nbc:].astype(BF16).astype(F32))
    for t in range(seq):
        r = rows[t]
        b_o[r, :] = bq[t]
        c_o[r, :] = cq[t]
        xdd_o[r, :] = xs[t] * _dot(_split2(dt[r] * jnp.exp(tot - acums[t])), exp_ref[...])
        eac_o[r, :] = _dot(_split2(jnp.exp(acums[t])), exp_ref[...])
        yp = dsk_ref[...] * xs[t]
        for s in range(t + 1):
            cbx = _dot(_split2(cq[t] * bq[s]), g2_ref[...])
            coef = cbx * jnp.exp(acums[t] - acums[s]) * dt[rows[s]]
            yp = yp + _dot(_split2(coef), exp_ref[...]) * xs[s]
        yp_o[r, :] = yp


def _ssds_state_kernel(c_ref, b_ref, xdd_ref, yp_ref, eac_ref, zg_ref, cdec_ref, st_ref, nw_ref,
                       o_ref, sto_ref, yoff_s, xddt_s):
    seq, nb, width = xdd_ref.shape
    nheads, hd_dim, nst = st_ref.shape[1:]
    hpg = nheads // SSD_GROUPS
    gw = width // SSD_GROUPS
    rws = seq * nb
    cst = c_ref[...].reshape(rws, c_ref.shape[2]).astype(BF16)
    bst = b_ref[...].reshape(rws, b_ref.shape[2])
    xdd = xdd_ref[...].reshape(rws, width)
    xddt_s[...] = jnp.concatenate([xdd, jnp.zeros((LANES - rws, width), F32)], axis=0).T.astype(BF16)
    rowb = lax.broadcasted_iota(jnp.int32, (rws, gw), 0) % nb
    zpad = jnp.zeros((LANES - rws, nst), BF16)
    for b in range(nb):
        mine = rowb == b
        for g in range(SSD_GROUPS):
            gl = slice(g * gw, (g + 1) * gw)
            hg = st_ref[b, g * hpg:(g + 1) * hpg].reshape(gw, nst)
            res = _dot_nt(cst[:, g * nst:(g + 1) * nst], hg.astype(BF16))
            yoff_s[:, gl] = jnp.where(mine, res, 0.0) if b == 0 else jnp.where(mine, res, yoff_s[:, gl])
            bm = jnp.where(mine[:, :nst], bst[:, g * nst:(g + 1) * nst], 0.0).astype(BF16)
            upd = _dot(xddt_s[gl, :], jnp.concatenate([bm, zpad], axis=0))
            for jh in range(hpg):
                hd = g * hpg + jh
                hrows = slice(jh * hd_dim, (jh + 1) * hd_dim)
                sto_ref[b, hd] = hg[hrows] * cdec_ref[b, hd] + upd[hrows]
    y = yp_ref[...].reshape(rws, width) + yoff_s[...] * eac_ref[...].reshape(rws, width)
    o = _group_norm(y, zg_ref[...].reshape(rws, width), nw_ref[...], gw)
    o_ref[...] = o.reshape(seq, nb, width)


def _ssds(h_tm, wz, wx, wdt, cw, cb, dtb, alog, dsk, nw, ssd0, conv0_tm, nb):
    m, d = h_tm.shape
    nseq, nheads, hd_dim, nst = ssd0.shape
    seq = m // nseq
    width = wz.shape[1]
    xbc_w = wx.shape[1]
    nbc = SSD_GROUPS * nst
    hpg = nheads // SSD_GROUPS
    head = jnp.arange(LANES)
    exp1 = (head[:, None] == (jnp.arange(width) // hd_dim)[None, :]).astype(BF16)
    g1 = ((jnp.arange(nbc) // nst)[:, None] == (head // hpg)[None, :]) & (head < nheads)[None, :]
    exp_m = jnp.concatenate([exp1, exp1], axis=0)
    g2 = jnp.concatenate([g1, g1], axis=0).astype(BF16)
    full = lambda shape: pl.BlockSpec(shape, lambda: (0,) * len(shape))
    ins = (h_tm, wz, wx, wdt, cw, cb, dtb, alog, dsk, conv0_tm, g2, exp_m)
    outs = [jax.ShapeDtypeStruct((m, nbc), F32), jax.ShapeDtypeStruct((m, nbc), F32),
            jax.ShapeDtypeStruct((m, width), F32), jax.ShapeDtypeStruct((m, width), F32),
            jax.ShapeDtypeStruct((m, width), F32), jax.ShapeDtypeStruct((m, width), F32),
            jax.ShapeDtypeStruct((nseq, LANES), F32),
            jax.ShapeDtypeStruct((SSD_CONV - 1, nseq, xbc_w), F32)]
    c, b, xdd, yp, eac, zg, cdec, conv1_tm = pl.pallas_call(
        _ssds_pre_kernel,
        in_specs=[full(a.shape) for a in ins],
        out_specs=[full(o.shape) for o in outs],
        out_shape=outs,
        compiler_params=pltpu.CompilerParams(vmem_limit_bytes=VMEM_LIMIT),
        name="ssd_sample_pre",
    )(*ins)

    tm3 = lambda a: a.reshape(seq, nseq, a.shape[1])
    blk = lambda w: pl.BlockSpec((seq, nb, w), lambda i: (0, i, 0))
    st_spec = pl.BlockSpec((nb, nheads, hd_dim, nst), lambda i: (i, 0, 0, 0))
    o, ssd1 = pl.pallas_call(
        _ssds_state_kernel,
        grid=(nseq // nb,),
        in_specs=[blk(nbc), blk(nbc), blk(width), blk(width), blk(width), blk(width),
                  pl.BlockSpec((nb, LANES), lambda i: (i, 0), memory_space=pltpu.SMEM),
                  st_spec, _const_spec(nw.shape)],
        out_specs=[blk(width), st_spec],
        out_shape=[jax.ShapeDtypeStruct((seq, nseq, width), F32),
                   jax.ShapeDtypeStruct(ssd0.shape, F32)],
        scratch_shapes=[pltpu.VMEM((seq * nb, width), F32), pltpu.VMEM((width, LANES), BF16)],
        compiler_params=_params(("arbitrary",)),
        name="ssd_sample_state",
    )(tm3(c), tm3(b), tm3(xdd), tm3(yp), tm3(eac), tm3(zg), cdec, ssd0, nw)
    return o.reshape(m, width), ssd1, conv1_tm


def _ssdp_kernel(h_ref, wz_ref, wx_ref, wdt_ref, cw_ref, cb_ref, dtb_ref, alog_ref, dsk_ref,
                 nw_ref, btril_ref, exp_ref, perm_ref, permt_ref, o_ref, st_ref, cv_ref,
                 xbc_s, zg_s, acum_s, acumt_s, e2_s, e3_s, xd_s, xdd_s, ysk_s, b_s, c_s,
                 cdec_s, ht_s, tail_s, *, nch, col_chunk):
    t = SSD_CHUNK
    r = nch * t
    j = pl.program_id(1)
    nheads, hd_dim, nst = st_ref.shape[1:]
    width = nheads * hd_dim
    gw = width // SSD_GROUPS
    hpg = nheads // SSD_GROUPS
    xbc_w = xbc_s.shape[1]
    ntap = SSD_CONV - 1
    pre = ntap * SUBLANES
    ext = pre + t
    steps = t // SUBLANES
    last_rows = [t - 1 - SUBLANES * (ntap - 1 - v) for v in range(ntap)]

    @pl.when(j == 0)
    def _():
        ht_s[...] = jnp.zeros(ht_s.shape, F32)
        tail_s[...] = jnp.zeros(tail_s.shape, F32)

    hb = jnp.concatenate([_dot(perm_ref[...], h_ref[c * t:(c + 1) * t, :]) for c in range(nch)],
                         axis=0).astype(BF16)
    dt = _softplus(_dot(hb, wdt_ref[...]) + dtb_ref[...])
    la = dt * (-jnp.exp(alog_ref[...]))

    def xbc_cols(jc):
        lanes = slice(jc * col_chunk, (jc + 1) * col_chunk)
        xall = _dot(hb, wx_ref[:, lanes])
        for c in range(nch):
            xbc_s[c * ext + pre:(c + 1) * ext, lanes] = xall[c * t:(c + 1) * t, :]
        for c in range(nch):
            base = c * ext
            xbc_s[base:base + pre, lanes] = xbc_s[base + t - 1:base + pre + t - 1, lanes]
            for v in range(ntap):
                if c == 0:
                    prev = tail_s[v:v + 1, lanes]
                else:
                    prow = base - ext + pre + last_rows[v]
                    prev = xbc_s[prow:prow + 1, lanes]
                xbc_s[base + v * SUBLANES:base + v * SUBLANES + 1, lanes] = prev
            acc = cb_ref[:, lanes]
            for k in range(SSD_CONV):
                acc = acc + cw_ref[k:k + 1, lanes] * xbc_s[base + k * SUBLANES:
                                                           base + k * SUBLANES + t, lanes]
            xc = _silu(acc)
            rows = slice(c * t, (c + 1) * t)
            lo = jc * col_chunk
            if lo < width:
                xd_s[rows, lanes] = xc.astype(BF16)
                xdd_s[rows, lanes] = (xc * e2_s[rows, lanes]).astype(BF16)
                ysk_s[rows, lanes] = xc * dsk_ref[:, lanes]
            elif lo < width + SSD_GROUPS * nst:
                b_s[rows, lo - width:lo - width + col_chunk] = xc.astype(BF16)
            else:
                cl = lo - width - SSD_GROUPS * nst
                c_s[rows, cl:cl + col_chunk] = xc.astype(BF16)

    x_cols = width // col_chunk
    for jc in range(x_cols, xbc_w // col_chunk):
        xbc_cols(jc)

    r3 = _dot(btril_ref[...], _split3(la))
    acum = r3[:, :LANES] + r3[:, LANES:2 * LANES] + r3[:, 2 * LANES:]
    tots = [acum[(c + 1) * t - 1:(c + 1) * t, :] for c in range(nch)]
    arow = acum - jnp.log(dt)
    for c in range(nch):
        acumt_s[c] = arow[c * t:(c + 1) * t, :].T
    tot_rows = jnp.concatenate([jnp.broadcast_to(v, (t, LANES)) for v in tots], axis=0)
    dstate = jnp.exp(tot_rows - acum)
    cdec = jnp.exp(jnp.concatenate([jnp.broadcast_to(v, (SUBLANES, LANES)) for v in tots], axis=0))
    e2_s[...] = _dot(_pack2(dt * dstate), exp_ref[...])
    e3_s[...] = _dot(_pack2(jnp.exp(acum)), exp_ref[...])
    cdx = _dot(_pack2(cdec), exp_ref[...])
    for c in range(nch):
        cdec_s[c] = cdx[c * SUBLANES:(c + 1) * SUBLANES, :]
    acum_s[...] = acum

    for jc in range(x_cols):
        lanes = slice(jc * col_chunk, (jc + 1) * col_chunk)
        xbc_cols(jc)
        zg_s[:, lanes] = _silu(_dot(hb, wz_ref[:, lanes]))

    for v in range(ntap):
        row = (nch - 1) * ext + pre + last_rows[v]
        tail_s[v:v + 1, :] = xbc_s[row:row + 1, :]
    cv_ref[0] = tail_s[...]

    def token(i):
        return (i % SUBLANES) * steps + i // SUBLANES

    causal = (token(lax.broadcasted_iota(jnp.int32, (t, t), 0))
              >= token(lax.broadcasted_iota(jnp.int32, (t, t), 1)))
    lane = lax.broadcasted_iota(jnp.int32, (t, LANES), 1)
    keep_lo = jnp.where(lane < hd_dim, 1.0, 0.0).astype(BF16)
    keep_hi = jnp.where(lane < hd_dim, 0.0, 1.0).astype(BF16)
    nw = nw_ref[...]

    def chunk_body(c, carry):
        rows = pl.ds(pl.multiple_of(c * t, t), t)
        ys = []
        for g in range(SSD_GROUPS):
            gl = slice(g * gw, (g + 1) * gw)
            cg = c_s[rows, g * nst:(g + 1) * nst]
            bg = b_s[rows, g * nst:(g + 1) * nst]
            cbm = jnp.where(causal, _dot_nt(cg, bg), 0.0)
            pairs = []
            for jp in range(hpg // 2):
                ha = g * hpg + 2 * jp
                ms = []
                for hd in (ha, ha + 1):
                    diff = acum_s[rows, hd:hd + 1] - acumt_s[c, hd:hd + 1, :]
                    ms.append((cbm * jnp.exp(jnp.where(causal, diff, -jnp.inf))).astype(BF16))
                xdp = xd_s[rows, ha * hd_dim:ha * hd_dim + LANES]
                rhs = jnp.concatenate([xdp * keep_lo, xdp * keep_hi], axis=0)
                pairs.append(_dot(jnp.concatenate(ms, axis=1), rhs))
            htg = ht_s[:, gl]
            y = (jnp.concatenate(pairs, axis=1) + _dot(cg, htg.astype(BF16)) * e3_s[rows, gl]
                 + ysk_s[rows, gl])
            ys.append(y)
            upd = lax.dot_general(bg, xdd_s[rows, gl], (((0,), (0,)), ((), ())),
                                  preferred_element_type=F32)
            ht_s[:, gl] = htg * cdec_s[c, 0:1, gl] + upd
        o = _group_norm(jnp.concatenate(ys, axis=1), zg_s[rows, :], nw, gw).astype(BF16)
        o_ref[rows, :] = _dot(permt_ref[...], o).astype(BF16)
        return carry

    lax.fori_loop(0, nch, chunk_body, 0, unroll=True)

    @pl.when(j == pl.num_programs(1) - 1)
    def _():
        for pr in range(nheads // 2):
            tt = ht_s[:, pr * LANES:(pr + 1) * LANES].T
            st_ref[0, 2 * pr] = tt[:hd_dim]
            st_ref[0, 2 * pr + 1] = tt[hd_dim:]


def _ssdp(h, wz, wx, wdt, cw, cb, dtb, alog, dsk, nw, nseq, nheads, nch):
    m, d = h.shape
    width = wz.shape[1]
    xbc_w = wx.shape[1]
    hd_dim = width // nheads
    nst = SSD_STATE
    t = SSD_CHUNK
    r = nch * t
    nblk = m // (nseq * r)
    ri = jnp.arange(r)
    steps = t // SUBLANES
    tok = (ri % t % SUBLANES) * steps + ri % t // SUBLANES
    btril = ((tok[:, None] >= tok[None, :]) & (ri[:, None] // t == ri[None, :] // t)).astype(BF16)
    perm = (tok[:t, None] == jnp.arange(t)[None, :]).astype(BF16)
    lane = jnp.arange(LANES)
    head = (lane % HEAD_REP)[:, None]
    exp_m = ((head == (jnp.arange(width) // hd_dim)[None, :])
             & (lane < 2 * HEAD_REP)[:, None]).astype(BF16)
    row = lambda w: pl.BlockSpec((r, w), lambda i, j: (i * nblk + j, 0))
    kern = functools.partial(_ssdp_kernel, nch=nch, col_chunk=4 * LANES)
    vm = lambda shape, dt: pltpu.VMEM(shape, dt)
    return pl.pallas_call(
        kern,
        grid=(nseq, nblk),
        in_specs=[row(d), _const_spec(wz.shape), _const_spec(wx.shape), _const_spec(wdt.shape),
                  _const_spec(cw.shape), _const_spec(cb.shape), _const_spec(dtb.shape),
                  _const_spec(alog.shape), _const_spec(dsk.shape), _const_spec(nw.shape),
                  _const_spec(btril.shape), _const_spec(exp_m.shape),
                  _const_spec(perm.shape), _const_spec(perm.shape)],
        out_specs=[row(width),
                   pl.BlockSpec((1, nheads, hd_dim, nst), lambda i, j: (i, 0, 0, 0)),
                   pl.BlockSpec((1, SSD_CONV - 1, xbc_w), lambda i, j: (i, 0, 0))],
        out_shape=[jax.ShapeDtypeStruct((m, width), BF16),
                   jax.ShapeDtypeStruct((nseq, nheads, hd_dim, nst), F32),
                   jax.ShapeDtypeStruct((nseq, SSD_CONV - 1, xbc_w), F32)],
        scratch_shapes=[vm((nch * (t + (SSD_CONV - 1) * SUBLANES), xbc_w), F32),
                        vm((r, width), F32),
                        vm((r, LANES), F32), vm((nch, LANES, t), F32),
                        vm((r, width), F32), vm((r, width), F32),
                        vm((r, width), BF16), vm((r, width), BF16), vm((r, width), F32),
                        vm((r, SSD_GROUPS * nst), BF16), vm((r, SSD_GROUPS * nst), BF16),
                        vm((nch, SUBLANES, width), F32), vm((nst, width), F32),
                        vm((SSD_CONV - 1, xbc_w), F32)],
        compiler_params=_params(("arbitrary", "arbitrary")),
        name="ssd_prompt",
    )(h, wz, wx, wdt, cw, cb, dtb, alog, dsk, nw, btril, exp_m, perm, perm.T)


def _rep_heads(v):
    grp = jnp.pad(v, [(0, 0)] * (v.ndim - 1) + [(0, HEAD_REP - v.shape[-1])])
    tail = jnp.zeros(v.shape[:-1] + (LANES - 3 * HEAD_REP,), v.dtype)
    return jnp.concatenate([grp, grp, grp, tail], axis=-1)


def _s5_call(h, s5re0, s5im0, w, nb, tl, casts=()):
    g, p = s5re0.shape[1:]
    o5, re1, im1, *cast = _s5(h, w["w_u"], w["s5_wb"], w["s5_wcre"], w["s5_wcim"], w["s5_are"],
                              w["s5_aim"], w["s5_d"], w["s5_wglu"], w["s5_bglu"],
                              s5re0.reshape(nb, g * p), s5im0.reshape(nb, g * p), nb, tl, casts)
    return o5, re1.reshape(nb, g, p), im1.reshape(nb, g, p), cast


def _layer(xp, xs, s5re0, s5im0, ssd0, conv0, w, *, tm, s5_tl, ssd_nch, ssd_nb):
    bp, lp, d = xp.shape
    bs, ls, _ = xs.shape
    mp, ms = bp * lp, bs * ls
    g, p = s5re0.shape[1:]
    nheads = ssd0.shape[1]

    xs_tm = xs.transpose(1, 0, 2).reshape(ms, d)
    x1p, hp, x1s, hs, w_u, w_z, w_x, w_dt = _ffn1(
        xp.reshape(mp, d), xs_tm, w["ffn1_norm"], w["ffn1_wg"], w["ffn1_wu"], w["ffn1_wd"],
        w["mix_norm"], w["w_in"], w["w_in_splits"], tm)
    w = dict(w, w_u=w_u, w_z=w_z, w_x=w_x, w_dt=_rep_heads(w_dt))

    zeros = jnp.zeros((bp, g, p), F32)
    o5p, p_re, p_im, (wg2, wu2, wd2, wo) = _s5_call(
        hp.reshape(bp, lp, d), zeros, zeros, w, bp, s5_tl,
        (w["ffn2_wg"], w["ffn2_wu"], w["ffn2_wd"], w["w_out"]))
    osp, p_ssd, p_conv = _ssdp(hp, w["w_z"], w["w_x"], w["w_dt"], w["conv_w"], w["conv_b"],
                               w["dt_bias"], w["a_log"], w["ssd_d"], w["ssd_norm"],
                               bp, nheads, ssd_nch)

    o5s, s_re, s_im, _ = _s5_call(hs, s5re0, s5im0, w, bs, ls)
    oss, s_ssd, s_conv_tm = _ssds(hs, w["w_z"], w["w_x"], w["w_dt"], w["conv_w"], w["conv_b"],
                                  w["dt_bias"], w["a_log"], w["ssd_d"], w["ssd_norm"], ssd0,
                                  conv0.transpose(1, 0, 2), ssd_nb)

    yp, ys = _out_ffn2(x1p, o5p.reshape(mp, -1), osp, x1s, o5s, oss, wo, w["ffn2_norm"],
                       wg2, wu2, wd2, w["final_norm"], tm)
    return (yp.reshape(bp, lp, d), ys.reshape(ls, bs, d).transpose(1, 0, 2),
            p_re, p_im, p_ssd, p_conv, s_re, s_im, s_ssd, s_conv_tm.transpose(1, 0, 2))


def kernel(x_prompt, x_sample, state_s5_re, state_s5_im, state_ssd, state_conv, ffn1_norm, ffn1_w_gate, ffn1_w_up, ffn1_w_down, mix_norm, w_in, s5_lambda_re, s5_lambda_im, s5_log_step, s5_b_re, s5_b_im, s5_c_re, s5_c_im, s5_d, s5_w_glu, s5_b_glu, ssd_conv_w, ssd_conv_b, ssd_dt_bias, ssd_a_log, ssd_d, ssd_norm, w_out, ffn2_norm, ffn2_w_gate, ffn2_w_up, ffn2_w_down, final_norm):
    depth = w_in.shape[0]
    assert depth == 1, "single-layer stack"
    i = 0
    d = x_prompt.shape[-1]
    g, p = s5_lambda_re.shape[1:]
    s5_width = g * S5_GROUP
    nheads = ssd_a_log.shape[1]
    ssd_width = nheads * SSD_HEAD_DIM
    xbc_w = ssd_conv_w.shape[-1]
    c0, c1, c2 = s5_width, s5_width + ssd_width, s5_width + ssd_width + xbc_w
    row = lambda v: v.astype(F32).reshape(1, -1)

    are, aim, bbre, bbim = _s5_params(s5_lambda_re[i], s5_lambda_im[i], s5_log_step[i],
                                      s5_b_re[i], s5_b_im[i])
    nblk = s5_width // LANES
    wb = jnp.concatenate([_block_diag(bbre, nblk), _block_diag(bbim, nblk)],
                         axis=-1).astype(BF16)
    wcre = _block_diag(s5_c_re[i].astype(F32).transpose(0, 2, 1), nblk).astype(BF16)
    wcim = _block_diag(s5_c_im[i].astype(F32).transpose(0, 2, 1), nblk).astype(BF16)

    w = {
        "ffn1_norm": row(ffn1_norm[i]), "mix_norm": row(mix_norm[i]),
        "ffn2_norm": row(ffn2_norm[i]), "final_norm": row(final_norm),
        "ffn1_wg": ffn1_w_gate[i].astype(BF16), "ffn1_wu": ffn1_w_up[i].astype(BF16),
        "ffn1_wd": ffn1_w_down[i].astype(BF16),
        "w_in": w_in.astype(F32), "w_in_splits": (c0, c1 - c0, c2 - c1, nheads),
        "ffn2_wg": ffn2_w_gate.astype(F32), "ffn2_wu": ffn2_w_up.astype(F32),
        "ffn2_wd": ffn2_w_down.astype(F32), "w_out": w_out.astype(F32),
        "s5_wb": wb, "s5_wcre": wcre, "s5_wcim": wcim, "s5_are": are, "s5_aim": aim,
        "s5_d": row(s5_d[i]), "s5_wglu": s5_w_glu[i].astype(BF16), "s5_bglu": row(s5_b_glu[i]),
        "conv_w": ssd_conv_w[i].astype(F32), "conv_b": row(ssd_conv_b[i]),
        "dt_bias": _rep_heads(row(ssd_dt_bias[i])), "a_log": _rep_heads(row(ssd_a_log[i])),
        "ssd_d": jnp.repeat(ssd_d[i].astype(F32), SSD_HEAD_DIM).reshape(1, -1),
        "ssd_norm": row(ssd_norm[i]),
    }

    lp = x_prompt.shape[1]
    yp, ys, *states = _layer(
        x_prompt, x_sample, state_s5_re[i], state_s5_im[i], state_ssd[i], state_conv[i], w,
        tm=512, s5_tl=min(lp, 128), ssd_nch=min(lp // SSD_CHUNK, 4), ssd_nb=SUBLANES)
    return (yp, ys) + tuple(v[None] for v in states)
```

```python
import functools
import math

import jax
import jax.numpy as jnp
from jax import lax
from jax.experimental import pallas as pl
from jax.experimental.pallas import tpu as pltpu

F32 = jnp.float32
BF16 = jnp.bfloat16
EPS = 1e-6

LANES = 128
SUBLANES = 8
VMEM_LIMIT = 56 * 1024 * 1024

S5_GROUP = 16
S5_STATE = 64
SSD_HEAD_DIM = 64
SSD_GROUPS = 4
SSD_STATE = 128
SSD_CONV = 4
SSD_CHUNK = 128


def _sigmoid(x):
    return 0.5 + 0.5 * jnp.tanh(0.5 * x)


def _silu(x):
    hx = 0.5 * x
    return hx + hx * jnp.tanh(hx)


def _gelu_tanh(x):
    c = math.sqrt(2.0 / math.pi)
    return 0.5 * x * (1.0 + jnp.tanh(c * (x + 0.044715 * (x * x * x))))


def _softplus(x):
    return jnp.maximum(x, 0.0) + jnp.log1p(jnp.exp(-jnp.abs(x)))


def _rms(x, w):
    return x * lax.rsqrt(jnp.mean(x * x, axis=-1, keepdims=True) + EPS) * w


def _dot(a, b):
    return jnp.dot(a, b, preferred_element_type=F32)


def _dot_nt(a, b):
    return lax.dot_general(a, b, (((1,), (1,)), ((), ())), preferred_element_type=F32)


def _const_spec(shape):
    nd = len(shape)
    return pl.BlockSpec(shape, lambda *_: (0,) * nd, pipeline_mode=pl.Buffered(1))


def _params(sem):
    return pltpu.CompilerParams(dimension_semantics=sem, vmem_limit_bytes=VMEM_LIMIT)


def _swiglu_half(x, nw, wg_ref, wu_ref, wd_ref):
    hn = _rms(x, nw).astype(BF16)
    g = _dot(hn, wg_ref[...])
    u = _dot(hn, wu_ref[...])
    a = (_silu(g) * u).astype(BF16)
    return x + 0.5 * _dot(a, wd_ref[...])


def _cast_specs(ws, steps, layer=0):
    ins, outs, shapes = [], [], []
    for w in ws:
        _, nrow, ncol = w.shape
        rows = nrow // steps
        assert rows * steps == nrow and rows % (2 * SUBLANES) == 0
        ins.append(pl.BlockSpec((None, rows, ncol),
                                lambda i: (layer, jnp.minimum(i, steps - 1), 0)))
        outs.append(pl.BlockSpec((rows, ncol), lambda i: (jnp.minimum(i, steps - 1), 0)))
        shapes.append(jax.ShapeDtypeStruct((nrow, ncol), BF16))
    return ins, outs, shapes


def _ffn1_kernel(xp_ref, xs_ref, nw_ref, wg_ref, wu_ref, wd_ref, mixw_ref, win_ref,
                 x1p_ref, hp_ref, x1s_ref, hs_ref, *wsplit_refs, n_p):
    is_p = pl.program_id(0) < n_p

    def tile(x_ref, x1_ref, h_ref):
        x1 = _swiglu_half(x_ref[...], nw_ref[...], wg_ref, wu_ref, wd_ref)
        x1_ref[...] = x1
        h_ref[...] = _rms(x1, mixw_ref[...]).astype(BF16)

    pl.when(is_p)(functools.partial(tile, xp_ref, x1p_ref, hp_ref))
    pl.when(jnp.logical_not(is_p))(functools.partial(tile, xs_ref, x1s_ref, hs_ref))

    lo = 0
    for o_ref in wsplit_refs:
        o_ref[...] = win_ref[:, lo:lo + o_ref.shape[1]].astype(BF16)
        lo += o_ref.shape[1]


def _ffn1(xp, xs, nw, wg, wu, wd, mixw, w_in, splits, tm):
    mp, d = xp.shape
    ms = xs.shape[0]
    assert ms == tm
    n_p = mp // tm
    dff = wg.shape[1]
    rowp = lambda w: pl.BlockSpec((tm, w), lambda i: (jnp.minimum(i, n_p - 1), 0))
    rows = lambda w: pl.BlockSpec((ms, w), lambda i: (0, 0))
    layer = 0
    wrows = w_in.shape[1] // n_p
    wblk = lambda w: pl.BlockSpec((wrows, w), lambda i: (jnp.minimum(i, n_p - 1), 0))
    w_in_blk = pl.BlockSpec((None, wrows, w_in.shape[2]),
                            lambda i: (layer, jnp.minimum(i, n_p - 1), 0))
    return pl.pallas_call(
        functools.partial(_ffn1_kernel, n_p=n_p),
        grid=(n_p + 1,),
        in_specs=[rowp(d), rows(d), _const_spec((1, d)), _const_spec((d, dff)),
                  _const_spec((d, dff)), _const_spec((dff, d)), _const_spec((1, d)),
                  w_in_blk],
        out_specs=[rowp(d), rowp(d), rows(d), rows(d)] + [wblk(c) for c in splits],
        out_shape=[jax.ShapeDtypeStruct((mp, d), F32), jax.ShapeDtypeStruct((mp, d), BF16),
                   jax.ShapeDtypeStruct((ms, d), F32), jax.ShapeDtypeStruct((ms, d), BF16)]
                  + [jax.ShapeDtypeStruct((w_in.shape[1], c), BF16) for c in splits],
        compiler_params=_params(("arbitrary",)),
        name="ffn1",
    )(xp, xs, nw, wg, wu, wd, mixw, w_in)


def _out_ffn2_kernel(x1p_ref, o5p_ref, osp_ref, x1s_ref, o5s_ref, oss_ref, wo_ref, nw_ref,
                     wg_ref, wu_ref, wd_ref, fin_ref, yp_ref, ys_ref, *, n_p):
    is_p = pl.program_id(0) < n_p
    k5 = o5p_ref.shape[1]

    def tile(x1_ref, o5_ref, os_ref, y_ref):
        x2 = x1_ref[...] + (_dot(o5_ref[...].astype(BF16), wo_ref[:k5, :])
                            + _dot(os_ref[...].astype(BF16), wo_ref[k5:, :]))
        x3 = _swiglu_half(x2, nw_ref[...], wg_ref, wu_ref, wd_ref)
        y_ref[...] = _rms(x3, fin_ref[...])

    pl.when(is_p)(functools.partial(tile, x1p_ref, o5p_ref, osp_ref, yp_ref))
    pl.when(jnp.logical_not(is_p))(functools.partial(tile, x1s_ref, o5s_ref, oss_ref, ys_ref))


def _out_ffn2(x1p, o5p, osp, x1s, o5s, oss, wo, nw, wg, wu, wd, fin, tm):
    mp, d = x1p.shape
    ms = x1s.shape[0]
    assert ms == tm
    n_p = mp // tm
    dff = wg.shape[1]
    rowp = lambda w: pl.BlockSpec((tm, w), lambda i: (jnp.minimum(i, n_p - 1), 0))
    rows = lambda w: pl.BlockSpec((ms, w), lambda i: (0, 0))
    return pl.pallas_call(
        functools.partial(_out_ffn2_kernel, n_p=n_p),
        grid=(n_p + 1,),
        in_specs=[rowp(d), rowp(o5p.shape[1]), rowp(osp.shape[1]),
                  rows(d), rows(o5s.shape[1]), rows(oss.shape[1]),
                  _const_spec(wo.shape), _const_spec((1, d)),
                  _const_spec((d, dff)), _const_spec((d, dff)), _const_spec((dff, d)),
                  _const_spec((1, d))],
        out_specs=[rowp(d), rows(d)],
        out_shape=[jax.ShapeDtypeStruct((mp, d), F32), jax.ShapeDtypeStruct((ms, d), F32)],
        compiler_params=_params(("arbitrary",)),
        name="out_ffn2",
    )(x1p, o5p, osp, x1s, o5s, oss, wo, nw, wg, wu, wd, fin)


def _s5_param_kernel(lre_ref, lim_ref, step_ref, bre_ref, bim_ref,
                     are_ref, aim_ref, bbre_ref, bbim_ref):
    lre, lim, step = lre_ref[...], lim_ref[...], step_ref[...]
    mag = jnp.exp(lre * step)
    ang = lim * step
    are = mag * jnp.cos(ang)
    aim = mag * jnp.sin(ang)
    den = lre * lre + lim * lim
    nre, nim = are - 1.0, aim
    cre = (nre * lre + nim * lim) / den
    cim = (nim * lre - nre * lim) / den
    are_ref[...] = are
    aim_ref[...] = aim
    bre, bim = bre_ref[...], bim_ref[...]
    bbre_ref[...] = cre * bre - cim * bim
    bbim_ref[...] = cre * bim + cim * bre


def _s5_params(lam_re, lam_im, log_step, b_re, b_im):
    g, p = lam_re.shape
    hh = b_re.shape[-1]
    gp = g * p
    step = jnp.exp(log_step.astype(F32))
    step_row = jnp.broadcast_to(step[:, None], (g, p)).reshape(1, gp)
    to_lanes = lambda b: b.astype(F32).reshape(gp, hh).T
    full = lambda shape: pl.BlockSpec(shape, lambda: (0,) * len(shape))
    are, aim, bbre, bbim = pl.pallas_call(
        _s5_param_kernel,
        in_specs=[full((1, gp)), full((1, gp)), full((1, gp)), full((hh, gp)), full((hh, gp))],
        out_specs=[full((1, gp)), full((1, gp)), full((hh, gp)), full((hh, gp))],
        out_shape=[jax.ShapeDtypeStruct((1, gp), F32), jax.ShapeDtypeStruct((1, gp), F32),
                   jax.ShapeDtypeStruct((hh, gp), F32), jax.ShapeDtypeStruct((hh, gp), F32)],
        name="s5_params",
    )(lam_re.astype(F32).reshape(1, gp), lam_im.astype(F32).reshape(1, gp), step_row,
      to_lanes(b_re), to_lanes(b_im))
    to_blocks = lambda b: b.reshape(hh, g, p).transpose(1, 0, 2)
    return are, aim, to_blocks(bbre), to_blocks(bbim)


def _block_diag(w, nblk):
    g, a, b = w.shape
    gl = g // nblk
    w = w.reshape(nblk, gl, a, b)
    eye = jnp.eye(gl, dtype=w.dtype)
    out = w[:, :, :, None, :] * eye[None, :, None, :, None]
    return out.reshape(nblk, gl * a, gl * b)


def _s5_kernel(*refs, nb, tl, n_cast):
    (h_ref, wu_ref, wb_ref, wcre_ref, wcim_ref, are_ref, aim_ref, d_ref, wglu_ref, bglu_ref,
     s0re_ref, s0im_ref) = refs[:12]
    cast_in = refs[12:12 + n_cast]
    o5_ref, sre_ref, sim_ref = refs[12 + n_cast:15 + n_cast]
    cast_out = refs[15 + n_cast:15 + 2 * n_cast]
    u_s, xre_s, xim_s, y_s, *perm_s = refs[15 + 2 * n_cast:]
    for src, dst in zip(cast_in, cast_out):
        dst[...] = src[...].astype(BF16)
    width = u_s.shape[1]
    nstate = xre_s.shape[1]
    nblk = wb_ref.shape[0]
    cin = width // nblk
    cst = nstate // nblk
    r = nb * tl
    pack = 2 * SUBLANES

    @pl.when(pl.program_id(0) == 0)
    def _():
        sre_ref[...] = s0re_ref[...]
        sim_ref[...] = s0im_ref[...]

    if perm_s:
        slab_s, = perm_s
        assert cin == LANES and nb == SUBLANES
        pitch = tl + SUBLANES
        u_bt = _dot(h_ref[...].reshape(r, h_ref.shape[2]), wu_ref[...])
        for b in range(nb):
            for k in range(nblk):
                slab_s[k, b * pitch:b * pitch + tl, :] = u_bt[b * tl:(b + 1) * tl,
                                                              k * cin:(k + 1) * cin]

        for t in range(tl):
            for k in range(nblk):
                u_s[t * nb:(t + 1) * nb, k * cin:(k + 1) * cin] = (
                    slab_s[k, pl.ds(t, nb, stride=pitch), :])
        u = u_s[...]
    else:
        u = _dot(h_ref[...], wu_ref[...])
        u_s[...] = u
    ub = u.astype(BF16)

    def drive(k):
        bu = _dot(ub[:, k * cin:(k + 1) * cin], wb_ref[k])
        xre_s[:, k * cst:(k + 1) * cst] = bu[:, :cst]
        xim_s[:, k * cst:(k + 1) * cst] = bu[:, cst:]

    def scan(k):
        lanes = slice(k * cst, (k + 1) * cst)
        a_re = jnp.broadcast_to(are_ref[:, lanes], (SUBLANES, cst))
        a_im = jnp.broadcast_to(aim_ref[:, lanes], (SUBLANES, cst))

        def step(rows, s_re, s_im):
            n_re = a_re * s_re - a_im * s_im + xre_s[rows, lanes]
            n_im = a_re * s_im + a_im * s_re + xim_s[rows, lanes]
            xre_s[rows, lanes] = n_re
            xim_s[rows, lanes] = n_im
            return n_re, n_im

        if nb == SUBLANES:
            s_re, s_im = sre_ref[:, lanes], sim_ref[:, lanes]
            for t in range(tl):
                s_re, s_im = step(slice(t * nb, (t + 1) * nb), s_re, s_im)
            sre_ref[:, lanes] = s_re
            sim_ref[:, lanes] = s_im
        else:
            def group_body(sg, carry):
                srow = pl.ds(pl.multiple_of(sg * SUBLANES, SUBLANES), SUBLANES)
                st = (sre_ref[srow, lanes], sim_ref[srow, lanes])
                for t in range(tl):
                    st = step(pl.ds(pl.multiple_of(t * nb + sg * SUBLANES, SUBLANES), SUBLANES), *st)
                sre_ref[srow, lanes], sim_ref[srow, lanes] = st
                return carry

            lax.fori_loop(0, nb // SUBLANES, group_body, 0)

    def readout(k):
        st = slice(k * cst, (k + 1) * cst)
        y_s[:, k * cin:(k + 1) * cin] = (_dot(xre_s[:, st].astype(BF16), wcre_ref[k])
                                         - _dot(xim_s[:, st].astype(BF16), wcim_ref[k]))

    for k in range(nblk + 2):
        if k < nblk:
            drive(k)
        if 1 <= k <= nblk:
            scan(k - 1)
        if k >= 2:
            readout(k - 2)
    y = y_s[...] + d_ref[...] * u_s[...]
    v = _gelu_tanh(y)
    gate = _dot(v.astype(BF16), wglu_ref[...]) + bglu_ref[...]
    o5 = v * _sigmoid(gate)
    if perm_s:
        for k in range(nblk):
            slab_s[k, 0:r, :] = o5[:, k * cin:(k + 1) * cin]
        for b in range(nb):
            for t0 in range(0, tl, pack):
                for k in range(nblk):
                    lo = slab_s[k, pl.ds(t0 * nb + b, SUBLANES, stride=nb), :]
                    hi = slab_s[k, pl.ds((t0 + SUBLANES) * nb + b, SUBLANES, stride=nb), :]
                    o5_ref[b, t0:t0 + pack, k * cin:(k + 1) * cin] = (
                        jnp.concatenate([lo, hi], axis=0).astype(BF16))
    else:
        o5_ref[...] = o5.astype(BF16)


def _s5(h, wu, wb, wcre, wcim, are, aim, dsk, wglu, bglu, s0re, s0im, nb, tl, casts=()):
    width = wu.shape[1]
    nstate = are.shape[1]
    r = nb * tl
    batch_major = h.ndim == 3
    if batch_major:
        steps = h.shape[1] // tl
        blk = lambda w: pl.BlockSpec((nb, tl, w), lambda i: (0, i, 0))
        o_shape = (nb, h.shape[1], width)
        extra = [pltpu.VMEM((width // LANES, nb * (tl + SUBLANES), LANES), F32)]
    else:
        steps = h.shape[0] // r
        blk = lambda w: pl.BlockSpec((r, w), lambda i: (i, 0))
        o_shape = (h.shape[0], width)
        extra = []
    kern = functools.partial(_s5_kernel, nb=nb, tl=tl, n_cast=len(casts))
    c_in, c_out, c_shapes = _cast_specs(casts, steps)
    return pl.pallas_call(
        kern,
        grid=(steps,),
        in_specs=[blk(h.shape[-1]), _const_spec(wu.shape), _const_spec(wb.shape),
                  _const_spec(wcre.shape), _const_spec(wcim.shape), _const_spec(are.shape),
                  _const_spec(aim.shape), _const_spec(dsk.shape), _const_spec(wglu.shape),
                  _const_spec(bglu.shape), _const_spec(s0re.shape), _const_spec(s0im.shape)]
                 + c_in,
        out_specs=[blk(width), pl.BlockSpec((nb, nstate), lambda i: (0, 0)),
                   pl.BlockSpec((nb, nstate), lambda i: (0, 0))] + c_out,
        out_shape=[jax.ShapeDtypeStruct(o_shape, BF16),
                   jax.ShapeDtypeStruct((nb, nstate), F32),
                   jax.ShapeDtypeStruct((nb, nstate), F32)] + c_shapes,
        scratch_shapes=[pltpu.VMEM((r, width), F32), pltpu.VMEM((r, nstate), F32),
                        pltpu.VMEM((r, nstate), F32), pltpu.VMEM((r, width), F32)] + extra,
        compiler_params=_params(("arbitrary",)),
        name="s5",
    )(h, wu, wb, wcre, wcim, are, aim, dsk, wglu, bglu, s0re, s0im, *casts)


def _split2(x):
    hi = x.astype(BF16)
    lo = (x - hi.astype(F32)).astype(BF16)
    return jnp.concatenate([hi, lo], axis=1)


def _split3(x):
    hi = x.astype(BF16)
    r1 = x - hi.astype(F32)
    mid = r1.astype(BF16)
    lo = (r1 - mid.astype(F32)).astype(BF16)
    return jnp.concatenate([hi, mid, lo], axis=1)


HEAD_REP = 32


def _pack2(x):
    grp = lax.broadcasted_iota(jnp.int32, x.shape, 1) // HEAD_REP
    hi = x.astype(BF16).astype(F32)
    return jnp.where(grp == 0, hi, x - hi).astype(BF16)


def _pack3(x):
    grp = lax.broadcasted_iota(jnp.int32, x.shape, 1) // HEAD_REP
    hi = x.astype(BF16).astype(F32)
    r1 = x - hi
    mid = r1.astype(BF16).astype(F32)
    return jnp.where(grp == 0, hi, jnp.where(grp == 1, mid, r1 - mid)).astype(BF16)


def _group_norm(y, zg, nw, gw):
    outs = []
    for g in range(SSD_GROUPS):
        seg = y[:, g * gw:(g + 1) * gw] * zg[:, g * gw:(g + 1) * gw]
        outs.append(seg * lax.rsqrt(jnp.mean(seg * seg, axis=-1, keepdims=True) + EPS))
    return jnp.concatenate(outs, axis=1) * nw


def _ssds_pre_kernel(h_ref, wz_ref, wx_ref, wdt_ref, cw_ref, cb_ref, dtb_ref, alog_ref, dsk_ref,
                     conv0_ref, g2_ref, exp_ref, c_o, b_o, xdd_o, yp_o, eac_o, zg_o, cdec_o, cv_o):
    nbt = conv0_ref.shape[1]
    seq = h_ref.shape[0] // nbt
    width = zg_o.shape[1]
    nbc = c_o.shape[1]
    hb = h_ref[...]
    zg_o[...] = _silu(_dot(hb, wz_ref[...]))
    dt = _softplus(_dot(hb, wdt_ref[...]) + dtb_ref[...])
    la = dt * (-jnp.exp(alog_ref[...]))
    xbc = _dot(hb, wx_ref[...])
    rows = [slice(t * nbt, (t + 1) * nbt) for t in range(seq)]
    full = [conv0_ref[k] for k in range(SSD_CONV - 1)] + [xbc[r] for r in rows]
    for k in range(SSD_CONV - 1):
        cv_o[k] = full[seq + k]
    acums = []
    for t in range(seq):
        acums.append(la[rows[t]] if t == 0 else acums[-1] + la[rows[t]])
    tot = acums[-1]
    cdec_o[...] = jnp.exp(tot)
    xs, bq, cq = [], [], []
    for t in range(seq):
        acc = cb_ref[...]
        for k in range(SSD_CONV):
            acc = acc + cw_ref[k:k + 1, :] * full[t + k]
        xc = _silu(acc)
        xs.append(xc[:, :width])
        bq.append(xc[:, width:width + nbc].astype(BF16).astype(F32))
        cq.append(xc[:, width + nbc:].astype(BF16).astype(F32))
    for t in range(seq):
        r = rows[t]
        b_o[r, :] = bq[t]
        c_o[r, :] = cq[t]
        xdd_o[r, :] = xs[t] * _dot(_split2(dt[r] * jnp.exp(tot - acums[t])), exp_ref[...])
        eac_o[r, :] = _dot(_split2(jnp.exp(acums[t])), exp_ref[...])
        yp = dsk_ref[...] * xs[t]
        for s in range(t + 1):
            cbx = _dot(_split2(cq[t] * bq[s]), g2_ref[...])
            coef = cbx * jnp.exp(acums[t] - acums[s]) * dt[rows[s]]
            yp = yp + _dot(_split2(coef), exp_ref[...]) * xs[s]
        yp_o[r, :] = yp


def _ssds_state_kernel(c_ref, b_ref, xdd_ref, yp_ref, eac_ref, zg_ref, cdec_ref, st_ref, nw_ref,
                       o_ref, sto_ref, yoff_s, xddt_s):
    seq, nb, width = xdd_ref.shape
    nheads, hd_dim, nst = st_ref.shape[1:]
    hpg = nheads // SSD_GROUPS
    gw = width // SSD_GROUPS
    rws = seq * nb
    cst = c_ref[...].reshape(rws, c_ref.shape[2]).astype(BF16)
    bst = b_ref[...].reshape(rws, b_ref.shape[2])
    xdd = xdd_ref[...].reshape(rws, width)
    xddt_s[...] = jnp.concatenate([xdd, jnp.zeros((LANES - rws, width), F32)], axis=0).T.astype(BF16)
    rowb = lax.broadcasted_iota(jnp.int32, (rws, gw), 0) % nb
    zpad = jnp.zeros((LANES - rws, nst), BF16)
    for b in range(nb):
        mine = rowb == b
        for g in range(SSD_GROUPS):
            gl = slice(g * gw, (g + 1) * gw)
            hg = st_ref[b, g * hpg:(g + 1) * hpg].reshape(gw, nst)
            res = _dot_nt(cst[:, g * nst:(g + 1) * nst], hg.astype(BF16))
            yoff_s[:, gl] = jnp.where(mine, res, 0.0) if b == 0 else jnp.where(mine, res, yoff_s[:, gl])
            bm = jnp.where(mine[:, :nst], bst[:, g * nst:(g + 1) * nst], 0.0).astype(BF16)
            upd = _dot(xddt_s[gl, :], jnp.concatenate([bm, zpad], axis=0))
            for jh in range(hpg):
                hd = g * hpg + jh
                hrows = slice(jh * hd_dim, (jh + 1) * hd_dim)
                sto_ref[b, hd] = hg[hrows] * cdec_ref[b, hd] + upd[hrows]
    y = yp_ref[...].reshape(rws, width) + yoff_s[...] * eac_ref[...].reshape(rws, width)
    o = _group_norm(y, zg_ref[...].reshape(rws, width), nw_ref[...], gw)
    o_ref[...] = o.reshape(seq, nb, width)


def _ssds(h_tm, wz, wx, wdt, cw, cb, dtb, alog, dsk, nw, ssd0, conv0_tm, nb):
    m, d = h_tm.shape
    nseq, nheads, hd_dim, nst = ssd0.shape
    seq = m // nseq
    width = wz.shape[1]
    xbc_w = wx.shape[1]
    nbc = SSD_GROUPS * nst
    hpg = nheads // SSD_GROUPS
    head = jnp.arange(LANES)
    exp1 = (head[:, None] == (jnp.arange(width) // hd_dim)[None, :]).astype(BF16)
    g1 = ((jnp.arange(nbc) // nst)[:, None] == (head // hpg)[None, :]) & (head < nheads)[None, :]
    exp_m = jnp.concatenate([exp1, exp1], axis=0)
    g2 = jnp.concatenate([g1, g1], axis=0).astype(BF16)
    full = lambda shape: pl.BlockSpec(shape, lambda: (0,) * len(shape))
    ins = (h_tm, wz, wx, wdt, cw, cb, dtb, alog, dsk, conv0_tm, g2, exp_m)
    outs = [jax.ShapeDtypeStruct((m, nbc), F32), jax.ShapeDtypeStruct((m, nbc), F32),
            jax.ShapeDtypeStruct((m, width), F32), jax.ShapeDtypeStruct((m, width), F32),
            jax.ShapeDtypeStruct((m, width), F32), jax.ShapeDtypeStruct((m, width), F32),
            jax.ShapeDtypeStruct((nseq, LANES), F32),
            jax.ShapeDtypeStruct((SSD_CONV - 1, nseq, xbc_w), F32)]
    c, b, xdd, yp, eac, zg, cdec, conv1_tm = pl.pallas_call(
        _ssds_pre_kernel,
        in_specs=[full(a.shape) for a in ins],
        out_specs=[full(o.shape) for o in outs],
        out_shape=outs,
        compiler_params=pltpu.CompilerParams(vmem_limit_bytes=VMEM_LIMIT),
        name="ssd_sample_pre",
    )(*ins)

    tm3 = lambda a: a.reshape(seq, nseq, a.shape[1])
    blk = lambda w: pl.BlockSpec((seq, nb, w), lambda i: (0, i, 0))
    st_spec = pl.BlockSpec((nb, nheads, hd_dim, nst), lambda i: (i, 0, 0, 0))
    o, ssd1 = pl.pallas_call(
        _ssds_state_kernel,
        grid=(nseq // nb,),
        in_specs=[blk(nbc), blk(nbc), blk(width), blk(width), blk(width), blk(width),
                  pl.BlockSpec((nb, LANES), lambda i: (i, 0), memory_space=pltpu.SMEM),
                  st_spec, _const_spec(nw.shape)],
        out_specs=[blk(width), st_spec],
        out_shape=[jax.ShapeDtypeStruct((seq, nseq, width), F32),
                   jax.ShapeDtypeStruct(ssd0.shape, F32)],
        scratch_shapes=[pltpu.VMEM((seq * nb, width), F32), pltpu.VMEM((width, LANES), BF16)],
        compiler_params=_params(("arbitrary",)),
        name="ssd_sample_state",
    )(tm3(c), tm3(b), tm3(xdd), tm3(yp), tm3(eac), tm3(zg), cdec, ssd0, nw)
    return o.reshape(m, width), ssd1, conv1_tm


def _ssdp_kernel(h_ref, wz_ref, wx_ref, wdt_ref, cw_ref, cb_ref, dtb_ref, alog_ref, dsk_ref,
                 nw_ref, btril_ref, exp_ref, perm_ref, permt_ref, o_ref, st_ref, cv_ref,
                 xbc_s, zg_s, acum_s, acumt_s, e2_s, e3_s, xd_s, xdd_s, ysk_s, b_s, c_s,
                 cdec_s, ht_s, tail_s, m_s, *, nch, col_chunk):
    t = SSD_CHUNK
    r = nch * t
    j = pl.program_id(1)
    nheads, hd_dim, nst = st_ref.shape[1:]
    width = nheads * hd_dim
    gw = width // SSD_GROUPS
    hpg = nheads // SSD_GROUPS
    xbc_w = xbc_s.shape[1]
    ntap = SSD_CONV - 1
    pre = ntap * SUBLANES
    ext = pre + t
    steps = t // SUBLANES
    last_rows = [t - 1 - SUBLANES * (ntap - 1 - v) for v in range(ntap)]

    @pl.when(j == 0)
    def _():
        ht_s[...] = jnp.zeros(ht_s.shape, F32)
        tail_s[...] = jnp.zeros(tail_s.shape, F32)

    hb = jnp.concatenate([_dot(perm_ref[...], h_ref[c * t:(c + 1) * t, :]) for c in range(nch)],
                         axis=0).astype(BF16)
    dt = _softplus(_dot(hb, wdt_ref[...]) + dtb_ref[...])
    la = dt * (-jnp.exp(alog_ref[...]))

    def xbc_cols(jc):
        lanes = slice(jc * col_chunk, (jc + 1) * col_chunk)
        xall = _dot(hb, wx_ref[:, lanes])
        for c in range(nch):
            xbc_s[c * ext + pre:(c + 1) * ext, lanes] = xall[c * t:(c + 1) * t, :]
        for c in range(nch):
            base = c * ext
            xbc_s[base:base + pre, lanes] = xbc_s[base + t - 1:base + pre + t - 1, lanes]
            for v in range(ntap):
                if c == 0:
                    prev = tail_s[v:v + 1, lanes]
                else:
                    prow = base - ext + pre + last_rows[v]
                    prev = xbc_s[prow:prow + 1, lanes]
                xbc_s[base + v * SUBLANES:base + v * SUBLANES + 1, lanes] = prev
            acc = cb_ref[:, lanes]
            for k in range(SSD_CONV):
                acc = acc + cw_ref[k:k + 1, lanes] * xbc_s[base + k * SUBLANES:
                                                           base + k * SUBLANES + t, lanes]
            xc = _silu(acc)
            rows = slice(c * t, (c + 1) * t)
            lo = jc * col_chunk
            if lo < width:
                xd_s[rows, lanes] = xc.astype(BF16)
                xdd_s[rows, lanes] = (xc * e2_s[rows, lanes]).astype(BF16)
                ysk_s[rows, lanes] = xc * dsk_ref[:, lanes]
            elif lo < width + SSD_GROUPS * nst:
                b_s[rows, lo - width:lo - width + col_chunk] = xc.astype(BF16)
            else:
                cl = lo - width - SSD_GROUPS * nst
                c_s[rows, cl:cl + col_chunk] = xc.astype(BF16)

    x_cols = width // col_chunk
    for jc in range(x_cols, xbc_w // col_chunk):
        xbc_cols(jc)

    r3 = _dot(btril_ref[...], _split3(la))
    acum = r3[:, :LANES] + r3[:, LANES:2 * LANES] + r3[:, 2 * LANES:]
    tots = [acum[(c + 1) * t - 1:(c + 1) * t, :] for c in range(nch)]
    arow = acum - jnp.log(dt)
    for c in range(nch):
        acumt_s[c] = arow[c * t:(c + 1) * t, :].T
    tot_rows = jnp.concatenate([jnp.broadcast_to(v, (t, LANES)) for v in tots], axis=0)
    dstate = jnp.exp(tot_rows - acum)
    cdec = jnp.exp(jnp.concatenate([jnp.broadcast_to(v, (SUBLANES, LANES)) for v in tots], axis=0))
    e2_s[...] = _dot(_pack2(dt * dstate), exp_ref[...])
    e3_s[...] = _dot(_pack2(jnp.exp(acum)), exp_ref[...])
    cdx = _dot(_pack2(cdec), exp_ref[...])
    for c in range(nch):
        cdec_s[c] = cdx[c * SUBLANES:(c + 1) * SUBLANES, :]
    acum_s[...] = acum

    def token(i):
        return (i % SUBLANES) * steps + i // SUBLANES

    causal = (token(lax.broadcasted_iota(jnp.int32, (t, t), 0))
              >= token(lax.broadcasted_iota(jnp.int32, (t, t), 1)))
    for c in range(nch):
        rows = slice(c * t, (c + 1) * t)
        for g in range(SSD_GROUPS):
            cbm = jnp.where(causal, _dot_nt(c_s[rows, g * nst:(g + 1) * nst],
                                            b_s[rows, g * nst:(g + 1) * nst]), 0.0)
            for hd in range(g * hpg, (g + 1) * hpg):
                diff = acum_s[rows, hd:hd + 1] - acumt_s[c, hd:hd + 1, :]
                m_s[(c * nheads + hd) * t:(c * nheads + hd + 1) * t, :] = (
                    cbm * jnp.exp(jnp.where(causal, diff, -jnp.inf))).astype(BF16)

    for jc in range(x_cols):
        lanes = slice(jc * col_chunk, (jc + 1) * col_chunk)
        xbc_cols(jc)
        zg_s[:, lanes] = _silu(_dot(hb, wz_ref[:, lanes]))

    for v in range(ntap):
        row = (nch - 1) * ext + pre + last_rows[v]
        tail_s[v:v + 1, :] = xbc_s[row:row + 1, :]
    cv_ref[0] = tail_s[...]

    lane = lax.broadcasted_iota(jnp.int32, (t, LANES), 1)
    keep_lo = jnp.where(lane < hd_dim, 1.0, 0.0).astype(BF16)
    keep_hi = jnp.where(lane < hd_dim, 0.0, 1.0).astype(BF16)
    nw = nw_ref[...]

    for c in range(nch):
        rows = slice(c * t, (c + 1) * t)
        ys = []
        for g in range(SSD_GROUPS):
            gl = slice(g * gw, (g + 1) * gw)
            cg = c_s[rows, g * nst:(g + 1) * nst]
            bg = b_s[rows, g * nst:(g + 1) * nst]
            pairs = []
            for jp in range(hpg // 2):
                ha = g * hpg + 2 * jp
                m2 = jnp.concatenate([m_s[(c * nheads + hd) * t:(c * nheads + hd + 1) * t, :]
                                      for hd in (ha, ha + 1)], axis=1)
                xdp = xd_s[rows, ha * hd_dim:ha * hd_dim + LANES]
                rhs = jnp.concatenate([xdp * keep_lo, xdp * keep_hi], axis=0)
                pairs.append(_dot(m2, rhs))
            htg = ht_s[:, gl]
            y = (jnp.concatenate(pairs, axis=1) + _dot(cg, htg.astype(BF16)) * e3_s[rows, gl]
                 + ysk_s[rows, gl])
            ys.append(y)
            upd = lax.dot_general(bg, xdd_s[rows, gl], (((0,), (0,)), ((), ())),
                                  preferred_element_type=F32)
            ht_s[:, gl] = htg * cdec_s[c, 0:1, gl] + upd
        o = _group_norm(jnp.concatenate(ys, axis=1), zg_s[rows, :], nw, gw).astype(BF16)
        o_ref[rows, :] = _dot(permt_ref[...], o).astype(BF16)

    @pl.when(j == pl.num_programs(1) - 1)
    def _():
        for pr in range(nheads // 2):
            tt = ht_s[:, pr * LANES:(pr + 1) * LANES].T
            st_ref[0, 2 * pr] = tt[:hd_dim]
            st_ref[0, 2 * pr + 1] = tt[hd_dim:]


def _ssdp(h, wz, wx, wdt, cw, cb, dtb, alog, dsk, nw, nseq, nheads, nch):
    m, d = h.shape
    width = wz.shape[1]
    xbc_w = wx.shape[1]
    hd_dim = width // nheads
    nst = SSD_STATE
    t = SSD_CHUNK
    r = nch * t
    nblk = m // (nseq * r)
    ri = jnp.arange(r)
    steps = t // SUBLANES
    tok = (ri % t % SUBLANES) * steps + ri % t // SUBLANES
    btril = ((tok[:, None] >= tok[None, :]) & (ri[:, None] // t == ri[None, :] // t)).astype(BF16)
    perm = (tok[:t, None] == jnp.arange(t)[None, :]).astype(BF16)
    lane = jnp.arange(LANES)
    head = (lane % HEAD_REP)[:, None]
    exp_m = ((head == (jnp.arange(width) // hd_dim)[None, :])
             & (lane < 2 * HEAD_REP)[:, None]).astype(BF16)
    row = lambda w: pl.BlockSpec((r, w), lambda i, j: (i * nblk + j, 0))
    kern = functools.partial(_ssdp_kernel, nch=nch, col_chunk=4 * LANES)
    vm = lambda shape, dt: pltpu.VMEM(shape, dt)
    return pl.pallas_call(
        kern,
        grid=(nseq, nblk),
        in_specs=[row(d), _const_spec(wz.shape), _const_spec(wx.shape), _const_spec(wdt.shape),
                  _const_spec(cw.shape), _const_spec(cb.shape), _const_spec(dtb.shape),
                  _const_spec(alog.shape), _const_spec(dsk.shape), _const_spec(nw.shape),
                  _const_spec(btril.shape), _const_spec(exp_m.shape),
                  _const_spec(perm.shape), _const_spec(perm.shape)],
        out_specs=[row(width),
                   pl.BlockSpec((1, nheads, hd_dim, nst), lambda i, j: (i, 0, 0, 0)),
                   pl.BlockSpec((1, SSD_CONV - 1, xbc_w), lambda i, j: (i, 0, 0))],
        out_shape=[jax.ShapeDtypeStruct((m, width), BF16),
                   jax.ShapeDtypeStruct((nseq, nheads, hd_dim, nst), F32),
                   jax.ShapeDtypeStruct((nseq, SSD_CONV - 1, xbc_w), F32)],
        scratch_shapes=[vm((nch * (t + (SSD_CONV - 1) * SUBLANES), xbc_w), F32),
                        vm((r, width), F32),
                        vm((r, LANES), F32), vm((nch, LANES, t), F32),
                        vm((r, width), F32), vm((r, width), F32),
                        vm((r, width), BF16), vm((r, width), BF16), vm((r, width), F32),
                        vm((r, SSD_GROUPS * nst), BF16), vm((r, SSD_GROUPS * nst), BF16),
                        vm((nch, SUBLANES, width), F32), vm((nst, width), F32),
                        vm((SSD_CONV - 1, xbc_w), F32), vm((nch * nheads * t, t), BF16)],
        compiler_params=_params(("arbitrary", "arbitrary")),
        name="ssd_prompt",
    )(h, wz, wx, wdt, cw, cb, dtb, alog, dsk, nw, btril, exp_m, perm, perm.T)


def _rep_heads(v):
    grp = jnp.pad(v, [(0, 0)] * (v.ndim - 1) + [(0, HEAD_REP - v.shape[-1])])
    tail = jnp.zeros(v.shape[:-1] + (LANES - 3 * HEAD_REP,), v.dtype)
    return jnp.concatenate([grp, grp, grp, tail], axis=-1)


def _s5_call(h, s5re0, s5im0, w, nb, tl, casts=()):
    g, p = s5re0.shape[1:]
    o5, re1, im1, *cast = _s5(h, w["w_u"], w["s5_wb"], w["s5_wcre"], w["s5_wcim"], w["s5_are"],
                              w["s5_aim"], w["s5_d"], w["s5_wglu"], w["s5_bglu"],
                              s5re0.reshape(nb, g * p), s5im0.reshape(nb, g * p), nb, tl, casts)
    return o5, re1.reshape(nb, g, p), im1.reshape(nb, g, p), cast


def _layer(xp, xs, s5re0, s5im0, ssd0, conv0, w, *, tm, s5_tl, ssd_nch, ssd_nb):
    bp, lp, d = xp.shape
    bs, ls, _ = xs.shape
    mp, ms = bp * lp, bs * ls
    g, p = s5re0.shape[1:]
    nheads = ssd0.shape[1]

    xs_tm = xs.transpose(1, 0, 2).reshape(ms, d)
    x1p, hp, x1s, hs, w_u, w_z, w_x, w_dt = _ffn1(
        xp.reshape(mp, d), xs_tm, w["ffn1_norm"], w["ffn1_wg"], w["ffn1_wu"], w["ffn1_wd"],
        w["mix_norm"], w["w_in"], w["w_in_splits"], tm)
    w = dict(w, w_u=w_u, w_z=w_z, w_x=w_x, w_dt=_rep_heads(w_dt))

    zeros = jnp.zeros((bp, g, p), F32)
    o5p, p_re, p_im, (wg2, wu2, wd2, wo) = _s5_call(
        hp.reshape(bp, lp, d), zeros, zeros, w, bp, s5_tl,
        (w["ffn2_wg"], w["ffn2_wu"], w["ffn2_wd"], w["w_out"]))
    osp, p_ssd, p_conv = _ssdp(hp, w["w_z"], w["w_x"], w["w_dt"], w["conv_w"], w["conv_b"],
                               w["dt_bias"], w["a_log"], w["ssd_d"], w["ssd_norm"],
                               bp, nheads, ssd_nch)

    o5s, s_re, s_im, _ = _s5_call(hs, s5re0, s5im0, w, bs, ls)
    oss, s_ssd, s_conv_tm = _ssds(hs, w["w_z"], w["w_x"], w["w_dt"], w["conv_w"], w["conv_b"],
                                  w["dt_bias"], w["a_log"], w["ssd_d"], w["ssd_norm"], ssd0,
                                  conv0.transpose(1, 0, 2), ssd_nb)

    yp, ys = _out_ffn2(x1p, o5p.reshape(mp, -1), osp, x1s, o5s, oss, wo, w["ffn2_norm"],
                       wg2, wu2, wd2, w["final_norm"], tm)
    return (yp.reshape(bp, lp, d), ys.reshape(ls, bs, d).transpose(1, 0, 2),
            p_re, p_im, p_ssd, p_conv, s_re, s_im, s_ssd, s_conv_tm.transpose(1, 0, 2))


def kernel(x_prompt, x_sample, state_s5_re, state_s5_im, state_ssd, state_conv, ffn1_norm, ffn1_w_gate, ffn1_w_up, ffn1_w_down, mix_norm, w_in, s5_lambda_re, s5_lambda_im, s5_log_step, s5_b_re, s5_b_im, s5_c_re, s5_c_im, s5_d, s5_w_glu, s5_b_glu, ssd_conv_w, ssd_conv_b, ssd_dt_bias, ssd_a_log, ssd_d, ssd_norm, w_out, ffn2_norm, ffn2_w_gate, ffn2_w_up, ffn2_w_down, final_norm):
    depth = w_in.shape[0]
    assert depth == 1, "single-layer stack"
    i = 0
    d = x_prompt.shape[-1]
    g, p = s5_lambda_re.shape[1:]
    s5_width = g * S5_GROUP
    nheads = ssd_a_log.shape[1]
    ssd_width = nheads * SSD_HEAD_DIM
    xbc_w = ssd_conv_w.shape[-1]
    c0, c1, c2 = s5_width, s5_width + ssd_width, s5_width + ssd_width + xbc_w
    row = lambda v: v.astype(F32).reshape(1, -1)

    are, aim, bbre, bbim = _s5_params(s5_lambda_re[i], s5_lambda_im[i], s5_log_step[i],
                                      s5_b_re[i], s5_b_im[i])
    nblk = s5_width // LANES
    wb = jnp.concatenate([_block_diag(bbre, nblk), _block_diag(bbim, nblk)],
                         axis=-1).astype(BF16)
    wcre = _block_diag(s5_c_re[i].astype(F32).transpose(0, 2, 1), nblk).astype(BF16)
    wcim = _block_diag(s5_c_im[i].astype(F32).transpose(0, 2, 1), nblk).astype(BF16)

    w = {
        "ffn1_norm": row(ffn1_norm[i]), "mix_norm": row(mix_norm[i]),
        "ffn2_norm": row(ffn2_norm[i]), "final_norm": row(final_norm),
        "ffn1_wg": ffn1_w_gate[i].astype(BF16), "ffn1_wu": ffn1_w_up[i].astype(BF16),
        "ffn1_wd": ffn1_w_down[i].astype(BF16),
        "w_in": w_in.astype(F32), "w_in_splits": (c0, c1 - c0, c2 - c1, nheads),
        "ffn2_wg": ffn2_w_gate.astype(F32), "ffn2_wu": ffn2_w_up.astype(F32),
        "ffn2_wd": ffn2_w_down.astype(F32), "w_out": w_out.astype(F32),
        "s5_wb": wb, "s5_wcre": wcre, "s5_wcim": wcim, "s5_are": are, "s5_aim": aim,
        "s5_d": row(s5_d[i]), "s5_wglu": s5_w_glu[i].astype(BF16), "s5_bglu": row(s5_b_glu[i]),
        "conv_w": ssd_conv_w[i].astype(F32), "conv_b": row(ssd_conv_b[i]),
        "dt_bias": _rep_heads(row(ssd_dt_bias[i])), "a_log": _rep_heads(row(ssd_a_log[i])),
        "ssd_d": jnp.repeat(ssd_d[i].astype(F32), SSD_HEAD_DIM).reshape(1, -1),
        "ssd_norm": row(ssd_norm[i]),
    }

    lp = x_prompt.shape[1]
    yp, ys, *states = _layer(
        x_prompt, x_sample, state_s5_re[i], state_s5_im[i], state_ssd[i], state_conv[i], w,
        tm=512, s5_tl=min(lp, 128), ssd_nch=min(lp // SSD_CHUNK, 4), ssd_nb=SUBLANES)
    return (yp, ys) + tuple(v[None] for v in states)
```

```python
import functools
import math

import jax
import jax.numpy as jnp
from jax import lax
from jax.experimental import pallas as pl
from jax.experimental.pallas import tpu as pltpu

F32 = jnp.float32
BF16 = jnp.bfloat16
EPS = 1e-6

LANES = 128
SUBLANES = 8
VMEM_LIMIT = 56 * 1024 * 1024

S5_GROUP = 16
S5_STATE = 64
SSD_HEAD_DIM = 64
SSD_GROUPS = 4
SSD_STATE = 128
SSD_CONV = 4
SSD_CHUNK = 128


def _sigmoid(x):
    return 0.5 + 0.5 * jnp.tanh(0.5 * x)


def _silu(x):
    hx = 0.5 * x
    return hx + hx * jnp.tanh(hx)


def _gelu_tanh(x):
    c = math.sqrt(2.0 / math.pi)
    return 0.5 * x * (1.0 + jnp.tanh(c * (x + 0.044715 * (x * x * x))))


def _softplus(x):
    return jnp.maximum(x, 0.0) + jnp.log1p(jnp.exp(-jnp.abs(x)))


def _rms(x, w):
    return x * lax.rsqrt(jnp.mean(x * x, axis=-1, keepdims=True) + EPS) * w


def _dot(a, b):
    return jnp.dot(a, b, preferred_element_type=F32)


def _dot_nt(a, b):
    return lax.dot_general(a, b, (((1,), (1,)), ((), ())), preferred_element_type=F32)


def _const_spec(shape):
    nd = len(shape)
    return pl.BlockSpec(shape, lambda *_: (0,) * nd, pipeline_mode=pl.Buffered(1))


def _params(sem):
    return pltpu.CompilerParams(dimension_semantics=sem, vmem_limit_bytes=VMEM_LIMIT)


def _swiglu_half(x, nw, wg_ref, wu_ref, wd_ref):
    hn = _rms(x, nw).astype(BF16)
    g = _dot(hn, wg_ref[...])
    u = _dot(hn, wu_ref[...])
    a = (_silu(g) * u).astype(BF16)
    return x + 0.5 * _dot(a, wd_ref[...])


def _cast_specs(ws, steps, layer=0):
    ins, outs, shapes = [], [], []
    for w in ws:
        _, nrow, ncol = w.shape
        rows = nrow // steps
        assert rows * steps == nrow and rows % (2 * SUBLANES) == 0
        ins.append(pl.BlockSpec((None, rows, ncol),
                                lambda i: (layer, jnp.minimum(i, steps - 1), 0)))
        outs.append(pl.BlockSpec((rows, ncol), lambda i: (jnp.minimum(i, steps - 1), 0)))
        shapes.append(jax.ShapeDtypeStruct((nrow, ncol), BF16))
    return ins, outs, shapes


def _ffn1_kernel(xp_ref, xs_ref, nw_ref, wg_ref, wu_ref, wd_ref, mixw_ref, win_ref,
                 x1p_ref, hp_ref, x1s_ref, hs_ref, *wsplit_refs, n_p):
    is_p = pl.program_id(0) < n_p

    def tile(x_ref, x1_ref, h_ref):
        half = x_ref.shape[0] // 2
        for rows in (slice(0, half), slice(half, 2 * half)):
            x1 = _swiglu_half(x_ref[rows, :], nw_ref[...], wg_ref, wu_ref, wd_ref)
            x1_ref[rows, :] = x1
            h_ref[rows, :] = _rms(x1, mixw_ref[...]).astype(BF16)

    pl.when(is_p)(functools.partial(tile, xp_ref, x1p_ref, hp_ref))
    pl.when(jnp.logical_not(is_p))(functools.partial(tile, xs_ref, x1s_ref, hs_ref))

    lo = 0
    for o_ref in wsplit_refs:
        o_ref[...] = win_ref[:, lo:lo + o_ref.shape[1]].astype(BF16)
        lo += o_ref.shape[1]


def _ffn1(xp, xs, nw, wg, wu, wd, mixw, w_in, splits, tm):
    mp, d = xp.shape
    ms = xs.shape[0]
    assert ms == tm
    n_p = mp // tm
    dff = wg.shape[1]
    rowp = lambda w: pl.BlockSpec((tm, w), lambda i: (jnp.minimum(i, n_p - 1), 0))
    rows = lambda w: pl.BlockSpec((ms, w), lambda i: (0, 0))
    layer = 0
    wrows = w_in.shape[1] // n_p
    wblk = lambda w: pl.BlockSpec((wrows, w), lambda i: (jnp.minimum(i, n_p - 1), 0))
    w_in_blk = pl.BlockSpec((None, wrows, w_in.shape[2]),
                            lambda i: (layer, jnp.minimum(i, n_p - 1), 0))
    return pl.pallas_call(
        functools.partial(_ffn1_kernel, n_p=n_p),
        grid=(n_p + 1,),
        in_specs=[rowp(d), rows(d), _const_spec((1, d)), _const_spec((d, dff)),
                  _const_spec((d, dff)), _const_spec((dff, d)), _const_spec((1, d)),
                  w_in_blk],
        out_specs=[rowp(d), rowp(d), rows(d), rows(d)] + [wblk(c) for c in splits],
        out_shape=[jax.ShapeDtypeStruct((mp, d), F32), jax.ShapeDtypeStruct((mp, d), BF16),
                   jax.ShapeDtypeStruct((ms, d), F32), jax.ShapeDtypeStruct((ms, d), BF16)]
                  + [jax.ShapeDtypeStruct((w_in.shape[1], c), BF16) for c in splits],
        compiler_params=_params(("arbitrary",)),
        name="ffn1",
    )(xp, xs, nw, wg, wu, wd, mixw, w_in)


def _out_ffn2_kernel(x1p_ref, o5p_ref, osp_ref, x1s_ref, o5s_ref, oss_ref, wo_ref, nw_ref,
                     wg_ref, wu_ref, wd_ref, fin_ref, yp_ref, ys_ref, *, n_p):
    is_p = pl.program_id(0) < n_p
    k5 = o5p_ref.shape[1]

    def tile(x1_ref, o5_ref, os_ref, y_ref):
        half = x1_ref.shape[0] // 2
        for rows in (slice(0, half), slice(half, 2 * half)):
            x2 = x1_ref[rows, :] + (_dot(o5_ref[rows, :].astype(BF16), wo_ref[:k5, :])
                                    + _dot(os_ref[rows, :].astype(BF16), wo_ref[k5:, :]))
            x3 = _swiglu_half(x2, nw_ref[...], wg_ref, wu_ref, wd_ref)
            y_ref[rows, :] = _rms(x3, fin_ref[...])

    pl.when(is_p)(functools.partial(tile, x1p_ref, o5p_ref, osp_ref, yp_ref))
    pl.when(jnp.logical_not(is_p))(functools.partial(tile, x1s_ref, o5s_ref, oss_ref, ys_ref))


def _out_ffn2(x1p, o5p, osp, x1s, o5s, oss, wo, nw, wg, wu, wd, fin, tm):
    mp, d = x1p.shape
    ms = x1s.shape[0]
    assert ms == tm
    n_p = mp // tm
    dff = wg.shape[1]
    rowp = lambda w: pl.BlockSpec((tm, w), lambda i: (jnp.minimum(i, n_p - 1), 0))
    rows = lambda w: pl.BlockSpec((ms, w), lambda i: (0, 0))
    return pl.pallas_call(
        functools.partial(_out_ffn2_kernel, n_p=n_p),
        grid=(n_p + 1,),
        in_specs=[rowp(d), rowp(o5p.shape[1]), rowp(osp.shape[1]),
                  rows(d), rows(o5s.shape[1]), rows(oss.shape[1]),
                  _const_spec(wo.shape), _const_spec((1, d)),
                  _const_spec((d, dff)), _const_spec((d, dff)), _const_spec((dff, d)),
                  _const_spec((1, d))],
        out_specs=[rowp(d), rows(d)],
        out_shape=[jax.ShapeDtypeStruct((mp, d), F32), jax.ShapeDtypeStruct((ms, d), F32)],
        compiler_params=_params(("arbitrary",)),
        name="out_ffn2",
    )(x1p, o5p, osp, x1s, o5s, oss, wo, nw, wg, wu, wd, fin)


def _s5_param_kernel(lre_ref, lim_ref, step_ref, bre_ref, bim_ref,
                     are_ref, aim_ref, bbre_ref, bbim_ref):
    lre, lim, step = lre_ref[...], lim_ref[...], step_ref[...]
    mag = jnp.exp(lre * step)
    ang = lim * step
    are = mag * jnp.cos(ang)
    aim = mag * jnp.sin(ang)
    den = lre * lre + lim * lim
    nre, nim = are - 1.0, aim
    cre = (nre * lre + nim * lim) / den
    cim = (nim * lre - nre * lim) / den
    are_ref[...] = are
    aim_ref[...] = aim
    bre, bim = bre_ref[...], bim_ref[...]
    bbre_ref[...] = cre * bre - cim * bim
    bbim_ref[...] = cre * bim + cim * bre


def _s5_params(lam_re, lam_im, log_step, b_re, b_im):
    g, p = lam_re.shape
    hh = b_re.shape[-1]
    gp = g * p
    step = jnp.exp(log_step.astype(F32))
    step_row = jnp.broadcast_to(step[:, None], (g, p)).reshape(1, gp)
    to_lanes = lambda b: b.astype(F32).reshape(gp, hh).T
    full = lambda shape: pl.BlockSpec(shape, lambda: (0,) * len(shape))
    are, aim, bbre, bbim = pl.pallas_call(
        _s5_param_kernel,
        in_specs=[full((1, gp)), full((1, gp)), full((1, gp)), full((hh, gp)), full((hh, gp))],
        out_specs=[full((1, gp)), full((1, gp)), full((hh, gp)), full((hh, gp))],
        out_shape=[jax.ShapeDtypeStruct((1, gp), F32), jax.ShapeDtypeStruct((1, gp), F32),
                   jax.ShapeDtypeStruct((hh, gp), F32), jax.ShapeDtypeStruct((hh, gp), F32)],
        name="s5_params",
    )(lam_re.astype(F32).reshape(1, gp), lam_im.astype(F32).reshape(1, gp), step_row,
      to_lanes(b_re), to_lanes(b_im))
    to_blocks = lambda b: b.reshape(hh, g, p).transpose(1, 0, 2)
    return are, aim, to_blocks(bbre), to_blocks(bbim)


def _block_diag(w, nblk):
    g, a, b = w.shape
    gl = g // nblk
    w = w.reshape(nblk, gl, a, b)
    eye = jnp.eye(gl, dtype=w.dtype)
    out = w[:, :, :, None, :] * eye[None, :, None, :, None]
    return out.reshape(nblk, gl * a, gl * b)


def _s5_kernel(*refs, nb, tl, n_cast):
    (h_ref, wu_ref, wb_ref, wcre_ref, wcim_ref, are_ref, aim_ref, d_ref, wglu_ref, bglu_ref,
     s0re_ref, s0im_ref) = refs[:12]
    cast_in = refs[12:12 + n_cast]
    o5_ref, sre_ref, sim_ref = refs[12 + n_cast:15 + n_cast]
    cast_out = refs[15 + n_cast:15 + 2 * n_cast]
    u_s, xre_s, xim_s, y_s, *perm_s = refs[15 + 2 * n_cast:]
    for src, dst in zip(cast_in, cast_out):
        dst[...] = src[...].astype(BF16)
    width = u_s.shape[1]
    nstate = xre_s.shape[1]
    nblk = wb_ref.shape[0]
    cin = width // nblk
    cst = nstate // nblk
    r = nb * tl
    pack = 2 * SUBLANES

    @pl.when(pl.program_id(0) == 0)
    def _():
        sre_ref[...] = s0re_ref[...]
        sim_ref[...] = s0im_ref[...]

    if perm_s:
        slab_s, = perm_s
        assert cin == LANES and nb == SUBLANES
        pitch = tl + SUBLANES
        u_bt = _dot(h_ref[...].reshape(r, h_ref.shape[2]), wu_ref[...])
        for b in range(nb):
            for k in range(nblk):
                slab_s[k, b * pitch:b * pitch + tl, :] = u_bt[b * tl:(b + 1) * tl,
                                                              k * cin:(k + 1) * cin]

        for t in range(tl):
            for k in range(nblk):
                u_s[t * nb:(t + 1) * nb, k * cin:(k + 1) * cin] = (
                    slab_s[k, pl.ds(t, nb, stride=pitch), :])
        u = u_s[...]
    else:
        u = _dot(h_ref[...], wu_ref[...])
        u_s[...] = u
    ub = u.astype(BF16)

    def drive(k):
        bu = _dot(ub[:, k * cin:(k + 1) * cin], wb_ref[k])
        xre_s[:, k * cst:(k + 1) * cst] = bu[:, :cst]
        xim_s[:, k * cst:(k + 1) * cst] = bu[:, cst:]

    def scan(k):
        lanes = slice(k * cst, (k + 1) * cst)
        a_re = jnp.broadcast_to(are_ref[:, lanes], (SUBLANES, cst))
        a_im = jnp.broadcast_to(aim_ref[:, lanes], (SUBLANES, cst))

        def step(rows, s_re, s_im):
            n_re = a_re * s_re - a_im * s_im + xre_s[rows, lanes]
            n_im = a_re * s_im + a_im * s_re + xim_s[rows, lanes]
            xre_s[rows, lanes] = n_re
            xim_s[rows, lanes] = n_im
            return n_re, n_im

        if nb == SUBLANES:
            s_re, s_im = sre_ref[:, lanes], sim_ref[:, lanes]
            for t in range(tl):
                s_re, s_im = step(slice(t * nb, (t + 1) * nb), s_re, s_im)
            sre_ref[:, lanes] = s_re
            sim_ref[:, lanes] = s_im
        else:
            def group_body(sg, carry):
                srow = pl.ds(pl.multiple_of(sg * SUBLANES, SUBLANES), SUBLANES)
                st = (sre_ref[srow, lanes], sim_ref[srow, lanes])
                for t in range(tl):
                    st = step(pl.ds(pl.multiple_of(t * nb + sg * SUBLANES, SUBLANES), SUBLANES), *st)
                sre_ref[srow, lanes], sim_ref[srow, lanes] = st
                return carry

            lax.fori_loop(0, nb // SUBLANES, group_body, 0)

    def readout(k):
        st = slice(k * cst, (k + 1) * cst)
        y_s[:, k * cin:(k + 1) * cin] = (_dot(xre_s[:, st].astype(BF16), wcre_ref[k])
                                         - _dot(xim_s[:, st].astype(BF16), wcim_ref[k]))

    for k in range(nblk + 2):
        if k < nblk:
            drive(k)
        if 1 <= k <= nblk:
            scan(k - 1)
        if k >= 2:
            readout(k - 2)
    y = y_s[...] + d_ref[...] * u_s[...]
    v = _gelu_tanh(y)
    gate = _dot(v.astype(BF16), wglu_ref[...]) + bglu_ref[...]
    o5 = v * _sigmoid(gate)
    if perm_s:
        for k in range(nblk):
            slab_s[k, 0:r, :] = o5[:, k * cin:(k + 1) * cin]
        for b in range(nb):
            for t0 in range(0, tl, pack):
                for k in range(nblk):
                    lo = slab_s[k, pl.ds(t0 * nb + b, SUBLANES, stride=nb), :]
                    hi = slab_s[k, pl.ds((t0 + SUBLANES) * nb + b, SUBLANES, stride=nb), :]
                    o5_ref[b, t0:t0 + pack, k * cin:(k + 1) * cin] = (
                        jnp.concatenate([lo, hi], axis=0).astype(BF16))
    else:
        o5_ref[...] = o5.astype(BF16)


def _s5(h, wu, wb, wcre, wcim, are, aim, dsk, wglu, bglu, s0re, s0im, nb, tl, casts=()):
    width = wu.shape[1]
    nstate = are.shape[1]
    r = nb * tl
    batch_major = h.ndim == 3
    if batch_major:
        steps = h.shape[1] // tl
        blk = lambda w: pl.BlockSpec((nb, tl, w), lambda i: (0, i, 0))
        o_shape = (nb, h.shape[1], width)
        extra = [pltpu.VMEM((width // LANES, nb * (tl + SUBLANES), LANES), F32)]
    else:
        steps = h.shape[0] // r
        blk = lambda w: pl.BlockSpec((r, w), lambda i: (i, 0))
        o_shape = (h.shape[0], width)
        extra = []
    kern = functools.partial(_s5_kernel, nb=nb, tl=tl, n_cast=len(casts))
    c_in, c_out, c_shapes = _cast_specs(casts, steps)
    return pl.pallas_call(
        kern,
        grid=(steps,),
        in_specs=[blk(h.shape[-1]), _const_spec(wu.shape), _const_spec(wb.shape),
                  _const_spec(wcre.shape), _const_spec(wcim.shape), _const_spec(are.shape),
                  _const_spec(aim.shape), _const_spec(dsk.shape), _const_spec(wglu.shape),
                  _const_spec(bglu.shape), _const_spec(s0re.shape), _const_spec(s0im.shape)]
                 + c_in,
        out_specs=[blk(width), pl.BlockSpec((nb, nstate), lambda i: (0, 0)),
                   pl.BlockSpec((nb, nstate), lambda i: (0, 0))] + c_out,
        out_shape=[jax.ShapeDtypeStruct(o_shape, BF16),
                   jax.ShapeDtypeStruct((nb, nstate), F32),
                   jax.ShapeDtypeStruct((nb, nstate), F32)] + c_shapes,
        scratch_shapes=[pltpu.VMEM((r, width), F32), pltpu.VMEM((r, nstate), F32),
                        pltpu.VMEM((r, nstate), F32), pltpu.VMEM((r, width), F32)] + extra,
        compiler_params=_params(("arbitrary",)),
        name="s5",
    )(h, wu, wb, wcre, wcim, are, aim, dsk, wglu, bglu, s0re, s0im, *casts)


def _split2(x):
    hi = x.astype(BF16)
    lo = (x - hi.astype(F32)).astype(BF16)
    return jnp.concatenate([hi, lo], axis=1)


def _split3(x):
    hi = x.astype(BF16)
    r1 = x - hi.astype(F32)
    mid = r1.astype(BF16)
    lo = (r1 - mid.astype(F32)).astype(BF16)
    return jnp.concatenate([hi, mid, lo], axis=1)


HEAD_REP = 32


def _pack2(x):
    grp = lax.broadcasted_iota(jnp.int32, x.shape, 1) // HEAD_REP
    hi = x.astype(BF16).astype(F32)
    return jnp.where(grp == 0, hi, x - hi).astype(BF16)


def _pack3(x):
    grp = lax.broadcasted_iota(jnp.int32, x.shape, 1) // HEAD_REP
    hi = x.astype(BF16).astype(F32)
    r1 = x - hi
    mid = r1.astype(BF16).astype(F32)
    return jnp.where(grp == 0, hi, jnp.where(grp == 1, mid, r1 - mid)).astype(BF16)


def _group_norm(y, zg, nw, gw):
    outs = []
    for g in range(SSD_GROUPS):
        seg = y[:, g * gw:(g + 1) * gw] * zg[:, g * gw:(g + 1) * gw]
        outs.append(seg * lax.rsqrt(jnp.mean(seg * seg, axis=-1, keepdims=True) + EPS))
    return jnp.concatenate(outs, axis=1) * nw


def _ssds_pre_kernel(h_ref, wz_ref, wx_ref, wdt_ref, cw_ref, cb_ref, dtb_ref, alog_ref, dsk_ref,
                     conv0_ref, g2_ref, exp_ref, c_o, b_o, xdd_o, yp_o, eac_o, zg_o, cdec_o, cv_o):
    nbt = conv0_ref.shape[1]
    seq = h_ref.shape[0] // nbt
    width = zg_o.shape[1]
    nbc = c_o.shape[1]
    hb = h_ref[...]
    zg_o[...] = _silu(_dot(hb, wz_ref[...]))
    dt = _softplus(_dot(hb, wdt_ref[...]) + dtb_ref[...])
    la = dt * (-jnp.exp(alog_ref[...]))
    xbc = _dot(hb, wx_ref[...])
    rows = [slice(t * nbt, (t + 1) * nbt) for t in range(seq)]
    full = [conv0_ref[k] for k in range(SSD_CONV - 1)] + [xbc[r] for r in rows]
    for k in range(SSD_CONV - 1):
        cv_o[k] = full[seq + k]
    acums = []
    for t in range(seq):
        acums.append(la[rows[t]] if t == 0 else acums[-1] + la[rows[t]])
    tot = acums[-1]
    cdec_o[...] = jnp.exp(tot)
    xs, bq, cq = [], [], []
    for t in range(seq):
        acc = cb_ref[...]
        for k in range(SSD_CONV):
            acc = acc + cw_ref[k:k + 1, :] * full[t + k]
        xc = _silu(acc)
        xs.append(xc[:, :width])
        bq.append(xc[:, width:width + nbc].astype(BF16).astype(F32))
        cq.append(xc[:, width + nbc:].astype(BF16).astype(F32))
    for t in range(seq):
        r = rows[t]
        b_o[r, :] = bq[t]
        c_o[r, :] = cq[t]
        xdd_o[r, :] = xs[t] * _dot(_split2(dt[r] * jnp.exp(tot - acums[t])), exp_ref[...])
        eac_o[r, :] = _dot(_split2(jnp.exp(acums[t])), exp_ref[...])
        yp = dsk_ref[...] * xs[t]
        for s in range(t + 1):
            cbx = _dot(_split2(cq[t] * bq[s]), g2_ref[...])
            coef = cbx * jnp.exp(acums[t] - acums[s]) * dt[rows[s]]
            yp = yp + _dot(_split2(coef), exp_ref[...]) * xs[s]
        yp_o[r, :] = yp


def _ssds_state_kernel(c_ref, b_ref, xdd_ref, yp_ref, eac_ref, zg_ref, cdec_ref, st_ref, nw_ref,
                       o_ref, sto_ref, yoff_s, xddt_s):
    seq, nb, width = xdd_ref.shape
    nheads, hd_dim, nst = st_ref.shape[1:]
    hpg = nheads // SSD_GROUPS
    gw = width // SSD_GROUPS
    rws = seq * nb
    cst = c_ref[...].reshape(rws, c_ref.shape[2]).astype(BF16)
    bst = b_ref[...].reshape(rws, b_ref.shape[2])
    xdd = xdd_ref[...].reshape(rws, width)
    xddt_s[...] = jnp.concatenate([xdd, jnp.zeros((LANES - rws, width), F32)], axis=0).T.astype(BF16)
    rowb = lax.broadcasted_iota(jnp.int32, (rws, gw), 0) % nb
    zpad = jnp.zeros((LANES - rws, nst), BF16)
    for b in range(nb):
        mine = rowb == b
        for g in range(SSD_GROUPS):
            gl = slice(g * gw, (g + 1) * gw)
            hg = st_ref[b, g * hpg:(g + 1) * hpg].reshape(gw, nst)
            res = _dot_nt(cst[:, g * nst:(g + 1) * nst], hg.astype(BF16))
            yoff_s[:, gl] = jnp.where(mine, res, 0.0) if b == 0 else jnp.where(mine, res, yoff_s[:, gl])
            bm = jnp.where(mine[:, :nst], bst[:, g * nst:(g + 1) * nst], 0.0).astype(BF16)
            upd = _dot(xddt_s[gl, :], jnp.concatenate([bm, zpad], axis=0))
            for jh in range(hpg):
                hd = g * hpg + jh
                hrows = slice(jh * hd_dim, (jh + 1) * hd_dim)
                sto_ref[b, hd] = hg[hrows] * cdec_ref[b, hd] + upd[hrows]
    y = yp_ref[...].reshape(rws, width) + yoff_s[...] * eac_ref[...].reshape(rws, width)
    o = _group_norm(y, zg_ref[...].reshape(rws, width), nw_ref[...], gw)
    o_ref[...] = o.reshape(seq, nb, width)


def _ssds(h_tm, wz, wx, wdt, cw, cb, dtb, alog, dsk, nw, ssd0, conv0_tm, nb):
    m, d = h_tm.shape
    nseq, nheads, hd_dim, nst = ssd0.shape
    seq = m // nseq
    width = wz.shape[1]
    xbc_w = wx.shape[1]
    nbc = SSD_GROUPS * nst
    hpg = nheads // SSD_GROUPS
    head = jnp.arange(LANES)
    exp1 = (head[:, None] == (jnp.arange(width) // hd_dim)[None, :]).astype(BF16)
    g1 = ((jnp.arange(nbc) // nst)[:, None] == (head // hpg)[None, :]) & (head < nheads)[None, :]
    exp_m = jnp.concatenate([exp1, exp1], axis=0)
    g2 = jnp.concatenate([g1, g1], axis=0).astype(BF16)
    full = lambda shape: pl.BlockSpec(shape, lambda: (0,) * len(shape))
    ins = (h_tm, wz, wx, wdt, cw, cb, dtb, alog, dsk, conv0_tm, g2, exp_m)
    outs = [jax.ShapeDtypeStruct((m, nbc), F32), jax.ShapeDtypeStruct((m, nbc), F32),
            jax.ShapeDtypeStruct((m, width), F32), jax.ShapeDtypeStruct((m, width), F32),
            jax.ShapeDtypeStruct((m, width), F32), jax.ShapeDtypeStruct((m, width), F32),
            jax.ShapeDtypeStruct((nseq, LANES), F32),
            jax.ShapeDtypeStruct((SSD_CONV - 1, nseq, xbc_w), F32)]
    c, b, xdd, yp, eac, zg, cdec, conv1_tm = pl.pallas_call(
        _ssds_pre_kernel,
        in_specs=[full(a.shape) for a in ins],
        out_specs=[full(o.shape) for o in outs],
        out_shape=outs,
        compiler_params=pltpu.CompilerParams(vmem_limit_bytes=VMEM_LIMIT),
        name="ssd_sample_pre",
    )(*ins)

    tm3 = lambda a: a.reshape(seq, nseq, a.shape[1])
    blk = lambda w: pl.BlockSpec((seq, nb, w), lambda i: (0, i, 0))
    st_spec = pl.BlockSpec((nb, nheads, hd_dim, nst), lambda i: (i, 0, 0, 0))
    o, ssd1 = pl.pallas_call(
        _ssds_state_kernel,
        grid=(nseq // nb,),
        in_specs=[blk(nbc), blk(nbc), blk(width), blk(width), blk(width), blk(width),
                  pl.BlockSpec((nb, LANES), lambda i: (i, 0), memory_space=pltpu.SMEM),
                  st_spec, _const_spec(nw.shape)],
        out_specs=[blk(width), st_spec],
        out_shape=[jax.ShapeDtypeStruct((seq, nseq, width), F32),
                   jax.ShapeDtypeStruct(ssd0.shape, F32)],
        scratch_shapes=[pltpu.VMEM((seq * nb, width), F32), pltpu.VMEM((width, LANES), BF16)],
        compiler_params=_params(("arbitrary",)),
        name="ssd_sample_state",
    )(tm3(c), tm3(b), tm3(xdd), tm3(yp), tm3(eac), tm3(zg), cdec, ssd0, nw)
    return o.reshape(m, width), ssd1, conv1_tm


def _ssdp_kernel(h_ref, wz_ref, wx_ref, wdt_ref, cw_ref, cb_ref, dtb_ref, alog_ref, dsk_ref,
                 nw_ref, btril_ref, exp_ref, perm_ref, permt_ref, o_ref, st_ref, cv_ref,
                 xbc_s, zg_s, acum_s, acumt_s, e2_s, e3_s, xd_s, xdd_s, ysk_s, b_s, c_s,
                 cdec_s, ht_s, tail_s, m_s, *, nch, col_chunk):
    t = SSD_CHUNK
    r = nch * t
    j = pl.program_id(1)
    nheads, hd_dim, nst = st_ref.shape[1:]
    width = nheads * hd_dim
    gw = width // SSD_GROUPS
    hpg = nheads // SSD_GROUPS
    xbc_w = xbc_s.shape[1]
    ntap = SSD_CONV - 1
    pre = ntap * SUBLANES
    ext = pre + t
    steps = t // SUBLANES
    last_rows = [t - 1 - SUBLANES * (ntap - 1 - v) for v in range(ntap)]

    @pl.when(j == 0)
    def _():
        ht_s[...] = jnp.zeros(ht_s.shape, F32)
        tail_s[...] = jnp.zeros(tail_s.shape, F32)

    hb = jnp.concatenate([_dot(perm_ref[...], h_ref[c * t:(c + 1) * t, :]) for c in range(nch)],
                         axis=0).astype(BF16)
    dt = _softplus(_dot(hb, wdt_ref[...]) + dtb_ref[...])
    la = dt * (-jnp.exp(alog_ref[...]))

    def xbc_cols(jc):
        lanes = slice(jc * col_chunk, (jc + 1) * col_chunk)
        xall = _dot(hb, wx_ref[:, lanes])
        for c in range(nch):
            xbc_s[c * ext + pre:(c + 1) * ext, lanes] = xall[c * t:(c + 1) * t, :]
        for c in range(nch):
            base = c * ext
            xbc_s[base:base + pre, lanes] = xbc_s[base + t - 1:base + pre + t - 1, lanes]
            for v in range(ntap):
                if c == 0:
                    prev = tail_s[v:v + 1, lanes]
                else:
                    prow = base - ext + pre + last_rows[v]
                    prev = xbc_s[prow:prow + 1, lanes]
                xbc_s[base + v * SUBLANES:base + v * SUBLANES + 1, lanes] = prev
            acc = cb_ref[:, lanes]
            for k in range(SSD_CONV):
                acc = acc + cw_ref[k:k + 1, lanes] * xbc_s[base + k * SUBLANES:
                                                           base + k * SUBLANES + t, lanes]
            xc = _silu(acc)
            rows = slice(c * t, (c + 1) * t)
            lo = jc * col_chunk
            if lo < width:
                xd_s[rows, lanes] = xc.astype(BF16)
                xdd_s[rows, lanes] = (xc * e2_s[rows, lanes]).astype(BF16)
                ysk_s[rows, lanes] = xc * dsk_ref[:, lanes]
            elif lo < width + SSD_GROUPS * nst:
                b_s[rows, lo - width:lo - width + col_chunk] = xc.astype(BF16)
            else:
                cl = lo - width - SSD_GROUPS * nst
                c_s[rows, cl:cl + col_chunk] = xc.astype(BF16)

    x_cols = width // col_chunk
    for jc in range(x_cols, xbc_w // col_chunk):
        xbc_cols(jc)

    r3 = _dot(btril_ref[...], _split3(la))
    acum = r3[:, :LANES] + r3[:, LANES:2 * LANES] + r3[:, 2 * LANES:]
    tots = [acum[(c + 1) * t - 1:(c + 1) * t, :] for c in range(nch)]
    arow = acum - jnp.log(dt)
    for c in range(nch):
        acumt_s[c] = arow[c * t:(c + 1) * t, :].T
    tot_rows = jnp.concatenate([jnp.broadcast_to(v, (t, LANES)) for v in tots], axis=0)
    dstate = jnp.exp(tot_rows - acum)
    cdec = jnp.exp(jnp.concatenate([jnp.broadcast_to(v, (SUBLANES, LANES)) for v in tots], axis=0))
    e2_s[...] = _dot(_pack2(dt * dstate), exp_ref[...])
    e3_s[...] = _dot(_pack2(jnp.exp(acum)), exp_ref[...])
    cdx = _dot(_pack2(cdec), exp_ref[...])
    for c in range(nch):
        cdec_s[c] = cdx[c * SUBLANES:(c + 1) * SUBLANES, :]
    acum_s[...] = acum

    def token(i):
        return (i % SUBLANES) * steps + i // SUBLANES

    causal = (token(lax.broadcasted_iota(jnp.int32, (t, t), 0))
              >= token(lax.broadcasted_iota(jnp.int32, (t, t), 1)))
    for c in range(nch):
        rows = slice(c * t, (c + 1) * t)
        for g in range(SSD_GROUPS):
            cbm = jnp.where(causal, _dot_nt(c_s[rows, g * nst:(g + 1) * nst],
                                            b_s[rows, g * nst:(g + 1) * nst]), 0.0)
            for hd in range(g * hpg, (g + 1) * hpg):
                diff = acum_s[rows, hd:hd + 1] - acumt_s[c, hd:hd + 1, :]
                m_s[(c * nheads + hd) * t:(c * nheads + hd + 1) * t, :] = (
                    cbm * jnp.exp(jnp.where(causal, diff, -jnp.inf))).astype(BF16)

    for jc in range(x_cols):
        lanes = slice(jc * col_chunk, (jc + 1) * col_chunk)
        xbc_cols(jc)
        zg_s[:, lanes] = _silu(_dot(hb, wz_ref[:, lanes]))

    for v in range(ntap):
        row = (nch - 1) * ext + pre + last_rows[v]
        tail_s[v:v + 1, :] = xbc_s[row:row + 1, :]
    cv_ref[0] = tail_s[...]

    lane = lax.broadcasted_iota(jnp.int32, (t, LANES), 1)
    keep_lo = jnp.where(lane < hd_dim, 1.0, 0.0).astype(BF16)
    keep_hi = jnp.where(lane < hd_dim, 0.0, 1.0).astype(BF16)
    nw = nw_ref[...]

    for c in range(nch):
        rows = slice(c * t, (c + 1) * t)
        ys = []
        for g in range(SSD_GROUPS):
            gl = slice(g * gw, (g + 1) * gw)
            cg = c_s[rows, g * nst:(g + 1) * nst]
            bg = b_s[rows, g * nst:(g + 1) * nst]
            pairs = []
            for jp in range(hpg // 2):
                ha = g * hpg + 2 * jp
                m2 = jnp.concatenate([m_s[(c * nheads + hd) * t:(c * nheads + hd + 1) * t, :]
                                      for hd in (ha, ha + 1)], axis=1)
                xdp = xd_s[rows, ha * hd_dim:ha * hd_dim + LANES]
                rhs = jnp.concatenate([xdp * keep_lo, xdp * keep_hi], axis=0)
                pairs.append(_dot(m2, rhs))
            htg = ht_s[:, gl]
            y = (jnp.concatenate(pairs, axis=1) + _dot(cg, htg.astype(BF16)) * e3_s[rows, gl]
                 + ysk_s[rows, gl])
            ys.append(y)
            upd = lax.dot_general(bg, xdd_s[rows, gl], (((0,), (0,)), ((), ())),
                                  preferred_element_type=F32)
            ht_s[:, gl] = htg * cdec_s[c, 0:1, gl] + upd
        o = _group_norm(jnp.concatenate(ys, axis=1), zg_s[rows, :], nw, gw).astype(BF16)
        o_ref[rows, :] = _dot(permt_ref[...], o).astype(BF16)

    @pl.when(j == pl.num_programs(1) - 1)
    def _():
        for pr in range(nheads // 2):
            tt = ht_s[:, pr * LANES:(pr + 1) * LANES].T
            st_ref[0, 2 * pr] = tt[:hd_dim]
            st_ref[0, 2 * pr + 1] = tt[hd_dim:]


def _ssdp(h, wz, wx, wdt, cw, cb, dtb, alog, dsk, nw, nseq, nheads, nch):
    m, d = h.shape
    width = wz.shape[1]
    xbc_w = wx.shape[1]
    hd_dim = width // nheads
    nst = SSD_STATE
    t = SSD_CHUNK
    r = nch * t
    nblk = m // (nseq * r)
    ri = jnp.arange(r)
    steps = t // SUBLANES
    tok = (ri % t % SUBLANES) * steps + ri % t // SUBLANES
    btril = ((tok[:, None] >= tok[None, :]) & (ri[:, None] // t == ri[None, :] // t)).astype(BF16)
    perm = (tok[:t, None] == jnp.arange(t)[None, :]).astype(BF16)
    lane = jnp.arange(LANES)
    head = (lane % HEAD_REP)[:, None]
    exp_m = ((head == (jnp.arange(width) // hd_dim)[None, :])
             & (lane < 2 * HEAD_REP)[:, None]).astype(BF16)
    row = lambda w: pl.BlockSpec((r, w), lambda i, j: (i * nblk + j, 0))
    kern = functools.partial(_ssdp_kernel, nch=nch, col_chunk=4 * LANES)
    vm = lambda shape, dt: pltpu.VMEM(shape, dt)
    return pl.pallas_call(
        kern,
        grid=(nseq, nblk),
        in_specs=[row(d), _const_spec(wz.shape), _const_spec(wx.shape), _const_spec(wdt.shape),
                  _const_spec(cw.shape), _const_spec(cb.shape), _const_spec(dtb.shape),
                  _const_spec(alog.shape), _const_spec(dsk.shape), _const_spec(nw.shape),
                  _const_spec(btril.shape), _const_spec(exp_m.shape),
                  _const_spec(perm.shape), _const_spec(perm.shape)],
        out_specs=[row(width),
                   pl.BlockSpec((1, nheads, hd_dim, nst), lambda i, j: (i, 0, 0, 0)),
                   pl.BlockSpec((1, SSD_CONV - 1, xbc_w), lambda i, j: (i, 0, 0))],
        out_shape=[jax.ShapeDtypeStruct((m, width), BF16),
                   jax.ShapeDtypeStruct((nseq, nheads, hd_dim, nst), F32),
                   jax.ShapeDtypeStruct((nseq, SSD_CONV - 1, xbc_w), F32)],
        scratch_shapes=[vm((nch * (t + (SSD_CONV - 1) * SUBLANES), xbc_w), F32),
                        vm((r, width), F32),
                        vm((r, LANES), F32), vm((nch, LANES, t), F32),
                        vm((r, width), F32), vm((r, width), F32),
                        vm((r, width), BF16), vm((r, width), BF16), vm((r, width), F32),
                        vm((r, SSD_GROUPS * nst), BF16), vm((r, SSD_GROUPS * nst), BF16),
                        vm((nch, SUBLANES, width), F32), vm((nst, width), F32),
                        vm((SSD_CONV - 1, xbc_w), F32), vm((nch * nheads * t, t), BF16)],
        compiler_params=_params(("arbitrary", "arbitrary")),
        name="ssd_prompt",
    )(h, wz, wx, wdt, cw, cb, dtb, alog, dsk, nw, btril, exp_m, perm, perm.T)


def _rep_heads(v):
    grp = jnp.pad(v, [(0, 0)] * (v.ndim - 1) + [(0, HEAD_REP - v.shape[-1])])
    tail = jnp.zeros(v.shape[:-1] + (LANES - 3 * HEAD_REP,), v.dtype)
    return jnp.concatenate([grp, grp, grp, tail], axis=-1)


def _s5_call(h, s5re0, s5im0, w, nb, tl, casts=()):
    g, p = s5re0.shape[1:]
    o5, re1, im1, *cast = _s5(h, w["w_u"], w["s5_wb"], w["s5_wcre"], w["s5_wcim"], w["s5_are"],
                              w["s5_aim"], w["s5_d"], w["s5_wglu"], w["s5_bglu"],
                              s5re0.reshape(nb, g * p), s5im0.reshape(nb, g * p), nb, tl, casts)
    return o5, re1.reshape(nb, g, p), im1.reshape(nb, g, p), cast


def _layer(xp, xs, s5re0, s5im0, ssd0, conv0, w, *, tm, s5_tl, ssd_nch, ssd_nb):
    bp, lp, d = xp.shape
    bs, ls, _ = xs.shape
    mp, ms = bp * lp, bs * ls
    g, p = s5re0.shape[1:]
    nheads = ssd0.shape[1]

    xs_tm = xs.transpose(1, 0, 2).reshape(ms, d)
    x1p, hp, x1s, hs, w_u, w_z, w_x, w_dt = _ffn1(
        xp.reshape(mp, d), xs_tm, w["ffn1_norm"], w["ffn1_wg"], w["ffn1_wu"], w["ffn1_wd"],
        w["mix_norm"], w["w_in"], w["w_in_splits"], tm)
    w = dict(w, w_u=w_u, w_z=w_z, w_x=w_x, w_dt=_rep_heads(w_dt))

    zeros = jnp.zeros((bp, g, p), F32)
    o5p, p_re, p_im, (wg2, wu2, wd2, wo) = _s5_call(
        hp.reshape(bp, lp, d), zeros, zeros, w, bp, s5_tl,
        (w["ffn2_wg"], w["ffn2_wu"], w["ffn2_wd"], w["w_out"]))
    osp, p_ssd, p_conv = _ssdp(hp, w["w_z"], w["w_x"], w["w_dt"], w["conv_w"], w["conv_b"],
                               w["dt_bias"], w["a_log"], w["ssd_d"], w["ssd_norm"],
                               bp, nheads, ssd_nch)

    o5s, s_re, s_im, _ = _s5_call(hs, s5re0, s5im0, w, bs, ls)
    oss, s_ssd, s_conv_tm = _ssds(hs, w["w_z"], w["w_x"], w["w_dt"], w["conv_w"], w["conv_b"],
                                  w["dt_bias"], w["a_log"], w["ssd_d"], w["ssd_norm"], ssd0,
                                  conv0.transpose(1, 0, 2), ssd_nb)

    yp, ys = _out_ffn2(x1p, o5p.reshape(mp, -1), osp, x1s, o5s, oss, wo, w["ffn2_norm"],
                       wg2, wu2, wd2, w["final_norm"], tm)
    return (yp.reshape(bp, lp, d), ys.reshape(ls, bs, d).transpose(1, 0, 2),
            p_re, p_im, p_ssd, p_conv, s_re, s_im, s_ssd, s_conv_tm.transpose(1, 0, 2))


def kernel(x_prompt, x_sample, state_s5_re, state_s5_im, state_ssd, state_conv, ffn1_norm, ffn1_w_gate, ffn1_w_up, ffn1_w_down, mix_norm, w_in, s5_lambda_re, s5_lambda_im, s5_log_step, s5_b_re, s5_b_im, s5_c_re, s5_c_im, s5_d, s5_w_glu, s5_b_glu, ssd_conv_w, ssd_conv_b, ssd_dt_bias, ssd_a_log, ssd_d, ssd_norm, w_out, ffn2_norm, ffn2_w_gate, ffn2_w_up, ffn2_w_down, final_norm):
    depth = w_in.shape[0]
    assert depth == 1, "single-layer stack"
    i = 0
    d = x_prompt.shape[-1]
    g, p = s5_lambda_re.shape[1:]
    s5_width = g * S5_GROUP
    nheads = ssd_a_log.shape[1]
    ssd_width = nheads * SSD_HEAD_DIM
    xbc_w = ssd_conv_w.shape[-1]
    c0, c1, c2 = s5_width, s5_width + ssd_width, s5_width + ssd_width + xbc_w
    row = lambda v: v.astype(F32).reshape(1, -1)

    are, aim, bbre, bbim = _s5_params(s5_lambda_re[i], s5_lambda_im[i], s5_log_step[i],
                                      s5_b_re[i], s5_b_im[i])
    nblk = s5_width // LANES
    wb = jnp.concatenate([_block_diag(bbre, nblk), _block_diag(bbim, nblk)],
                         axis=-1).astype(BF16)
    wcre = _block_diag(s5_c_re[i].astype(F32).transpose(0, 2, 1), nblk).astype(BF16)
    wcim = _block_diag(s5_c_im[i].astype(F32).transpose(0, 2, 1), nblk).astype(BF16)

    w = {
        "ffn1_norm": row(ffn1_norm[i]), "mix_norm": row(mix_norm[i]),
        "ffn2_norm": row(ffn2_norm[i]), "final_norm": row(final_norm),
        "ffn1_wg": ffn1_w_gate[i].astype(BF16), "ffn1_wu": ffn1_w_up[i].astype(BF16),
        "ffn1_wd": ffn1_w_down[i].astype(BF16),
        "w_in": w_in.astype(F32), "w_in_splits": (c0, c1 - c0, c2 - c1, nheads),
        "ffn2_wg": ffn2_w_gate.astype(F32), "ffn2_wu": ffn2_w_up.astype(F32),
        "ffn2_wd": ffn2_w_down.astype(F32), "w_out": w_out.astype(F32),
        "s5_wb": wb, "s5_wcre": wcre, "s5_wcim": wcim, "s5_are": are, "s5_aim": aim,
        "s5_d": row(s5_d[i]), "s5_wglu": s5_w_glu[i].astype(BF16), "s5_bglu": row(s5_b_glu[i]),
        "conv_w": ssd_conv_w[i].astype(F32), "conv_b": row(ssd_conv_b[i]),
        "dt_bias": _rep_heads(row(ssd_dt_bias[i])), "a_log": _rep_heads(row(ssd_a_log[i])),
        "ssd_d": jnp.repeat(ssd_d[i].astype(F32), SSD_HEAD_DIM).reshape(1, -1),
        "ssd_norm": row(ssd_norm[i]),
    }

    lp = x_prompt.shape[1]
    yp, ys, *states = _layer(
        x_prompt, x_sample, state_s5_re[i], state_s5_im[i], state_ssd[i], state_conv[i], w,
        tm=512, s5_tl=min(lp, 128), ssd_nch=min(lp // SSD_CHUNK, 4), ssd_nb=SUBLANES)
    return (yp, ys) + tuple(v[None] for v in states)
```

```python
import functools
import math

import jax
import jax.numpy as jnp
from jax import lax
from jax.experimental import pallas as pl
from jax.experimental.pallas import tpu as pltpu

F32 = jnp.float32
BF16 = jnp.bfloat16
EPS = 1e-6

LANES = 128
SUBLANES = 8
VMEM_LIMIT = 56 * 1024 * 1024

S5_GROUP = 16
S5_STATE = 64
SSD_HEAD_DIM = 64
SSD_GROUPS = 4
SSD_STATE = 128
SSD_CONV = 4
SSD_CHUNK = 128


def _sigmoid(x):
    return 0.5 + 0.5 * jnp.tanh(0.5 * x)


def _silu(x):
    hx = 0.5 * x
    return hx + hx * jnp.tanh(hx)


def _gelu_tanh(x):
    c = math.sqrt(2.0 / math.pi)
    return 0.5 * x * (1.0 + jnp.tanh(c * (x + 0.044715 * (x * x * x))))


def _softplus(x):
    return jnp.maximum(x, 0.0) + jnp.log1p(jnp.exp(-jnp.abs(x)))


def _rms(x, w):
    return x * lax.rsqrt(jnp.mean(x * x, axis=-1, keepdims=True) + EPS) * w


def _dot(a, b):
    return jnp.dot(a, b, preferred_element_type=F32)


def _dot_nt(a, b):
    return lax.dot_general(a, b, (((1,), (1,)), ((), ())), preferred_element_type=F32)


def _const_spec(shape):
    nd = len(shape)
    return pl.BlockSpec(shape, lambda *_: (0,) * nd, pipeline_mode=pl.Buffered(1))


def _params(sem):
    return pltpu.CompilerParams(dimension_semantics=sem, vmem_limit_bytes=VMEM_LIMIT)


def _swiglu_half(x, nw, wg_ref, wu_ref, wd_ref):
    hn = _rms(x, nw).astype(BF16)
    g = _dot(hn, wg_ref[...])
    u = _dot(hn, wu_ref[...])
    a = (_silu(g) * u).astype(BF16)
    return x + 0.5 * _dot(a, wd_ref[...])


def _cast_specs(ws, steps, layer=0):
    ins, outs, shapes = [], [], []
    for w in ws:
        _, nrow, ncol = w.shape
        rows = nrow // steps
        assert rows * steps == nrow and rows % (2 * SUBLANES) == 0
        ins.append(pl.BlockSpec((None, rows, ncol),
                                lambda i: (layer, jnp.minimum(i, steps - 1), 0)))
        outs.append(pl.BlockSpec((rows, ncol), lambda i: (jnp.minimum(i, steps - 1), 0)))
        shapes.append(jax.ShapeDtypeStruct((nrow, ncol), BF16))
    return ins, outs, shapes


W_CHUNKS = 8


def _weight_copy(w_hbm, layer, stage, sem, k, slot):
    rows = stage.shape[1]
    return pltpu.make_async_copy(w_hbm.at[layer, pl.ds(k * rows, rows), :], stage.at[slot],
                                 sem.at[slot])


def _load_weight_bf16(w_hbm, layer, dst, stage, sem):
    rows = stage.shape[1]
    _weight_copy(w_hbm, layer, stage, sem, 0, 0).start()
    for k in range(W_CHUNKS):
        slot = k % 2
        if k + 1 < W_CHUNKS:
            _weight_copy(w_hbm, layer, stage, sem, k + 1, 1 - slot).start()
        _weight_copy(w_hbm, layer, stage, sem, k, slot).wait()
        dst[k * rows:(k + 1) * rows, :] = stage[slot].astype(BF16)


def _ffn1_kernel(xp_ref, xs_ref, nw_ref, wg_hbm, wu_hbm, wd_hbm, mixw_ref, win_ref,
                 x1p_ref, hp_ref, x1s_ref, hs_ref, *rest, n_p, layer):
    *wsplit_refs, wg_ref, wu_ref, wd_ref, stage_a, stage_b, sem = rest
    is_p = pl.program_id(0) < n_p

    @pl.when(pl.program_id(0) == 0)
    def _():
        _load_weight_bf16(wg_hbm, layer, wg_ref, stage_a, sem)
        _load_weight_bf16(wu_hbm, layer, wu_ref, stage_a, sem)
        _load_weight_bf16(wd_hbm, layer, wd_ref, stage_b, sem)

    def tile(x_ref, x1_ref, h_ref):
        half = x_ref.shape[0] // 2
        for rows in (slice(0, half), slice(half, 2 * half)):
            x1 = _swiglu_half(x_ref[rows, :], nw_ref[...], wg_ref, wu_ref, wd_ref)
            x1_ref[rows, :] = x1
            h_ref[rows, :] = _rms(x1, mixw_ref[...]).astype(BF16)

    pl.when(is_p)(functools.partial(tile, xp_ref, x1p_ref, hp_ref))
    pl.when(jnp.logical_not(is_p))(functools.partial(tile, xs_ref, x1s_ref, hs_ref))

    lo = 0
    for o_ref in wsplit_refs:
        o_ref[...] = win_ref[:, lo:lo + o_ref.shape[1]].astype(BF16)
        lo += o_ref.shape[1]


def _ffn1(xp, xs, nw, wg, wu, wd, mixw, w_in, splits, tm):
    mp, d = xp.shape
    ms = xs.shape[0]
    assert ms == tm
    n_p = mp // tm
    dff = wg.shape[2]
    assert d % (W_CHUNKS * 2 * SUBLANES) == 0 and dff % (W_CHUNKS * 2 * SUBLANES) == 0
    rowp = lambda w: pl.BlockSpec((tm, w), lambda i: (jnp.minimum(i, n_p - 1), 0))
    rows = lambda w: pl.BlockSpec((ms, w), lambda i: (0, 0))
    hbm = pl.BlockSpec(memory_space=pl.ANY)
    layer = 0
    wrows = w_in.shape[1] // n_p
    wblk = lambda w: pl.BlockSpec((wrows, w), lambda i: (jnp.minimum(i, n_p - 1), 0))
    w_in_blk = pl.BlockSpec((None, wrows, w_in.shape[2]),
                            lambda i: (layer, jnp.minimum(i, n_p - 1), 0))
    return pl.pallas_call(
        functools.partial(_ffn1_kernel, n_p=n_p, layer=layer),
        grid=(n_p + 1,),
        in_specs=[rowp(d), rows(d), _const_spec((1, d)), hbm, hbm, hbm, _const_spec((1, d)),
                  w_in_blk],
        out_specs=[rowp(d), rowp(d), rows(d), rows(d)] + [wblk(c) for c in splits],
        out_shape=[jax.ShapeDtypeStruct((mp, d), F32), jax.ShapeDtypeStruct((mp, d), BF16),
                   jax.ShapeDtypeStruct((ms, d), F32), jax.ShapeDtypeStruct((ms, d), BF16)]
                  + [jax.ShapeDtypeStruct((w_in.shape[1], c), BF16) for c in splits],
        scratch_shapes=[pltpu.VMEM((d, dff), BF16), pltpu.VMEM((d, dff), BF16),
                        pltpu.VMEM((dff, d), BF16),
                        pltpu.VMEM((2, d // W_CHUNKS, dff), F32),
                        pltpu.VMEM((2, dff // W_CHUNKS, d), F32),
                        pltpu.SemaphoreType.DMA((2,))],
        compiler_params=_params(("arbitrary",)),
        name="ffn1",
    )(xp, xs, nw, wg, wu, wd, mixw, w_in)


def _out_ffn2_kernel(x1p_ref, o5p_ref, osp_ref, x1s_ref, o5s_ref, oss_ref, wo_ref, nw_ref,
                     wg_ref, wu_ref, wd_ref, fin_ref, yp_ref, ys_ref, *, n_p):
    is_p = pl.program_id(0) < n_p
    k5 = o5p_ref.shape[1]

    def tile(x1_ref, o5_ref, os_ref, y_ref):
        x2 = x1_ref[...] + (_dot(o5_ref[...].astype(BF16), wo_ref[:k5, :])
                            + _dot(os_ref[...].astype(BF16), wo_ref[k5:, :]))
        x3 = _swiglu_half(x2, nw_ref[...], wg_ref, wu_ref, wd_ref)
        y_ref[...] = _rms(x3, fin_ref[...])

    pl.when(is_p)(functools.partial(tile, x1p_ref, o5p_ref, osp_ref, yp_ref))
    pl.when(jnp.logical_not(is_p))(functools.partial(tile, x1s_ref, o5s_ref, oss_ref, ys_ref))


def _out_ffn2(x1p, o5p, osp, x1s, o5s, oss, wo, nw, wg, wu, wd, fin, tm):
    mp, d = x1p.shape
    ms = x1s.shape[0]
    assert ms == tm
    n_p = mp // tm
    dff = wg.shape[1]
    rowp = lambda w: pl.BlockSpec((tm, w), lambda i: (jnp.minimum(i, n_p - 1), 0))
    rows = lambda w: pl.BlockSpec((ms, w), lambda i: (0, 0))
    return pl.pallas_call(
        functools.partial(_out_ffn2_kernel, n_p=n_p),
        grid=(n_p + 1,),
        in_specs=[rowp(d), rowp(o5p.shape[1]), rowp(osp.shape[1]),
                  rows(d), rows(o5s.shape[1]), rows(oss.shape[1]),
                  _const_spec(wo.shape), _const_spec((1, d)),
                  _const_spec((d, dff)), _const_spec((d, dff)), _const_spec((dff, d)),
                  _const_spec((1, d))],
        out_specs=[rowp(d), rows(d)],
        out_shape=[jax.ShapeDtypeStruct((mp, d), F32), jax.ShapeDtypeStruct((ms, d), F32)],
        compiler_params=_params(("arbitrary",)),
        name="out_ffn2",
    )(x1p, o5p, osp, x1s, o5s, oss, wo, nw, wg, wu, wd, fin)


def _s5_param_kernel(lre_ref, lim_ref, step_ref, bre_ref, bim_ref,
                     are_ref, aim_ref, bbre_ref, bbim_ref):
    lre, lim, step = lre_ref[...], lim_ref[...], step_ref[...]
    mag = jnp.exp(lre * step)
    ang = lim * step
    are = mag * jnp.cos(ang)
    aim = mag * jnp.sin(ang)
    den = lre * lre + lim * lim
    nre, nim = are - 1.0, aim
    cre = (nre * lre + nim * lim) / den
    cim = (nim * lre - nre * lim) / den
    are_ref[...] = are
    aim_ref[...] = aim
    bre, bim = bre_ref[...], bim_ref[...]
    bbre_ref[...] = cre * bre - cim * bim
    bbim_ref[...] = cre * bim + cim * bre


def _s5_params(lam_re, lam_im, log_step, b_re, b_im):
    g, p = lam_re.shape
    hh = b_re.shape[-1]
    gp = g * p
    step = jnp.exp(log_step.astype(F32))
    step_row = jnp.broadcast_to(step[:, None], (g, p)).reshape(1, gp)
    to_lanes = lambda b: b.astype(F32).reshape(gp, hh).T
    full = lambda shape: pl.BlockSpec(shape, lambda: (0,) * len(shape))
    are, aim, bbre, bbim = pl.pallas_call(
        _s5_param_kernel,
        in_specs=[full((1, gp)), full((1, gp)), full((1, gp)), full((hh, gp)), full((hh, gp))],
        out_specs=[full((1, gp)), full((1, gp)), full((hh, gp)), full((hh, gp))],
        out_shape=[jax.ShapeDtypeStruct((1, gp), F32), jax.ShapeDtypeStruct((1, gp), F32),
                   jax.ShapeDtypeStruct((hh, gp), F32), jax.ShapeDtypeStruct((hh, gp), F32)],
        name="s5_params",
    )(lam_re.astype(F32).reshape(1, gp), lam_im.astype(F32).reshape(1, gp), step_row,
      to_lanes(b_re), to_lanes(b_im))
    to_blocks = lambda b: b.reshape(hh, g, p).transpose(1, 0, 2)
    return are, aim, to_blocks(bbre), to_blocks(bbim)


def _block_diag(w, nblk):
    g, a, b = w.shape
    gl = g // nblk
    w = w.reshape(nblk, gl, a, b)
    eye = jnp.eye(gl, dtype=w.dtype)
    out = w[:, :, :, None, :] * eye[None, :, None, :, None]
    return out.reshape(nblk, gl * a, gl * b)


def _s5_kernel(*refs, nb, tl, n_cast):
    (h_ref, wu_ref, wb_ref, wcre_ref, wcim_ref, are_ref, aim_ref, d_ref, wglu_ref, bglu_ref,
     s0re_ref, s0im_ref) = refs[:12]
    cast_in = refs[12:12 + n_cast]
    o5_ref, sre_ref, sim_ref = refs[12 + n_cast:15 + n_cast]
    cast_out = refs[15 + n_cast:15 + 2 * n_cast]
    u_s, xre_s, xim_s, y_s, *perm_s = refs[15 + 2 * n_cast:]
    for src, dst in zip(cast_in, cast_out):
        dst[...] = src[...].astype(BF16)
    width = u_s.shape[1]
    nstate = xre_s.shape[1]
    nblk = wb_ref.shape[0]
    cin = width // nblk
    cst = nstate // nblk
    r = nb * tl
    pack = 2 * SUBLANES

    @pl.when(pl.program_id(0) == 0)
    def _():
        sre_ref[...] = s0re_ref[...]
        sim_ref[...] = s0im_ref[...]

    if perm_s:
        slab_s, = perm_s
        assert cin == LANES and nb == SUBLANES
        pitch = tl + SUBLANES
        u_bt = _dot(h_ref[...].reshape(r, h_ref.shape[2]), wu_ref[...])
        for b in range(nb):
            for k in range(nblk):
                slab_s[k, b * pitch:b * pitch + tl, :] = u_bt[b * tl:(b + 1) * tl,
                                                              k * cin:(k + 1) * cin]

        for t in range(tl):
            for k in range(nblk):
                u_s[t * nb:(t + 1) * nb, k * cin:(k + 1) * cin] = (
                    slab_s[k, pl.ds(t, nb, stride=pitch), :])
        u = u_s[...]
    else:
        u = _dot(h_ref[...], wu_ref[...])
        u_s[...] = u
    ub = u.astype(BF16)

    def drive(k):
        bu = _dot(ub[:, k * cin:(k + 1) * cin], wb_ref[k])
        xre_s[:, k * cst:(k + 1) * cst] = bu[:, :cst]
        xim_s[:, k * cst:(k + 1) * cst] = bu[:, cst:]

    def scan(k):
        lanes = slice(k * cst, (k + 1) * cst)
        a_re = jnp.broadcast_to(are_ref[:, lanes], (SUBLANES, cst))
        a_im = jnp.broadcast_to(aim_ref[:, lanes], (SUBLANES, cst))

        def step(rows, s_re, s_im):
            n_re = a_re * s_re - a_im * s_im + xre_s[rows, lanes]
            n_im = a_re * s_im + a_im * s_re + xim_s[rows, lanes]
            xre_s[rows, lanes] = n_re
            xim_s[rows, lanes] = n_im
            return n_re, n_im

        if nb == SUBLANES:
            s_re, s_im = sre_ref[:, lanes], sim_ref[:, lanes]
            for t in range(tl):
                s_re, s_im = step(slice(t * nb, (t + 1) * nb), s_re, s_im)
            sre_ref[:, lanes] = s_re
            sim_ref[:, lanes] = s_im
        else:
            def group_body(sg, carry):
                srow = pl.ds(pl.multiple_of(sg * SUBLANES, SUBLANES), SUBLANES)
                st = (sre_ref[srow, lanes], sim_ref[srow, lanes])
                for t in range(tl):
                    st = step(pl.ds(pl.multiple_of(t * nb + sg * SUBLANES, SUBLANES), SUBLANES), *st)
                sre_ref[srow, lanes], sim_ref[srow, lanes] = st
                return carry

            lax.fori_loop(0, nb // SUBLANES, group_body, 0)

    def readout(k):
        st = slice(k * cst, (k + 1) * cst)
        y_s[:, k * cin:(k + 1) * cin] = (_dot(xre_s[:, st].astype(BF16), wcre_ref[k])
                                         - _dot(xim_s[:, st].astype(BF16), wcim_ref[k]))

    for k in range(nblk + 2):
        if k < nblk:
            drive(k)
        if 1 <= k <= nblk:
            scan(k - 1)
        if k >= 2:
            readout(k - 2)
    y = y_s[...] + d_ref[...] * u_s[...]
    v = _gelu_tanh(y)
    gate = _dot(v.astype(BF16), wglu_ref[...]) + bglu_ref[...]
    o5 = v * _sigmoid(gate)
    if perm_s:
        for k in range(nblk):
            slab_s[k, 0:r, :] = o5[:, k * cin:(k + 1) * cin]
        for b in range(nb):
            for t0 in range(0, tl, pack):
                for k in range(nblk):
                    lo = slab_s[k, pl.ds(t0 * nb + b, SUBLANES, stride=nb), :]
                    hi = slab_s[k, pl.ds((t0 + SUBLANES) * nb + b, SUBLANES, stride=nb), :]
                    o5_ref[b, t0:t0 + pack, k * cin:(k + 1) * cin] = (
                        jnp.concatenate([lo, hi], axis=0).astype(BF16))
    else:
        o5_ref[...] = o5.astype(BF16)


def _s5(h, wu, wb, wcre, wcim, are, aim, dsk, wglu, bglu, s0re, s0im, nb, tl, casts=()):
    width = wu.shape[1]
    nstate = are.shape[1]
    r = nb * tl
    batch_major = h.ndim == 3
    if batch_major:
        steps = h.shape[1] // tl
        blk = lambda w: pl.BlockSpec((nb, tl, w), lambda i: (0, i, 0))
        o_shape = (nb, h.shape[1], width)
        extra = [pltpu.VMEM((width // LANES, nb * (tl + SUBLANES), LANES), F32)]
    else:
        steps = h.shape[0] // r
        blk = lambda w: pl.BlockSpec((r, w), lambda i: (i, 0))
        o_shape = (h.shape[0], width)
        extra = []
    kern = functools.partial(_s5_kernel, nb=nb, tl=tl, n_cast=len(casts))
    c_in, c_out, c_shapes = _cast_specs(casts, steps)
    return pl.pallas_call(
        kern,
        grid=(steps,),
        in_specs=[blk(h.shape[-1]), _const_spec(wu.shape), _const_spec(wb.shape),
                  _const_spec(wcre.shape), _const_spec(wcim.shape), _const_spec(are.shape),
                  _const_spec(aim.shape), _const_spec(dsk.shape), _const_spec(wglu.shape),
                  _const_spec(bglu.shape), _const_spec(s0re.shape), _const_spec(s0im.shape)]
                 + c_in,
        out_specs=[blk(width), pl.BlockSpec((nb, nstate), lambda i: (0, 0)),
                   pl.BlockSpec((nb, nstate), lambda i: (0, 0))] + c_out,
        out_shape=[jax.ShapeDtypeStruct(o_shape, BF16),
                   jax.ShapeDtypeStruct((nb, nstate), F32),
                   jax.ShapeDtypeStruct((nb, nstate), F32)] + c_shapes,
        scratch_shapes=[pltpu.VMEM((r, width), F32), pltpu.VMEM((r, nstate), F32),
                        pltpu.VMEM((r, nstate), F32), pltpu.VMEM((r, width), F32)] + extra,
        compiler_params=_params(("arbitrary",)),
        name="s5",
    )(h, wu, wb, wcre, wcim, are, aim, dsk, wglu, bglu, s0re, s0im, *casts)


def _split2(x):
    hi = x.astype(BF16)
    lo = (x - hi.astype(F32)).astype(BF16)
    return jnp.concatenate([hi, lo], axis=1)


def _split3(x):
    hi = x.astype(BF16)
    r1 = x - hi.astype(F32)
    mid = r1.astype(BF16)
    lo = (r1 - mid.astype(F32)).astype(BF16)
    return jnp.concatenate([hi, mid, lo], axis=1)


HEAD_REP = 32


def _pack2(x):
    grp = lax.broadcasted_iota(jnp.int32, x.shape, 1) // HEAD_REP
    hi = x.astype(BF16).astype(F32)
    return jnp.where(grp == 0, hi, x - hi).astype(BF16)


def _pack3(x):
    grp = lax.broadcasted_iota(jnp.int32, x.shape, 1) // HEAD_REP
    hi = x.astype(BF16).astype(F32)
    r1 = x - hi
    mid = r1.astype(BF16).astype(F32)
    return jnp.where(grp == 0, hi, jnp.where(grp == 1, mid, r1 - mid)).astype(BF16)


def _group_norm(y, zg, nw, gw):
    outs = []
    for g in range(SSD_GROUPS):
        seg = y[:, g * gw:(g + 1) * gw] * zg[:, g * gw:(g + 1) * gw]
        outs.append(seg * lax.rsqrt(jnp.mean(seg * seg, axis=-1, keepdims=True) + EPS))
    return jnp.concatenate(outs, axis=1) * nw


def _ssds_pre_kernel(h_ref, wz_ref, wx_ref, wdt_ref, cw_ref, cb_ref, dtb_ref, alog_ref, dsk_ref,
                     conv0_ref, g2_ref, exp_ref, c_o, b_o, xdd_o, yp_o, eac_o, zg_o, cdec_o, cv_o):
    nbt = conv0_ref.shape[1]
    seq = h_ref.shape[0] // nbt
    width = zg_o.shape[1]
    nbc = c_o.shape[1]
    hb = h_ref[...]
    zg_o[...] = _silu(_dot(hb, wz_ref[...]))
    dt = _softplus(_dot(hb, wdt_ref[...]) + dtb_ref[...])
    la = dt * (-jnp.exp(alog_ref[...]))
    xbc = _dot(hb, wx_ref[...])
    rows = [slice(t * nbt, (t + 1) * nbt) for t in range(seq)]
    full = [conv0_ref[k] for k in range(SSD_CONV - 1)] + [xbc[r] for r in rows]
    for k in range(SSD_CONV - 1):
        cv_o[k] = full[seq + k]
    acums = []
    for t in range(seq):
        acums.append(la[rows[t]] if t == 0 else acums[-1] + la[rows[t]])
    tot = acums[-1]
    cdec_o[...] = jnp.exp(tot)
    xs, bq, cq = [], [], []
    for t in range(seq):
        acc = cb_ref[...]
        for k in range(SSD_CONV):
            acc = acc + cw_ref[k:k + 1, :] * full[t + k]
        xc = _silu(acc)
        xs.append(xc[:, :width])
        bq.append(xc[:, width:width + nbc].astype(BF16).astype(F32))
        cq.append(xc[:, width + nbc:].astype(BF16).astype(F32))
    for t in range(seq):
        r = rows[t]
        b_o[r, :] = bq[t]
        c_o[r, :] = cq[t]
        xdd_o[r, :] = xs[t] * _dot(_split2(dt[r] * jnp.exp(tot - acums[t])), exp_ref[...])
        eac_o[r, :] = _dot(_split2(jnp.exp(acums[t])), exp_ref[...])
        yp = dsk_ref[...] * xs[t]
        for s in range(t + 1):
            cbx = _dot(_split2(cq[t] * bq[s]), g2_ref[...])
            coef = cbx * jnp.exp(acums[t] - acums[s]) * dt[rows[s]]
            yp = yp + _dot(_split2(coef), exp_ref[...]) * xs[s]
        yp_o[r, :] = yp


def _ssds_state_kernel(c_ref, b_ref, xdd_ref, yp_ref, eac_ref, zg_ref, cdec_ref, st_ref, nw_ref,
                       o_ref, sto_ref, yoff_s, xddt_s):
    seq, nb, width = xdd_ref.shape
    nheads, hd_dim, nst = st_ref.shape[1:]
    hpg = nheads // SSD_GROUPS
    gw = width // SSD_GROUPS
    rws = seq * nb
    cst = c_ref[...].reshape(rws, c_ref.shape[2]).astype(BF16)
    bst = b_ref[...].reshape(rws, b_ref.shape[2])
    xdd = xdd_ref[...].reshape(rws, width)
    xddt_s[...] = jnp.concatenate([xdd, jnp.zeros((LANES - rws, width), F32)], axis=0).T.astype(BF16)
    rowb = lax.broadcasted_iota(jnp.int32, (rws, gw), 0) % nb
    zpad = jnp.zeros((LANES - rws, nst), BF16)
    for b in range(nb):
        mine = rowb == b
        for g in range(SSD_GROUPS):
            gl = slice(g * gw, (g + 1) * gw)
            hg = st_ref[b, g * hpg:(g + 1) * hpg].reshape(gw, nst)
            res = _dot_nt(cst[:, g * nst:(g + 1) * nst], hg.astype(BF16))
            yoff_s[:, gl] = jnp.where(mine, res, 0.0) if b == 0 else jnp.where(mine, res, yoff_s[:, gl])
            bm = jnp.where(mine[:, :nst], bst[:, g * nst:(g + 1) * nst], 0.0).astype(BF16)
            upd = _dot(xddt_s[gl, :], jnp.concatenate([bm, zpad], axis=0))
            for jh in range(hpg):
                hd = g * hpg + jh
                hrows = slice(jh * hd_dim, (jh + 1) * hd_dim)
                sto_ref[b, hd] = hg[hrows] * cdec_ref[b, hd] + upd[hrows]
    y = yp_ref[...].reshape(rws, width) + yoff_s[...] * eac_ref[...].reshape(rws, width)
    o = _group_norm(y, zg_ref[...].reshape(rws, width), nw_ref[...], gw)
    o_ref[...] = o.reshape(seq, nb, width)


def _ssds(h_tm, wz, wx, wdt, cw, cb, dtb, alog, dsk, nw, ssd0, conv0_tm, nb):
    m, d = h_tm.shape
    nseq, nheads, hd_dim, nst = ssd0.shape
    seq = m // nseq
    width = wz.shape[1]
    xbc_w = wx.shape[1]
    nbc = SSD_GROUPS * nst
    hpg = nheads // SSD_GROUPS
    head = jnp.arange(LANES)
    exp1 = (head[:, None] == (jnp.arange(width) // hd_dim)[None, :]).astype(BF16)
    g1 = ((jnp.arange(nbc) // nst)[:, None] == (head // hpg)[None, :]) & (head < nheads)[None, :]
    exp_m = jnp.concatenate([exp1, exp1], axis=0)
    g2 = jnp.concatenate([g1, g1], axis=0).astype(BF16)
    full = lambda shape: pl.BlockSpec(shape, lambda: (0,) * len(shape))
    ins = (h_tm, wz, wx, wdt, cw, cb, dtb, alog, dsk, conv0_tm, g2, exp_m)
    outs = [jax.ShapeDtypeStruct((m, nbc), F32), jax.ShapeDtypeStruct((m, nbc), F32),
            jax.ShapeDtypeStruct((m, width), F32), jax.ShapeDtypeStruct((m, width), F32),
            jax.ShapeDtypeStruct((m, width), F32), jax.ShapeDtypeStruct((m, width), F32),
            jax.ShapeDtypeStruct((nseq, LANES), F32),
            jax.ShapeDtypeStruct((SSD_CONV - 1, nseq, xbc_w), F32)]
    c, b, xdd, yp, eac, zg, cdec, conv1_tm = pl.pallas_call(
        _ssds_pre_kernel,
        in_specs=[full(a.shape) for a in ins],
        out_specs=[full(o.shape) for o in outs],
        out_shape=outs,
        compiler_params=pltpu.CompilerParams(vmem_limit_bytes=VMEM_LIMIT),
        name="ssd_sample_pre",
    )(*ins)

    tm3 = lambda a: a.reshape(seq, nseq, a.shape[1])
    blk = lambda w: pl.BlockSpec((seq, nb, w), lambda i: (0, i, 0))
    st_spec = pl.BlockSpec((nb, nheads, hd_dim, nst), lambda i: (i, 0, 0, 0))
    o, ssd1 = pl.pallas_call(
        _ssds_state_kernel,
        grid=(nseq // nb,),
        in_specs=[blk(nbc), blk(nbc), blk(width), blk(width), blk(width), blk(width),
                  pl.BlockSpec((nb, LANES), lambda i: (i, 0), memory_space=pltpu.SMEM),
                  st_spec, _const_spec(nw.shape)],
        out_specs=[blk(width), st_spec],
        out_shape=[jax.ShapeDtypeStruct((seq, nseq, width), F32),
                   jax.ShapeDtypeStruct(ssd0.shape, F32)],
        scratch_shapes=[pltpu.VMEM((seq * nb, width), F32), pltpu.VMEM((width, LANES), BF16)],
        compiler_params=_params(("arbitrary",)),
        name="ssd_sample_state",
    )(tm3(c), tm3(b), tm3(xdd), tm3(yp), tm3(eac), tm3(zg), cdec, ssd0, nw)
    return o.reshape(m, width), ssd1, conv1_tm


def _ssdp_kernel(h_ref, wz_ref, wx_ref, wdt_ref, cw_ref, cb_ref, dtb_ref, alog_ref, dsk_ref,
                 nw_ref, btril_ref, exp_ref, perm_ref, permt_ref, o_ref, st_ref, cv_ref,
                 xbc_s, zg_s, acum_s, acumt_s, e2_s, e3_s, xd_s, xdd_s, ysk_s, b_s, c_s,
                 cdec_s, ht_s, tail_s, m_s, *, nch, col_chunk):
    t = SSD_CHUNK
    r = nch * t
    j = pl.program_id(1)
    nheads, hd_dim, nst = st_ref.shape[1:]
    width = nheads * hd_dim
    gw = width // SSD_GROUPS
    hpg = nheads // SSD_GROUPS
    xbc_w = xbc_s.shape[1]
    ntap = SSD_CONV - 1
    pre = ntap * SUBLANES
    ext = pre + t
    steps = t // SUBLANES
    last_rows = [t - 1 - SUBLANES * (ntap - 1 - v) for v in range(ntap)]

    @pl.when(j == 0)
    def _():
        ht_s[...] = jnp.zeros(ht_s.shape, F32)
        tail_s[...] = jnp.zeros(tail_s.shape, F32)

    hb = jnp.concatenate([_dot(perm_ref[...], h_ref[c * t:(c + 1) * t, :]) for c in range(nch)],
                         axis=0).astype(BF16)
    dt = _softplus(_dot(hb, wdt_ref[...]) + dtb_ref[...])
    la = dt * (-jnp.exp(alog_ref[...]))

    def xbc_cols(jc):
        lanes = slice(jc * col_chunk, (jc + 1) * col_chunk)
        xall = _dot(hb, wx_ref[:, lanes])
        for c in range(nch):
            xbc_s[c * ext + pre:(c + 1) * ext, lanes] = xall[c * t:(c + 1) * t, :]
        for c in range(nch):
            base = c * ext
            xbc_s[base:base + pre, lanes] = xbc_s[base + t - 1:base + pre + t - 1, lanes]
            for v in range(ntap):
                if c == 0:
                    prev = tail_s[v:v + 1, lanes]
                else:
                    prow = base - ext + pre + last_rows[v]
                    prev = xbc_s[prow:prow + 1, lanes]
                xbc_s[base + v * SUBLANES:base + v * SUBLANES + 1, lanes] = prev
            acc = cb_ref[:, lanes]
            for k in range(SSD_CONV):
                acc = acc + cw_ref[k:k + 1, lanes] * xbc_s[base + k * SUBLANES:
                                                           base + k * SUBLANES + t, lanes]
            xc = _silu(acc)
            rows = slice(c * t, (c + 1) * t)
            lo = jc * col_chunk
            if lo < width:
                xd_s[rows, lanes] = xc.astype(BF16)
                xdd_s[rows, lanes] = (xc * e2_s[rows, lanes]).astype(BF16)
                ysk_s[rows, lanes] = xc * dsk_ref[:, lanes]
            elif lo < width + SSD_GROUPS * nst:
                b_s[rows, lo - width:lo - width + col_chunk] = xc.astype(BF16)
            else:
                cl = lo - width - SSD_GROUPS * nst
                c_s[rows, cl:cl + col_chunk] = xc.astype(BF16)

    x_cols = width // col_chunk
    for jc in range(x_cols, xbc_w // col_chunk):
        xbc_cols(jc)

    r3 = _dot(btril_ref[...], _split3(la))
    acum = r3[:, :LANES] + r3[:, LANES:2 * LANES] + r3[:, 2 * LANES:]
    tots = [acum[(c + 1) * t - 1:(c + 1) * t, :] for c in range(nch)]
    arow = acum - jnp.log(dt)
    for c in range(nch):
        acumt_s[c] = arow[c * t:(c + 1) * t, :].T
    tot_rows = jnp.concatenate([jnp.broadcast_to(v, (t, LANES)) for v in tots], axis=0)
    dstate = jnp.exp(tot_rows - acum)
    cdec = jnp.exp(jnp.concatenate([jnp.broadcast_to(v, (SUBLANES, LANES)) for v in tots], axis=0))
    e2_s[...] = _dot(_pack2(dt * dstate), exp_ref[...])
    e3_s[...] = _dot(_pack2(jnp.exp(acum)), exp_ref[...])
    cdx = _dot(_pack2(cdec), exp_ref[...])
    for c in range(nch):
        cdec_s[c] = cdx[c * SUBLANES:(c + 1) * SUBLANES, :]
    acum_s[...] = acum

    def token(i):
        return (i % SUBLANES) * steps + i // SUBLANES

    causal = (token(lax.broadcasted_iota(jnp.int32, (t, t), 0))
              >= token(lax.broadcasted_iota(jnp.int32, (t, t), 1)))
    for c in range(nch):
        rows = slice(c * t, (c + 1) * t)
        for g in range(SSD_GROUPS):
            cbm = jnp.where(causal, _dot_nt(c_s[rows, g * nst:(g + 1) * nst],
                                            b_s[rows, g * nst:(g + 1) * nst]), 0.0)
            for hd in range(g * hpg, (g + 1) * hpg):
                diff = acum_s[rows, hd:hd + 1] - acumt_s[c, hd:hd + 1, :]
                m_s[(c * nheads + hd) * t:(c * nheads + hd + 1) * t, :] = (
                    cbm * jnp.exp(jnp.where(causal, diff, -jnp.inf))).astype(BF16)

    for jc in range(x_cols):
        lanes = slice(jc * col_chunk, (jc + 1) * col_chunk)
        xbc_cols(jc)
        zg_s[:, lanes] = _silu(_dot(hb, wz_ref[:, lanes]))

    for v in range(ntap):
        row = (nch - 1) * ext + pre + last_rows[v]
        tail_s[v:v + 1, :] = xbc_s[row:row + 1, :]
    cv_ref[0] = tail_s[...]

    lane = lax.broadcasted_iota(jnp.int32, (t, LANES), 1)
    keep_lo = jnp.where(lane < hd_dim, 1.0, 0.0).astype(BF16)
    keep_hi = jnp.where(lane < hd_dim, 0.0, 1.0).astype(BF16)
    nw = nw_ref[...]

    for c in range(nch):
        rows = slice(c * t, (c + 1) * t)
        ys = []
        for g in range(SSD_GROUPS):
            gl = slice(g * gw, (g + 1) * gw)
            cg = c_s[rows, g * nst:(g + 1) * nst]
            bg = b_s[rows, g * nst:(g + 1) * nst]
            pairs = []
            for jp in range(hpg // 2):
                ha = g * hpg + 2 * jp
                m2 = jnp.concatenate([m_s[(c * nheads + hd) * t:(c * nheads + hd + 1) * t, :]
                                      for hd in (ha, ha + 1)], axis=1)
                xdp = xd_s[rows, ha * hd_dim:ha * hd_dim + LANES]
                rhs = jnp.concatenate([xdp * keep_lo, xdp * keep_hi], axis=0)
                pairs.append(_dot(m2, rhs))
            htg = ht_s[:, gl]
            y = (jnp.concatenate(pairs, axis=1) + _dot(cg, htg.astype(BF16)) * e3_s[rows, gl]
                 + ysk_s[rows, gl])
            ys.append(y)
            upd = lax.dot_general(bg, xdd_s[rows, gl], (((0,), (0,)), ((), ())),
                                  preferred_element_type=F32)
            ht_s[:, gl] = htg * cdec_s[c, 0:1, gl] + upd
        o = _group_norm(jnp.concatenate(ys, axis=1), zg_s[rows, :], nw, gw).astype(BF16)
        o_ref[rows, :] = _dot(permt_ref[...], o).astype(BF16)

    @pl.when(j == pl.num_programs(1) - 1)
    def _():
        for pr in range(nheads // 2):
            tt = ht_s[:, pr * LANES:(pr + 1) * LANES].T
            st_ref[0, 2 * pr] = tt[:hd_dim]
            st_ref[0, 2 * pr + 1] = tt[hd_dim:]


def _ssdp(h, wz, wx, wdt, cw, cb, dtb, alog, dsk, nw, nseq, nheads, nch):
    m, d = h.shape
    width = wz.shape[1]
    xbc_w = wx.shape[1]
    hd_dim = width // nheads
    nst = SSD_STATE
    t = SSD_CHUNK
    r = nch * t
    nblk = m // (nseq * r)
    ri = jnp.arange(r)
    steps = t // SUBLANES
    tok = (ri % t % SUBLANES) * steps + ri % t // SUBLANES
    btril = ((tok[:, None] >= tok[None, :]) & (ri[:, None] // t == ri[None, :] // t)).astype(BF16)
    perm = (tok[:t, None] == jnp.arange(t)[None, :]).astype(BF16)
    lane = jnp.arange(LANES)
    head = (lane % HEAD_REP)[:, None]
    exp_m = ((head == (jnp.arange(width) // hd_dim)[None, :])
             & (lane < 2 * HEAD_REP)[:, None]).astype(BF16)
    row = lambda w: pl.BlockSpec((r, w), lambda i, j: (i * nblk + j, 0))
    kern = functools.partial(_ssdp_kernel, nch=nch, col_chunk=4 * LANES)
    vm = lambda shape, dt: pltpu.VMEM(shape, dt)
    return pl.pallas_call(
        kern,
        grid=(nseq, nblk),
        in_specs=[row(d), _const_spec(wz.shape), _const_spec(wx.shape), _const_spec(wdt.shape),
                  _const_spec(cw.shape), _const_spec(cb.shape), _const_spec(dtb.shape),
                  _const_spec(alog.shape), _const_spec(dsk.shape), _const_spec(nw.shape),
                  _const_spec(btril.shape), _const_spec(exp_m.shape),
                  _const_spec(perm.shape), _const_spec(perm.shape)],
        out_specs=[row(width),
                   pl.BlockSpec((1, nheads, hd_dim, nst), lambda i, j: (i, 0, 0, 0)),
                   pl.BlockSpec((1, SSD_CONV - 1, xbc_w), lambda i, j: (i, 0, 0))],
        out_shape=[jax.ShapeDtypeStruct((m, width), BF16),
                   jax.ShapeDtypeStruct((nseq, nheads, hd_dim, nst), F32),
                   jax.ShapeDtypeStruct((nseq, SSD_CONV - 1, xbc_w), F32)],
        scratch_shapes=[vm((nch * (t + (SSD_CONV - 1) * SUBLANES), xbc_w), F32),
                        vm((r, width), F32),
                        vm((r, LANES), F32), vm((nch, LANES, t), F32),
                        vm((r, width), F32), vm((r, width), F32),
                        vm((r, width), BF16), vm((r, width), BF16), vm((r, width), F32),
                        vm((r, SSD_GROUPS * nst), BF16), vm((r, SSD_GROUPS * nst), BF16),
                        vm((nch, SUBLANES, width), F32), vm((nst, width), F32),
                        vm((SSD_CONV - 1, xbc_w), F32), vm((nch * nheads * t, t), BF16)],
        compiler_params=_params(("arbitrary", "arbitrary")),
        name="ssd_prompt",
    )(h, wz, wx, wdt, cw, cb, dtb, alog, dsk, nw, btril, exp_m, perm, perm.T)


def _rep_heads(v):
    grp = jnp.pad(v, [(0, 0)] * (v.ndim - 1) + [(0, HEAD_REP - v.shape[-1])])
    tail = jnp.zeros(v.shape[:-1] + (LANES - 3 * HEAD_REP,), v.dtype)
    return jnp.concatenate([grp, grp, grp, tail], axis=-1)


def _s5_call(h, s5re0, s5im0, w, nb, tl, casts=()):
    g, p = s5re0.shape[1:]
    o5, re1, im1, *cast = _s5(h, w["w_u"], w["s5_wb"], w["s5_wcre"], w["s5_wcim"], w["s5_are"],
                              w["s5_aim"], w["s5_d"], w["s5_wglu"], w["s5_bglu"],
                              s5re0.reshape(nb, g * p), s5im0.reshape(nb, g * p), nb, tl, casts)
    return o5, re1.reshape(nb, g, p), im1.reshape(nb, g, p), cast


def _layer(xp, xs, s5re0, s5im0, ssd0, conv0, w, *, tm, s5_tl, ssd_nch, ssd_nb):
    bp, lp, d = xp.shape
    bs, ls, _ = xs.shape
    mp, ms = bp * lp, bs * ls
    g, p = s5re0.shape[1:]
    nheads = ssd0.shape[1]

    xs_tm = xs.transpose(1, 0, 2).reshape(ms, d)
    x1p, hp, x1s, hs, w_u, w_z, w_x, w_dt = _ffn1(
        xp.reshape(mp, d), xs_tm, w["ffn1_norm"], w["ffn1_wg"], w["ffn1_wu"], w["ffn1_wd"],
        w["mix_norm"], w["w_in"], w["w_in_splits"], tm)
    w = dict(w, w_u=w_u, w_z=w_z, w_x=w_x, w_dt=_rep_heads(w_dt))

    zeros = jnp.zeros((bp, g, p), F32)
    o5p, p_re, p_im, (wg2, wu2, wd2, wo) = _s5_call(
        hp.reshape(bp, lp, d), zeros, zeros, w, bp, s5_tl,
        (w["ffn2_wg"], w["ffn2_wu"], w["ffn2_wd"], w["w_out"]))
    osp, p_ssd, p_conv = _ssdp(hp, w["w_z"], w["w_x"], w["w_dt"], w["conv_w"], w["conv_b"],
                               w["dt_bias"], w["a_log"], w["ssd_d"], w["ssd_norm"],
                               bp, nheads, ssd_nch)

    o5s, s_re, s_im, _ = _s5_call(hs, s5re0, s5im0, w, bs, ls)
    oss, s_ssd, s_conv_tm = _ssds(hs, w["w_z"], w["w_x"], w["w_dt"], w["conv_w"], w["conv_b"],
                                  w["dt_bias"], w["a_log"], w["ssd_d"], w["ssd_norm"], ssd0,
                                  conv0.transpose(1, 0, 2), ssd_nb)

    yp, ys = _out_ffn2(x1p, o5p.reshape(mp, -1), osp, x1s, o5s, oss, wo, w["ffn2_norm"],
                       wg2, wu2, wd2, w["final_norm"], tm)
    return (yp.reshape(bp, lp, d), ys.reshape(ls, bs, d).transpose(1, 0, 2),
            p_re, p_im, p_ssd, p_conv, s_re, s_im, s_ssd, s_conv_tm.transpose(1, 0, 2))


def kernel(x_prompt, x_sample, state_s5_re, state_s5_im, state_ssd, state_conv, ffn1_norm, ffn1_w_gate, ffn1_w_up, ffn1_w_down, mix_norm, w_in, s5_lambda_re, s5_lambda_im, s5_log_step, s5_b_re, s5_b_im, s5_c_re, s5_c_im, s5_d, s5_w_glu, s5_b_glu, ssd_conv_w, ssd_conv_b, ssd_dt_bias, ssd_a_log, ssd_d, ssd_norm, w_out, ffn2_norm, ffn2_w_gate, ffn2_w_up, ffn2_w_down, final_norm):
    depth = w_in.shape[0]
    assert depth == 1, "single-layer stack"
    i = 0
    d = x_prompt.shape[-1]
    g, p = s5_lambda_re.shape[1:]
    s5_width = g * S5_GROUP
    nheads = ssd_a_log.shape[1]
    ssd_width = nheads * SSD_HEAD_DIM
    xbc_w = ssd_conv_w.shape[-1]
    c0, c1, c2 = s5_width, s5_width + ssd_width, s5_width + ssd_width + xbc_w
    row = lambda v: v.astype(F32).reshape(1, -1)

    are, aim, bbre, bbim = _s5_params(s5_lambda_re[i], s5_lambda_im[i], s5_log_step[i],
                                      s5_b_re[i], s5_b_im[i])
    nblk = s5_width // LANES
    wb = jnp.concatenate([_block_diag(bbre, nblk), _block_diag(bbim, nblk)],
                         axis=-1).astype(BF16)
    wcre = _block_diag(s5_c_re[i].astype(F32).transpose(0, 2, 1), nblk).astype(BF16)
    wcim = _block_diag(s5_c_im[i].astype(F32).transpose(0, 2, 1), nblk).astype(BF16)

    w = {
        "ffn1_norm": row(ffn1_norm[i]), "mix_norm": row(mix_norm[i]),
        "ffn2_norm": row(ffn2_norm[i]), "final_norm": row(final_norm),
        "ffn1_wg": ffn1_w_gate.astype(F32), "ffn1_wu": ffn1_w_up.astype(F32),
        "ffn1_wd": ffn1_w_down.astype(F32),
        "w_in": w_in.astype(F32), "w_in_splits": (c0, c1 - c0, c2 - c1, nheads),
        "ffn2_wg": ffn2_w_gate.astype(F32), "ffn2_wu": ffn2_w_up.astype(F32),
        "ffn2_wd": ffn2_w_down.astype(F32), "w_out": w_out.astype(F32),
        "s5_wb": wb, "s5_wcre": wcre, "s5_wcim": wcim, "s5_are": are, "s5_aim": aim,
        "s5_d": row(s5_d[i]), "s5_wglu": s5_w_glu[i].astype(BF16), "s5_bglu": row(s5_b_glu[i]),
        "conv_w": ssd_conv_w[i].astype(F32), "conv_b": row(ssd_conv_b[i]),
        "dt_bias": _rep_heads(row(ssd_dt_bias[i])), "a_log": _rep_heads(row(ssd_a_log[i])),
        "ssd_d": jnp.repeat(ssd_d[i].astype(F32), SSD_HEAD_DIM).reshape(1, -1),
        "ssd_norm": row(ssd_norm[i]),
    }

    lp = x_prompt.shape[1]
    yp, ys, *states = _layer(
        x_prompt, x_sample, state_s5_re[i], state_s5_im[i], state_ssd[i], state_conv[i], w,
        tm=512, s5_tl=min(lp, 128), ssd_nch=min(lp // SSD_CHUNK, 4), ssd_nb=SUBLANES)
    return (yp, ys) + tuple(v[None] for v in states)
```

```python
import functools
import math

import jax
import jax.numpy as jnp
from jax import lax
from jax.experimental import pallas as pl
from jax.experimental.pallas import tpu as pltpu

F32 = jnp.float32
BF16 = jnp.bfloat16
EPS = 1e-6

LANES = 128
SUBLANES = 8
VMEM_LIMIT = 56 * 1024 * 1024

S5_GROUP = 16
S5_STATE = 64
SSD_HEAD_DIM = 64
SSD_GROUPS = 4
SSD_STATE = 128
SSD_CONV = 4
SSD_CHUNK = 128


def _sigmoid(x):
    return 0.5 + 0.5 * jnp.tanh(0.5 * x)


def _silu(x):
    hx = 0.5 * x
    return hx + hx * jnp.tanh(hx)


def _gelu_tanh(x):
    c = math.sqrt(2.0 / math.pi)
    return 0.5 * x * (1.0 + jnp.tanh(c * (x + 0.044715 * (x * x * x))))


def _softplus(x):
    return jnp.maximum(x, 0.0) + jnp.log1p(jnp.exp(-jnp.abs(x)))


def _rms(x, w):
    return x * lax.rsqrt(jnp.mean(x * x, axis=-1, keepdims=True) + EPS) * w


def _dot(a, b):
    return jnp.dot(a, b, preferred_element_type=F32)


def _dot_nt(a, b):
    return lax.dot_general(a, b, (((1,), (1,)), ((), ())), preferred_element_type=F32)


def _const_spec(shape):
    nd = len(shape)
    return pl.BlockSpec(shape, lambda *_: (0,) * nd, pipeline_mode=pl.Buffered(1))


def _params(sem):
    return pltpu.CompilerParams(dimension_semantics=sem, vmem_limit_bytes=VMEM_LIMIT)


def _swiglu_half(x, nw, wg_ref, wu_ref, wd_ref):
    hn = _rms(x, nw).astype(BF16)
    g = _dot(hn, wg_ref[...])
    u = _dot(hn, wu_ref[...])
    a = (_silu(g) * u).astype(BF16)
    return x + 0.5 * _dot(a, wd_ref[...])


def _cast_specs(ws, steps, layer=0):
    ins, outs, shapes = [], [], []
    for w in ws:
        _, nrow, ncol = w.shape
        rep = 1 if (nrow // steps) % (2 * SUBLANES) == 0 else 2
        rows = rep * nrow // steps
        assert rows * steps == rep * nrow and rows % (2 * SUBLANES) == 0
        idx = lambda i, rep=rep: jnp.minimum(i, steps - 1) // rep
        ins.append(pl.BlockSpec((None, rows, ncol), lambda i, idx=idx: (layer, idx(i), 0)))
        outs.append(pl.BlockSpec((rows, ncol), lambda i, idx=idx: (idx(i), 0)))
        shapes.append(jax.ShapeDtypeStruct((nrow, ncol), BF16))
    return ins, outs, shapes


W_CHUNKS = 8


def _weight_copy(w_hbm, layer, stage, sem, k, slot):
    rows = stage.shape[1]
    return pltpu.make_async_copy(w_hbm.at[layer, pl.ds(k * rows, rows), :], stage.at[slot],
                                 sem.at[slot])


def _load_weight_bf16(w_hbm, layer, dst, stage, sem):
    rows = stage.shape[1]
    _weight_copy(w_hbm, layer, stage, sem, 0, 0).start()
    for k in range(W_CHUNKS):
        slot = k % 2
        if k + 1 < W_CHUNKS:
            _weight_copy(w_hbm, layer, stage, sem, k + 1, 1 - slot).start()
        _weight_copy(w_hbm, layer, stage, sem, k, slot).wait()
        dst[k * rows:(k + 1) * rows, :] = stage[slot].astype(BF16)


def _ffn1_kernel(xp_ref, xs_ref, nw_ref, wg_hbm, wu_hbm, wd_hbm, mixw_ref, win_ref,
                 x1p_ref, hp_ref, x1s_ref, hs_ref, *rest, n_p, layer):
    *wsplit_refs, wg_ref, wu_ref, wd_ref, stage_a, stage_b, sem = rest
    is_p = pl.program_id(0) < n_p

    @pl.when(pl.program_id(0) == 0)
    def _():
        _load_weight_bf16(wg_hbm, layer, wg_ref, stage_a, sem)
        _load_weight_bf16(wu_hbm, layer, wu_ref, stage_a, sem)
        _load_weight_bf16(wd_hbm, layer, wd_ref, stage_b, sem)

    def tile(x_ref, x1_ref, h_ref):
        half = x_ref.shape[0] // 2
        for rows in (slice(0, half), slice(half, 2 * half)):
            x1 = _swiglu_half(x_ref[rows, :], nw_ref[...], wg_ref, wu_ref, wd_ref)
            x1_ref[rows, :] = x1
            h_ref[rows, :] = _rms(x1, mixw_ref[...]).astype(BF16)

    pl.when(is_p)(functools.partial(tile, xp_ref, x1p_ref, hp_ref))
    pl.when(jnp.logical_not(is_p))(functools.partial(tile, xs_ref, x1s_ref, hs_ref))

    lo = 0
    for o_ref in wsplit_refs:
        o_ref[...] = win_ref[:, lo:lo + o_ref.shape[1]].astype(BF16)
        lo += o_ref.shape[1]


def _ffn1(xp, xs, nw, wg, wu, wd, mixw, w_in, splits, tm):
    mp, d = xp.shape
    ms = xs.shape[0]
    assert ms == tm
    n_p = mp // tm
    dff = wg.shape[2]
    assert d % (W_CHUNKS * 2 * SUBLANES) == 0 and dff % (W_CHUNKS * 2 * SUBLANES) == 0
    rowp = lambda w: pl.BlockSpec((tm, w), lambda i: (jnp.minimum(i, n_p - 1), 0))
    rows = lambda w: pl.BlockSpec((ms, w), lambda i: (0, 0))
    hbm = pl.BlockSpec(memory_space=pl.ANY)
    layer = 0
    wrows = w_in.shape[1] // n_p
    wblk = lambda w: pl.BlockSpec((wrows, w), lambda i: (jnp.minimum(i, n_p - 1), 0))
    w_in_blk = pl.BlockSpec((None, wrows, w_in.shape[2]),
                            lambda i: (layer, jnp.minimum(i, n_p - 1), 0))
    return pl.pallas_call(
        functools.partial(_ffn1_kernel, n_p=n_p, layer=layer),
        grid=(n_p + 1,),
        in_specs=[rowp(d), rows(d), _const_spec((1, d)), hbm, hbm, hbm, _const_spec((1, d)),
                  w_in_blk],
        out_specs=[rowp(d), rowp(d), rows(d), rows(d)] + [wblk(c) for c in splits],
        out_shape=[jax.ShapeDtypeStruct((mp, d), F32), jax.ShapeDtypeStruct((mp, d), BF16),
                   jax.ShapeDtypeStruct((ms, d), F32), jax.ShapeDtypeStruct((ms, d), BF16)]
                  + [jax.ShapeDtypeStruct((w_in.shape[1], c), BF16) for c in splits],
        scratch_shapes=[pltpu.VMEM((d, dff), BF16), pltpu.VMEM((d, dff), BF16),
                        pltpu.VMEM((dff, d), BF16),
                        pltpu.VMEM((2, d // W_CHUNKS, dff), F32),
                        pltpu.VMEM((2, dff // W_CHUNKS, d), F32),
                        pltpu.SemaphoreType.DMA((2,))],
        compiler_params=_params(("arbitrary",)),
        name="ffn1",
    )(xp, xs, nw, wg, wu, wd, mixw, w_in)


def _out_ffn2_kernel(x1p_ref, o5p_ref, osp_ref, x1s_ref, o5s_ref, oss_ref, wo_ref, nw_ref,
                     wg_ref, wu_ref, wd_ref, fin_ref, yp_ref, ys_ref, *, n_p):
    is_p = pl.program_id(0) < n_p
    k5 = o5p_ref.shape[1]

    def tile(x1_ref, o5_ref, os_ref, y_ref):
        x2 = x1_ref[...] + (_dot(o5_ref[...].astype(BF16), wo_ref[:k5, :])
                            + _dot(os_ref[...].astype(BF16), wo_ref[k5:, :]))
        x3 = _swiglu_half(x2, nw_ref[...], wg_ref, wu_ref, wd_ref)
        y_ref[...] = _rms(x3, fin_ref[...])

    pl.when(is_p)(functools.partial(tile, x1p_ref, o5p_ref, osp_ref, yp_ref))
    pl.when(jnp.logical_not(is_p))(functools.partial(tile, x1s_ref, o5s_ref, oss_ref, ys_ref))


def _out_ffn2(x1p, o5p, osp, x1s, o5s, oss, wo, nw, wg, wu, wd, fin, tm):
    mp, d = x1p.shape
    ms = x1s.shape[0]
    assert ms == tm
    n_p = mp // tm
    dff = wg.shape[1]
    rowp = lambda w: pl.BlockSpec((tm, w), lambda i: (jnp.minimum(i, n_p - 1), 0))
    rows = lambda w: pl.BlockSpec((ms, w), lambda i: (0, 0))
    return pl.pallas_call(
        functools.partial(_out_ffn2_kernel, n_p=n_p),
        grid=(n_p + 1,),
        in_specs=[rowp(d), rowp(o5p.shape[1]), rowp(osp.shape[1]),
                  rows(d), rows(o5s.shape[1]), rows(oss.shape[1]),
                  _const_spec(wo.shape), _const_spec((1, d)),
                  _const_spec((d, dff)), _const_spec((d, dff)), _const_spec((dff, d)),
                  _const_spec((1, d))],
        out_specs=[rowp(d), rows(d)],
        out_shape=[jax.ShapeDtypeStruct((mp, d), F32), jax.ShapeDtypeStruct((ms, d), F32)],
        compiler_params=_params(("arbitrary",)),
        name="out_ffn2",
    )(x1p, o5p, osp, x1s, o5s, oss, wo, nw, wg, wu, wd, fin)


def _s5_param_kernel(lre_ref, lim_ref, step_ref, bre_ref, bim_ref,
                     are_ref, aim_ref, bbre_ref, bbim_ref):
    lre, lim, step = lre_ref[...], lim_ref[...], step_ref[...]
    mag = jnp.exp(lre * step)
    ang = lim * step
    are = mag * jnp.cos(ang)
    aim = mag * jnp.sin(ang)
    den = lre * lre + lim * lim
    nre, nim = are - 1.0, aim
    cre = (nre * lre + nim * lim) / den
    cim = (nim * lre - nre * lim) / den
    are_ref[...] = are
    aim_ref[...] = aim
    bre, bim = bre_ref[...], bim_ref[...]
    bbre_ref[...] = cre * bre - cim * bim
    bbim_ref[...] = cre * bim + cim * bre


def _s5_params(lam_re, lam_im, log_step, b_re, b_im):
    g, p = lam_re.shape
    hh = b_re.shape[-1]
    gp = g * p
    step = jnp.exp(log_step.astype(F32))
    step_row = jnp.broadcast_to(step[:, None], (g, p)).reshape(1, gp)
    to_lanes = lambda b: b.astype(F32).reshape(gp, hh).T
    full = lambda shape: pl.BlockSpec(shape, lambda: (0,) * len(shape))
    are, aim, bbre, bbim = pl.pallas_call(
        _s5_param_kernel,
        in_specs=[full((1, gp)), full((1, gp)), full((1, gp)), full((hh, gp)), full((hh, gp))],
        out_specs=[full((1, gp)), full((1, gp)), full((hh, gp)), full((hh, gp))],
        out_shape=[jax.ShapeDtypeStruct((1, gp), F32), jax.ShapeDtypeStruct((1, gp), F32),
                   jax.ShapeDtypeStruct((hh, gp), F32), jax.ShapeDtypeStruct((hh, gp), F32)],
        name="s5_params",
    )(lam_re.astype(F32).reshape(1, gp), lam_im.astype(F32).reshape(1, gp), step_row,
      to_lanes(b_re), to_lanes(b_im))
    to_blocks = lambda b: b.reshape(hh, g, p).transpose(1, 0, 2)
    return are, aim, to_blocks(bbre), to_blocks(bbim)


def _block_diag(w, nblk):
    g, a, b = w.shape
    gl = g // nblk
    w = w.reshape(nblk, gl, a, b)
    eye = jnp.eye(gl, dtype=w.dtype)
    out = w[:, :, :, None, :] * eye[None, :, None, :, None]
    return out.reshape(nblk, gl * a, gl * b)


def _s5_kernel(*refs, nb, tl, n_cast, ride_per):
    refs = list(refs)
    take = lambda n: [refs.pop(0) for _ in range(n)]
    (h_ref, wu_ref, wb_ref, wcre_ref, wcim_ref, are_ref, aim_ref, d_ref, wglu_ref, bglu_ref,
     s0re_ref, s0im_ref) = take(12)
    cast_in = take(n_cast)
    ride_in = take(9 if ride_per else 0)
    o5_ref, sre_ref, sim_ref = take(3)
    cast_out = take(n_cast)
    ride_out = take(2 if ride_per else 0)
    u_s, xre_s, xim_s, y_s = take(4)
    ride_s = [refs.pop() for _ in range(2 if ride_per else 0)][::-1]
    perm_s = refs
    for src, dst in zip(cast_in, cast_out):
        dst[...] = src[...].astype(BF16)
    if ride_per:
        _ssds_state_body(*ride_in, *ride_out, *ride_s, part=pl.program_id(0) % ride_per)
    width = u_s.shape[1]
    nstate = xre_s.shape[1]
    nblk = wb_ref.shape[0]
    cin = width // nblk
    cst = nstate // nblk
    r = nb * tl
    pack = 2 * SUBLANES

    @pl.when(pl.program_id(0) == 0)
    def _():
        sre_ref[...] = s0re_ref[...]
        sim_ref[...] = s0im_ref[...]

    if perm_s:
        slab_s, = perm_s
        assert cin == LANES and nb == SUBLANES
        pitch = tl + SUBLANES
        u_bt = _dot(h_ref[...].reshape(r, h_ref.shape[2]), wu_ref[...])
        for b in range(nb):
            for k in range(nblk):
                slab_s[k, b * pitch:b * pitch + tl, :] = u_bt[b * tl:(b + 1) * tl,
                                                              k * cin:(k + 1) * cin]

        for t in range(tl):
            for k in range(nblk):
                u_s[t * nb:(t + 1) * nb, k * cin:(k + 1) * cin] = (
                    slab_s[k, pl.ds(t, nb, stride=pitch), :])
        u = u_s[...]
    else:
        u = _dot(h_ref[...], wu_ref[...])
        u_s[...] = u
    ub = u.astype(BF16)

    def drive(k):
        bu = _dot(ub[:, k * cin:(k + 1) * cin], wb_ref[k])
        xre_s[:, k * cst:(k + 1) * cst] = bu[:, :cst]
        xim_s[:, k * cst:(k + 1) * cst] = bu[:, cst:]

    def scan(k):
        lanes = slice(k * cst, (k + 1) * cst)
        a_re = jnp.broadcast_to(are_ref[:, lanes], (SUBLANES, cst))
        a_im = jnp.broadcast_to(aim_ref[:, lanes], (SUBLANES, cst))

        def step(rows, s_re, s_im):
            n_re = a_re * s_re - a_im * s_im + xre_s[rows, lanes]
            n_im = a_re * s_im + a_im * s_re + xim_s[rows, lanes]
            xre_s[rows, lanes] = n_re
            xim_s[rows, lanes] = n_im
            return n_re, n_im

        if nb == SUBLANES:
            s_re, s_im = sre_ref[:, lanes], sim_ref[:, lanes]
            for t in range(tl):
                s_re, s_im = step(slice(t * nb, (t + 1) * nb), s_re, s_im)
            sre_ref[:, lanes] = s_re
            sim_ref[:, lanes] = s_im
        else:
            def group_body(sg, carry):
                srow = pl.ds(pl.multiple_of(sg * SUBLANES, SUBLANES), SUBLANES)
                st = (sre_ref[srow, lanes], sim_ref[srow, lanes])
                for t in range(tl):
                    st = step(pl.ds(pl.multiple_of(t * nb + sg * SUBLANES, SUBLANES), SUBLANES), *st)
                sre_ref[srow, lanes], sim_ref[srow, lanes] = st
                return carry

            lax.fori_loop(0, nb // SUBLANES, group_body, 0)

    def readout(k):
        st = slice(k * cst, (k + 1) * cst)
        y_s[:, k * cin:(k + 1) * cin] = (_dot(xre_s[:, st].astype(BF16), wcre_ref[k])
                                         - _dot(xim_s[:, st].astype(BF16), wcim_ref[k]))

    for k in range(nblk + 2):
        if k < nblk:
            drive(k)
        if 1 <= k <= nblk:
            scan(k - 1)
        if k >= 2:
            readout(k - 2)
    y = y_s[...] + d_ref[...] * u_s[...]
    v = _gelu_tanh(y)
    gate = _dot(v.astype(BF16), wglu_ref[...]) + bglu_ref[...]
    o5 = v * _sigmoid(gate)
    if perm_s:
        for k in range(nblk):
            slab_s[k, 0:r, :] = o5[:, k * cin:(k + 1) * cin]
        for b in range(nb):
            for t0 in range(0, tl, pack):
                for k in range(nblk):
                    lo = slab_s[k, pl.ds(t0 * nb + b, SUBLANES, stride=nb), :]
                    hi = slab_s[k, pl.ds((t0 + SUBLANES) * nb + b, SUBLANES, stride=nb), :]
                    o5_ref[b, t0:t0 + pack, k * cin:(k + 1) * cin] = (
                        jnp.concatenate([lo, hi], axis=0).astype(BF16))
    else:
        o5_ref[...] = o5.astype(BF16)


def _s5(h, wu, wb, wcre, wcim, are, aim, dsk, wglu, bglu, s0re, s0im, nb, tl, casts=(),
        ride=None):
    width = wu.shape[1]
    nstate = are.shape[1]
    r = nb * tl
    batch_major = h.ndim == 3
    if batch_major:
        steps = h.shape[1] // tl
        blk = lambda w: pl.BlockSpec((nb, tl, w), lambda i: (0, i, 0))
        o_shape = (nb, h.shape[1], width)
        extra = [pltpu.VMEM((width // LANES, nb * (tl + SUBLANES), LANES), F32)]
    else:
        steps = h.shape[0] // r
        blk = lambda w: pl.BlockSpec((r, w), lambda i: (i, 0))
        o_shape = (h.shape[0], width)
        extra = []
    c_in, c_out, c_shapes = _cast_specs(casts, steps)
    r_in, r_out, r_shapes, r_scratch, ride_per = (
        _ride_specs(ride, steps) if ride else ([], [], [], [], 0))
    kern = functools.partial(_s5_kernel, nb=nb, tl=tl, n_cast=len(casts), ride_per=ride_per)
    return pl.pallas_call(
        kern,
        grid=(steps,),
        in_specs=[blk(h.shape[-1]), _const_spec(wu.shape), _const_spec(wb.shape),
                  _const_spec(wcre.shape), _const_spec(wcim.shape), _const_spec(are.shape),
                  _const_spec(aim.shape), _const_spec(dsk.shape), _const_spec(wglu.shape),
                  _const_spec(bglu.shape), _const_spec(s0re.shape), _const_spec(s0im.shape)]
                 + c_in + r_in,
        out_specs=[blk(width), pl.BlockSpec((nb, nstate), lambda i: (0, 0)),
                   pl.BlockSpec((nb, nstate), lambda i: (0, 0))] + c_out + r_out,
        out_shape=[jax.ShapeDtypeStruct(o_shape, BF16),
                   jax.ShapeDtypeStruct((nb, nstate), F32),
                   jax.ShapeDtypeStruct((nb, nstate), F32)] + c_shapes + r_shapes,
        scratch_shapes=[pltpu.VMEM((r, width), F32), pltpu.VMEM((r, nstate), F32),
                        pltpu.VMEM((r, nstate), F32), pltpu.VMEM((r, width), F32)] + extra
                       + r_scratch,
        compiler_params=_params(("arbitrary",)),
        name="s5",
    )(h, wu, wb, wcre, wcim, are, aim, dsk, wglu, bglu, s0re, s0im, *casts, *(ride or ()))


def _split2(x):
    hi = x.astype(BF16)
    lo = (x - hi.astype(F32)).astype(BF16)
    return jnp.concatenate([hi, lo], axis=1)


def _split3(x):
    hi = x.astype(BF16)
    r1 = x - hi.astype(F32)
    mid = r1.astype(BF16)
    lo = (r1 - mid.astype(F32)).astype(BF16)
    return jnp.concatenate([hi, mid, lo], axis=1)


HEAD_REP = 32
RIDE_NB = SUBLANES


def _pack2(x):
    grp = lax.broadcasted_iota(jnp.int32, x.shape, 1) // HEAD_REP
    hi = x.astype(BF16).astype(F32)
    return jnp.where(grp == 0, hi, x - hi).astype(BF16)


def _pack3(x):
    grp = lax.broadcasted_iota(jnp.int32, x.shape, 1) // HEAD_REP
    hi = x.astype(BF16).astype(F32)
    r1 = x - hi
    mid = r1.astype(BF16).astype(F32)
    return jnp.where(grp == 0, hi, jnp.where(grp == 1, mid, r1 - mid)).astype(BF16)


def _group_norm(y, zg, nw, gw):
    outs = []
    for g in range(SSD_GROUPS):
        seg = y[:, g * gw:(g + 1) * gw] * zg[:, g * gw:(g + 1) * gw]
        outs.append(seg * lax.rsqrt(jnp.mean(seg * seg, axis=-1, keepdims=True) + EPS))
    return jnp.concatenate(outs, axis=1) * nw


def _ssds_pre_kernel(h_ref, wz_ref, wx_ref, wdt_ref, cw_ref, cb_ref, dtb_ref, alog_ref, dsk_ref,
                     conv0_ref, g2_ref, exp_ref, c_o, b_o, xdd_o, yp_o, eac_o, zg_o, cdec_o, cv_o):
    nbt = conv0_ref.shape[1]
    seq = h_ref.shape[0] // nbt
    width = zg_o.shape[1]
    nbc = c_o.shape[1]
    hb = h_ref[...]
    zg_o[...] = _silu(_dot(hb, wz_ref[...]))
    dt = _softplus(_dot(hb, wdt_ref[...]) + dtb_ref[...])
    la = dt * (-jnp.exp(alog_ref[...]))
    xbc = _dot(hb, wx_ref[...])
    rows = [slice(t * nbt, (t + 1) * nbt) for t in range(seq)]
    full = [conv0_ref[k] for k in range(SSD_CONV - 1)] + [xbc[r] for r in rows]
    for k in range(SSD_CONV - 1):
        cv_o[k] = full[seq + k]
    acums = []
    for t in range(seq):
        acums.append(la[rows[t]] if t == 0 else acums[-1] + la[rows[t]])
    tot = acums[-1]
    cdec_o[...] = jnp.exp(tot)
    xs, bq, cq = [], [], []
    for t in range(seq):
        acc = cb_ref[...]
        for k in range(SSD_CONV):
            acc = acc + cw_ref[k:k + 1, :] * full[t + k]
        xc = _silu(acc)
        xs.append(xc[:, :width])
        bq.append(xc[:, width:width + nbc].astype(BF16).astype(F32))
        cq.append(xc[:, width + nbc:].astype(BF16).astype(F32))
    for t in range(seq):
        r = rows[t]
        b_o[r, :] = bq[t]
        c_o[r, :] = cq[t]
        xdd_o[r, :] = xs[t] * _dot(_split2(dt[r] * jnp.exp(tot - acums[t])), exp_ref[...])
        eac_o[r, :] = _dot(_split2(jnp.exp(acums[t])), exp_ref[...])
        yp = dsk_ref[...] * xs[t]
        for s in range(t + 1):
            cbx = _dot(_split2(cq[t] * bq[s]), g2_ref[...])
            coef = cbx * jnp.exp(acums[t] - acums[s]) * dt[rows[s]]
            yp = yp + _dot(_split2(coef), exp_ref[...]) * xs[s]
        yp_o[r, :] = yp


def _ssds_state_body(c_ref, b_ref, xdd_ref, yp_ref, eac_ref, zg_ref, cdec_ref, st_ref, nw_ref,
                     o_ref, sto_ref, yoff_s, xddt_s, part):
    seq, nb, width = xdd_ref.shape
    nsub, nheads, hd_dim, nst = st_ref.shape
    hpg = nheads // SSD_GROUPS
    gw = width // SSD_GROUPS
    rws = seq * nb
    cst = c_ref[...].reshape(rws, c_ref.shape[2]).astype(BF16)
    bst = b_ref[...].reshape(rws, b_ref.shape[2])
    xdd = xdd_ref[...].reshape(rws, width)
    xddt_s[...] = jnp.concatenate([xdd, jnp.zeros((LANES - rws, width), F32)], axis=0).T.astype(BF16)
    rowb = lax.broadcasted_iota(jnp.int32, (rws, gw), 0) % nb
    zpad = jnp.zeros((LANES - rws, nst), BF16)
    for bl in range(nsub):
        b = part * nsub + bl
        mine = rowb == b
        for g in range(SSD_GROUPS):
            gl = slice(g * gw, (g + 1) * gw)
            hg = st_ref[bl, g * hpg:(g + 1) * hpg].reshape(gw, nst)
            res = _dot_nt(cst[:, g * nst:(g + 1) * nst], hg.astype(BF16))
            yoff_s[:, gl] = jnp.where(mine, res, 0.0 if bl == 0 else yoff_s[:, gl])
            bm = jnp.where(mine[:, :nst], bst[:, g * nst:(g + 1) * nst], 0.0).astype(BF16)
            upd = _dot(xddt_s[gl, :], jnp.concatenate([bm, zpad], axis=0))
            for jh in range(hpg):
                hd = g * hpg + jh
                hrows = slice(jh * hd_dim, (jh + 1) * hd_dim)
                sto_ref[bl, hd] = hg[hrows] * cdec_ref[b, hd] + upd[hrows]
    y = yp_ref[...].reshape(rws, width) + yoff_s[...] * eac_ref[...].reshape(rws, width)
    o = _group_norm(y, zg_ref[...].reshape(rws, width), nw_ref[...], gw).reshape(seq, nb, width)
    if nsub == nb:
        o_ref[...] = o
    else:
        @pl.when(part == 0)
        def _():
            o_ref[...] = o

        @pl.when(part != 0)
        def _():
            own = lax.broadcasted_iota(jnp.int32, o.shape, 1) // nsub == part
            o_ref[...] = jnp.where(own, o, o_ref[...])


def _ssds(h_tm, wz, wx, wdt, cw, cb, dtb, alog, dsk, nw, ssd0, conv0_tm):
    m, d = h_tm.shape
    nseq, nheads, hd_dim, nst = ssd0.shape
    seq = m // nseq
    width = wz.shape[1]
    xbc_w = wx.shape[1]
    nbc = SSD_GROUPS * nst
    hpg = nheads // SSD_GROUPS
    head = jnp.arange(LANES)
    exp1 = (head[:, None] == (jnp.arange(width) // hd_dim)[None, :]).astype(BF16)
    g1 = ((jnp.arange(nbc) // nst)[:, None] == (head // hpg)[None, :]) & (head < nheads)[None, :]
    exp_m = jnp.concatenate([exp1, exp1], axis=0)
    g2 = jnp.concatenate([g1, g1], axis=0).astype(BF16)
    full = lambda shape: pl.BlockSpec(shape, lambda: (0,) * len(shape))
    ins = (h_tm, wz, wx, wdt, cw, cb, dtb, alog, dsk, conv0_tm, g2, exp_m)
    outs = [jax.ShapeDtypeStruct((m, nbc), F32), jax.ShapeDtypeStruct((m, nbc), F32),
            jax.ShapeDtypeStruct((m, width), F32), jax.ShapeDtypeStruct((m, width), F32),
            jax.ShapeDtypeStruct((m, width), F32), jax.ShapeDtypeStruct((m, width), F32),
            jax.ShapeDtypeStruct((nseq, LANES), F32),
            jax.ShapeDtypeStruct((SSD_CONV - 1, nseq, xbc_w), F32)]
    c, b, xdd, yp, eac, zg, cdec, conv1_tm = pl.pallas_call(
        _ssds_pre_kernel,
        in_specs=[full(a.shape) for a in ins],
        out_specs=[full(o.shape) for o in outs],
        out_shape=outs,
        compiler_params=pltpu.CompilerParams(vmem_limit_bytes=VMEM_LIMIT),
        name="ssd_sample_pre",
    )(*ins)

    tm3 = lambda a: a.reshape(seq, nseq, a.shape[1])
    return (tm3(c), tm3(b), tm3(xdd), tm3(yp), tm3(eac), tm3(zg), cdec, ssd0, nw), conv1_tm


def _ride_specs(ride, steps):
    c, b, xdd, yp, eac, zg, cdec, ssd0, nw = ride
    seq, nseq, width = xdd.shape
    nsub = nseq // steps
    assert nsub * steps == nseq and RIDE_NB % nsub == 0
    per = RIDE_NB // nsub
    blk = lambda a: pl.BlockSpec((seq, RIDE_NB, a.shape[2]), lambda i: (0, i // per, 0))
    st_spec = pl.BlockSpec((nsub,) + ssd0.shape[1:], lambda i: (i, 0, 0, 0))
    ins = [blk(c), blk(b), blk(xdd), blk(yp), blk(eac), blk(zg),
           pl.BlockSpec((RIDE_NB, LANES), lambda i: (i // per, 0), memory_space=pltpu.SMEM),
           st_spec, _const_spec(nw.shape)]
    outs = [blk(xdd), st_spec]
    shapes = [jax.ShapeDtypeStruct((seq, nseq, width), F32), jax.ShapeDtypeStruct(ssd0.shape, F32)]
    scratch = [pltpu.VMEM((seq * RIDE_NB, width), F32), pltpu.VMEM((width, LANES), BF16)]
    return ins, outs, shapes, scratch, per


def _ssdp_kernel(h_ref, wz_ref, wx_ref, wdt_ref, cw_ref, cb_ref, dtb_ref, alog_ref, dsk_ref,
                 nw_ref, btril_ref, exp_ref, perm_ref, permt_ref, o_ref, st_ref, cv_ref,
                 xbc_s, zg_s, acum_s, acumt_s, e2_s, e3_s, xd_s, xdd_s, ysk_s, b_s, c_s,
                 cdec_s, ht_s, tail_s, m_s, *, nch, col_chunk):
    t = SSD_CHUNK
    r = nch * t
    j = pl.program_id(1)
    nheads, hd_dim, nst = st_ref.shape[1:]
    width = nheads * hd_dim
    gw = width // SSD_GROUPS
    hpg = nheads // SSD_GROUPS
    xbc_w = xbc_s.shape[1]
    ntap = SSD_CONV - 1
    pre = ntap * SUBLANES
    ext = pre + t
    steps = t // SUBLANES
    last_rows = [t - 1 - SUBLANES * (ntap - 1 - v) for v in range(ntap)]

    @pl.when(j == 0)
    def _():
        ht_s[...] = jnp.zeros(ht_s.shape, F32)
        tail_s[...] = jnp.zeros(tail_s.shape, F32)

    hb = jnp.concatenate([_dot(perm_ref[...], h_ref[c * t:(c + 1) * t, :]) for c in range(nch)],
                         axis=0).astype(BF16)
    dt = _softplus(_dot(hb, wdt_ref[...]) + dtb_ref[...])
    la = dt * (-jnp.exp(alog_ref[...]))

    def xbc_cols(jc):
        lanes = slice(jc * col_chunk, (jc + 1) * col_chunk)
        xall = _dot(hb, wx_ref[:, lanes])
        for c in range(nch):
            xbc_s[c * ext + pre:(c + 1) * ext, lanes] = xall[c * t:(c + 1) * t, :]
        for c in range(nch):
            base = c * ext
            xbc_s[base:base + pre, lanes] = xbc_s[base + t - 1:base + pre + t - 1, lanes]
            for v in range(ntap):
                if c == 0:
                    prev = tail_s[v:v + 1, lanes]
                else:
                    prow = base - ext + pre + last_rows[v]
                    prev = xbc_s[prow:prow + 1, lanes]
                xbc_s[base + v * SUBLANES:base + v * SUBLANES + 1, lanes] = prev
            acc = cb_ref[:, lanes]
            for k in range(SSD_CONV):
                acc = acc + cw_ref[k:k + 1, lanes] * xbc_s[base + k * SUBLANES:
                                                           base + k * SUBLANES + t, lanes]
            xc = _silu(acc)
            rows = slice(c * t, (c + 1) * t)
            lo = jc * col_chunk
            if lo < width:
                xd_s[rows, lanes] = xc.astype(BF16)
                xdd_s[rows, lanes] = (xc * e2_s[rows, lanes]).astype(BF16)
                ysk_s[rows, lanes] = xc * dsk_ref[:, lanes]
            elif lo < width + SSD_GROUPS * nst:
                b_s[rows, lo - width:lo - width + col_chunk] = xc.astype(BF16)
            else:
                cl = lo - width - SSD_GROUPS * nst
                c_s[rows, cl:cl + col_chunk] = xc.astype(BF16)

    x_cols = width // col_chunk
    for jc in range(x_cols, xbc_w // col_chunk):
        xbc_cols(jc)

    r3 = _dot(btril_ref[...], _split3(la))
    acum = r3[:, :LANES] + r3[:, LANES:2 * LANES] + r3[:, 2 * LANES:]
    tots = [acum[(c + 1) * t - 1:(c + 1) * t, :] for c in range(nch)]
    arow = acum - jnp.log(dt)
    for c in range(nch):
        acumt_s[c] = arow[c * t:(c + 1) * t, :].T
    tot_rows = jnp.concatenate([jnp.broadcast_to(v, (t, LANES)) for v in tots], axis=0)
    dstate = jnp.exp(tot_rows - acum)
    cdec = jnp.exp(jnp.concatenate([jnp.broadcast_to(v, (SUBLANES, LANES)) for v in tots], axis=0))
    e2_s[...] = _dot(_pack2(dt * dstate), exp_ref[...])
    e3_s[...] = _dot(_pack2(jnp.exp(acum)), exp_ref[...])
    cdx = _dot(_pack2(cdec), exp_ref[...])
    for c in range(nch):
        cdec_s[c] = cdx[c * SUBLANES:(c + 1) * SUBLANES, :]
    acum_s[...] = acum

    def token(i):
        return (i % SUBLANES) * steps + i // SUBLANES

    causal = (token(lax.broadcasted_iota(jnp.int32, (t, t), 0))
              >= token(lax.broadcasted_iota(jnp.int32, (t, t), 1)))
    for c in range(nch):
        rows = slice(c * t, (c + 1) * t)
        for g in range(SSD_GROUPS):
            cbm = jnp.where(causal, _dot_nt(c_s[rows, g * nst:(g + 1) * nst],
                                            b_s[rows, g * nst:(g + 1) * nst]), 0.0)
            for hd in range(g * hpg, (g + 1) * hpg):
                diff = acum_s[rows, hd:hd + 1] - acumt_s[c, hd:hd + 1, :]
                m_s[(c * nheads + hd) * t:(c * nheads + hd + 1) * t, :] = (
                    cbm * jnp.exp(jnp.where(causal, diff, -jnp.inf))).astype(BF16)

    for jc in range(x_cols):
        lanes = slice(jc * col_chunk, (jc + 1) * col_chunk)
        xbc_cols(jc)
        zg_s[:, lanes] = _silu(_dot(hb, wz_ref[:, lanes]))

    for v in range(ntap):
        row = (nch - 1) * ext + pre + last_rows[v]
        tail_s[v:v + 1, :] = xbc_s[row:row + 1, :]
    cv_ref[0] = tail_s[...]

    lane = lax.broadcasted_iota(jnp.int32, (t, LANES), 1)
    keep_lo = jnp.where(lane < hd_dim, 1.0, 0.0).astype(BF16)
    keep_hi = jnp.where(lane < hd_dim, 0.0, 1.0).astype(BF16)
    nw = nw_ref[...]

    for c in range(nch):
        rows = slice(c * t, (c + 1) * t)
        ys = []
        for g in range(SSD_GROUPS):
            gl = slice(g * gw, (g + 1) * gw)
            cg = c_s[rows, g * nst:(g + 1) * nst]
            bg = b_s[rows, g * nst:(g + 1) * nst]
            pairs = []
            for jp in range(hpg // 2):
                ha = g * hpg + 2 * jp
                m2 = jnp.concatenate([m_s[(c * nheads + hd) * t:(c * nheads + hd + 1) * t, :]
                                      for hd in (ha, ha + 1)], axis=1)
                xdp = xd_s[rows, ha * hd_dim:ha * hd_dim + LANES]
                rhs = jnp.concatenate([xdp * keep_lo, xdp * keep_hi], axis=0)
                pairs.append(_dot(m2, rhs))
            htg = ht_s[:, gl]
            y = (jnp.concatenate(pairs, axis=1) + _dot(cg, htg.astype(BF16)) * e3_s[rows, gl]
                 + ysk_s[rows, gl])
            ys.append(y)
            upd = lax.dot_general(bg, xdd_s[rows, gl], (((0,), (0,)), ((), ())),
                                  preferred_element_type=F32)
            ht_s[:, gl] = htg * cdec_s[c, 0:1, gl] + upd
        o = _group_norm(jnp.concatenate(ys, axis=1), zg_s[rows, :], nw, gw).astype(BF16)
        o_ref[rows, :] = _dot(permt_ref[...], o).astype(BF16)

    @pl.when(j == pl.num_programs(1) - 1)
    def _():
        for pr in range(nheads // 2):
            tt = ht_s[:, pr * LANES:(pr + 1) * LANES].T
            st_ref[0, 2 * pr] = tt[:hd_dim]
            st_ref[0, 2 * pr + 1] = tt[hd_dim:]


def _ssdp(h, wz, wx, wdt, cw, cb, dtb, alog, dsk, nw, nseq, nheads, nch):
    m, d = h.shape
    width = wz.shape[1]
    xbc_w = wx.shape[1]
    hd_dim = width // nheads
    nst = SSD_STATE
    t = SSD_CHUNK
    r = nch * t
    nblk = m // (nseq * r)
    ri = jnp.arange(r)
    steps = t // SUBLANES
    tok = (ri % t % SUBLANES) * steps + ri % t // SUBLANES
    btril = ((tok[:, None] >= tok[None, :]) & (ri[:, None] // t == ri[None, :] // t)).astype(BF16)
    perm = (tok[:t, None] == jnp.arange(t)[None, :]).astype(BF16)
    lane = jnp.arange(LANES)
    head = (lane % HEAD_REP)[:, None]
    exp_m = ((head == (jnp.arange(width) // hd_dim)[None, :])
             & (lane < 2 * HEAD_REP)[:, None]).astype(BF16)
    row = lambda w: pl.BlockSpec((r, w), lambda i, j: (i * nblk + j, 0))
    kern = functools.partial(_ssdp_kernel, nch=nch, col_chunk=4 * LANES)
    vm = lambda shape, dt: pltpu.VMEM(shape, dt)
    return pl.pallas_call(
        kern,
        grid=(nseq, nblk),
        in_specs=[row(d), _const_spec(wz.shape), _const_spec(wx.shape), _const_spec(wdt.shape),
                  _const_spec(cw.shape), _const_spec(cb.shape), _const_spec(dtb.shape),
                  _const_spec(alog.shape), _const_spec(dsk.shape), _const_spec(nw.shape),
                  _const_spec(btril.shape), _const_spec(exp_m.shape),
                  _const_spec(perm.shape), _const_spec(perm.shape)],
        out_specs=[row(width),
                   pl.BlockSpec((1, nheads, hd_dim, nst), lambda i, j: (i, 0, 0, 0)),
                   pl.BlockSpec((1, SSD_CONV - 1, xbc_w), lambda i, j: (i, 0, 0))],
        out_shape=[jax.ShapeDtypeStruct((m, width), BF16),
                   jax.ShapeDtypeStruct((nseq, nheads, hd_dim, nst), F32),
                   jax.ShapeDtypeStruct((nseq, SSD_CONV - 1, xbc_w), F32)],
        scratch_shapes=[vm((nch * (t + (SSD_CONV - 1) * SUBLANES), xbc_w), F32),
                        vm((r, width), F32),
                        vm((r, LANES), F32), vm((nch, LANES, t), F32),
                        vm((r, width), F32), vm((r, width), F32),
                        vm((r, width), BF16), vm((r, width), BF16), vm((r, width), F32),
                        vm((r, SSD_GROUPS * nst), BF16), vm((r, SSD_GROUPS * nst), BF16),
                        vm((nch, SUBLANES, width), F32), vm((nst, width), F32),
                        vm((SSD_CONV - 1, xbc_w), F32), vm((nch * nheads * t, t), BF16)],
        compiler_params=_params(("arbitrary", "arbitrary")),
        name="ssd_prompt",
    )(h, wz, wx, wdt, cw, cb, dtb, alog, dsk, nw, btril, exp_m, perm, perm.T)


def _rep_heads(v):
    grp = jnp.pad(v, [(0, 0)] * (v.ndim - 1) + [(0, HEAD_REP - v.shape[-1])])
    tail = jnp.zeros(v.shape[:-1] + (LANES - 3 * HEAD_REP,), v.dtype)
    return jnp.concatenate([grp, grp, grp, tail], axis=-1)


def _s5_call(h, s5re0, s5im0, w, nb, tl, casts=(), ride=None):
    g, p = s5re0.shape[1:]
    o5, re1, im1, *more = _s5(h, w["w_u"], w["s5_wb"], w["s5_wcre"], w["s5_wcim"], w["s5_are"],
                              w["s5_aim"], w["s5_d"], w["s5_wglu"], w["s5_bglu"],
                              s5re0.reshape(nb, g * p), s5im0.reshape(nb, g * p), nb, tl, casts,
                              ride)
    return o5, re1.reshape(nb, g, p), im1.reshape(nb, g, p), more


def _layer(xp, xs, s5re0, s5im0, ssd0, conv0, w, *, tm, s5_tl, ssd_nch):
    bp, lp, d = xp.shape
    bs, ls, _ = xs.shape
    mp, ms = bp * lp, bs * ls
    g, p = s5re0.shape[1:]
    nheads = ssd0.shape[1]

    xs_tm = xs.transpose(1, 0, 2).reshape(ms, d)
    x1p, hp, x1s, hs, w_u, w_z, w_x, w_dt = _ffn1(
        xp.reshape(mp, d), xs_tm, w["ffn1_norm"], w["ffn1_wg"], w["ffn1_wu"], w["ffn1_wd"],
        w["mix_norm"], w["w_in"], w["w_in_splits"], tm)
    w = dict(w, w_u=w_u, w_z=w_z, w_x=w_x, w_dt=_rep_heads(w_dt))

    ride, s_conv_tm = _ssds(hs, w["w_z"], w["w_x"], w["w_dt"], w["conv_w"], w["conv_b"],
                            w["dt_bias"], w["a_log"], w["ssd_d"], w["ssd_norm"], ssd0,
                            conv0.transpose(1, 0, 2))
    zeros = jnp.zeros((bp, g, p), F32)
    o5p, p_re, p_im, (wg2, wu2, wd2, wo, oss, s_ssd) = _s5_call(
        hp.reshape(bp, lp, d), zeros, zeros, w, bp, s5_tl,
        (w["ffn2_wg"], w["ffn2_wu"], w["ffn2_wd"], w["w_out"]), ride)
    oss = oss.reshape(ms, -1)
    osp, p_ssd, p_conv = _ssdp(hp, w["w_z"], w["w_x"], w["w_dt"], w["conv_w"], w["conv_b"],
                               w["dt_bias"], w["a_log"], w["ssd_d"], w["ssd_norm"],
                               bp, nheads, ssd_nch)
    o5s, s_re, s_im, _ = _s5_call(hs, s5re0, s5im0, w, bs, ls)

    yp, ys = _out_ffn2(x1p, o5p.reshape(mp, -1), osp, x1s, o5s, oss, wo, w["ffn2_norm"],
                       wg2, wu2, wd2, w["final_norm"], tm)
    return (yp.reshape(bp, lp, d), ys.reshape(ls, bs, d).transpose(1, 0, 2),
            p_re, p_im, p_ssd, p_conv, s_re, s_im, s_ssd, s_conv_tm.transpose(1, 0, 2))


def kernel(x_prompt, x_sample, state_s5_re, state_s5_im, state_ssd, state_conv, ffn1_norm, ffn1_w_gate, ffn1_w_up, ffn1_w_down, mix_norm, w_in, s5_lambda_re, s5_lambda_im, s5_log_step, s5_b_re, s5_b_im, s5_c_re, s5_c_im, s5_d, s5_w_glu, s5_b_glu, ssd_conv_w, ssd_conv_b, ssd_dt_bias, ssd_a_log, ssd_d, ssd_norm, w_out, ffn2_norm, ffn2_w_gate, ffn2_w_up, ffn2_w_down, final_norm):
    depth = w_in.shape[0]
    assert depth == 1, "single-layer stack"
    i = 0
    d = x_prompt.shape[-1]
    g, p = s5_lambda_re.shape[1:]
    s5_width = g * S5_GROUP
    nheads = ssd_a_log.shape[1]
    ssd_width = nheads * SSD_HEAD_DIM
    xbc_w = ssd_conv_w.shape[-1]
    c0, c1, c2 = s5_width, s5_width + ssd_width, s5_width + ssd_width + xbc_w
    row = lambda v: v.astype(F32).reshape(1, -1)

    are, aim, bbre, bbim = _s5_params(s5_lambda_re[i], s5_lambda_im[i], s5_log_step[i],
                                      s5_b_re[i], s5_b_im[i])
    nblk = s5_width // LANES
    wb = jnp.concatenate([_block_diag(bbre, nblk), _block_diag(bbim, nblk)],
                         axis=-1).astype(BF16)
    wcre = _block_diag(s5_c_re[i].astype(F32).transpose(0, 2, 1), nblk).astype(BF16)
    wcim = _block_diag(s5_c_im[i].astype(F32).transpose(0, 2, 1), nblk).astype(BF16)

    w = {
        "ffn1_norm": row(ffn1_norm[i]), "mix_norm": row(mix_norm[i]),
        "ffn2_norm": row(ffn2_norm[i]), "final_norm": row(final_norm),
        "ffn1_wg": ffn1_w_gate.astype(F32), "ffn1_wu": ffn1_w_up.astype(F32),
        "ffn1_wd": ffn1_w_down.astype(F32),
        "w_in": w_in.astype(F32), "w_in_splits": (c0, c1 - c0, c2 - c1, nheads),
        "ffn2_wg": ffn2_w_gate.astype(F32), "ffn2_wu": ffn2_w_up.astype(F32),
        "ffn2_wd": ffn2_w_down.astype(F32), "w_out": w_out.astype(F32),
        "s5_wb": wb, "s5_wcre": wcre, "s5_wcim": wcim, "s5_are": are, "s5_aim": aim,
        "s5_d": row(s5_d[i]), "s5_wglu": s5_w_glu[i].astype(BF16), "s5_bglu": row(s5_b_glu[i]),
        "conv_w": ssd_conv_w[i].astype(F32), "conv_b": row(ssd_conv_b[i]),
        "dt_bias": _rep_heads(row(ssd_dt_bias[i])), "a_log": _rep_heads(row(ssd_a_log[i])),
        "ssd_d": jnp.repeat(ssd_d[i].astype(F32), SSD_HEAD_DIM).reshape(1, -1),
        "ssd_norm": row(ssd_norm[i]),
    }

    lp = x_prompt.shape[1]
    yp, ys, *states = _layer(
        x_prompt, x_sample, state_s5_re[i], state_s5_im[i], state_ssd[i], state_conv[i], w,
        tm=512, s5_tl=min(lp, 64), ssd_nch=min(lp // SSD_CHUNK, 4))
    return (yp, ys) + tuple(v[None] for v in states)
```

```python
import functools
import math

import jax
import jax.numpy as jnp
from jax import lax
from jax.experimental import pallas as pl
from jax.experimental.pallas import tpu as pltpu

F32 = jnp.float32
BF16 = jnp.bfloat16
EPS = 1e-6

LANES = 128
SUBLANES = 8
VMEM_LIMIT = 56 * 1024 * 1024

S5_GROUP = 16
S5_STATE = 64
SSD_HEAD_DIM = 64
SSD_GROUPS = 4
SSD_STATE = 128
SSD_CONV = 4
SSD_CHUNK = 128


def _sigmoid(x):
    return 0.5 + 0.5 * jnp.tanh(0.5 * x)


def _silu(x):
    hx = 0.5 * x
    return hx + hx * jnp.tanh(hx)


def _gelu_tanh(x):
    c = math.sqrt(2.0 / math.pi)
    return 0.5 * x * (1.0 + jnp.tanh(c * (x + 0.044715 * (x * x * x))))


def _softplus(x):
    return jnp.maximum(x, 0.0) + jnp.log1p(jnp.exp(-jnp.abs(x)))


def _rms(x, w):
    return x * lax.rsqrt(jnp.mean(x * x, axis=-1, keepdims=True) + EPS) * w


def _dot(a, b):
    return jnp.dot(a, b, preferred_element_type=F32)


def _dot_nt(a, b):
    return lax.dot_general(a, b, (((1,), (1,)), ((), ())), preferred_element_type=F32)


def _const_spec(shape):
    nd = len(shape)
    return pl.BlockSpec(shape, lambda *_: (0,) * nd, pipeline_mode=pl.Buffered(1))


def _params(sem):
    return pltpu.CompilerParams(dimension_semantics=sem, vmem_limit_bytes=VMEM_LIMIT)


def _swiglu_half(x, nw, wg_ref, wu_ref, wd_ref):
    hn = _rms(x, nw).astype(BF16)
    g = _dot(hn, wg_ref[...])
    u = _dot(hn, wu_ref[...])
    a = (_silu(g) * u).astype(BF16)
    return x + 0.5 * _dot(a, wd_ref[...])


def _cast_specs(ws, steps, layer=0):
    ins, outs, shapes = [], [], []
    for w in ws:
        _, nrow, ncol = w.shape
        rep = 1 if (nrow // steps) % (2 * SUBLANES) == 0 else 2
        rows = rep * nrow // steps
        assert rows * steps == rep * nrow and rows % (2 * SUBLANES) == 0
        idx = lambda i, rep=rep: jnp.minimum(i, steps - 1) // rep
        ins.append(pl.BlockSpec((None, rows, ncol), lambda i, idx=idx: (layer, idx(i), 0)))
        outs.append(pl.BlockSpec((rows, ncol), lambda i, idx=idx: (idx(i), 0)))
        shapes.append(jax.ShapeDtypeStruct((nrow, ncol), BF16))
    return ins, outs, shapes


W_CHUNKS = 8


def _weight_copy(w_hbm, layer, stage, sem, k, slot):
    rows = stage.shape[1]
    return pltpu.make_async_copy(w_hbm.at[layer, pl.ds(k * rows, rows), :], stage.at[slot],
                                 sem.at[slot])


def _load_weights_bf16(jobs, layer):
    for w_hbm, _, stage, sem in jobs:
        _weight_copy(w_hbm, layer, stage, sem, 0, 0).start()
    for k in range(W_CHUNKS):
        slot = k % 2
        for w_hbm, dst, stage, sem in jobs:
            rows = stage.shape[1]
            if k + 1 < W_CHUNKS:
                _weight_copy(w_hbm, layer, stage, sem, k + 1, 1 - slot).start()
            _weight_copy(w_hbm, layer, stage, sem, k, slot).wait()
            dst[k * rows:(k + 1) * rows, :] = stage[slot].astype(BF16)


def _ffn1_kernel(xp_ref, xs_ref, nw_ref, wg_hbm, wu_hbm, wd_hbm, mixw_ref, win_ref,
                 x1p_ref, hp_ref, x1s_ref, hs_ref, *rest, n_p, layer):
    *wsplit_refs, wg_ref, wu_ref, wd_ref, stage_g, stage_u, stage_d, sem_g, sem_u, sem_d = rest
    is_p = pl.program_id(0) < n_p

    @pl.when(pl.program_id(0) == 0)
    def _():
        _load_weights_bf16([(wg_hbm, wg_ref, stage_g, sem_g), (wu_hbm, wu_ref, stage_u, sem_u),
                            (wd_hbm, wd_ref, stage_d, sem_d)], layer)

    def tile(x_ref, x1_ref, h_ref):
        half = x_ref.shape[0] // 2
        for rows in (slice(0, half), slice(half, 2 * half)):
            x1 = _swiglu_half(x_ref[rows, :], nw_ref[...], wg_ref, wu_ref, wd_ref)
            x1_ref[rows, :] = x1
            h_ref[rows, :] = _rms(x1, mixw_ref[...]).astype(BF16)

    pl.when(is_p)(functools.partial(tile, xp_ref, x1p_ref, hp_ref))
    pl.when(jnp.logical_not(is_p))(functools.partial(tile, xs_ref, x1s_ref, hs_ref))

    lo = 0
    for o_ref in wsplit_refs:
        o_ref[...] = win_ref[:, lo:lo + o_ref.shape[1]].astype(BF16)
        lo += o_ref.shape[1]


def _ffn1(xp, xs, nw, wg, wu, wd, mixw, w_in, splits, tm):
    mp, d = xp.shape
    ms = xs.shape[0]
    assert ms == tm
    n_p = mp // tm
    dff = wg.shape[2]
    assert d % (W_CHUNKS * 2 * SUBLANES) == 0 and dff % (W_CHUNKS * 2 * SUBLANES) == 0
    rowp = lambda w: pl.BlockSpec((tm, w), lambda i: (jnp.minimum(i, n_p - 1), 0))
    rows = lambda w: pl.BlockSpec((ms, w), lambda i: (0, 0))
    hbm = pl.BlockSpec(memory_space=pl.ANY)
    layer = 0
    wrows = w_in.shape[1] // n_p
    wblk = lambda w: pl.BlockSpec((wrows, w), lambda i: (jnp.minimum(i, n_p - 1), 0))
    w_in_blk = pl.BlockSpec((None, wrows, w_in.shape[2]),
                            lambda i: (layer, jnp.minimum(i, n_p - 1), 0))
    return pl.pallas_call(
        functools.partial(_ffn1_kernel, n_p=n_p, layer=layer),
        grid=(n_p + 1,),
        in_specs=[rowp(d), rows(d), _const_spec((1, d)), hbm, hbm, hbm, _const_spec((1, d)),
                  w_in_blk],
        out_specs=[rowp(d), rowp(d), rows(d), rows(d)] + [wblk(c) for c in splits],
        out_shape=[jax.ShapeDtypeStruct((mp, d), F32), jax.ShapeDtypeStruct((mp, d), BF16),
                   jax.ShapeDtypeStruct((ms, d), F32), jax.ShapeDtypeStruct((ms, d), BF16)]
                  + [jax.ShapeDtypeStruct((w_in.shape[1], c), BF16) for c in splits],
        scratch_shapes=[pltpu.VMEM((d, dff), BF16), pltpu.VMEM((d, dff), BF16),
                        pltpu.VMEM((dff, d), BF16),
                        pltpu.VMEM((2, d // W_CHUNKS, dff), F32),
                        pltpu.VMEM((2, d // W_CHUNKS, dff), F32),
                        pltpu.VMEM((2, dff // W_CHUNKS, d), F32),
                        pltpu.SemaphoreType.DMA((2,)), pltpu.SemaphoreType.DMA((2,)),
                        pltpu.SemaphoreType.DMA((2,))],
        compiler_params=_params(("arbitrary",)),
        name="ffn1",
    )(xp, xs, nw, wg, wu, wd, mixw, w_in)


def _out_ffn2_kernel(x1p_ref, o5p_ref, osp_ref, x1s_ref, o5s_ref, oss_ref, wo_ref, nw_ref,
                     wg_ref, wu_ref, wd_ref, fin_ref, yp_ref, ys_ref, *, n_p):
    is_p = pl.program_id(0) < n_p
    k5 = o5p_ref.shape[1]

    def tile(x1_ref, o5_ref, os_ref, y_ref):
        x2 = x1_ref[...] + (_dot(o5_ref[...].astype(BF16), wo_ref[:k5, :])
                            + _dot(os_ref[...].astype(BF16), wo_ref[k5:, :]))
        x3 = _swiglu_half(x2, nw_ref[...], wg_ref, wu_ref, wd_ref)
        y_ref[...] = _rms(x3, fin_ref[...])

    pl.when(is_p)(functools.partial(tile, x1p_ref, o5p_ref, osp_ref, yp_ref))
    pl.when(jnp.logical_not(is_p))(functools.partial(tile, x1s_ref, o5s_ref, oss_ref, ys_ref))


def _out_ffn2(x1p, o5p, osp, x1s, o5s, oss, wo, nw, wg, wu, wd, fin, tm):
    mp, d = x1p.shape
    ms = x1s.shape[0]
    assert ms == tm
    n_p = mp // tm
    dff = wg.shape[1]
    rowp = lambda w: pl.BlockSpec((tm, w), lambda i: (jnp.minimum(i, n_p - 1), 0))
    rows = lambda w: pl.BlockSpec((ms, w), lambda i: (0, 0))
    return pl.pallas_call(
        functools.partial(_out_ffn2_kernel, n_p=n_p),
        grid=(n_p + 1,),
        in_specs=[rowp(d), rowp(o5p.shape[1]), rowp(osp.shape[1]),
                  rows(d), rows(o5s.shape[1]), rows(oss.shape[1]),
                  _const_spec(wo.shape), _const_spec((1, d)),
                  _const_spec((d, dff)), _const_spec((d, dff)), _const_spec((dff, d)),
                  _const_spec((1, d))],
        out_specs=[rowp(d), rows(d)],
        out_shape=[jax.ShapeDtypeStruct((mp, d), F32), jax.ShapeDtypeStruct((ms, d), F32)],
        compiler_params=_params(("arbitrary",)),
        name="out_ffn2",
    )(x1p, o5p, osp, x1s, o5s, oss, wo, nw, wg, wu, wd, fin)


def _s5_param_kernel(lre_ref, lim_ref, step_ref, bre_ref, bim_ref,
                     are_ref, aim_ref, bbre_ref, bbim_ref):
    lre, lim, step = lre_ref[...], lim_ref[...], step_ref[...]
    mag = jnp.exp(lre * step)
    ang = lim * step
    are = mag * jnp.cos(ang)
    aim = mag * jnp.sin(ang)
    den = lre * lre + lim * lim
    nre, nim = are - 1.0, aim
    cre = (nre * lre + nim * lim) / den
    cim = (nim * lre - nre * lim) / den
    are_ref[...] = are
    aim_ref[...] = aim
    bre, bim = bre_ref[...], bim_ref[...]
    bbre_ref[...] = cre * bre - cim * bim
    bbim_ref[...] = cre * bim + cim * bre


def _s5_params(lam_re, lam_im, log_step, b_re, b_im):
    g, p = lam_re.shape
    hh = b_re.shape[-1]
    gp = g * p
    step = jnp.exp(log_step.astype(F32))
    step_row = jnp.broadcast_to(step[:, None], (g, p)).reshape(1, gp)
    to_lanes = lambda b: b.astype(F32).reshape(gp, hh).T
    full = lambda shape: pl.BlockSpec(shape, lambda: (0,) * len(shape))
    are, aim, bbre, bbim = pl.pallas_call(
        _s5_param_kernel,
        in_specs=[full((1, gp)), full((1, gp)), full((1, gp)), full((hh, gp)), full((hh, gp))],
        out_specs=[full((1, gp)), full((1, gp)), full((hh, gp)), full((hh, gp))],
        out_shape=[jax.ShapeDtypeStruct((1, gp), F32), jax.ShapeDtypeStruct((1, gp), F32),
                   jax.ShapeDtypeStruct((hh, gp), F32), jax.ShapeDtypeStruct((hh, gp), F32)],
        name="s5_params",
    )(lam_re.astype(F32).reshape(1, gp), lam_im.astype(F32).reshape(1, gp), step_row,
      to_lanes(b_re), to_lanes(b_im))
    to_blocks = lambda b: b.reshape(hh, g, p).transpose(1, 0, 2)
    return are, aim, to_blocks(bbre), to_blocks(bbim)


def _block_diag(w, nblk):
    g, a, b = w.shape
    gl = g // nblk
    w = w.reshape(nblk, gl, a, b)
    eye = jnp.eye(gl, dtype=w.dtype)
    out = w[:, :, :, None, :] * eye[None, :, None, :, None]
    return out.reshape(nblk, gl * a, gl * b)


def _s5_kernel(*refs, nb, tl, n_cast, ride_per):
    refs = list(refs)
    take = lambda n: [refs.pop(0) for _ in range(n)]
    (h_ref, wu_ref, wb_ref, wcre_ref, wcim_ref, are_ref, aim_ref, d_ref, wglu_ref, bglu_ref,
     s0re_ref, s0im_ref) = take(12)
    cast_in = take(n_cast)
    ride_in = take(9 if ride_per else 0)
    o5_ref, sre_ref, sim_ref = take(3)
    cast_out = take(n_cast)
    ride_out = take(2 if ride_per else 0)
    u_s, xre_s, xim_s, y_s = take(4)
    ride_s = [refs.pop() for _ in range(2 if ride_per else 0)][::-1]
    perm_s = refs
    for src, dst in zip(cast_in, cast_out):
        dst[...] = src[...].astype(BF16)
    if ride_per:
        _ssds_state_body(*ride_in, *ride_out, *ride_s, part=pl.program_id(0) % ride_per)
    width = u_s.shape[1]
    nstate = xre_s.shape[1]
    nblk = wb_ref.shape[0]
    cin = width // nblk
    cst = nstate // nblk
    r = nb * tl
    pack = 2 * SUBLANES

    @pl.when(pl.program_id(0) == 0)
    def _():
        sre_ref[...] = s0re_ref[...]
        sim_ref[...] = s0im_ref[...]

    if perm_s:
        slab_s, = perm_s
        assert cin == LANES and nb == SUBLANES
        pitch = tl + SUBLANES
        u_bt = _dot(h_ref[...].reshape(r, h_ref.shape[2]), wu_ref[...])
        for b in range(nb):
            for k in range(nblk):
                slab_s[k, b * pitch:b * pitch + tl, :] = u_bt[b * tl:(b + 1) * tl,
                                                              k * cin:(k + 1) * cin]

        for t in range(tl):
            for k in range(nblk):
                u_s[t * nb:(t + 1) * nb, k * cin:(k + 1) * cin] = (
                    slab_s[k, pl.ds(t, nb, stride=pitch), :])
        u = u_s[...]
    else:
        u = _dot(h_ref[...], wu_ref[...])
        u_s[...] = u
    ub = u.astype(BF16)

    def drive(k):
        bu = _dot(ub[:, k * cin:(k + 1) * cin], wb_ref[k])
        xre_s[:, k * cst:(k + 1) * cst] = bu[:, :cst]
        xim_s[:, k * cst:(k + 1) * cst] = bu[:, cst:]

    def scan(k):
        lanes = slice(k * cst, (k + 1) * cst)
        a_re = jnp.broadcast_to(are_ref[:, lanes], (SUBLANES, cst))
        a_im = jnp.broadcast_to(aim_ref[:, lanes], (SUBLANES, cst))

        def step(rows, s_re, s_im):
            n_re = a_re * s_re - a_im * s_im + xre_s[rows, lanes]
            n_im = a_re * s_im + a_im * s_re + xim_s[rows, lanes]
            xre_s[rows, lanes] = n_re
            xim_s[rows, lanes] = n_im
            return n_re, n_im

        if nb == SUBLANES:
            s_re, s_im = sre_ref[:, lanes], sim_ref[:, lanes]
            for t in range(tl):
                s_re, s_im = step(slice(t * nb, (t + 1) * nb), s_re, s_im)
            sre_ref[:, lanes] = s_re
            sim_ref[:, lanes] = s_im
        else:
            def group_body(sg, carry):
                srow = pl.ds(pl.multiple_of(sg * SUBLANES, SUBLANES), SUBLANES)
                st = (sre_ref[srow, lanes], sim_ref[srow, lanes])
                for t in range(tl):
                    st = step(pl.ds(pl.multiple_of(t * nb + sg * SUBLANES, SUBLANES), SUBLANES), *st)
                sre_ref[srow, lanes], sim_ref[srow, lanes] = st
                return carry

            lax.fori_loop(0, nb // SUBLANES, group_body, 0)

    def readout(k):
        st = slice(k * cst, (k + 1) * cst)
        y_s[:, k * cin:(k + 1) * cin] = (_dot(xre_s[:, st].astype(BF16), wcre_ref[k])
                                         - _dot(xim_s[:, st].astype(BF16), wcim_ref[k]))

    for k in range(nblk + 2):
        if k < nblk:
            drive(k)
        if 1 <= k <= nblk:
            scan(k - 1)
        if k >= 2:
            readout(k - 2)
    y = y_s[...] + d_ref[...] * u_s[...]
    v = _gelu_tanh(y)
    gate = _dot(v.astype(BF16), wglu_ref[...]) + bglu_ref[...]
    o5 = v * _sigmoid(gate)
    if perm_s:
        for k in range(nblk):
            slab_s[k, 0:r, :] = o5[:, k * cin:(k + 1) * cin]
        for b in range(nb):
            for t0 in range(0, tl, pack):
                for k in range(nblk):
                    lo = slab_s[k, pl.ds(t0 * nb + b, SUBLANES, stride=nb), :]
                    hi = slab_s[k, pl.ds((t0 + SUBLANES) * nb + b, SUBLANES, stride=nb), :]
                    o5_ref[b, t0:t0 + pack, k * cin:(k + 1) * cin] = (
                        jnp.concatenate([lo, hi], axis=0).astype(BF16))
    else:
        o5_ref[...] = o5.astype(BF16)


def _s5(h, wu, wb, wcre, wcim, are, aim, dsk, wglu, bglu, s0re, s0im, nb, tl, casts=(),
        ride=None):
    width = wu.shape[1]
    nstate = are.shape[1]
    r = nb * tl
    batch_major = h.ndim == 3
    if batch_major:
        steps = h.shape[1] // tl
        blk = lambda w: pl.BlockSpec((nb, tl, w), lambda i: (0, i, 0))
        o_shape = (nb, h.shape[1], width)
        extra = [pltpu.VMEM((width // LANES, nb * (tl + SUBLANES), LANES), F32)]
    else:
        steps = h.shape[0] // r
        blk = lambda w: pl.BlockSpec((r, w), lambda i: (i, 0))
        o_shape = (h.shape[0], width)
        extra = []
    c_in, c_out, c_shapes = _cast_specs(casts, steps)
    r_in, r_out, r_shapes, r_scratch, ride_per = (
        _ride_specs(ride, steps) if ride else ([], [], [], [], 0))
    kern = functools.partial(_s5_kernel, nb=nb, tl=tl, n_cast=len(casts), ride_per=ride_per)
    return pl.pallas_call(
        kern,
        grid=(steps,),
        in_specs=[blk(h.shape[-1]), _const_spec(wu.shape), _const_spec(wb.shape),
                  _const_spec(wcre.shape), _const_spec(wcim.shape), _const_spec(are.shape),
                  _const_spec(aim.shape), _const_spec(dsk.shape), _const_spec(wglu.shape),
                  _const_spec(bglu.shape), _const_spec(s0re.shape), _const_spec(s0im.shape)]
                 + c_in + r_in,
        out_specs=[blk(width), pl.BlockSpec((nb, nstate), lambda i: (0, 0)),
                   pl.BlockSpec((nb, nstate), lambda i: (0, 0))] + c_out + r_out,
        out_shape=[jax.ShapeDtypeStruct(o_shape, BF16),
                   jax.ShapeDtypeStruct((nb, nstate), F32),
                   jax.ShapeDtypeStruct((nb, nstate), F32)] + c_shapes + r_shapes,
        scratch_shapes=[pltpu.VMEM((r, width), F32), pltpu.VMEM((r, nstate), F32),
                        pltpu.VMEM((r, nstate), F32), pltpu.VMEM((r, width), F32)] + extra
                       + r_scratch,
        compiler_params=_params(("arbitrary",)),
        name="s5",
    )(h, wu, wb, wcre, wcim, are, aim, dsk, wglu, bglu, s0re, s0im, *casts, *(ride or ()))


def _split2(x):
    hi = x.astype(BF16)
    lo = (x - hi.astype(F32)).astype(BF16)
    return jnp.concatenate([hi, lo], axis=1)


def _split3(x):
    hi = x.astype(BF16)
    r1 = x - hi.astype(F32)
    mid = r1.astype(BF16)
    lo = (r1 - mid.astype(F32)).astype(BF16)
    return jnp.concatenate([hi, mid, lo], axis=1)


HEAD_REP = 32
RIDE_NB = SUBLANES


def _pack2(x):
    grp = lax.broadcasted_iota(jnp.int32, x.shape, 1) // HEAD_REP
    hi = x.astype(BF16).astype(F32)
    return jnp.where(grp == 0, hi, x - hi).astype(BF16)


def _pack3(x):
    grp = lax.broadcasted_iota(jnp.int32, x.shape, 1) // HEAD_REP
    hi = x.astype(BF16).astype(F32)
    r1 = x - hi
    mid = r1.astype(BF16).astype(F32)
    return jnp.where(grp == 0, hi, jnp.where(grp == 1, mid, r1 - mid)).astype(BF16)


def _group_norm(y, zg, nw, gw):
    outs = []
    for g in range(SSD_GROUPS):
        seg = y[:, g * gw:(g + 1) * gw] * zg[:, g * gw:(g + 1) * gw]
        outs.append(seg * lax.rsqrt(jnp.mean(seg * seg, axis=-1, keepdims=True) + EPS))
    return jnp.concatenate(outs, axis=1) * nw


def _ssds_pre_kernel(h_ref, wz_ref, wx_ref, wdt_ref, cw_ref, cb_ref, dtb_ref, alog_ref, dsk_ref,
                     conv0_ref, g2_ref, exp_ref, c_o, b_o, xdd_o, yp_o, eac_o, zg_o, cdec_o, cv_o):
    nbt = conv0_ref.shape[1]
    seq = h_ref.shape[0] // nbt
    width = zg_o.shape[1]
    nbc = c_o.shape[1]
    hb = h_ref[...]
    zg_o[...] = _silu(_dot(hb, wz_ref[...]))
    dt = _softplus(_dot(hb, wdt_ref[...]) + dtb_ref[...])
    la = dt * (-jnp.exp(alog_ref[...]))
    xbc = _dot(hb, wx_ref[...])
    rows = [slice(t * nbt, (t + 1) * nbt) for t in range(seq)]
    full = [conv0_ref[k] for k in range(SSD_CONV - 1)] + [xbc[r] for r in rows]
    for k in range(SSD_CONV - 1):
        cv_o[k] = full[seq + k]
    acums = []
    for t in range(seq):
        acums.append(la[rows[t]] if t == 0 else acums[-1] + la[rows[t]])
    tot = acums[-1]
    cdec_o[...] = jnp.exp(tot)
    xs, bq, cq = [], [], []
    for t in range(seq):
        acc = cb_ref[...]
        for k in range(SSD_CONV):
            acc = acc + cw_ref[k:k + 1, :] * full[t + k]
        xc = _silu(acc)
        xs.append(xc[:, :width])
        bq.append(xc[:, width:width + nbc].astype(BF16).astype(F32))
        cq.append(xc[:, width + nbc:].astype(BF16).astype(F32))
    for t in range(seq):
        r = rows[t]
        b_o[r, :] = bq[t]
        c_o[r, :] = cq[t]
        xdd_o[r, :] = xs[t] * _dot(_split2(dt[r] * jnp.exp(tot - acums[t])), exp_ref[...])
        eac_o[r, :] = _dot(_split2(jnp.exp(acums[t])), exp_ref[...])
        yp = dsk_ref[...] * xs[t]
        for s in range(t + 1):
            cbx = _dot(_split2(cq[t] * bq[s]), g2_ref[...])
            coef = cbx * jnp.exp(acums[t] - acums[s]) * dt[rows[s]]
            yp = yp + _dot(_split2(coef), exp_ref[...]) * xs[s]
        yp_o[r, :] = yp


def _ssds_state_body(c_ref, b_ref, xdd_ref, yp_ref, eac_ref, zg_ref, cdec_ref, st_ref, nw_ref,
                     o_ref, sto_ref, yoff_s, xddt_s, part):
    seq, nb, width = xdd_ref.shape
    nsub, nheads, hd_dim, nst = st_ref.shape
    hpg = nheads // SSD_GROUPS
    gw = width // SSD_GROUPS
    rws = seq * nb
    cst = c_ref[...].reshape(rws, c_ref.shape[2]).astype(BF16)
    bst = b_ref[...].reshape(rws, b_ref.shape[2])
    xdd = xdd_ref[...].reshape(rws, width)
    xddt_s[...] = jnp.concatenate([xdd, jnp.zeros((LANES - rws, width), F32)], axis=0).T.astype(BF16)
    rowb = lax.broadcasted_iota(jnp.int32, (rws, gw), 0) % nb
    zpad = jnp.zeros((LANES - rws, nst), BF16)
    for bl in range(nsub):
        b = part * nsub + bl
        mine = rowb == b
        for g in range(SSD_GROUPS):
            gl = slice(g * gw, (g + 1) * gw)
            hg = st_ref[bl, g * hpg:(g + 1) * hpg].reshape(gw, nst)
            res = _dot_nt(cst[:, g * nst:(g + 1) * nst], hg.astype(BF16))
            yoff_s[:, gl] = jnp.where(mine, res, 0.0 if bl == 0 else yoff_s[:, gl])
            bm = jnp.where(mine[:, :nst], bst[:, g * nst:(g + 1) * nst], 0.0).astype(BF16)
            upd = _dot(xddt_s[gl, :], jnp.concatenate([bm, zpad], axis=0))
            for jh in range(hpg):
                hd = g * hpg + jh
                hrows = slice(jh * hd_dim, (jh + 1) * hd_dim)
                sto_ref[bl, hd] = hg[hrows] * cdec_ref[b, hd] + upd[hrows]
    y = yp_ref[...].reshape(rws, width) + yoff_s[...] * eac_ref[...].reshape(rws, width)
    o = _group_norm(y, zg_ref[...].reshape(rws, width), nw_ref[...], gw).reshape(seq, nb, width)
    if nsub == nb:
        o_ref[...] = o
    else:
        @pl.when(part == 0)
        def _():
            o_ref[...] = o

        @pl.when(part != 0)
        def _():
            own = lax.broadcasted_iota(jnp.int32, o.shape, 1) // nsub == part
            o_ref[...] = jnp.where(own, o, o_ref[...])


def _ssds(h_tm, wz, wx, wdt, cw, cb, dtb, alog, dsk, nw, ssd0, conv0_tm):
    m, d = h_tm.shape
    nseq, nheads, hd_dim, nst = ssd0.shape
    seq = m // nseq
    width = wz.shape[1]
    xbc_w = wx.shape[1]
    nbc = SSD_GROUPS * nst
    hpg = nheads // SSD_GROUPS
    head = jnp.arange(LANES)
    exp1 = (head[:, None] == (jnp.arange(width) // hd_dim)[None, :]).astype(BF16)
    g1 = ((jnp.arange(nbc) // nst)[:, None] == (head // hpg)[None, :]) & (head < nheads)[None, :]
    exp_m = jnp.concatenate([exp1, exp1], axis=0)
    g2 = jnp.concatenate([g1, g1], axis=0).astype(BF16)
    full = lambda shape: pl.BlockSpec(shape, lambda: (0,) * len(shape))
    ins = (h_tm, wz, wx, wdt, cw, cb, dtb, alog, dsk, conv0_tm, g2, exp_m)
    outs = [jax.ShapeDtypeStruct((m, nbc), F32), jax.ShapeDtypeStruct((m, nbc), F32),
            jax.ShapeDtypeStruct((m, width), F32), jax.ShapeDtypeStruct((m, width), F32),
            jax.ShapeDtypeStruct((m, width), F32), jax.ShapeDtypeStruct((m, width), F32),
            jax.ShapeDtypeStruct((nseq, LANES), F32),
            jax.ShapeDtypeStruct((SSD_CONV - 1, nseq, xbc_w), F32)]
    c, b, xdd, yp, eac, zg, cdec, conv1_tm = pl.pallas_call(
        _ssds_pre_kernel,
        in_specs=[full(a.shape) for a in ins],
        out_specs=[full(o.shape) for o in outs],
        out_shape=outs,
        compiler_params=pltpu.CompilerParams(vmem_limit_bytes=VMEM_LIMIT),
        name="ssd_sample_pre",
    )(*ins)

    tm3 = lambda a: a.reshape(seq, nseq, a.shape[1])
    return (tm3(c), tm3(b), tm3(xdd), tm3(yp), tm3(eac), tm3(zg), cdec, ssd0, nw), conv1_tm


def _ride_specs(ride, steps):
    c, b, xdd, yp, eac, zg, cdec, ssd0, nw = ride
    seq, nseq, width = xdd.shape
    nsub = nseq // steps
    assert nsub * steps == nseq and RIDE_NB % nsub == 0
    per = RIDE_NB // nsub
    blk = lambda a: pl.BlockSpec((seq, RIDE_NB, a.shape[2]), lambda i: (0, i // per, 0))
    st_spec = pl.BlockSpec((nsub,) + ssd0.shape[1:], lambda i: (i, 0, 0, 0))
    ins = [blk(c), blk(b), blk(xdd), blk(yp), blk(eac), blk(zg),
           pl.BlockSpec((RIDE_NB, LANES), lambda i: (i // per, 0), memory_space=pltpu.SMEM),
           st_spec, _const_spec(nw.shape)]
    outs = [blk(xdd), st_spec]
    shapes = [jax.ShapeDtypeStruct((seq, nseq, width), F32), jax.ShapeDtypeStruct(ssd0.shape, F32)]
    scratch = [pltpu.VMEM((seq * RIDE_NB, width), F32), pltpu.VMEM((width, LANES), BF16)]
    return ins, outs, shapes, scratch, per


def _ssdp_kernel(h_ref, wz_ref, wx_ref, wdt_ref, cw_ref, cb_ref, dtb_ref, alog_ref, dsk_ref,
                 nw_ref, btril_ref, exp_ref, perm_ref, permt_ref, o_ref, st_ref, cv_ref,
                 xbc_s, zg_s, acum_s, acumt_s, e2_s, e3_s, xd_s, xdd_s, ysk_s, b_s, c_s,
                 cdec_s, ht_s, tail_s, m_s, *, nch, col_chunk):
    t = SSD_CHUNK
    r = nch * t
    j = pl.program_id(1)
    nheads, hd_dim, nst = st_ref.shape[1:]
    width = nheads * hd_dim
    gw = width // SSD_GROUPS
    hpg = nheads // SSD_GROUPS
    xbc_w = xbc_s.shape[1]
    ntap = SSD_CONV - 1
    pre = ntap * SUBLANES
    ext = pre + t
    steps = t // SUBLANES
    last_rows = [t - 1 - SUBLANES * (ntap - 1 - v) for v in range(ntap)]

    @pl.when(j == 0)
    def _():
        ht_s[...] = jnp.zeros(ht_s.shape, F32)
        tail_s[...] = jnp.zeros(tail_s.shape, F32)

    hb = jnp.concatenate([_dot(perm_ref[...], h_ref[c * t:(c + 1) * t, :]) for c in range(nch)],
                         axis=0).astype(BF16)
    dt = _softplus(_dot(hb, wdt_ref[...]) + dtb_ref[...])
    la = dt * (-jnp.exp(alog_ref[...]))

    def xbc_cols(jc):
        lanes = slice(jc * col_chunk, (jc + 1) * col_chunk)
        xall = _dot(hb, wx_ref[:, lanes])
        for c in range(nch):
            xbc_s[c * ext + pre:(c + 1) * ext, lanes] = xall[c * t:(c + 1) * t, :]
        for c in range(nch):
            base = c * ext
            xbc_s[base:base + pre, lanes] = xbc_s[base + t - 1:base + pre + t - 1, lanes]
            for v in range(ntap):
                if c == 0:
                    prev = tail_s[v:v + 1, lanes]
                else:
                    prow = base - ext + pre + last_rows[v]
                    prev = xbc_s[prow:prow + 1, lanes]
                xbc_s[base + v * SUBLANES:base + v * SUBLANES + 1, lanes] = prev
            acc = cb_ref[:, lanes]
            for k in range(SSD_CONV):
                acc = acc + cw_ref[k:k + 1, lanes] * xbc_s[base + k * SUBLANES:
                                                           base + k * SUBLANES + t, lanes]
            xc = _silu(acc)
            rows = slice(c * t, (c + 1) * t)
            lo = jc * col_chunk
            if lo < width:
                xd_s[rows, lanes] = xc.astype(BF16)
                xdd_s[rows, lanes] = (xc * e2_s[rows, lanes]).astype(BF16)
                ysk_s[rows, lanes] = xc * dsk_ref[:, lanes]
            elif lo < width + SSD_GROUPS * nst:
                b_s[rows, lo - width:lo - width + col_chunk] = xc.astype(BF16)
            else:
                cl = lo - width - SSD_GROUPS * nst
                c_s[rows, cl:cl + col_chunk] = xc.astype(BF16)

    x_cols = width // col_chunk
    for jc in range(x_cols, xbc_w // col_chunk):
        xbc_cols(jc)

    r3 = _dot(btril_ref[...], _split3(la))
    acum = r3[:, :LANES] + r3[:, LANES:2 * LANES] + r3[:, 2 * LANES:]
    tots = [acum[(c + 1) * t - 1:(c + 1) * t, :] for c in range(nch)]
    arow = acum - jnp.log(dt)
    for c in range(nch):
        acumt_s[c] = arow[c * t:(c + 1) * t, :].T
    tot_rows = jnp.concatenate([jnp.broadcast_to(v, (t, LANES)) for v in tots], axis=0)
    dstate = jnp.exp(tot_rows - acum)
    cdec = jnp.exp(jnp.concatenate([jnp.broadcast_to(v, (SUBLANES, LANES)) for v in tots], axis=0))
    e2_s[...] = _dot(_pack2(dt * dstate), exp_ref[...])
    e3_s[...] = _dot(_pack2(jnp.exp(acum)), exp_ref[...])
    cdx = _dot(_pack2(cdec), exp_ref[...])
    for c in range(nch):
        cdec_s[c] = cdx[c * SUBLANES:(c + 1) * SUBLANES, :]
    acum_s[...] = acum

    def token(i):
        return (i % SUBLANES) * steps + i // SUBLANES

    causal = (token(lax.broadcasted_iota(jnp.int32, (t, t), 0))
              >= token(lax.broadcasted_iota(jnp.int32, (t, t), 1)))
    for c in range(nch):
        rows = slice(c * t, (c + 1) * t)
        for g in range(SSD_GROUPS):
            cbm = jnp.where(causal, _dot_nt(c_s[rows, g * nst:(g + 1) * nst],
                                            b_s[rows, g * nst:(g + 1) * nst]), 0.0)
            for hd in range(g * hpg, (g + 1) * hpg):
                diff = acum_s[rows, hd:hd + 1] - acumt_s[c, hd:hd + 1, :]
                m_s[(c * nheads + hd) * t:(c * nheads + hd + 1) * t, :] = (
                    cbm * jnp.exp(jnp.where(causal, diff, -jnp.inf))).astype(BF16)

    for jc in range(x_cols):
        lanes = slice(jc * col_chunk, (jc + 1) * col_chunk)
        xbc_cols(jc)
        zg_s[:, lanes] = _silu(_dot(hb, wz_ref[:, lanes]))

    for v in range(ntap):
        row = (nch - 1) * ext + pre + last_rows[v]
        tail_s[v:v + 1, :] = xbc_s[row:row + 1, :]
    cv_ref[0] = tail_s[...]

    lane = lax.broadcasted_iota(jnp.int32, (t, LANES), 1)
    keep_lo = jnp.where(lane < hd_dim, 1.0, 0.0).astype(BF16)
    keep_hi = jnp.where(lane < hd_dim, 0.0, 1.0).astype(BF16)
    nw = nw_ref[...]

    for c in range(nch):
        rows = slice(c * t, (c + 1) * t)
        ys = []
        for g in range(SSD_GROUPS):
            gl = slice(g * gw, (g + 1) * gw)
            cg = c_s[rows, g * nst:(g + 1) * nst]
            bg = b_s[rows, g * nst:(g + 1) * nst]
            pairs = []
            for jp in range(hpg // 2):
                ha = g * hpg + 2 * jp
                m2 = jnp.concatenate([m_s[(c * nheads + hd) * t:(c * nheads + hd + 1) * t, :]
                                      for hd in (ha, ha + 1)], axis=1)
                xdp = xd_s[rows, ha * hd_dim:ha * hd_dim + LANES]
                rhs = jnp.concatenate([xdp * keep_lo, xdp * keep_hi], axis=0)
                pairs.append(_dot(m2, rhs))
            htg = ht_s[:, gl]
            y = (jnp.concatenate(pairs, axis=1) + _dot(cg, htg.astype(BF16)) * e3_s[rows, gl]
                 + ysk_s[rows, gl])
            ys.append(y)
            upd = lax.dot_general(bg, xdd_s[rows, gl], (((0,), (0,)), ((), ())),
                                  preferred_element_type=F32)
            ht_s[:, gl] = htg * cdec_s[c, 0:1, gl] + upd
        o = _group_norm(jnp.concatenate(ys, axis=1), zg_s[rows, :], nw, gw).astype(BF16)
        o_ref[rows, :] = _dot(permt_ref[...], o).astype(BF16)

    @pl.when(j == pl.num_programs(1) - 1)
    def _():
        for pr in range(nheads // 2):
            tt = ht_s[:, pr * LANES:(pr + 1) * LANES].T
            st_ref[0, 2 * pr] = tt[:hd_dim]
            st_ref[0, 2 * pr + 1] = tt[hd_dim:]


def _ssdp(h, wz, wx, wdt, cw, cb, dtb, alog, dsk, nw, nseq, nheads, nch):
    m, d = h.shape
    width = wz.shape[1]
    xbc_w = wx.shape[1]
    hd_dim = width // nheads
    nst = SSD_STATE
    t = SSD_CHUNK
    r = nch * t
    nblk = m // (nseq * r)
    ri = jnp.arange(r)
    steps = t // SUBLANES
    tok = (ri % t % SUBLANES) * steps + ri % t // SUBLANES
    btril = ((tok[:, None] >= tok[None, :]) & (ri[:, None] // t == ri[None, :] // t)).astype(BF16)
    perm = (tok[:t, None] == jnp.arange(t)[None, :]).astype(BF16)
    lane = jnp.arange(LANES)
    head = (lane % HEAD_REP)[:, None]
    exp_m = ((head == (jnp.arange(width) // hd_dim)[None, :])
             & (lane < 2 * HEAD_REP)[:, None]).astype(BF16)
    row = lambda w: pl.BlockSpec((r, w), lambda i, j: (i * nblk + j, 0))
    kern = functools.partial(_ssdp_kernel, nch=nch, col_chunk=4 * LANES)
    vm = lambda shape, dt: pltpu.VMEM(shape, dt)
    return pl.pallas_call(
        kern,
        grid=(nseq, nblk),
        in_specs=[row(d), _const_spec(wz.shape), _const_spec(wx.shape), _const_spec(wdt.shape),
                  _const_spec(cw.shape), _const_spec(cb.shape), _const_spec(dtb.shape),
                  _const_spec(alog.shape), _const_spec(dsk.shape), _const_spec(nw.shape),
                  _const_spec(btril.shape), _const_spec(exp_m.shape),
                  _const_spec(perm.shape), _const_spec(perm.shape)],
        out_specs=[row(width),
                   pl.BlockSpec((1, nheads, hd_dim, nst), lambda i, j: (i, 0, 0, 0)),
                   pl.BlockSpec((1, SSD_CONV - 1, xbc_w), lambda i, j: (i, 0, 0))],
        out_shape=[jax.ShapeDtypeStruct((m, width), BF16),
                   jax.ShapeDtypeStruct((nseq, nheads, hd_dim, nst), F32),
                   jax.ShapeDtypeStruct((nseq, SSD_CONV - 1, xbc_w), F32)],
        scratch_shapes=[vm((nch * (t + (SSD_CONV - 1) * SUBLANES), xbc_w), F32),
                        vm((r, width), F32),
                        vm((r, LANES), F32), vm((nch, LANES, t), F32),
                        vm((r, width), F32), vm((r, width), F32),
                        vm((r, width), BF16), vm((r, width), BF16), vm((r, width), F32),
                        vm((r, SSD_GROUPS * nst), BF16), vm((r, SSD_GROUPS * nst), BF16),
                        vm((nch, SUBLANES, width), F32), vm((nst, width), F32),
                        vm((SSD_CONV - 1, xbc_w), F32), vm((nch * nheads * t, t), BF16)],
        compiler_params=_params(("arbitrary", "arbitrary")),
        name="ssd_prompt",
    )(h, wz, wx, wdt, cw, cb, dtb, alog, dsk, nw, btril, exp_m, perm, perm.T)


def _rep_heads(v):
    grp = jnp.pad(v, [(0, 0)] * (v.ndim - 1) + [(0, HEAD_REP - v.shape[-1])])
    tail = jnp.zeros(v.shape[:-1] + (LANES - 3 * HEAD_REP,), v.dtype)
    return jnp.concatenate([grp, grp, grp, tail], axis=-1)


def _s5_call(h, s5re0, s5im0, w, nb, tl, casts=(), ride=None):
    g, p = s5re0.shape[1:]
    o5, re1, im1, *more = _s5(h, w["w_u"], w["s5_wb"], w["s5_wcre"], w["s5_wcim"], w["s5_are"],
                              w["s5_aim"], w["s5_d"], w["s5_wglu"], w["s5_bglu"],
                              s5re0.reshape(nb, g * p), s5im0.reshape(nb, g * p), nb, tl, casts,
                              ride)
    return o5, re1.reshape(nb, g, p), im1.reshape(nb, g, p), more


def _layer(xp, xs, s5re0, s5im0, ssd0, conv0, w, *, tm, s5_tl, ssd_nch):
    bp, lp, d = xp.shape
    bs, ls, _ = xs.shape
    mp, ms = bp * lp, bs * ls
    g, p = s5re0.shape[1:]
    nheads = ssd0.shape[1]

    xs_tm = xs.transpose(1, 0, 2).reshape(ms, d)
    x1p, hp, x1s, hs, w_u, w_z, w_x, w_dt = _ffn1(
        xp.reshape(mp, d), xs_tm, w["ffn1_norm"], w["ffn1_wg"], w["ffn1_wu"], w["ffn1_wd"],
        w["mix_norm"], w["w_in"], w["w_in_splits"], tm)
    w = dict(w, w_u=w_u, w_z=w_z, w_x=w_x, w_dt=_rep_heads(w_dt))

    ride, s_conv_tm = _ssds(hs, w["w_z"], w["w_x"], w["w_dt"], w["conv_w"], w["conv_b"],
                            w["dt_bias"], w["a_log"], w["ssd_d"], w["ssd_norm"], ssd0,
                            conv0.transpose(1, 0, 2))
    zeros = jnp.zeros((bp, g, p), F32)
    o5p, p_re, p_im, (wg2, wu2, wd2, wo, oss, s_ssd) = _s5_call(
        hp.reshape(bp, lp, d), zeros, zeros, w, bp, s5_tl,
        (w["ffn2_wg"], w["ffn2_wu"], w["ffn2_wd"], w["w_out"]), ride)
    oss = oss.reshape(ms, -1)
    osp, p_ssd, p_conv = _ssdp(hp, w["w_z"], w["w_x"], w["w_dt"], w["conv_w"], w["conv_b"],
                               w["dt_bias"], w["a_log"], w["ssd_d"], w["ssd_norm"],
                               bp, nheads, ssd_nch)
    o5s, s_re, s_im, _ = _s5_call(hs, s5re0, s5im0, w, bs, ls)

    yp, ys = _out_ffn2(x1p, o5p.reshape(mp, -1), osp, x1s, o5s, oss, wo, w["ffn2_norm"],
                       wg2, wu2, wd2, w["final_norm"], tm)
    return (yp.reshape(bp, lp, d), ys.reshape(ls, bs, d).transpose(1, 0, 2),
            p_re, p_im, p_ssd, p_conv, s_re, s_im, s_ssd, s_conv_tm.transpose(1, 0, 2))


def kernel(x_prompt, x_sample, state_s5_re, state_s5_im, state_ssd, state_conv, ffn1_norm, ffn1_w_gate, ffn1_w_up, ffn1_w_down, mix_norm, w_in, s5_lambda_re, s5_lambda_im, s5_log_step, s5_b_re, s5_b_im, s5_c_re, s5_c_im, s5_d, s5_w_glu, s5_b_glu, ssd_conv_w, ssd_conv_b, ssd_dt_bias, ssd_a_log, ssd_d, ssd_norm, w_out, ffn2_norm, ffn2_w_gate, ffn2_w_up, ffn2_w_down, final_norm):
    depth = w_in.shape[0]
    assert depth == 1, "single-layer stack"
    i = 0
    d = x_prompt.shape[-1]
    g, p = s5_lambda_re.shape[1:]
    s5_width = g * S5_GROUP
    nheads = ssd_a_log.shape[1]
    ssd_width = nheads * SSD_HEAD_DIM
    xbc_w = ssd_conv_w.shape[-1]
    c0, c1, c2 = s5_width, s5_width + ssd_width, s5_width + ssd_width + xbc_w
    row = lambda v: v.astype(F32).reshape(1, -1)

    are, aim, bbre, bbim = _s5_params(s5_lambda_re[i], s5_lambda_im[i], s5_log_step[i],
                                      s5_b_re[i], s5_b_im[i])
    nblk = s5_width // LANES
    wb = jnp.concatenate([_block_diag(bbre, nblk), _block_diag(bbim, nblk)],
                         axis=-1).astype(BF16)
    wcre = _block_diag(s5_c_re[i].astype(F32).transpose(0, 2, 1), nblk).astype(BF16)
    wcim = _block_diag(s5_c_im[i].astype(F32).transpose(0, 2, 1), nblk).astype(BF16)

    w = {
        "ffn1_norm": row(ffn1_norm[i]), "mix_norm": row(mix_norm[i]),
        "ffn2_norm": row(ffn2_norm[i]), "final_norm": row(final_norm),
        "ffn1_wg": ffn1_w_gate.astype(F32), "ffn1_wu": ffn1_w_up.astype(F32),
        "ffn1_wd": ffn1_w_down.astype(F32),
        "w_in": w_in.astype(F32), "w_in_splits": (c0, c1 - c0, c2 - c1, nheads),
        "ffn2_wg": ffn2_w_gate.astype(F32), "ffn2_wu": ffn2_w_up.astype(F32),
        "ffn2_wd": ffn2_w_down.astype(F32), "w_out": w_out.astype(F32),
        "s5_wb": wb, "s5_wcre": wcre, "s5_wcim": wcim, "s5_are": are, "s5_aim": aim,
        "s5_d": row(s5_d[i]), "s5_wglu": s5_w_glu[i].astype(BF16), "s5_bglu": row(s5_b_glu[i]),
        "conv_w": ssd_conv_w[i].astype(F32), "conv_b": row(ssd_conv_b[i]),
        "dt_bias": _rep_heads(row(ssd_dt_bias[i])), "a_log": _rep_heads(row(ssd_a_log[i])),
        "ssd_d": jnp.repeat(ssd_d[i].astype(F32), SSD_HEAD_DIM).reshape(1, -1),
        "ssd_norm": row(ssd_norm[i]),
    }

    lp = x_prompt.shape[1]
    yp, ys, *states = _layer(
        x_prompt, x_sample, state_s5_re[i], state_s5_im[i], state_ssd[i], state_conv[i], w,
        tm=512, s5_tl=min(lp, 64), ssd_nch=min(lp // SSD_CHUNK, 4))
    return (yp, ys) + tuple(v[None] for v in states)
```

```python
import functools
import math

import jax
import jax.numpy as jnp
from jax import lax
from jax.experimental import pallas as pl
from jax.experimental.pallas import tpu as pltpu

F32 = jnp.float32
BF16 = jnp.bfloat16
EPS = 1e-6

LANES = 128
SUBLANES = 8
VMEM_LIMIT = 56 * 1024 * 1024

S5_GROUP = 16
S5_STATE = 64
SSD_HEAD_DIM = 64
SSD_GROUPS = 4
SSD_STATE = 128
SSD_CONV = 4
SSD_CHUNK = 128


def _sigmoid(x):
    return 0.5 + 0.5 * jnp.tanh(0.5 * x)


def _silu(x):
    hx = 0.5 * x
    return hx + hx * jnp.tanh(hx)


def _gelu_tanh(x):
    c = math.sqrt(2.0 / math.pi)
    return 0.5 * x * (1.0 + jnp.tanh(c * (x + 0.044715 * (x * x * x))))


def _softplus(x):
    return jnp.maximum(x, 0.0) + jnp.log1p(jnp.exp(-jnp.abs(x)))


def _rms(x, w):
    return x * lax.rsqrt(jnp.mean(x * x, axis=-1, keepdims=True) + EPS) * w


def _dot(a, b):
    return jnp.dot(a, b, preferred_element_type=F32)


def _dot_nt(a, b):
    return lax.dot_general(a, b, (((1,), (1,)), ((), ())), preferred_element_type=F32)


def _const_spec(shape):
    nd = len(shape)
    return pl.BlockSpec(shape, lambda *_: (0,) * nd, pipeline_mode=pl.Buffered(1))


def _params(sem):
    return pltpu.CompilerParams(dimension_semantics=sem, vmem_limit_bytes=VMEM_LIMIT)


def _swiglu_half(x, nw, wg_ref, wu_ref, wd_ref):
    hn = _rms(x, nw).astype(BF16)
    g = _dot(hn, wg_ref[...])
    u = _dot(hn, wu_ref[...])
    a = (_silu(g) * u).astype(BF16)
    return x + 0.5 * _dot(a, wd_ref[...])


def _cast_specs(ws, steps, layer=0):
    ins, outs, shapes = [], [], []
    for w in ws:
        _, nrow, ncol = w.shape
        rep = 1 if (nrow // steps) % (2 * SUBLANES) == 0 else 2
        rows = rep * nrow // steps
        assert rows * steps == rep * nrow and rows % (2 * SUBLANES) == 0
        idx = lambda i, rep=rep: jnp.minimum(i, steps - 1) // rep
        ins.append(pl.BlockSpec((None, rows, ncol), lambda i, idx=idx: (layer, idx(i), 0)))
        outs.append(pl.BlockSpec((rows, ncol), lambda i, idx=idx: (idx(i), 0)))
        shapes.append(jax.ShapeDtypeStruct((nrow, ncol), BF16))
    return ins, outs, shapes


W_CHUNKS = 8


def _weight_copy(w_hbm, layer, stage, sem, k, slot):
    rows = stage.shape[1]
    return pltpu.make_async_copy(w_hbm.at[layer, pl.ds(k * rows, rows), :], stage.at[slot],
                                 sem.at[slot])


def _load_weights_bf16(jobs, layer):
    for w_hbm, _, stage, sem in jobs:
        _weight_copy(w_hbm, layer, stage, sem, 0, 0).start()
    for k in range(W_CHUNKS):
        slot = k % 2
        for w_hbm, dst, stage, sem in jobs:
            rows = stage.shape[1]
            if k + 1 < W_CHUNKS:
                _weight_copy(w_hbm, layer, stage, sem, k + 1, 1 - slot).start()
            _weight_copy(w_hbm, layer, stage, sem, k, slot).wait()
            dst[k * rows:(k + 1) * rows, :] = stage[slot].astype(BF16)


def _ffn1_kernel(xp_ref, xs_ref, nw_ref, wg_hbm, wu_hbm, wd_hbm, mixw_ref, win_ref,
                 x1p_ref, hp_ref, x1s_ref, hs_ref, *rest, n_p, layer):
    *wsplit_refs, wg_ref, wu_ref, wd_ref, stage_g, stage_u, stage_d, sem_g, sem_u, sem_d = rest
    is_p = pl.program_id(0) < n_p

    @pl.when(pl.program_id(0) == 0)
    def _():
        _load_weights_bf16([(wg_hbm, wg_ref, stage_g, sem_g), (wu_hbm, wu_ref, stage_u, sem_u),
                            (wd_hbm, wd_ref, stage_d, sem_d)], layer)

    def tile(x_ref, x1_ref, h_ref):
        half = x_ref.shape[0] // 2
        for rows in (slice(0, half), slice(half, 2 * half)):
            x1 = _swiglu_half(x_ref[rows, :], nw_ref[...], wg_ref, wu_ref, wd_ref)
            x1_ref[rows, :] = x1
            h_ref[rows, :] = _rms(x1, mixw_ref[...]).astype(BF16)

    pl.when(is_p)(functools.partial(tile, xp_ref, x1p_ref, hp_ref))
    pl.when(jnp.logical_not(is_p))(functools.partial(tile, xs_ref, x1s_ref, hs_ref))

    lo = 0
    for o_ref in wsplit_refs:
        o_ref[...] = win_ref[:, lo:lo + o_ref.shape[1]].astype(BF16)
        lo += o_ref.shape[1]


def _ffn1(xp, xs, nw, wg, wu, wd, mixw, w_in, splits, tm):
    mp, d = xp.shape
    ms = xs.shape[0]
    assert ms == tm
    n_p = mp // tm
    dff = wg.shape[2]
    assert d % (W_CHUNKS * 2 * SUBLANES) == 0 and dff % (W_CHUNKS * 2 * SUBLANES) == 0
    rowp = lambda w: pl.BlockSpec((tm, w), lambda i: (jnp.minimum(i, n_p - 1), 0))
    rows = lambda w: pl.BlockSpec((ms, w), lambda i: (0, 0))
    hbm = pl.BlockSpec(memory_space=pl.ANY)
    layer = 0
    wrows = w_in.shape[1] // n_p
    wblk = lambda w: pl.BlockSpec((wrows, w), lambda i: (jnp.minimum(i, n_p - 1), 0))
    w_in_blk = pl.BlockSpec((None, wrows, w_in.shape[2]),
                            lambda i: (layer, jnp.minimum(i, n_p - 1), 0))
    return pl.pallas_call(
        functools.partial(_ffn1_kernel, n_p=n_p, layer=layer),
        grid=(n_p + 1,),
        in_specs=[rowp(d), rows(d), _const_spec((1, d)), hbm, hbm, hbm, _const_spec((1, d)),
                  w_in_blk],
        out_specs=[rowp(d), rowp(d), rows(d), rows(d)] + [wblk(c) for c in splits],
        out_shape=[jax.ShapeDtypeStruct((mp, d), F32), jax.ShapeDtypeStruct((mp, d), BF16),
                   jax.ShapeDtypeStruct((ms, d), F32), jax.ShapeDtypeStruct((ms, d), BF16)]
                  + [jax.ShapeDtypeStruct((w_in.shape[1], c), BF16) for c in splits],
        scratch_shapes=[pltpu.VMEM((d, dff), BF16), pltpu.VMEM((d, dff), BF16),
                        pltpu.VMEM((dff, d), BF16),
                        pltpu.VMEM((2, d // W_CHUNKS, dff), F32),
                        pltpu.VMEM((2, d // W_CHUNKS, dff), F32),
                        pltpu.VMEM((2, dff // W_CHUNKS, d), F32),
                        pltpu.SemaphoreType.DMA((2,)), pltpu.SemaphoreType.DMA((2,)),
                        pltpu.SemaphoreType.DMA((2,))],
        compiler_params=_params(("arbitrary",)),
        name="ffn1",
    )(xp, xs, nw, wg, wu, wd, mixw, w_in)


def _out_ffn2_kernel(x1p_ref, o5p_ref, osp_ref, x1s_ref, o5s_ref, oss_ref, wo_ref, nw_ref,
                     wg_ref, wu_ref, wd_ref, fin_ref, yp_ref, ys_ref, *, n_p):
    is_p = pl.program_id(0) < n_p
    k5 = o5p_ref.shape[1]

    def tile(x1_ref, o5_ref, os_ref, y_ref):
        x2 = x1_ref[...] + (_dot(o5_ref[...].astype(BF16), wo_ref[:k5, :])
                            + _dot(os_ref[...].astype(BF16), wo_ref[k5:, :]))
        x3 = _swiglu_half(x2, nw_ref[...], wg_ref, wu_ref, wd_ref)
        y_ref[...] = _rms(x3, fin_ref[...])

    pl.when(is_p)(functools.partial(tile, x1p_ref, o5p_ref, osp_ref, yp_ref))
    pl.when(jnp.logical_not(is_p))(functools.partial(tile, x1s_ref, o5s_ref, oss_ref, ys_ref))


def _out_ffn2(x1p, o5p, osp, x1s, o5s, oss, wo, nw, wg, wu, wd, fin, tm):
    mp, d = x1p.shape
    ms = x1s.shape[0]
    assert ms == tm
    n_p = mp // tm
    dff = wg.shape[1]
    rowp = lambda w: pl.BlockSpec((tm, w), lambda i: (jnp.minimum(i, n_p - 1), 0))
    rows = lambda w: pl.BlockSpec((ms, w), lambda i: (0, 0))
    return pl.pallas_call(
        functools.partial(_out_ffn2_kernel, n_p=n_p),
        grid=(n_p + 1,),
        in_specs=[rowp(d), rowp(o5p.shape[1]), rowp(osp.shape[1]),
                  rows(d), rows(o5s.shape[1]), rows(oss.shape[1]),
                  _const_spec(wo.shape), _const_spec((1, d)),
                  _const_spec((d, dff)), _const_spec((d, dff)), _const_spec((dff, d)),
                  _const_spec((1, d))],
        out_specs=[rowp(d), rows(d)],
        out_shape=[jax.ShapeDtypeStruct((mp, d), F32), jax.ShapeDtypeStruct((ms, d), F32)],
        compiler_params=_params(("arbitrary",)),
        name="out_ffn2",
    )(x1p, o5p, osp, x1s, o5s, oss, wo, nw, wg, wu, wd, fin)


def _s5_param_kernel(lre_ref, lim_ref, step_ref, bre_ref, bim_ref,
                     are_ref, aim_ref, bbre_ref, bbim_ref):
    lre, lim, step = lre_ref[...], lim_ref[...], step_ref[...]
    mag = jnp.exp(lre * step)
    ang = lim * step
    are = mag * jnp.cos(ang)
    aim = mag * jnp.sin(ang)
    den = lre * lre + lim * lim
    nre, nim = are - 1.0, aim
    cre = (nre * lre + nim * lim) / den
    cim = (nim * lre - nre * lim) / den
    are_ref[...] = are
    aim_ref[...] = aim
    bre, bim = bre_ref[...], bim_ref[...]
    bbre_ref[...] = cre * bre - cim * bim
    bbim_ref[...] = cre * bim + cim * bre


def _s5_params(lam_re, lam_im, log_step, b_re, b_im):
    g, p = lam_re.shape
    hh = b_re.shape[-1]
    gp = g * p
    step = jnp.exp(log_step.astype(F32))
    step_row = jnp.broadcast_to(step[:, None], (g, p)).reshape(1, gp)
    to_lanes = lambda b: b.astype(F32).reshape(gp, hh).T
    full = lambda shape: pl.BlockSpec(shape, lambda: (0,) * len(shape))
    are, aim, bbre, bbim = pl.pallas_call(
        _s5_param_kernel,
        in_specs=[full((1, gp)), full((1, gp)), full((1, gp)), full((hh, gp)), full((hh, gp))],
        out_specs=[full((1, gp)), full((1, gp)), full((hh, gp)), full((hh, gp))],
        out_shape=[jax.ShapeDtypeStruct((1, gp), F32), jax.ShapeDtypeStruct((1, gp), F32),
                   jax.ShapeDtypeStruct((hh, gp), F32), jax.ShapeDtypeStruct((hh, gp), F32)],
        name="s5_params",
    )(lam_re.astype(F32).reshape(1, gp), lam_im.astype(F32).reshape(1, gp), step_row,
      to_lanes(b_re), to_lanes(b_im))
    to_blocks = lambda b: b.reshape(hh, g, p).transpose(1, 0, 2)
    return are, aim, to_blocks(bbre), to_blocks(bbim)


def _block_diag(w, nblk):
    g, a, b = w.shape
    gl = g // nblk
    w = w.reshape(nblk, gl, a, b)
    eye = jnp.eye(gl, dtype=w.dtype)
    out = w[:, :, :, None, :] * eye[None, :, None, :, None]
    return out.reshape(nblk, gl * a, gl * b)


def _s5_kernel(*refs, nb, tl, n_cast, ride_per):
    refs = list(refs)
    take = lambda n: [refs.pop(0) for _ in range(n)]
    (h_ref, wu_ref, wb_ref, wcre_ref, wcim_ref, are_ref, aim_ref, d_ref, wglu_ref, bglu_ref,
     s0re_ref, s0im_ref) = take(12)
    cast_in = take(n_cast)
    ride_in = take(9 if ride_per else 0)
    o5_ref, sre_ref, sim_ref = take(3)
    cast_out = take(n_cast)
    ride_out = take(2 if ride_per else 0)
    u_s, xre_s, xim_s, y_s = take(4)
    ride_s = [refs.pop() for _ in range(2 if ride_per else 0)][::-1]
    perm_s = refs
    for src, dst in zip(cast_in, cast_out):
        dst[...] = src[...].astype(BF16)
    if ride_per:
        _ssds_state_body(*ride_in, *ride_out, *ride_s, part=pl.program_id(0) % ride_per)
    width = u_s.shape[1]
    nstate = xre_s.shape[1]
    nblk = wb_ref.shape[0]
    cin = width // nblk
    cst = nstate // nblk
    r = nb * tl
    pack = 2 * SUBLANES

    @pl.when(pl.program_id(0) == 0)
    def _():
        sre_ref[...] = s0re_ref[...]
        sim_ref[...] = s0im_ref[...]

    if perm_s:
        slab_s, = perm_s
        assert cin == LANES and nb == SUBLANES
        pitch = tl + SUBLANES
        u_bt = _dot(h_ref[...].reshape(r, h_ref.shape[2]), wu_ref[...])
        for b in range(nb):
            for k in range(nblk):
                slab_s[k, b * pitch:b * pitch + tl, :] = u_bt[b * tl:(b + 1) * tl,
                                                              k * cin:(k + 1) * cin]

        for t in range(tl):
            for k in range(nblk):
                u_s[t * nb:(t + 1) * nb, k * cin:(k + 1) * cin] = (
                    slab_s[k, pl.ds(t, nb, stride=pitch), :])
        u = u_s[...]
    else:
        u = _dot(h_ref[...], wu_ref[...])
        u_s[...] = u
    ub = u.astype(BF16)

    def drive(k):
        bu = _dot(ub[:, k * cin:(k + 1) * cin], wb_ref[k])
        xre_s[:, k * cst:(k + 1) * cst] = bu[:, :cst]
        xim_s[:, k * cst:(k + 1) * cst] = bu[:, cst:]

    def scan(k):
        lanes = slice(k * cst, (k + 1) * cst)
        a_re = jnp.broadcast_to(are_ref[:, lanes], (SUBLANES, cst))
        a_im = jnp.broadcast_to(aim_ref[:, lanes], (SUBLANES, cst))

        def step(rows, s_re, s_im):
            n_re = a_re * s_re - a_im * s_im + xre_s[rows, lanes]
            n_im = a_re * s_im + a_im * s_re + xim_s[rows, lanes]
            xre_s[rows, lanes] = n_re
            xim_s[rows, lanes] = n_im
            return n_re, n_im

        if nb == SUBLANES:
            s_re, s_im = sre_ref[:, lanes], sim_ref[:, lanes]
            for t in range(tl):
                s_re, s_im = step(slice(t * nb, (t + 1) * nb), s_re, s_im)
            sre_ref[:, lanes] = s_re
            sim_ref[:, lanes] = s_im
        else:
            def group_body(sg, carry):
                srow = pl.ds(pl.multiple_of(sg * SUBLANES, SUBLANES), SUBLANES)
                st = (sre_ref[srow, lanes], sim_ref[srow, lanes])
                for t in range(tl):
                    st = step(pl.ds(pl.multiple_of(t * nb + sg * SUBLANES, SUBLANES), SUBLANES), *st)
                sre_ref[srow, lanes], sim_ref[srow, lanes] = st
                return carry

            lax.fori_loop(0, nb // SUBLANES, group_body, 0)

    def readout(k):
        st = slice(k * cst, (k + 1) * cst)
        y_s[:, k * cin:(k + 1) * cin] = (_dot(xre_s[:, st].astype(BF16), wcre_ref[k])
                                         - _dot(xim_s[:, st].astype(BF16), wcim_ref[k]))

    for k in range(nblk + 2):
        if k < nblk:
            drive(k)
        if 1 <= k <= nblk:
            scan(k - 1)
        if k >= 2:
            readout(k - 2)
    y = y_s[...] + d_ref[...] * u_s[...]
    v = _gelu_tanh(y)
    gate = _dot(v.astype(BF16), wglu_ref[...]) + bglu_ref[...]
    o5 = v * _sigmoid(gate)
    if perm_s:
        for k in range(nblk):
            slab_s[k, 0:r, :] = o5[:, k * cin:(k + 1) * cin]
        for b in range(nb):
            for t0 in range(0, tl, pack):
                for k in range(nblk):
                    lo = slab_s[k, pl.ds(t0 * nb + b, SUBLANES, stride=nb), :]
                    hi = slab_s[k, pl.ds((t0 + SUBLANES) * nb + b, SUBLANES, stride=nb), :]
                    o5_ref[b, t0:t0 + pack, k * cin:(k + 1) * cin] = (
                        jnp.concatenate([lo, hi], axis=0).astype(BF16))
    else:
        o5_ref[...] = o5.astype(BF16)


def _s5(h, wu, wb, wcre, wcim, are, aim, dsk, wglu, bglu, s0re, s0im, nb, tl, casts=(),
        ride=None):
    width = wu.shape[1]
    nstate = are.shape[1]
    r = nb * tl
    batch_major = h.ndim == 3
    if batch_major:
        steps = h.shape[1] // tl
        blk = lambda w: pl.BlockSpec((nb, tl, w), lambda i: (0, i, 0))
        o_shape = (nb, h.shape[1], width)
        extra = [pltpu.VMEM((width // LANES, nb * (tl + SUBLANES), LANES), F32)]
    else:
        steps = h.shape[0] // r
        blk = lambda w: pl.BlockSpec((r, w), lambda i: (i, 0))
        o_shape = (h.shape[0], width)
        extra = []
    c_in, c_out, c_shapes = _cast_specs(casts, steps)
    r_in, r_out, r_shapes, r_scratch, ride_per = (
        _ride_specs(ride, steps) if ride else ([], [], [], [], 0))
    kern = functools.partial(_s5_kernel, nb=nb, tl=tl, n_cast=len(casts), ride_per=ride_per)
    return pl.pallas_call(
        kern,
        grid=(steps,),
        in_specs=[blk(h.shape[-1]), _const_spec(wu.shape), _const_spec(wb.shape),
                  _const_spec(wcre.shape), _const_spec(wcim.shape), _const_spec(are.shape),
                  _const_spec(aim.shape), _const_spec(dsk.shape), _const_spec(wglu.shape),
                  _const_spec(bglu.shape), _const_spec(s0re.shape), _const_spec(s0im.shape)]
                 + c_in + r_in,
        out_specs=[blk(width), pl.BlockSpec((nb, nstate), lambda i: (0, 0)),
                   pl.BlockSpec((nb, nstate), lambda i: (0, 0))] + c_out + r_out,
        out_shape=[jax.ShapeDtypeStruct(o_shape, BF16),
                   jax.ShapeDtypeStruct((nb, nstate), F32),
                   jax.ShapeDtypeStruct((nb, nstate), F32)] + c_shapes + r_shapes,
        scratch_shapes=[pltpu.VMEM((r, width), F32), pltpu.VMEM((r, nstate), F32),
                        pltpu.VMEM((r, nstate), F32), pltpu.VMEM((r, width), F32)] + extra
                       + r_scratch,
        compiler_params=_params(("arbitrary",)),
        name="s5",
    )(h, wu, wb, wcre, wcim, are, aim, dsk, wglu, bglu, s0re, s0im, *casts, *(ride or ()))


def _split2(x):
    hi = x.astype(BF16)
    lo = (x - hi.astype(F32)).astype(BF16)
    return jnp.concatenate([hi, lo], axis=1)


def _split3(x):
    hi = x.astype(BF16)
    r1 = x - hi.astype(F32)
    mid = r1.astype(BF16)
    lo = (r1 - mid.astype(F32)).astype(BF16)
    return jnp.concatenate([hi, mid, lo], axis=1)


HEAD_REP = 32
RIDE_NB = SUBLANES


def _pack2(x):
    grp = lax.broadcasted_iota(jnp.int32, x.shape, 1) // HEAD_REP
    hi = x.astype(BF16).astype(F32)
    return jnp.where(grp == 0, hi, x - hi).astype(BF16)


def _pack3(x):
    grp = lax.broadcasted_iota(jnp.int32, x.shape, 1) // HEAD_REP
    hi = x.astype(BF16).astype(F32)
    r1 = x - hi
    mid = r1.astype(BF16).astype(F32)
    return jnp.where(grp == 0, hi, jnp.where(grp == 1, mid, r1 - mid)).astype(BF16)


def _group_norm(y, zg, nw, gw):
    outs = []
    for g in range(SSD_GROUPS):
        seg = y[:, g * gw:(g + 1) * gw] * zg[:, g * gw:(g + 1) * gw]
        outs.append(seg * lax.rsqrt(jnp.mean(seg * seg, axis=-1, keepdims=True) + EPS))
    return jnp.concatenate(outs, axis=1) * nw


def _ssds_pre_kernel(h_ref, wz_ref, wx_ref, wdt_ref, cw_ref, cb_ref, dtb_ref, alog_ref, dsk_ref,
                     conv0_ref, g2_ref, exp_ref, c_o, b_o, xdd_o, yp_o, eac_o, zg_o, cdec_o, cv_o):
    nbt = conv0_ref.shape[1]
    seq = h_ref.shape[0] // nbt
    width = zg_o.shape[1]
    nbc = c_o.shape[1]
    hb = h_ref[...]
    zg_o[...] = _silu(_dot(hb, wz_ref[...]))
    dt = _softplus(_dot(hb, wdt_ref[...]) + dtb_ref[...])
    la = dt * (-jnp.exp(alog_ref[...]))
    xbc = _dot(hb, wx_ref[...])
    rows = [slice(t * nbt, (t + 1) * nbt) for t in range(seq)]
    full = [conv0_ref[k] for k in range(SSD_CONV - 1)] + [xbc[r] for r in rows]
    for k in range(SSD_CONV - 1):
        cv_o[k] = full[seq + k]
    acums = []
    for t in range(seq):
        acums.append(la[rows[t]] if t == 0 else acums[-1] + la[rows[t]])
    tot = acums[-1]
    cdec_o[...] = jnp.exp(tot)
    xs, bq, cq = [], [], []
    for t in range(seq):
        acc = cb_ref[...]
        for k in range(SSD_CONV):
            acc = acc + cw_ref[k:k + 1, :] * full[t + k]
        xc = _silu(acc)
        xs.append(xc[:, :width])
        bq.append(xc[:, width:width + nbc].astype(BF16).astype(F32))
        cq.append(xc[:, width + nbc:].astype(BF16).astype(F32))
    for t in range(seq):
        r = rows[t]
        b_o[r, :] = bq[t]
        c_o[r, :] = cq[t]
        xdd_o[r, :] = xs[t] * _dot(_split2(dt[r] * jnp.exp(tot - acums[t])), exp_ref[...])
        eac_o[r, :] = _dot(_split2(jnp.exp(acums[t])), exp_ref[...])
        yp = dsk_ref[...] * xs[t]
        for s in range(t + 1):
            cbx = _dot(_split2(cq[t] * bq[s]), g2_ref[...])
            coef = cbx * jnp.exp(acums[t] - acums[s]) * dt[rows[s]]
            yp = yp + _dot(_split2(coef), exp_ref[...]) * xs[s]
        yp_o[r, :] = yp


def _ssds_state_body(c_ref, b_ref, xdd_ref, yp_ref, eac_ref, zg_ref, cdec_ref, st_ref, nw_ref,
                     o_ref, sto_ref, yoff_s, xddt_s, part):
    seq, nb, width = xdd_ref.shape
    nsub, nheads, hd_dim, nst = st_ref.shape
    hpg = nheads // SSD_GROUPS
    gw = width // SSD_GROUPS
    rws = seq * nb
    cst = c_ref[...].reshape(rws, c_ref.shape[2]).astype(BF16)
    bst = b_ref[...].reshape(rws, b_ref.shape[2])
    xdd = xdd_ref[...].reshape(rws, width)
    xddt_s[...] = jnp.concatenate([xdd, jnp.zeros((LANES - rws, width), F32)], axis=0).T.astype(BF16)
    rowb = lax.broadcasted_iota(jnp.int32, (rws, gw), 0) % nb
    zpad = jnp.zeros((LANES - rws, nst), BF16)
    for bl in range(nsub):
        b = part * nsub + bl
        mine = rowb == b
        for g in range(SSD_GROUPS):
            gl = slice(g * gw, (g + 1) * gw)
            hg = st_ref[bl, g * hpg:(g + 1) * hpg].reshape(gw, nst)
            res = _dot_nt(cst[:, g * nst:(g + 1) * nst], hg.astype(BF16))
            yoff_s[:, gl] = jnp.where(mine, res, 0.0 if bl == 0 else yoff_s[:, gl])
            bm = jnp.where(mine[:, :nst], bst[:, g * nst:(g + 1) * nst], 0.0).astype(BF16)
            upd = _dot(xddt_s[gl, :], jnp.concatenate([bm, zpad], axis=0))
            for jh in range(hpg):
                hd = g * hpg + jh
                hrows = slice(jh * hd_dim, (jh + 1) * hd_dim)
                sto_ref[bl, hd] = hg[hrows] * cdec_ref[b, hd] + upd[hrows]
    y = yp_ref[...].reshape(rws, width) + yoff_s[...] * eac_ref[...].reshape(rws, width)
    o = _group_norm(y, zg_ref[...].reshape(rws, width), nw_ref[...], gw).reshape(seq, nb, width)
    if nsub == nb:
        o_ref[...] = o
    else:
        @pl.when(part == 0)
        def _():
            o_ref[...] = o

        @pl.when(part != 0)
        def _():
            own = lax.broadcasted_iota(jnp.int32, o.shape, 1) // nsub == part
            o_ref[...] = jnp.where(own, o, o_ref[...])


def _ssds(h_tm, wz, wx, wdt, cw, cb, dtb, alog, dsk, nw, ssd0, conv0_tm):
    m, d = h_tm.shape
    nseq, nheads, hd_dim, nst = ssd0.shape
    seq = m // nseq
    width = wz.shape[1]
    xbc_w = wx.shape[1]
    nbc = SSD_GROUPS * nst
    hpg = nheads // SSD_GROUPS
    head = jnp.arange(LANES)
    exp1 = (head[:, None] == (jnp.arange(width) // hd_dim)[None, :]).astype(BF16)
    g1 = ((jnp.arange(nbc) // nst)[:, None] == (head // hpg)[None, :]) & (head < nheads)[None, :]
    exp_m = jnp.concatenate([exp1, exp1], axis=0)
    g2 = jnp.concatenate([g1, g1], axis=0).astype(BF16)
    full = lambda shape: pl.BlockSpec(shape, lambda: (0,) * len(shape))
    ins = (h_tm, wz, wx, wdt, cw, cb, dtb, alog, dsk, conv0_tm, g2, exp_m)
    outs = [jax.ShapeDtypeStruct((m, nbc), F32), jax.ShapeDtypeStruct((m, nbc), F32),
            jax.ShapeDtypeStruct((m, width), F32), jax.ShapeDtypeStruct((m, width), F32),
            jax.ShapeDtypeStruct((m, width), F32), jax.ShapeDtypeStruct((m, width), F32),
            jax.ShapeDtypeStruct((nseq, LANES), F32),
            jax.ShapeDtypeStruct((SSD_CONV - 1, nseq, xbc_w), F32)]
    c, b, xdd, yp, eac, zg, cdec, conv1_tm = pl.pallas_call(
        _ssds_pre_kernel,
        in_specs=[full(a.shape) for a in ins],
        out_specs=[full(o.shape) for o in outs],
        out_shape=outs,
        compiler_params=pltpu.CompilerParams(vmem_limit_bytes=VMEM_LIMIT),
        name="ssd_sample_pre",
    )(*ins)

    tm3 = lambda a: a.reshape(seq, nseq, a.shape[1])
    return (tm3(c), tm3(b), tm3(xdd), tm3(yp), tm3(eac), tm3(zg), cdec, ssd0, nw), conv1_tm


def _ride_specs(ride, steps):
    c, b, xdd, yp, eac, zg, cdec, ssd0, nw = ride
    seq, nseq, width = xdd.shape
    nsub = nseq // steps
    assert nsub * steps == nseq and RIDE_NB % nsub == 0
    per = RIDE_NB // nsub
    blk = lambda a: pl.BlockSpec((seq, RIDE_NB, a.shape[2]), lambda i: (0, i // per, 0))
    st_spec = pl.BlockSpec((nsub,) + ssd0.shape[1:], lambda i: (i, 0, 0, 0))
    ins = [blk(c), blk(b), blk(xdd), blk(yp), blk(eac), blk(zg),
           pl.BlockSpec((RIDE_NB, LANES), lambda i: (i // per, 0), memory_space=pltpu.SMEM),
           st_spec, _const_spec(nw.shape)]
    outs = [blk(xdd), st_spec]
    shapes = [jax.ShapeDtypeStruct((seq, nseq, width), F32), jax.ShapeDtypeStruct(ssd0.shape, F32)]
    scratch = [pltpu.VMEM((seq * RIDE_NB, width), F32), pltpu.VMEM((width, LANES), BF16)]
    return ins, outs, shapes, scratch, per


def _ssdp_kernel(h_ref, wz_ref, wx_ref, wdt_ref, cw_ref, cb_ref, dtb_ref, alog_ref, dsk_ref,
                 nw_ref, exp_ref, perm_ref, permt_ref, o_ref, st_ref, cv_ref,
                 xbc_s, zg_s, acum_s, acumt_s, e2_s, e3_s, xd_s, xdd_s, ysk_s, b_s, c_s,
                 cdec_s, ht_s, tail_s, m_s, *, nch, col_chunk):
    t = SSD_CHUNK
    r = nch * t
    j = pl.program_id(1)
    nheads, hd_dim, nst = st_ref.shape[1:]
    width = nheads * hd_dim
    gw = width // SSD_GROUPS
    hpg = nheads // SSD_GROUPS
    xbc_w = xbc_s.shape[1]
    ntap = SSD_CONV - 1
    pre = ntap * SUBLANES
    ext = pre + t
    steps = t // SUBLANES
    last_rows = [t - 1 - SUBLANES * (ntap - 1 - v) for v in range(ntap)]

    @pl.when(j == 0)
    def _():
        ht_s[...] = jnp.zeros(ht_s.shape, F32)
        tail_s[...] = jnp.zeros(tail_s.shape, F32)

    hb = jnp.concatenate([_dot(perm_ref[...], h_ref[c * t:(c + 1) * t, :]) for c in range(nch)],
                         axis=0).astype(BF16)
    dt = _softplus(_dot(hb, wdt_ref[...]) + dtb_ref[...])
    la = dt * (-jnp.exp(alog_ref[...]))

    def xbc_cols(jc):
        lanes = slice(jc * col_chunk, (jc + 1) * col_chunk)
        xall = _dot(hb, wx_ref[:, lanes])
        for c in range(nch):
            xbc_s[c * ext + pre:(c + 1) * ext, lanes] = xall[c * t:(c + 1) * t, :]
        for c in range(nch):
            base = c * ext
            xbc_s[base:base + pre, lanes] = xbc_s[base + t - 1:base + pre + t - 1, lanes]
            for v in range(ntap):
                if c == 0:
                    prev = tail_s[v:v + 1, lanes]
                else:
                    prow = base - ext + pre + last_rows[v]
                    prev = xbc_s[prow:prow + 1, lanes]
                xbc_s[base + v * SUBLANES:base + v * SUBLANES + 1, lanes] = prev
            acc = cb_ref[:, lanes]
            for k in range(SSD_CONV):
                acc = acc + cw_ref[k:k + 1, lanes] * xbc_s[base + k * SUBLANES:
                                                           base + k * SUBLANES + t, lanes]
            xc = _silu(acc)
            rows = slice(c * t, (c + 1) * t)
            lo = jc * col_chunk
            if lo < width:
                xd_s[rows, lanes] = xc.astype(BF16)
                xdd_s[rows, lanes] = (xc * e2_s[rows, lanes]).astype(BF16)
                ysk_s[rows, lanes] = xc * dsk_ref[:, lanes]
            elif lo < width + SSD_GROUPS * nst:
                b_s[rows, lo - width:lo - width + col_chunk] = xc.astype(BF16)
            else:
                cl = lo - width - SSD_GROUPS * nst
                c_s[rows, cl:cl + col_chunk] = xc.astype(BF16)

    x_cols = width // col_chunk
    for jc in range(x_cols, xbc_w // col_chunk):
        xbc_cols(jc)

    sub = lax.broadcasted_iota(jnp.int32, (SUBLANES, LANES), 0)
    tiles = []
    for c in range(nch):
        run = []
        for m in range(steps):
            v = la[c * t + m * SUBLANES:c * t + (m + 1) * SUBLANES, :]
            run.append(v if m == 0 else run[-1] + v)
        inc = run[-1]
        for s in (1, 2, 4):
            inc = inc + jnp.where(sub >= s, pltpu.roll(inc, s, axis=0), 0.0)
        offset = inc - run[-1]
        tiles += [r + offset for r in run]
    acum = jnp.concatenate(tiles, axis=0)
    tots = [acum[(c + 1) * t - 1:(c + 1) * t, :] for c in range(nch)]
    arow = acum - jnp.log(dt)
    for c in range(nch):
        acumt_s[c] = arow[c * t:(c + 1) * t, :].T
    tot_rows = jnp.concatenate([jnp.broadcast_to(v, (t, LANES)) for v in tots], axis=0)
    dstate = jnp.exp(tot_rows - acum)
    cdec = jnp.exp(jnp.concatenate([jnp.broadcast_to(v, (SUBLANES, LANES)) for v in tots], axis=0))
    e2_s[...] = _dot(_pack2(dt * dstate), exp_ref[...])
    e3_s[...] = _dot(_pack2(jnp.exp(acum)), exp_ref[...])
    cdx = _dot(_pack2(cdec), exp_ref[...])
    for c in range(nch):
        cdec_s[c] = cdx[c * SUBLANES:(c + 1) * SUBLANES, :]
    acum_s[...] = acum

    def token(i):
        return (i % SUBLANES) * steps + i // SUBLANES

    causal = (token(lax.broadcasted_iota(jnp.int32, (t, t), 0))
              >= token(lax.broadcasted_iota(jnp.int32, (t, t), 1)))
    for c in range(nch):
        rows = slice(c * t, (c + 1) * t)
        for g in range(SSD_GROUPS):
            cbm = jnp.where(causal, _dot_nt(c_s[rows, g * nst:(g + 1) * nst],
                                            b_s[rows, g * nst:(g + 1) * nst]), 0.0)
            for hd in range(g * hpg, (g + 1) * hpg):
                diff = acum_s[rows, hd:hd + 1] - acumt_s[c, hd:hd + 1, :]
                m_s[(c * nheads + hd) * t:(c * nheads + hd + 1) * t, :] = (
                    cbm * jnp.exp(jnp.where(causal, diff, -jnp.inf))).astype(BF16)

    for jc in range(x_cols):
        lanes = slice(jc * col_chunk, (jc + 1) * col_chunk)
        xbc_cols(jc)
        zg_s[:, lanes] = _silu(_dot(hb, wz_ref[:, lanes]))

    for v in range(ntap):
        row = (nch - 1) * ext + pre + last_rows[v]
        tail_s[v:v + 1, :] = xbc_s[row:row + 1, :]
    cv_ref[0] = tail_s[...]

    lane = lax.broadcasted_iota(jnp.int32, (t, LANES), 1)
    keep_lo = jnp.where(lane < hd_dim, 1.0, 0.0).astype(BF16)
    keep_hi = jnp.where(lane < hd_dim, 0.0, 1.0).astype(BF16)
    nw = nw_ref[...]

    for c in range(nch):
        rows = slice(c * t, (c + 1) * t)
        ys = []
        for g in range(SSD_GROUPS):
            gl = slice(g * gw, (g + 1) * gw)
            cg = c_s[rows, g * nst:(g + 1) * nst]
            bg = b_s[rows, g * nst:(g + 1) * nst]
            pairs = []
            for jp in range(hpg // 2):
                ha = g * hpg + 2 * jp
                m2 = jnp.concatenate([m_s[(c * nheads + hd) * t:(c * nheads + hd + 1) * t, :]
                                      for hd in (ha, ha + 1)], axis=1)
                xdp = xd_s[rows, ha * hd_dim:ha * hd_dim + LANES]
                rhs = jnp.concatenate([xdp * keep_lo, xdp * keep_hi], axis=0)
                pairs.append(_dot(m2, rhs))
            htg = ht_s[:, gl]
            y = (jnp.concatenate(pairs, axis=1) + _dot(cg, htg.astype(BF16)) * e3_s[rows, gl]
                 + ysk_s[rows, gl])
            ys.append(y)
            upd = lax.dot_general(bg, xdd_s[rows, gl], (((0,), (0,)), ((), ())),
                                  preferred_element_type=F32)
            ht_s[:, gl] = htg * cdec_s[c, 0:1, gl] + upd
        o = _group_norm(jnp.concatenate(ys, axis=1), zg_s[rows, :], nw, gw).astype(BF16)
        o_ref[rows, :] = _dot(permt_ref[...], o).astype(BF16)

    @pl.when(j == pl.num_programs(1) - 1)
    def _():
        for pr in range(nheads // 2):
            tt = ht_s[:, pr * LANES:(pr + 1) * LANES].T
            st_ref[0, 2 * pr] = tt[:hd_dim]
            st_ref[0, 2 * pr + 1] = tt[hd_dim:]


def _ssdp(h, wz, wx, wdt, cw, cb, dtb, alog, dsk, nw, nseq, nheads, nch):
    m, d = h.shape
    width = wz.shape[1]
    xbc_w = wx.shape[1]
    hd_dim = width // nheads
    nst = SSD_STATE
    t = SSD_CHUNK
    r = nch * t
    nblk = m // (nseq * r)
    ri = jnp.arange(r)
    steps = t // SUBLANES
    tok = (ri % t % SUBLANES) * steps + ri % t // SUBLANES
    perm = (tok[:t, None] == jnp.arange(t)[None, :]).astype(BF16)
    lane = jnp.arange(LANES)
    head = (lane % HEAD_REP)[:, None]
    exp_m = ((head == (jnp.arange(width) // hd_dim)[None, :])
             & (lane < 2 * HEAD_REP)[:, None]).astype(BF16)
    row = lambda w: pl.BlockSpec((r, w), lambda i, j: (i * nblk + j, 0))
    kern = functools.partial(_ssdp_kernel, nch=nch, col_chunk=4 * LANES)
    vm = lambda shape, dt: pltpu.VMEM(shape, dt)
    return pl.pallas_call(
        kern,
        grid=(nseq, nblk),
        in_specs=[row(d), _const_spec(wz.shape), _const_spec(wx.shape), _const_spec(wdt.shape),
                  _const_spec(cw.shape), _const_spec(cb.shape), _const_spec(dtb.shape),
                  _const_spec(alog.shape), _const_spec(dsk.shape), _const_spec(nw.shape),
                  _const_spec(exp_m.shape),
                  _const_spec(perm.shape), _const_spec(perm.shape)],
        out_specs=[row(width),
                   pl.BlockSpec((1, nheads, hd_dim, nst), lambda i, j: (i, 0, 0, 0)),
                   pl.BlockSpec((1, SSD_CONV - 1, xbc_w), lambda i, j: (i, 0, 0))],
        out_shape=[jax.ShapeDtypeStruct((m, width), BF16),
                   jax.ShapeDtypeStruct((nseq, nheads, hd_dim, nst), F32),
                   jax.ShapeDtypeStruct((nseq, SSD_CONV - 1, xbc_w), F32)],
        scratch_shapes=[vm((nch * (t + (SSD_CONV - 1) * SUBLANES), xbc_w), F32),
                        vm((r, width), F32),
                        vm((r, LANES), F32), vm((nch, LANES, t), F32),
                        vm((r, width), F32), vm((r, width), F32),
                        vm((r, width), BF16), vm((r, width), BF16), vm((r, width), F32),
                        vm((r, SSD_GROUPS * nst), BF16), vm((r, SSD_GROUPS * nst), BF16),
                        vm((nch, SUBLANES, width), F32), vm((nst, width), F32),
                        vm((SSD_CONV - 1, xbc_w), F32), vm((nch * nheads * t, t), BF16)],
        compiler_params=_params(("arbitrary", "arbitrary")),
        name="ssd_prompt",
    )(h, wz, wx, wdt, cw, cb, dtb, alog, dsk, nw, exp_m, perm, perm.T)


def _rep_heads(v):
    grp = jnp.pad(v, [(0, 0)] * (v.ndim - 1) + [(0, HEAD_REP - v.shape[-1])])
    tail = jnp.zeros(v.shape[:-1] + (LANES - 3 * HEAD_REP,), v.dtype)
    return jnp.concatenate([grp, grp, grp, tail], axis=-1)


def _s5_call(h, s5re0, s5im0, w, nb, tl, casts=(), ride=None):
    g, p = s5re0.shape[1:]
    o5, re1, im1, *more = _s5(h, w["w_u"], w["s5_wb"], w["s5_wcre"], w["s5_wcim"], w["s5_are"],
                              w["s5_aim"], w["s5_d"], w["s5_wglu"], w["s5_bglu"],
                              s5re0.reshape(nb, g * p), s5im0.reshape(nb, g * p), nb, tl, casts,
                              ride)
    return o5, re1.reshape(nb, g, p), im1.reshape(nb, g, p), more


def _layer(xp, xs, s5re0, s5im0, ssd0, conv0, w, *, tm, s5_tl, ssd_nch):
    bp, lp, d = xp.shape
    bs, ls, _ = xs.shape
    mp, ms = bp * lp, bs * ls
    g, p = s5re0.shape[1:]
    nheads = ssd0.shape[1]

    xs_tm = xs.transpose(1, 0, 2).reshape(ms, d)
    x1p, hp, x1s, hs, w_u, w_z, w_x, w_dt = _ffn1(
        xp.reshape(mp, d), xs_tm, w["ffn1_norm"], w["ffn1_wg"], w["ffn1_wu"], w["ffn1_wd"],
        w["mix_norm"], w["w_in"], w["w_in_splits"], tm)
    w = dict(w, w_u=w_u, w_z=w_z, w_x=w_x, w_dt=_rep_heads(w_dt))

    ride, s_conv_tm = _ssds(hs, w["w_z"], w["w_x"], w["w_dt"], w["conv_w"], w["conv_b"],
                            w["dt_bias"], w["a_log"], w["ssd_d"], w["ssd_norm"], ssd0,
                            conv0.transpose(1, 0, 2))
    zeros = jnp.zeros((bp, g, p), F32)
    o5p, p_re, p_im, (wg2, wu2, wd2, wo, oss, s_ssd) = _s5_call(
        hp.reshape(bp, lp, d), zeros, zeros, w, bp, s5_tl,
        (w["ffn2_wg"], w["ffn2_wu"], w["ffn2_wd"], w["w_out"]), ride)
    oss = oss.reshape(ms, -1)
    osp, p_ssd, p_conv = _ssdp(hp, w["w_z"], w["w_x"], w["w_dt"], w["conv_w"], w["conv_b"],
                               w["dt_bias"], w["a_log"], w["ssd_d"], w["ssd_norm"],
                               bp, nheads, ssd_nch)
    o5s, s_re, s_im, _ = _s5_call(hs, s5re0, s5im0, w, bs, ls)

    yp, ys = _out_ffn2(x1p, o5p.reshape(mp, -1), osp, x1s, o5s, oss, wo, w["ffn2_norm"],
                       wg2, wu2, wd2, w["final_norm"], tm)
    return (yp.reshape(bp, lp, d), ys.reshape(ls, bs, d).transpose(1, 0, 2),
            p_re, p_im, p_ssd, p_conv, s_re, s_im, s_ssd, s_conv_tm.transpose(1, 0, 2))


def kernel(x_prompt, x_sample, state_s5_re, state_s5_im, state_ssd, state_conv, ffn1_norm, ffn1_w_gate, ffn1_w_up, ffn1_w_down, mix_norm, w_in, s5_lambda_re, s5_lambda_im, s5_log_step, s5_b_re, s5_b_im, s5_c_re, s5_c_im, s5_d, s5_w_glu, s5_b_glu, ssd_conv_w, ssd_conv_b, ssd_dt_bias, ssd_a_log, ssd_d, ssd_norm, w_out, ffn2_norm, ffn2_w_gate, ffn2_w_up, ffn2_w_down, final_norm):
    depth = w_in.shape[0]
    assert depth == 1, "single-layer stack"
    i = 0
    d = x_prompt.shape[-1]
    g, p = s5_lambda_re.shape[1:]
    s5_width = g * S5_GROUP
    nheads = ssd_a_log.shape[1]
    ssd_width = nheads * SSD_HEAD_DIM
    xbc_w = ssd_conv_w.shape[-1]
    c0, c1, c2 = s5_width, s5_width + ssd_width, s5_width + ssd_width + xbc_w
    row = lambda v: v.astype(F32).reshape(1, -1)

    are, aim, bbre, bbim = _s5_params(s5_lambda_re[i], s5_lambda_im[i], s5_log_step[i],
                                      s5_b_re[i], s5_b_im[i])
    nblk = s5_width // LANES
    wb = jnp.concatenate([_block_diag(bbre, nblk), _block_diag(bbim, nblk)],
                         axis=-1).astype(BF16)
    wcre = _block_diag(s5_c_re[i].astype(F32).transpose(0, 2, 1), nblk).astype(BF16)
    wcim = _block_diag(s5_c_im[i].astype(F32).transpose(0, 2, 1), nblk).astype(BF16)

    w = {
        "ffn1_norm": row(ffn1_norm[i]), "mix_norm": row(mix_norm[i]),
        "ffn2_norm": row(ffn2_norm[i]), "final_norm": row(final_norm),
        "ffn1_wg": ffn1_w_gate.astype(F32), "ffn1_wu": ffn1_w_up.astype(F32),
        "ffn1_wd": ffn1_w_down.astype(F32),
        "w_in": w_in.astype(F32), "w_in_splits": (c0, c1 - c0, c2 - c1, nheads),
        "ffn2_wg": ffn2_w_gate.astype(F32), "ffn2_wu": ffn2_w_up.astype(F32),
        "ffn2_wd": ffn2_w_down.astype(F32), "w_out": w_out.astype(F32),
        "s5_wb": wb, "s5_wcre": wcre, "s5_wcim": wcim, "s5_are": are, "s5_aim": aim,
        "s5_d": row(s5_d[i]), "s5_wglu": s5_w_glu[i].astype(BF16), "s5_bglu": row(s5_b_glu[i]),
        "conv_w": ssd_conv_w[i].astype(F32), "conv_b": row(ssd_conv_b[i]),
        "dt_bias": _rep_heads(row(ssd_dt_bias[i])), "a_log": _rep_heads(row(ssd_a_log[i])),
        "ssd_d": jnp.repeat(ssd_d[i].astype(F32), SSD_HEAD_DIM).reshape(1, -1),
        "ssd_norm": row(ssd_norm[i]),
    }

    lp = x_prompt.shape[1]
    yp, ys, *states = _layer(
        x_prompt, x_sample, state_s5_re[i], state_s5_im[i], state_ssd[i], state_conv[i], w,
        tm=512, s5_tl=min(lp, 64), ssd_nch=min(lp // SSD_CHUNK, 4))
    return (yp, ys) + tuple(v[None] for v in states)
```
